```python
import math
import jax, jax.numpy as jnp
from jax import lax
import numpy as np

D_MODEL = 1024
BATCH = 8
SEQ = 2048
DEPTH = 1
DEC_BATCH = 128
DEC_SEQ = 8
PAST_LEN = 16384
PAGE_SIZE = 128

D_MIX = D_MODEL
D_S5 = D_MIX // 2
D_LRU = D_MIX - D_S5
S5_GROUP = 16
S5_GROUPS = D_S5 // S5_GROUP
S5_STATE = 64
LRU_HEADS = 8
LRU_HEAD_DIM = D_LRU // LRU_HEADS
CONV_W = 4
LRU_C = 8.0
N_MEM = 256
XA_HEADS = 4
XA_HEAD_DIM = D_MODEL // XA_HEADS
N_EXPERT_GROUPS = 4
EXPERTS_PER_GROUP = 8
N_EXPERTS = N_EXPERT_GROUPS * EXPERTS_PER_GROUP
TOP_K_IN_GROUP = 2
D_FF_EXPERT = D_MODEL // 4
EPS = 1e-6

kernel_name = "hymba_s5_rglru_hiermoe_step"


def _rmsnorm(x, g):
    xf = x.astype(jnp.float32)
    y = xf * lax.rsqrt(jnp.mean(xf * xf, axis=-1, keepdims=True) + EPS)
    return (y * g.astype(jnp.float32)).astype(x.dtype)


def _complex_diag_combine(e1, e2):
    a1r, a1i, b1r, b1i = e1
    a2r, a2i, b2r, b2i = e2
    return (a2r * a1r - a2i * a1i,
            a2r * a1i + a2i * a1r,
            a2r * b1r - a2i * b1i + b2r,
            a2r * b1i + a2i * b1r + b2i)


def _real_diag_combine(e1, e2):
    a1, b1 = e1
    a2, b2 = e2
    return a2 * a1, a2 * b1 + b2


def _s5(u, h0_re, h0_im, lam_re, lam_im, log_dt, b_re, b_im, c_re, c_im, d_skip, w_glu, b_glu):
    f32 = jnp.float32
    bsz, s, _ = u.shape
    uf = u.astype(f32).reshape(bsz, s, S5_GROUPS, S5_GROUP)
    lr = lam_re.astype(f32)
    li = lam_im.astype(f32)
    dt = jnp.exp(log_dt.astype(f32))[:, None]
    mag = jnp.exp(lr * dt)
    ab_re = mag * jnp.cos(li * dt)
    ab_im = mag * jnp.sin(li * dt)
    den = lr * lr + li * li
    nr = ab_re - 1.0
    f_re = (nr * lr + ab_im * li) / den
    f_im = (ab_im * lr - nr * li) / den
    br = b_re.astype(f32)
    bi = b_im.astype(f32)
    bb_re = f_re[..., None] * br - f_im[..., None] * bi
    bb_im = f_re[..., None] * bi + f_im[..., None] * br
    bu_re = jnp.einsum("bsgc,gnc->bsgn", uf, bb_re)
    bu_im = jnp.einsum("bsgc,gnc->bsgn", uf, bb_im)
    h0r = h0_re.astype(f32)
    h0i = h0_im.astype(f32)
    bu_re = bu_re.at[:, 0].add(ab_re * h0r - ab_im * h0i)
    bu_im = bu_im.at[:, 0].add(ab_re * h0i + ab_im * h0r)
    a_re = jnp.broadcast_to(ab_re, bu_re.shape)
    a_im = jnp.broadcast_to(ab_im, bu_im.shape)
    _, _, h_re, h_im = lax.associative_scan(_complex_diag_combine, (a_re, a_im, bu_re, bu_im), axis=1)
    y = (jnp.einsum("bsgn,gcn->bsgc", h_re, c_re.astype(f32))
         - jnp.einsum("bsgn,gcn->bsgc", h_im, c_im.astype(f32))
         + d_skip.astype(f32) * uf)
    y = jax.nn.gelu(y.reshape(bsz, s, D_S5)).astype(u.dtype)
    out = y * jax.nn.sigmoid(y @ w_glu + b_glu)
    return out, h_re[:, -1], h_im[:, -1]


def _rglru(xb, gb, h0, conv_buf, conv_w, conv_b, w_a, b_a, w_x, b_x, lam):
    f32 = jnp.float32
    bsz, s, _ = xb.shape
    xp = jnp.concatenate([conv_buf.astype(xb.dtype), xb], axis=1)
    xc = conv_b + xp[:, 0:s] * conv_w[0]
    for k in range(1, CONV_W):
        xc = xc + xp[:, k:k + s] * conv_w[k]
    new_buf = xp[:, s:]
    xcf = xc.astype(f32)
    xh = xcf.reshape(bsz, s, LRU_HEADS, LRU_HEAD_DIM)
    r = jax.nn.sigmoid(jnp.einsum("bshi,hij->bshj", xh, w_a.astype(f32)) + b_a.astype(f32)).reshape(bsz, s, D_LRU)
    ig = jax.nn.sigmoid(jnp.einsum("bshi,hij->bshj", xh, w_x.astype(f32)) + b_x.astype(f32)).reshape(bsz, s, D_LRU)
    log_a = -LRU_C * r * jax.nn.softplus(-lam.astype(f32))
    a = jnp.exp(log_a)
    b = jnp.sqrt(-jnp.expm1(2.0 * log_a)) * (ig * xcf)
    b = b.at[:, 0].add(a[:, 0] * h0.astype(f32))
    _, h = lax.associative_scan(_real_diag_combine, (a, b), axis=1)
    y = (h * jax.nn.gelu(gb.astype(f32))).astype(xb.dtype)
    return y, h[:, -1], new_buf


def _mem_kv(mem, g_mem, w_k, w_v):
    bsz = mem.shape[0]
    m = _rmsnorm(mem, g_mem)
    k = (m @ w_k).reshape(bsz, N_MEM, XA_HEADS, XA_HEAD_DIM)
    v = (m @ w_v).reshape(bsz, N_MEM, XA_HEADS, XA_HEAD_DIM)
    return k, v


def _cross_attend(x, mem_k, mem_v, g, w_q, w_o):
    bsz, s, _ = x.shape
    h = _rmsnorm(x, g)
    q = (h @ w_q).reshape(bsz, s, XA_HEADS, XA_HEAD_DIM).astype(jnp.float32)
    sc = jnp.einsum("bshd,bmhd->bhsm", q, mem_k.astype(jnp.float32)) * (XA_HEAD_DIM ** -0.5)
    p = jax.nn.softmax(sc, axis=-1)
    o = jnp.einsum("bhsm,bmhd->bshd", p, mem_v.astype(jnp.float32)).reshape(bsz, s, D_MODEL)
    return o.astype(x.dtype) @ w_o


def _hier_moe(x, g, w_group, w_expert, w_gate, w_up, w_down):
    f32 = jnp.float32
    bsz, s, _ = x.shape
    hf = _rmsnorm(x, g).reshape(bsz * s, D_MODEL)
    lg = (hf @ w_group).astype(f32)
    pg = jax.nn.softmax(lg, axis=-1)
    _, gsel = lax.top_k(lg, 1)
    g_onehot = jax.nn.one_hot(gsel[:, 0], N_EXPERT_GROUPS, dtype=f32)
    pg_sel = jnp.sum(pg * g_onehot, axis=-1, keepdims=True)
    le = (hf @ w_expert).astype(f32).reshape(-1, N_EXPERT_GROUPS, EXPERTS_PER_GROUP)
    le_sel = jnp.einsum("tge,tg->te", le, g_onehot)
    top_v, top_i = lax.top_k(le_sel, TOP_K_IN_GROUP)
    gate_w = jax.nn.softmax(top_v, axis=-1) * pg_sel
    expert_id = gsel * EXPERTS_PER_GROUP + top_i
    combine = jnp.sum(jax.nn.one_hot(expert_id, N_EXPERTS, dtype=f32) * gate_w[..., None], axis=1)
    act = jax.nn.silu(jnp.einsum("td,edf->tef", hf, w_gate)) * jnp.einsum("td,edf->tef", hf, w_up)
    out = jnp.einsum("tef,efd->td", act * combine[..., None].astype(act.dtype), w_down)
    return out.reshape(bsz, s, D_MODEL).astype(x.dtype)


def _layer(x, mem_k, mem_v, s5_re, s5_im, lru_h, lru_conv, w):
    h = _rmsnorm(x, w["g_mix"])
    proj = h @ w["w_in"]
    u_s5 = proj[..., :D_S5]
    x_lru = proj[..., D_S5:D_S5 + D_LRU]
    g_lru = proj[..., D_S5 + D_LRU:]
    y_s5, s5_re_n, s5_im_n = _s5(u_s5, s5_re, s5_im, w["s5_lam_re"], w["s5_lam_im"], w["s5_log_dt"],
                                 w["s5_b_re"], w["s5_b_im"], w["s5_c_re"], w["s5_c_im"], w["s5_d"],
                                 w["s5_w_glu"], w["s5_b_glu"])
    y_lru, lru_h_n, lru_conv_n = _rglru(x_lru, g_lru, lru_h, lru_conv, w["lru_conv_w"], w["lru_conv_b"],
                                        w["lru_w_a"], w["lru_b_a"], w["lru_w_x"], w["lru_b_x"], w["lru_lam"])
    x = x + jnp.concatenate([y_s5, y_lru], axis=-1) @ w["w_out"]
    x = x + _cross_attend(x, mem_k, mem_v, w["g_xa"], w["xa_w_q"], w["xa_w_o"])
    x = x + _hier_moe(x, w["g_moe"], w["moe_w_group"], w["moe_w_expert"], w["moe_w_gate"], w["moe_w_up"], w["moe_w_down"])
    return x, s5_re_n, s5_im_n, lru_h_n, lru_conv_n


def setup_inputs(seed: int = 0) -> dict:
    key = jax.random.key(seed)
    ks = iter(jax.random.split(key, 64))
    f32 = jnp.float32
    L = DEPTH

    def nrm(shape, scale):
        return scale * jax.random.normal(next(ks), shape, f32)

    n_idx = jnp.arange(S5_STATE, dtype=f32)
    a_c = jax.random.uniform(next(ks), (L, D_LRU), f32, 0.9, 0.999)
    a_base = a_c ** (1.0 / LRU_C)
    return {
        "x_prompt": nrm((BATCH, SEQ, D_MODEL), 1.0),
        "x_sample": nrm((DEC_BATCH, DEC_SEQ, D_MODEL), 1.0),
        "mem_prompt": nrm((BATCH, N_MEM, D_MODEL), 1.0),
        "cache_mem_k": nrm((L, DEC_BATCH, N_MEM, XA_HEADS, XA_HEAD_DIM), 1.0),
        "cache_mem_v": nrm((L, DEC_BATCH, N_MEM, XA_HEADS, XA_HEAD_DIM), 1.0),
        "state_s5_re": nrm((L, DEC_BATCH, S5_GROUPS, S5_STATE), 0.3),
        "state_s5_im": nrm((L, DEC_BATCH, S5_GROUPS, S5_STATE), 0.3),
        "state_lru_h": nrm((L, DEC_BATCH, D_LRU), 0.5),
        "state_lru_conv": nrm((L, DEC_BATCH, CONV_W - 1, D_LRU), 1.0),
        "g_mix": 1.0 + nrm((L, D_MODEL), 0.02),
        "w_in": nrm((L, D_MODEL, D_S5 + 2 * D_LRU), D_MODEL ** -0.5),
        "s5_lam_re": -0.5 + nrm((L, S5_GROUPS, S5_STATE), 0.01),
        "s5_lam_im": math.pi * n_idx + nrm((L, S5_GROUPS, S5_STATE), 0.01),
        "s5_log_dt": jax.random.uniform(next(ks), (L, S5_GROUPS), f32, math.log(1e-3), math.log(1e-1)),
        "s5_b_re": nrm((L, S5_GROUPS, S5_STATE, S5_GROUP), (2 * S5_GROUP) ** -0.5),
        "s5_b_im": nrm((L, S5_GROUPS, S5_STATE, S5_GROUP), (2 * S5_GROUP) ** -0.5),
        "s5_c_re": nrm((L, S5_GROUPS, S5_GROUP, S5_STATE), (2 * S5_STATE) ** -0.5),
        "s5_c_im": nrm((L, S5_GROUPS, S5_GROUP, S5_STATE), (2 * S5_STATE) ** -0.5),
        "s5_d": nrm((L, S5_GROUPS, S5_GROUP), 1.0),
        "s5_w_glu": nrm((L, D_S5, D_S5), D_S5 ** -0.5),
        "s5_b_glu": nrm((L, D_S5), 0.02),
        "lru_conv_w": nrm((L, CONV_W, D_LRU), CONV_W ** -0.5),
        "lru_conv_b": nrm((L, D_LRU), 0.02),
        "lru_w_a": nrm((L, LRU_HEADS, LRU_HEAD_DIM, LRU_HEAD_DIM), LRU_HEAD_DIM ** -0.5),
        "lru_b_a": nrm((L, LRU_HEADS, LRU_HEAD_DIM), 0.02),
        "lru_w_x": nrm((L, LRU_HEADS, LRU_HEAD_DIM, LRU_HEAD_DIM), LRU_HEAD_DIM ** -0.5),
        "lru_b_x": nrm((L, LRU_HEADS, LRU_HEAD_DIM), 0.02),
        "lru_lam": jnp.log(a_base) - jnp.log1p(-a_base),
        "w_out": nrm((L, D_MIX, D_MODEL), D_MIX ** -0.5),
        "g_xa": 1.0 + nrm((L, D_MODEL), 0.02),
        "g_mem": 1.0 + nrm((L, D_MODEL), 0.02),
        "xa_w_q": nrm((L, D_MODEL, D_MODEL), D_MODEL ** -0.5),
        "xa_w_k": nrm((L, D_MODEL, D_MODEL), D_MODEL ** -0.5),
        "xa_w_v": nrm((L, D_MODEL, D_MODEL), D_MODEL ** -0.5),
        "xa_w_o": nrm((L, D_MODEL, D_MODEL), D_MODEL ** -0.5),
        "g_moe": 1.0 + nrm((L, D_MODEL), 0.02),
        "moe_w_group": nrm((L, D_MODEL, N_EXPERT_GROUPS), D_MODEL ** -0.5),
        "moe_w_expert": nrm((L, D_MODEL, N_EXPERTS), D_MODEL ** -0.5),
        "moe_w_gate": nrm((L, N_EXPERTS, D_MODEL, D_FF_EXPERT), D_MODEL ** -0.5),
        "moe_w_up": nrm((L, N_EXPERTS, D_MODEL, D_FF_EXPERT), D_MODEL ** -0.5),
        "moe_w_down": nrm((L, N_EXPERTS, D_FF_EXPERT, D_MODEL), D_FF_EXPERT ** -0.5),
        "g_final": 1.0 + nrm((D_MODEL,), 0.02),
    }


def reference(x_prompt, x_sample, mem_prompt, cache_mem_k, cache_mem_v, state_s5_re, state_s5_im,
              state_lru_h, state_lru_conv, g_mix, w_in, s5_lam_re, s5_lam_im, s5_log_dt, s5_b_re, s5_b_im,
              s5_c_re, s5_c_im, s5_d, s5_w_glu, s5_b_glu, lru_conv_w, lru_conv_b, lru_w_a, lru_b_a, lru_w_x,
              lru_b_x, lru_lam, w_out, g_xa, g_mem, xa_w_q, xa_w_k, xa_w_v, xa_w_o, g_moe, moe_w_group,
              moe_w_expert, moe_w_gate, moe_w_up, moe_w_down, g_final):
    f32 = jnp.float32
    xp = x_prompt
    xs = x_sample
    mk_p, mv_p = [], []
    s5r_p, s5i_p, lh_p, lc_p = [], [], [], []
    s5r_s, s5i_s, lh_s, lc_s = [], [], [], []
    for l in range(DEPTH):
        w = {
            "g_mix": g_mix[l], "w_in": w_in[l],
            "s5_lam_re": s5_lam_re[l], "s5_lam_im": s5_lam_im[l], "s5_log_dt": s5_log_dt[l],
            "s5_b_re": s5_b_re[l], "s5_b_im": s5_b_im[l], "s5_c_re": s5_c_re[l], "s5_c_im": s5_c_im[l],
            "s5_d": s5_d[l], "s5_w_glu": s5_w_glu[l], "s5_b_glu": s5_b_glu[l],
            "lru_conv_w": lru_conv_w[l], "lru_conv_b": lru_conv_b[l], "lru_w_a": lru_w_a[l],
            "lru_b_a": lru_b_a[l], "lru_w_x": lru_w_x[l], "lru_b_x": lru_b_x[l], "lru_lam": lru_lam[l],
            "w_out": w_out[l], "g_xa": g_xa[l], "xa_w_q": xa_w_q[l], "xa_w_o": xa_w_o[l],
            "g_moe": g_moe[l], "moe_w_group": moe_w_group[l], "moe_w_expert": moe_w_expert[l],
            "moe_w_gate": moe_w_gate[l], "moe_w_up": moe_w_up[l], "moe_w_down": moe_w_down[l],
        }
        mk, mv = _mem_kv(mem_prompt, g_mem[l], xa_w_k[l], xa_w_v[l])
        zr = jnp.zeros((BATCH, S5_GROUPS, S5_STATE), f32)
        zh = jnp.zeros((BATCH, D_LRU), f32)
        zc = jnp.zeros((BATCH, CONV_W - 1, D_LRU), xp.dtype)
        xp, a1, a2, a3, a4 = _layer(xp, mk, mv, zr, zr, zh, zc, w)
        mk_p.append(mk)
        mv_p.append(mv)
        s5r_p.append(a1)
        s5i_p.append(a2)
        lh_p.append(a3)
        lc_p.append(a4)
        xs, b1, b2, b3, b4 = _layer(xs, cache_mem_k[l], cache_mem_v[l], state_s5_re[l], state_s5_im[l],
                                    state_lru_h[l], state_lru_conv[l], w)
        s5r_s.append(b1)
        s5i_s.append(b2)
        lh_s.append(b3)
        lc_s.append(b4)
    y_prompt = _rmsnorm(xp, g_final)
    y_sample = _rmsnorm(xs, g_final)
    return (y_prompt, y_sample,
            jnp.stack(mk_p), jnp.stack(mv_p),
            jnp.stack(s5r_p), jnp.stack(s5i_p), jnp.stack(lh_p), jnp.stack(lc_p),
            jnp.stack(s5r_s), jnp.stack(s5i_s), jnp.stack(lh_s), jnp.stack(lc_s))
```

```python
import functools
import math

import jax
import jax.numpy as jnp
from jax import lax
from jax.experimental import pallas as pl
from jax.experimental.pallas import tpu as pltpu

F32 = jnp.float32
BF16 = jnp.bfloat16

D_MODEL = 1024
D_S5 = 512
D_LRU = 512
S5_GROUP = 16
S5_GROUPS = 32
S5_STATE = 64
LRU_HEADS = 8
LRU_HEAD_DIM = 64
CONV_W = 4
LRU_C = 8.0
N_MEM = 256
XA_HEADS = 4
XA_HEAD_DIM = 256
N_EXPERT_GROUPS = 4
EXPERTS_PER_GROUP = 8
N_EXPERTS = 32
D_FF_EXPERT = 256
EPS = 1e-6

SUBLANES = 8
LANES = 128
ROUTER_LANES = 128
VMEM_LIMIT = 48 * 1024 * 1024


def _params(*sem):
    return pltpu.CompilerParams(dimension_semantics=sem, vmem_limit_bytes=VMEM_LIMIT)


def _rms(x, g):
    ms = jnp.mean(x * x, axis=-1, keepdims=True)
    return x * lax.rsqrt(ms + EPS) * g


def _bdot(a, b):
    return jnp.dot(a, b, preferred_element_type=F32)


def _memkv_body(m_ref, g_ref, wk_ref, wv_ref, k_ref, v_ref):
    hb = _rms(m_ref[...], g_ref[...]).astype(BF16)
    k_ref[...] = _bdot(hb, wk_ref[...])
    v_ref[...] = _bdot(hb, wv_ref[...])


def _memkv(mem2d, g_mem, wk, wv, tm=512):
    t = mem2d.shape[0]
    row = pl.BlockSpec((tm, D_MODEL), lambda i: (i, 0))
    full = pl.BlockSpec((D_MODEL, D_MODEL), lambda i: (0, 0))
    return pl.pallas_call(
        _memkv_body,
        grid=(t // tm,),
        in_specs=[row, pl.BlockSpec((1, D_MODEL), lambda i: (0, 0)), full, full],
        out_specs=[row, row],
        out_shape=[jax.ShapeDtypeStruct((t, D_MODEL), F32)] * 2,
        compiler_params=_params("parallel"),
        name="memkv",
    )(mem2d, g_mem, wk, wv)


def _mixin_body(x_ref, g_ref, w_ref, u_ref, xl_ref, gl_ref):
    hb = _rms(x_ref[...], g_ref[...]).astype(BF16)
    proj = _bdot(hb, w_ref[...])
    u_ref[...] = proj[:, :D_S5]
    xl_ref[...] = proj[:, D_S5:D_S5 + D_LRU]
    gl_ref[...] = proj[:, D_S5 + D_LRU:]


def _mixin(x2d, g_mix, w_in, tm=512):
    t = x2d.shape[0]
    half = pl.BlockSpec((tm, D_S5), lambda i: (i, 0))
    return pl.pallas_call(
        _mixin_body,
        grid=(t // tm,),
        in_specs=[pl.BlockSpec((tm, D_MODEL), lambda i: (i, 0)),
                  pl.BlockSpec((1, D_MODEL), lambda i: (0, 0)),
                  pl.BlockSpec((D_MODEL, D_S5 + 2 * D_LRU), lambda i: (0, 0))],
        out_specs=[half, half, half],
        out_shape=[jax.ShapeDtypeStruct((t, D_S5), F32)] * 3,
        compiler_params=_params("parallel"),
        name="mixin",
    )(x2d, g_mix, w_in)


def _s5_body(z_ref, wa_ref, wh_ref, ap_ref, d_ref, h0_ref, y_ref, hfin_ref, e_scr, hp_scr, *, nc, rb, lc):
    z = z_ref[0]
    full = _bdot(z.astype(BF16), wa_ref[0])
    e_scr[...] = full[:, lc:]
    ar = ap_ref[0, 0:1, :]
    ai = ap_ref[0, 1:2, :]

    def step(c, carry):
        hr, hi = carry
        rows = pl.ds(pl.multiple_of(c * rb, rb), rb)
        hp_scr[rows, 0:LANES] = hr
        hp_scr[rows, LANES:2 * LANES] = hi
        er = e_scr[rows, 0:LANES]
        ei = e_scr[rows, LANES:2 * LANES]
        return ar * hr - ai * hi + er, ar * hi + ai * hr + ei

    h0 = h0_ref[0]
    hr, hi = lax.fori_loop(0, nc, step, (h0[:, :LANES], h0[:, LANES:]))
    hfin_ref[0, :, 0:LANES] = hr
    hfin_ref[0, :, LANES:2 * LANES] = hi
    y_ref[0] = full[:, :lc] + _bdot(hp_scr[...].astype(BF16), wh_ref[0]) + z * d_ref[0]


def _s5(z, wa, wh, apow, dt, h0, nc, rb):
    g, rows, lc = z.shape
    body = functools.partial(_s5_body, nc=nc, rb=rb, lc=lc)
    return pl.pallas_call(
        body,
        grid=(g,),
        in_specs=[pl.BlockSpec((1, rows, lc), lambda i: (i, 0, 0)),
                  pl.BlockSpec((1, lc, lc + 2 * LANES), lambda i: (i, 0, 0)),
                  pl.BlockSpec((1, 2 * LANES, lc), lambda i: (i, 0, 0)),
                  pl.BlockSpec((1, 2, LANES), lambda i: (i, 0, 0)),
                  pl.BlockSpec((1, 1, lc), lambda i: (i, 0, 0)),
                  pl.BlockSpec((1, rb, 2 * LANES), lambda i: (i, 0, 0))],
        out_specs=[pl.BlockSpec((1, rows, lc), lambda i: (i, 0, 0)),
                   pl.BlockSpec((1, rb, 2 * LANES), lambda i: (i, 0, 0))],
        out_shape=[jax.ShapeDtypeStruct((g, rows, lc), F32),
                   jax.ShapeDtypeStruct((g, rb, 2 * LANES), F32)],
        scratch_shapes=[pltpu.VMEM((rows, 2 * LANES), F32), pltpu.VMEM((rows, 2 * LANES), F32)],
        compiler_params=_params("parallel"),
        name="s5",
    )(z, wa, wh, apow, dt, h0)


def _s5_weights(chunk, lam_re, lam_im, log_dt, b_re, b_im, c_re, c_im, d_skip):
    hi = lax.Precision.HIGHEST
    dt = jnp.exp(log_dt)[:, None]
    mag = jnp.exp(lam_re * dt)
    ab_re = mag * jnp.cos(lam_im * dt)
    ab_im = mag * jnp.sin(lam_im * dt)
    den = lam_re * lam_re + lam_im * lam_im
    nr = ab_re - 1.0
    f_re = (nr * lam_re + ab_im * lam_im) / den
    f_im = (ab_im * lam_re - nr * lam_im) / den
    bb_re = f_re[..., None] * b_re - f_im[..., None] * b_im
    bb_im = f_re[..., None] * b_im + f_im[..., None] * b_re
    pr = [jnp.ones_like(ab_re)]
    pi = [jnp.zeros_like(ab_im)]
    for _ in range(chunk):
        pr.append(pr[-1] * ab_re - pi[-1] * ab_im)
        pi.append(pr[-2] * ab_im + pi[-1] * ab_re)
    pw_re = jnp.stack(pr)
    pw_im = jnp.stack(pi)
    pb_re = pw_re[:chunk, ..., None] * bb_re - pw_im[:chunk, ..., None] * bb_im
    pb_im = pw_re[:chunk, ..., None] * bb_im + pw_im[:chunk, ..., None] * bb_re
    kk = (jnp.einsum("gon,kgni->kgoi", c_re, pb_re, precision=hi)
          - jnp.einsum("gon,kgni->kgoi", c_im, pb_im, precision=hi))
    tin = jnp.arange(chunk)[:, None]
    tout = jnp.arange(chunk)[None, :]
    lag = tout - tin
    toep = jnp.where((lag >= 0)[:, :, None, None, None], kk[jnp.clip(lag, 0, chunk - 1)], 0.0)
    lc = chunk * S5_GROUP
    toep = toep.transpose(2, 0, 4, 1, 3).reshape(S5_GROUPS, lc, lc)
    ue_re = pb_re[::-1].transpose(1, 0, 3, 2).reshape(S5_GROUPS, lc, S5_STATE)
    ue_im = pb_im[::-1].transpose(1, 0, 3, 2).reshape(S5_GROUPS, lc, S5_STATE)
    zpad = jnp.zeros((S5_GROUPS, lc, LANES - S5_STATE), F32)
    wa = jnp.concatenate([toep, ue_re, zpad, ue_im, zpad], axis=-1)
    gr = c_re[None] * pw_re[1:, :, None, :] - c_im[None] * pw_im[1:, :, None, :]
    gi = c_re[None] * pw_im[1:, :, None, :] + c_im[None] * pw_re[1:, :, None, :]
    hy_re = gr.transpose(1, 3, 0, 2).reshape(S5_GROUPS, S5_STATE, lc)
    hy_im = -gi.transpose(1, 3, 0, 2).reshape(S5_GROUPS, S5_STATE, lc)
    zrow = jnp.zeros((S5_GROUPS, LANES - S5_STATE, lc), F32)
    wh = jnp.concatenate([hy_re, zrow, hy_im, zrow], axis=1)
    lpad = jnp.zeros((S5_GROUPS, LANES - S5_STATE), F32)
    apow = jnp.stack([jnp.concatenate([pw_re[chunk], lpad], -1),
                      jnp.concatenate([pw_im[chunk], lpad], -1)], axis=1)
    dtile = jnp.tile(d_skip, (1, chunk))[:, None, :]
    return wa.astype(BF16), wh.astype(BF16), apow, dtile


def _s5_group(u2d, bsz, seq, chunk, h0_re, h0_im, weights):
    wa, wh, apow, dtile = weights
    nc = seq // chunk
    lc = chunk * S5_GROUP
    z = (u2d.reshape(bsz, nc, chunk, S5_GROUPS, S5_GROUP)
         .transpose(3, 1, 0, 2, 4).reshape(S5_GROUPS, nc * bsz, lc))
    pad = jnp.zeros((S5_GROUPS, bsz, LANES - S5_STATE), F32)
    h0 = jnp.concatenate([h0_re.transpose(1, 0, 2), pad, h0_im.transpose(1, 0, 2), pad], axis=-1)
    y, hfin = _s5(z, wa, wh, apow, dtile, h0, nc, bsz)
    y2d = (y.reshape(S5_GROUPS, nc, bsz, chunk, S5_GROUP)
           .transpose(2, 1, 3, 0, 4).reshape(bsz * seq, D_S5))
    hf_re = hfin[:, :, 0:S5_STATE].transpose(1, 0, 2)
    hf_im = hfin[:, :, LANES:LANES + S5_STATE].transpose(1, 0, 2)
    return y2d, hf_re, hf_im


def _lru_ab(xc, wg_ref, bg_ref, lam_ref):
    xb = xc.astype(BF16)
    half = D_LRU // 2
    g0 = _bdot(xb[:, :half], wg_ref[0]) + bg_ref[0]
    g1 = _bdot(xb[:, half:], wg_ref[1]) + bg_ref[1]
    r = jax.nn.sigmoid(jnp.concatenate([g0[:, :half], g1[:, :half]], axis=1))
    ig = jax.nn.sigmoid(jnp.concatenate([g0[:, half:], g1[:, half:]], axis=1))
    zl = -lam_ref[...]
    softplus = jnp.maximum(zl, 0.0) + jnp.log1p(jnp.exp(-jnp.abs(zl)))
    log_a = -LRU_C * r * softplus
    a = jnp.exp(log_a)
    b = jnp.sqrt(-jnp.tanh(log_a) * (a * a + 1.0)) * (ig * xc)
    return a, b


def _tile_scan(a, b):
    row = lax.broadcasted_iota(jnp.int32, a.shape, 0) % SUBLANES
    for d in (1, 2, 4):
        a_prev = pltpu.roll(a, d, axis=0)
        b_prev = pltpu.roll(b, d, axis=0)
        m = row >= d
        b = jnp.where(m, b + a * b_prev, b)
        a = jnp.where(m, a * a_prev, a)
    return a, b


def _lru_seq_body(xl_ref, gl_ref, cw_ref, cb_ref, wg_ref, bg_ref, lam_ref, y_ref, hl_ref,
                  xp_scr, a_scr, b_scr, h_scr, hc_scr, *, ts):
    ti = pl.program_id(1)

    @pl.when(ti == 0)
    def _():
        xp_scr[0:SUBLANES, :] = jnp.zeros((SUBLANES, D_LRU), F32)
        hc_scr[...] = jnp.zeros((SUBLANES, D_LRU), F32)

    xl = xl_ref[...]
    xp_scr[SUBLANES:SUBLANES + ts, :] = xl
    xc = cb_ref[...] + xl * cw_ref[CONV_W - 1:CONV_W, :]
    for j in range(1, CONV_W):
        xc = xc + xp_scr[SUBLANES - j:SUBLANES - j + ts, :] * cw_ref[CONV_W - 1 - j:CONV_W - j, :]
    xp_scr[0:SUBLANES, :] = xl[ts - SUBLANES:, :]
    a, b = _lru_ab(xc, wg_ref, bg_ref, lam_ref)
    a, b = _tile_scan(a, b)
    a_scr[...] = a
    b_scr[...] = b

    def step(i, hin):
        rows = pl.ds(pl.multiple_of(i * SUBLANES, SUBLANES), SUBLANES)
        h = b_scr[rows, :] + a_scr[rows, :] * hin
        h_scr[rows, :] = h
        return h[SUBLANES - 1:SUBLANES, :]

    hlast = lax.fori_loop(0, ts // SUBLANES, step, hc_scr[0:1, :], unroll=4)
    hc_scr[...] = jnp.broadcast_to(hlast, (SUBLANES, D_LRU))
    hl_ref[...] = hc_scr[...]
    y_ref[...] = h_scr[...] * jax.nn.gelu(gl_ref[...])


def _lru_dec_body(xl_ref, gl_ref, hist_ref, h0_ref, cw_ref, cb_ref, wg_ref, bg_ref, lam_ref, y_ref, hl_ref, *, tm):
    xl = xl_ref[...]
    hist = hist_ref[...]
    row = lax.broadcasted_iota(jnp.int32, xl.shape, 0) % SUBLANES
    xc = cb_ref[...] + xl * cw_ref[CONV_W - 1:CONV_W, :]
    for j in range(1, CONV_W):
        prev = jnp.where(row >= j, pltpu.roll(xl, j, axis=0), pltpu.roll(hist, tm - SUBLANES + j, axis=0))
        xc = xc + prev * cw_ref[CONV_W - 1 - j:CONV_W - j, :]
    a, b = _lru_ab(xc, wg_ref, bg_ref, lam_ref)
    a, b = _tile_scan(a, b)
    h = b + a * h0_ref[...]
    hl_ref[...] = h
    y_ref[...] = h * jax.nn.gelu(gl_ref[...])


def _lru_weights(w_a, w_x, b_a, b_x):
    eye = jnp.eye(LRU_HEADS, dtype=F32)
    bd_a = jnp.einsum("hij,hg->higj", w_a, eye).reshape(D_LRU, D_LRU)
    bd_x = jnp.einsum("hij,hg->higj", w_x, eye).reshape(D_LRU, D_LRU)
    half = D_LRU // 2
    wg = jnp.stack([jnp.concatenate([bd_a[:half, :half], bd_x[:half, :half]], axis=1),
                    jnp.concatenate([bd_a[half:, half:], bd_x[half:, half:]], axis=1)]).astype(BF16)
    ba = b_a.reshape(1, D_LRU)
    bx = b_x.reshape(1, D_LRU)
    bg = jnp.stack([jnp.concatenate([ba[:, :half], bx[:, :half]], axis=1),
                    jnp.concatenate([ba[:, half:], bx[:, half:]], axis=1)])
    return wg, bg


def _lru_common_specs(const):
    return [const((CONV_W, D_LRU)), const((1, D_LRU)), const((2, D_LRU // 2, D_LRU)),
            const((2, 1, D_LRU)), const((1, D_LRU))]


def _lru_seq(xl, gl, cw, cb, wg, bg, lam, bsz, seq, ts=512):
    nt = seq // ts
    blk = pl.BlockSpec((ts, D_LRU), lambda b, t: (b * nt + t, 0))

    def const(shape):
        return pl.BlockSpec(shape, lambda b, t: (0,) * len(shape))

    body = functools.partial(_lru_seq_body, ts=ts)
    return pl.pallas_call(
        body,
        grid=(bsz, nt),
        in_specs=[blk, blk] + _lru_common_specs(const),
        out_specs=[blk, pl.BlockSpec((SUBLANES, D_LRU), lambda b, t: (b, 0))],
        out_shape=[jax.ShapeDtypeStruct((bsz * seq, D_LRU), F32),
                   jax.ShapeDtypeStruct((bsz * SUBLANES, D_LRU), F32)],
        scratch_shapes=[pltpu.VMEM((ts + SUBLANES, D_LRU), F32), pltpu.VMEM((ts, D_LRU), F32),
                        pltpu.VMEM((ts, D_LRU), F32), pltpu.VMEM((ts, D_LRU), F32),
                        pltpu.VMEM((SUBLANES, D_LRU), F32)],
        compiler_params=_params("arbitrary", "arbitrary"),
        name="lru_seq",
    )(xl, gl, cw, cb, wg, bg, lam)


def _lru_dec(xl, gl, hist, h0rep, cw, cb, wg, bg, lam, tm=256):
    t = xl.shape[0]
    blk = pl.BlockSpec((tm, D_LRU), lambda i: (i, 0))

    def const(shape):
        return pl.BlockSpec(shape, lambda i: (0,) * len(shape))

    body = functools.partial(_lru_dec_body, tm=tm)
    return pl.pallas_call(
        body,
        grid=(t // tm,),
        in_specs=[blk, blk, blk, blk] + _lru_common_specs(const),
        out_specs=[blk, blk],
        out_shape=[jax.ShapeDtypeStruct((t, D_LRU), F32)] * 2,
        compiler_params=_params("parallel"),
        name="lru_dec",
    )(xl, gl, hist, h0rep, cw, cb, wg, bg, lam)


def _mixout_body(x_ref, ys_ref, yl_ref, wglu_ref, bglu_ref, wo_ref, gxa_ref, wq_ref, x1_ref, q_ref):
    ys = jax.nn.gelu(ys_ref[...])
    gate = jax.nn.sigmoid(_bdot(ys.astype(BF16), wglu_ref[...]) + bglu_ref[...])
    s5 = (ys * gate).astype(BF16)
    x1 = (x_ref[...] + _bdot(s5, wo_ref[0:D_S5, :]) + _bdot(yl_ref[...].astype(BF16), wo_ref[D_S5:, :]))
    x1_ref[...] = x1
    q_ref[...] = _bdot(_rms(x1, gxa_ref[...]).astype(BF16), wq_ref[...]).astype(BF16)


def _mixout(x2d, ys, yl, w_glu, b_glu, w_out, g_xa, w_q, tm=512):
    t = x2d.shape[0]
    row = pl.BlockSpec((tm, D_MODEL), lambda i: (i, 0))
    half = pl.BlockSpec((tm, D_S5), lambda i: (i, 0))

    def const(shape):
        return pl.BlockSpec(shape, lambda i: (0,) * len(shape))

    return pl.pallas_call(
        _mixout_body,
        grid=(t // tm,),
        in_specs=[row, half, half, const((D_S5, D_S5)), const((1, D_S5)), const((D_MODEL, D_MODEL)),
                  const((1, D_MODEL)), const((D_MODEL, D_MODEL))],
        out_specs=[row, row],
        out_shape=[jax.ShapeDtypeStruct((t, D_MODEL), F32), jax.ShapeDtypeStruct((t, D_MODEL), BF16)],
        compiler_params=_params("parallel"),
        name="mixout",
    )(x2d, ys, yl, w_glu, b_glu, w_out, g_xa, w_q)


def _attn_body(q_ref, x1_ref, k_ref, v_ref, wo_ref, x2_ref, *, nb):
    scale = XA_HEAD_DIM ** -0.5
    for i in range(nb):
        q = q_ref[i]
        kb = k_ref[i].astype(BF16)
        vb = v_ref[i].astype(BF16)
        heads = []
        for h in range(XA_HEADS):
            cols = slice(h * XA_HEAD_DIM, (h + 1) * XA_HEAD_DIM)
            sc = lax.dot_general(q[:, cols], kb[:, cols], (((1,), (1,)), ((), ())),
                                 preferred_element_type=F32) * scale
            sc = sc - jnp.max(sc, axis=-1, keepdims=True)
            p = jnp.exp(sc)
            p = p / jnp.sum(p, axis=-1, keepdims=True)
            heads.append(_bdot(p.astype(BF16), vb[:, cols]))
        o = jnp.concatenate(heads, axis=1).astype(BF16)
        x2_ref[i] = x1_ref[i] + _bdot(o, wo_ref[...])


def _attn(q3, x13, k3, v3, w_o, nb, tq):
    bsz, seq, _ = q3.shape
    blk = pl.BlockSpec((nb, tq, D_MODEL), lambda b, t: (b, t, 0))
    kv = pl.BlockSpec((nb, N_MEM, D_MODEL), lambda b, t: (b, 0, 0))
    body = functools.partial(_attn_body, nb=nb)
    return pl.pallas_call(
        body,
        grid=(bsz // nb, seq // tq),
        in_specs=[blk, blk, kv, kv, pl.BlockSpec((D_MODEL, D_MODEL), lambda b, t: (0, 0))],
        out_specs=blk,
        out_shape=jax.ShapeDtypeStruct((bsz, seq, D_MODEL), F32),
        compiler_params=_params("parallel", "parallel"),
        name="attn",
    )(q3, x13, k3, v3, w_o)


def _router(hm, wr_hi_ref, wr_lo_ref):
    a_hi = hm.astype(BF16)
    a_lo = (hm - a_hi.astype(F32)).astype(BF16)
    logits = _bdot(a_hi, wr_hi_ref[...]) + (_bdot(a_hi, wr_lo_ref[...]) + _bdot(a_lo, wr_hi_ref[...]))
    lane_i = lax.broadcasted_iota(jnp.int32, logits.shape, 1)
    lane = lane_i.astype(F32)
    neg = -jnp.inf
    big = float(ROUTER_LANES)
    is_g = lane_i < N_EXPERT_GROUPS
    glog = jnp.where(is_g, logits, neg)
    gmax = jnp.max(glog, axis=-1, keepdims=True)
    gsel = jnp.min(jnp.where(glog == gmax, lane, big), axis=-1, keepdims=True)
    pg_sel = 1.0 / jnp.sum(jnp.where(is_g, jnp.exp(logits - gmax), 0.0), axis=-1, keepdims=True)
    eidx = lane_i - N_EXPERT_GROUPS
    in_group = (eidx >= 0) & (eidx < N_EXPERTS) & ((eidx >> 3).astype(F32) == gsel)
    el = jnp.where(in_group, logits, neg)
    v1 = jnp.max(el, axis=-1, keepdims=True)
    i1 = jnp.min(jnp.where(el == v1, lane, big), axis=-1, keepdims=True)
    el2 = jnp.where(lane == i1, neg, el)
    v2 = jnp.max(el2, axis=-1, keepdims=True)
    i2 = jnp.min(jnp.where(el2 == v2, lane, big), axis=-1, keepdims=True)
    e2 = jnp.exp(v2 - v1)
    w1 = pg_sel / (1.0 + e2)
    w2 = pg_sel * e2 / (1.0 + e2)
    return jnp.where(lane == i1, w1, 0.0) + jnp.where(lane == i2, w2, 0.0)


def _moe_body(x2_ref, gm_ref, wr_hi_ref, wr_lo_ref, wg_ref, wu_ref, wd_ref, gf_ref, y_ref,
              hm_scr, comb_scr, acc_scr):
    e = pl.program_id(1)

    @pl.when(e == 0)
    def _():
        hm = _rms(x2_ref[...], gm_ref[...])
        hm_scr[...] = hm.astype(BF16)
        comb_scr[...] = _router(hm, wr_hi_ref, wr_lo_ref)
        acc_scr[...] = jnp.zeros_like(acc_scr)

    hb = hm_scr[...]
    comb = comb_scr[...]
    lane = lax.broadcasted_iota(jnp.int32, comb.shape, 1)
    ce = jnp.sum(jnp.where(lane == e + N_EXPERT_GROUPS, comb, 0.0), axis=-1, keepdims=True)
    act = jax.nn.silu(_bdot(hb, wg_ref[0])) * _bdot(hb, wu_ref[0])
    acc_scr[...] += _bdot((act * ce).astype(BF16), wd_ref[0])

    @pl.when(e == N_EXPERTS - 1)
    def _():
        y_ref[...] = _rms(x2_ref[...] + acc_scr[...], gf_ref[...])


def _moe(x2, g_moe, wr_hi, wr_lo, wg, wu, wd, g_final, tm=1024):
    t = x2.shape[0]
    row = pl.BlockSpec((tm, D_MODEL), lambda i, e: (i, 0))

    def const(shape):
        return pl.BlockSpec(shape, lambda i, e: (0,) * len(shape))

    return pl.pallas_call(
        _moe_body,
        grid=(t // tm, N_EXPERTS),
        in_specs=[row, const((1, D_MODEL)), const((D_MODEL, ROUTER_LANES)), const((D_MODEL, ROUTER_LANES)),
                  pl.BlockSpec((1, D_MODEL, D_FF_EXPERT), lambda i, e: (e, 0, 0)),
                  pl.BlockSpec((1, D_MODEL, D_FF_EXPERT), lambda i, e: (e, 0, 0)),
                  pl.BlockSpec((1, D_FF_EXPERT, D_MODEL), lambda i, e: (e, 0, 0)),
                  const((1, D_MODEL))],
        out_specs=row,
        out_shape=jax.ShapeDtypeStruct((t, D_MODEL), F32),
        scratch_shapes=[pltpu.VMEM((tm, D_MODEL), BF16), pltpu.VMEM((tm, ROUTER_LANES), F32),
                        pltpu.VMEM((tm, D_MODEL), F32)],
        compiler_params=_params("parallel", "arbitrary"),
        name="moe",
    )(x2, g_moe, wr_hi, wr_lo, wg, wu, wd, g_final)


def _layer(x3d, k3, v3, s5_h0, lru_h0, lru_conv, w, chunk, nb, tq):
    bsz, seq, _ = x3d.shape
    x2d = x3d.reshape(bsz * seq, D_MODEL)
    u, xl, gl = _mixin(x2d, w["g_mix"], w["w_in"])

    ys, s5_re, s5_im = _s5_group(u, bsz, seq, chunk, s5_h0[0], s5_h0[1], w["s5"][chunk])

    lru_w = (w["conv_w"], w["conv_b"], w["lru_wg"], w["lru_bg"], w["lru_lam"])
    if lru_h0 is None:
        yl, hl = _lru_seq(xl, gl, *lru_w, bsz, seq)
        lru_h = hl.reshape(bsz, SUBLANES, D_LRU)[:, 0]
    else:
        hist = jnp.pad(lru_conv, ((0, 0), (SUBLANES - (CONV_W - 1), 0), (0, 0))).reshape(bsz * seq, D_LRU)
        h0rep = jnp.repeat(lru_h0, seq, axis=0)
        yl, hall = _lru_dec(xl, gl, hist, h0rep, *lru_w)
        lru_h = hall.reshape(bsz, seq, D_LRU)[:, seq - 1]
    conv_new = xl.reshape(bsz, seq, D_LRU)[:, seq - (CONV_W - 1):]

    x1, q = _mixout(x2d, ys, yl, w["w_glu"], w["b_glu"], w["w_out"], w["g_xa"], w["w_q"])
    x2 = _attn(q.reshape(bsz, seq, D_MODEL), x1.reshape(bsz, seq, D_MODEL), k3, v3, w["w_o"], nb, tq)
    y = _moe(x2.reshape(bsz * seq, D_MODEL), w["g_moe"], w["wr_hi"], w["wr_lo"], w["moe_wg"], w["moe_wu"],
             w["moe_wd"], w["g_final"])
    return y.reshape(bsz, seq, D_MODEL), s5_re, s5_im, lru_h, conv_new


def kernel(x_prompt, x_sample, mem_prompt, cache_mem_k, cache_mem_v, state_s5_re, state_s5_im, state_lru_h, state_lru_conv, g_mix, w_in, s5_lam_re, s5_lam_im, s5_log_dt, s5_b_re, s5_b_im, s5_c_re, s5_c_im, s5_d, s5_w_glu, s5_b_glu, lru_conv_w, lru_conv_b, lru_w_a, lru_b_a, lru_w_x, lru_b_x, lru_lam, w_out, g_xa, g_mem, xa_w_q, xa_w_k, xa_w_v, xa_w_o, g_moe, moe_w_group, moe_w_expert, moe_w_gate, moe_w_up, moe_w_down, g_final):
    depth = g_mix.shape[0]
    assert depth == 1, "single-layer step"
    l = 0
    bsz, seq, _ = x_prompt.shape
    dbsz, dseq, _ = x_sample.shape
    chunk_p, chunk_s = 16, dseq
    assert seq % chunk_p == 0 and dseq == SUBLANES

    s5_args = (s5_lam_re[l], s5_lam_im[l], s5_log_dt[l], s5_b_re[l], s5_b_im[l], s5_c_re[l], s5_c_im[l], s5_d[l])
    wg, bg = _lru_weights(lru_w_a[l], lru_w_x[l], lru_b_a[l], lru_b_x[l])
    wr = jnp.concatenate([moe_w_group[l], moe_w_expert[l],
                          jnp.zeros((D_MODEL, ROUTER_LANES - N_EXPERT_GROUPS - N_EXPERTS), F32)], axis=1)
    wr_hi = wr.astype(BF16)
    wr_lo = (wr - wr_hi.astype(F32)).astype(BF16)
    w = {
        "g_mix": g_mix[l][None], "w_in": w_in[l].astype(BF16),
        "s5": {c: _s5_weights(c, *s5_args) for c in {chunk_p, chunk_s}},
        "conv_w": lru_conv_w[l], "conv_b": lru_conv_b[l][None], "lru_wg": wg, "lru_bg": bg,
        "lru_lam": lru_lam[l][None],
        "w_glu": s5_w_glu[l].astype(BF16), "b_glu": s5_b_glu[l][None], "w_out": w_out[l].astype(BF16),
        "g_xa": g_xa[l][None], "w_q": xa_w_q[l].astype(BF16), "w_o": xa_w_o[l].astype(BF16),
        "g_moe": g_moe[l][None], "wr_hi": wr_hi, "wr_lo": wr_lo,
        "moe_wg": moe_w_gate[l].astype(BF16), "moe_wu": moe_w_up[l].astype(BF16),
        "moe_wd": moe_w_down[l].astype(BF16), "g_final": g_final[None],
    }

    mk, mv = _memkv(mem_prompt.reshape(bsz * N_MEM, D_MODEL), g_mem[l][None],
                    xa_w_k[l].astype(BF16), xa_w_v[l].astype(BF16))
    mk3 = mk.reshape(bsz, N_MEM, D_MODEL)
    mv3 = mv.reshape(bsz, N_MEM, D_MODEL)

    zeros_state = jnp.zeros((bsz, S5_GROUPS, S5_STATE), F32)
    yp, p_re, p_im, p_h, p_conv = _layer(x_prompt, mk3, mv3, (zeros_state, zeros_state), None, None, w,
                                         chunk_p, nb=1, tq=512)
    ck = cache_mem_k[l].reshape(dbsz, N_MEM, D_MODEL)
    cv = cache_mem_v[l].reshape(dbsz, N_MEM, D_MODEL)
    ysmp, s_re, s_im, s_h, s_conv = _layer(x_sample, ck, cv, (state_s5_re[l], state_s5_im[l]), state_lru_h[l],
                                           state_lru_conv[l], w, chunk_s, nb=4, tq=dseq)

    return (yp, ysmp,
            mk.reshape(1, bsz, N_MEM, XA_HEADS, XA_HEAD_DIM), mv.reshape(1, bsz, N_MEM, XA_HEADS, XA_HEAD_DIM),
            p_re[None], p_im[None], p_h[None], p_conv[None],
            s_re[None], s_im[None], s_h[None], s_conv[None])
```

```python
import functools
import math

import jax
import jax.numpy as jnp
from jax import lax
from jax.experimental import pallas as pl
from jax.experimental.pallas import tpu as pltpu

F32 = jnp.float32
BF16 = jnp.bfloat16

D_MODEL = 1024
D_S5 = 512
D_LRU = 512
S5_GROUP = 16
S5_GROUPS = 32
S5_STATE = 64
LRU_HEADS = 8
LRU_HEAD_DIM = 64
CONV_W = 4
LRU_C = 8.0
N_MEM = 256
XA_HEADS = 4
XA_HEAD_DIM = 256
N_EXPERT_GROUPS = 4
EXPERTS_PER_GROUP = 8
N_EXPERTS = 32
D_FF_EXPERT = 256
EPS = 1e-6

SUBLANES = 8
LANES = 128
ROUTER_LANES = 128
VMEM_LIMIT = 48 * 1024 * 1024


def _params(*sem):
    return pltpu.CompilerParams(dimension_semantics=sem, vmem_limit_bytes=VMEM_LIMIT)


def _rms(x, g):
    ms = jnp.mean(x * x, axis=-1, keepdims=True)
    return x * lax.rsqrt(ms + EPS) * g


def _bdot(a, b):
    return jnp.dot(a, b, preferred_element_type=F32)


def _memkv_body(m_ref, g_ref, wk_ref, wv_ref, k_ref, v_ref):
    hb = _rms(m_ref[...], g_ref[...]).astype(BF16)
    k_ref[...] = _bdot(hb, wk_ref[...])
    v_ref[...] = _bdot(hb, wv_ref[...])


def _memkv(mem2d, g_mem, wk, wv, tm=512):
    t = mem2d.shape[0]
    row = pl.BlockSpec((tm, D_MODEL), lambda i: (i, 0))
    full = pl.BlockSpec((D_MODEL, D_MODEL), lambda i: (0, 0))
    return pl.pallas_call(
        _memkv_body,
        grid=(t // tm,),
        in_specs=[row, pl.BlockSpec((1, D_MODEL), lambda i: (0, 0)), full, full],
        out_specs=[row, row],
        out_shape=[jax.ShapeDtypeStruct((t, D_MODEL), F32)] * 2,
        compiler_params=_params("parallel"),
        name="memkv",
    )(mem2d, g_mem, wk, wv)


def _mixin_body(x_ref, g_ref, w_ref, u_ref, xl_ref, gl_ref):
    hb = _rms(x_ref[...], g_ref[...]).astype(BF16)
    proj = _bdot(hb, w_ref[...])
    u_ref[...] = proj[:, :D_S5]
    xl_ref[...] = proj[:, D_S5:D_S5 + D_LRU]
    gl_ref[...] = proj[:, D_S5 + D_LRU:]


def _mixin(x2d, g_mix, w_in, tm=512):
    t = x2d.shape[0]
    half = pl.BlockSpec((tm, D_S5), lambda i: (i, 0))
    return pl.pallas_call(
        _mixin_body,
        grid=(t // tm,),
        in_specs=[pl.BlockSpec((tm, D_MODEL), lambda i: (i, 0)),
                  pl.BlockSpec((1, D_MODEL), lambda i: (0, 0)),
                  pl.BlockSpec((D_MODEL, D_S5 + 2 * D_LRU), lambda i: (0, 0))],
        out_specs=[half, half, half],
        out_shape=[jax.ShapeDtypeStruct((t, D_S5), F32)] * 3,
        compiler_params=_params("parallel"),
        name="mixin",
    )(x2d, g_mix, w_in)


def _s5_body(z_ref, wa_ref, wh_ref, ap_ref, d_ref, h0_ref, y_ref, hfin_ref, e_scr, hp_scr, *, nc, rb, lc):
    z = z_ref[0]
    full = _bdot(z.astype(BF16), wa_ref[0])
    e_scr[...] = full[:, lc:]
    ar = ap_ref[0, 0:1, :]
    ai = ap_ref[0, 1:2, :]

    def step(c, carry):
        hr, hi = carry
        rows = pl.ds(pl.multiple_of(c * rb, rb), rb)
        hp_scr[rows, 0:LANES] = hr
        hp_scr[rows, LANES:2 * LANES] = hi
        er = e_scr[rows, 0:LANES]
        ei = e_scr[rows, LANES:2 * LANES]
        return ar * hr - ai * hi + er, ar * hi + ai * hr + ei

    h0 = h0_ref[0]
    hr, hi = lax.fori_loop(0, nc, step, (h0[:, :LANES], h0[:, LANES:]))
    hfin_ref[0, :, 0:LANES] = hr
    hfin_ref[0, :, LANES:2 * LANES] = hi
    y_ref[0] = full[:, :lc] + _bdot(hp_scr[...].astype(BF16), wh_ref[0]) + z * d_ref[0]


def _s5(z, wa, wh, apow, dt, h0, nc, rb):
    g, rows, lc = z.shape
    body = functools.partial(_s5_body, nc=nc, rb=rb, lc=lc)
    return pl.pallas_call(
        body,
        grid=(g,),
        in_specs=[pl.BlockSpec((1, rows, lc), lambda i: (i, 0, 0)),
                  pl.BlockSpec((1, lc, lc + 2 * LANES), lambda i: (i, 0, 0)),
                  pl.BlockSpec((1, 2 * LANES, lc), lambda i: (i, 0, 0)),
                  pl.BlockSpec((1, 2, LANES), lambda i: (i, 0, 0)),
                  pl.BlockSpec((1, 1, lc), lambda i: (i, 0, 0)),
                  pl.BlockSpec((1, rb, 2 * LANES), lambda i: (i, 0, 0))],
        out_specs=[pl.BlockSpec((1, rows, lc), lambda i: (i, 0, 0)),
                   pl.BlockSpec((1, rb, 2 * LANES), lambda i: (i, 0, 0))],
        out_shape=[jax.ShapeDtypeStruct((g, rows, lc), F32),
                   jax.ShapeDtypeStruct((g, rb, 2 * LANES), F32)],
        scratch_shapes=[pltpu.VMEM((rows, 2 * LANES), F32), pltpu.VMEM((rows, 2 * LANES), F32)],
        compiler_params=_params("parallel"),
        name="s5",
    )(z, wa, wh, apow, dt, h0)


def _s5_weights(chunk, lam_re, lam_im, log_dt, b_re, b_im, c_re, c_im, d_skip):
    hi = lax.Precision.HIGHEST
    dt = jnp.exp(log_dt)[:, None]
    mag = jnp.exp(lam_re * dt)
    ab_re = mag * jnp.cos(lam_im * dt)
    ab_im = mag * jnp.sin(lam_im * dt)
    den = lam_re * lam_re + lam_im * lam_im
    nr = ab_re - 1.0
    f_re = (nr * lam_re + ab_im * lam_im) / den
    f_im = (ab_im * lam_re - nr * lam_im) / den
    bb_re = f_re[..., None] * b_re - f_im[..., None] * b_im
    bb_im = f_re[..., None] * b_im + f_im[..., None] * b_re
    pr = [jnp.ones_like(ab_re)]
    pi = [jnp.zeros_like(ab_im)]
    for _ in range(chunk):
        pr.append(pr[-1] * ab_re - pi[-1] * ab_im)
        pi.append(pr[-2] * ab_im + pi[-1] * ab_re)
    pw_re = jnp.stack(pr)
    pw_im = jnp.stack(pi)
    pb_re = pw_re[:chunk, ..., None] * bb_re - pw_im[:chunk, ..., None] * bb_im
    pb_im = pw_re[:chunk, ..., None] * bb_im + pw_im[:chunk, ..., None] * bb_re
    kk = (jnp.einsum("gon,kgni->kgoi", c_re, pb_re, precision=hi)
          - jnp.einsum("gon,kgni->kgoi", c_im, pb_im, precision=hi))
    tin = jnp.arange(chunk)[:, None]
    tout = jnp.arange(chunk)[None, :]
    lag = tout - tin
    toep = jnp.where((lag >= 0)[:, :, None, None, None], kk[jnp.clip(lag, 0, chunk - 1)], 0.0)
    lc = chunk * S5_GROUP
    toep = toep.transpose(2, 0, 4, 1, 3).reshape(S5_GROUPS, lc, lc)
    ue_re = pb_re[::-1].transpose(1, 0, 3, 2).reshape(S5_GROUPS, lc, S5_STATE)
    ue_im = pb_im[::-1].transpose(1, 0, 3, 2).reshape(S5_GROUPS, lc, S5_STATE)
    zpad = jnp.zeros((S5_GROUPS, lc, LANES - S5_STATE), F32)
    wa = jnp.concatenate([toep, ue_re, zpad, ue_im, zpad], axis=-1)
    gr = c_re[None] * pw_re[1:, :, None, :] - c_im[None] * pw_im[1:, :, None, :]
    gi = c_re[None] * pw_im[1:, :, None, :] + c_im[None] * pw_re[1:, :, None, :]
    hy_re = gr.transpose(1, 3, 0, 2).reshape(S5_GROUPS, S5_STATE, lc)
    hy_im = -gi.transpose(1, 3, 0, 2).reshape(S5_GROUPS, S5_STATE, lc)
    zrow = jnp.zeros((S5_GROUPS, LANES - S5_STATE, lc), F32)
    wh = jnp.concatenate([hy_re, zrow, hy_im, zrow], axis=1)
    lpad = jnp.zeros((S5_GROUPS, LANES - S5_STATE), F32)
    apow = jnp.stack([jnp.concatenate([pw_re[chunk], lpad], -1),
                      jnp.concatenate([pw_im[chunk], lpad], -1)], axis=1)
    dtile = jnp.tile(d_skip, (1, chunk))[:, None, :]
    return wa.astype(BF16), wh.astype(BF16), apow, dtile


def _s5_group(u2d, bsz, seq, chunk, h0_re, h0_im, weights):
    wa, wh, apow, dtile = weights
    nc = seq // chunk
    lc = chunk * S5_GROUP
    z = (u2d.reshape(bsz, nc, chunk, S5_GROUPS, S5_GROUP)
         .transpose(3, 1, 0, 2, 4).reshape(S5_GROUPS, nc * bsz, lc))
    pad = jnp.zeros((S5_GROUPS, bsz, LANES - S5_STATE), F32)
    h0 = jnp.concatenate([h0_re.transpose(1, 0, 2), pad, h0_im.transpose(1, 0, 2), pad], axis=-1)
    y, hfin = _s5(z, wa, wh, apow, dtile, h0, nc, bsz)
    y2d = (y.reshape(S5_GROUPS, nc, bsz, chunk, S5_GROUP)
           .transpose(2, 1, 3, 0, 4).reshape(bsz * seq, D_S5))
    hf_re = hfin[:, :, 0:S5_STATE].transpose(1, 0, 2)
    hf_im = hfin[:, :, LANES:LANES + S5_STATE].transpose(1, 0, 2)
    return y2d, hf_re, hf_im


def _lru_ab(xc, wg_ref, bg_ref, lam_ref):
    xb = xc.astype(BF16)
    half = D_LRU // 2
    g0 = _bdot(xb[:, :half], wg_ref[0]) + bg_ref[0]
    g1 = _bdot(xb[:, half:], wg_ref[1]) + bg_ref[1]
    r = jax.nn.sigmoid(jnp.concatenate([g0[:, :half], g1[:, :half]], axis=1))
    ig = jax.nn.sigmoid(jnp.concatenate([g0[:, half:], g1[:, half:]], axis=1))
    zl = -lam_ref[...]
    softplus = jnp.maximum(zl, 0.0) + jnp.log1p(jnp.exp(-jnp.abs(zl)))
    log_a = -LRU_C * r * softplus
    a = jnp.exp(log_a)
    b = jnp.sqrt(-jnp.tanh(log_a) * (a * a + 1.0)) * (ig * xc)
    return a, b


def _tile_scan(a, b):
    row = lax.broadcasted_iota(jnp.int32, a.shape, 0) % SUBLANES
    for d in (1, 2, 4):
        a_prev = pltpu.roll(a, d, axis=0)
        b_prev = pltpu.roll(b, d, axis=0)
        m = row >= d
        b = jnp.where(m, b + a * b_prev, b)
        a = jnp.where(m, a * a_prev, a)
    return a, b


def _lru_seq_body(xl_ref, gl_ref, cw_ref, cb_ref, wg_ref, bg_ref, lam_ref, y_ref, hl_ref,
                  xp_scr, a_scr, b_scr, h_scr, hc_scr, *, ts):
    ti = pl.program_id(1)

    @pl.when(ti == 0)
    def _():
        xp_scr[0:SUBLANES, :] = jnp.zeros((SUBLANES, D_LRU), F32)
        hc_scr[...] = jnp.zeros((SUBLANES, D_LRU), F32)

    xl = xl_ref[...]
    xp_scr[SUBLANES:SUBLANES + ts, :] = xl
    xc = cb_ref[...] + xl * cw_ref[CONV_W - 1:CONV_W, :]
    for j in range(1, CONV_W):
        xc = xc + xp_scr[SUBLANES - j:SUBLANES - j + ts, :] * cw_ref[CONV_W - 1 - j:CONV_W - j, :]
    xp_scr[0:SUBLANES, :] = xl[ts - SUBLANES:, :]
    a, b = _lru_ab(xc, wg_ref, bg_ref, lam_ref)
    a, b = _tile_scan(a, b)
    a_scr[...] = a
    b_scr[...] = b

    def step(i, hin):
        rows = pl.ds(pl.multiple_of(i * SUBLANES, SUBLANES), SUBLANES)
        h = b_scr[rows, :] + a_scr[rows, :] * hin
        h_scr[rows, :] = h
        return h[SUBLANES - 1:SUBLANES, :]

    hlast = lax.fori_loop(0, ts // SUBLANES, step, hc_scr[0:1, :], unroll=4)
    hc_scr[...] = jnp.broadcast_to(hlast, (SUBLANES, D_LRU))
    hl_ref[...] = hc_scr[...]
    y_ref[...] = h_scr[...] * jax.nn.gelu(gl_ref[...])


def _lru_dec_body(xl_ref, gl_ref, hist_ref, h0_ref, cw_ref, cb_ref, wg_ref, bg_ref, lam_ref, y_ref, hl_ref, *, tm):
    xl = xl_ref[...]
    hist = hist_ref[...]
    row = lax.broadcasted_iota(jnp.int32, xl.shape, 0) % SUBLANES
    xc = cb_ref[...] + xl * cw_ref[CONV_W - 1:CONV_W, :]
    for j in range(1, CONV_W):
        prev = jnp.where(row >= j, pltpu.roll(xl, j, axis=0), pltpu.roll(hist, tm - SUBLANES + j, axis=0))
        xc = xc + prev * cw_ref[CONV_W - 1 - j:CONV_W - j, :]
    a, b = _lru_ab(xc, wg_ref, bg_ref, lam_ref)
    a, b = _tile_scan(a, b)
    h = b + a * h0_ref[...]
    hl_ref[...] = h
    y_ref[...] = h * jax.nn.gelu(gl_ref[...])


def _lru_weights(w_a, w_x, b_a, b_x):
    eye = jnp.eye(LRU_HEADS, dtype=F32)
    bd_a = jnp.einsum("hij,hg->higj", w_a, eye).reshape(D_LRU, D_LRU)
    bd_x = jnp.einsum("hij,hg->higj", w_x, eye).reshape(D_LRU, D_LRU)
    half = D_LRU // 2
    wg = jnp.stack([jnp.concatenate([bd_a[:half, :half], bd_x[:half, :half]], axis=1),
                    jnp.concatenate([bd_a[half:, half:], bd_x[half:, half:]], axis=1)]).astype(BF16)
    ba = b_a.reshape(1, D_LRU)
    bx = b_x.reshape(1, D_LRU)
    bg = jnp.stack([jnp.concatenate([ba[:, :half], bx[:, :half]], axis=1),
                    jnp.concatenate([ba[:, half:], bx[:, half:]], axis=1)])
    return wg, bg


def _lru_common_specs(const):
    return [const((CONV_W, D_LRU)), const((1, D_LRU)), const((2, D_LRU // 2, D_LRU)),
            const((2, 1, D_LRU)), const((1, D_LRU))]


def _lru_seq(xl, gl, cw, cb, wg, bg, lam, bsz, seq, ts=512):
    nt = seq // ts
    blk = pl.BlockSpec((ts, D_LRU), lambda b, t: (b * nt + t, 0))

    def const(shape):
        return pl.BlockSpec(shape, lambda b, t: (0,) * len(shape))

    body = functools.partial(_lru_seq_body, ts=ts)
    return pl.pallas_call(
        body,
        grid=(bsz, nt),
        in_specs=[blk, blk] + _lru_common_specs(const),
        out_specs=[blk, pl.BlockSpec((SUBLANES, D_LRU), lambda b, t: (b, 0))],
        out_shape=[jax.ShapeDtypeStruct((bsz * seq, D_LRU), F32),
                   jax.ShapeDtypeStruct((bsz * SUBLANES, D_LRU), F32)],
        scratch_shapes=[pltpu.VMEM((ts + SUBLANES, D_LRU), F32), pltpu.VMEM((ts, D_LRU), F32),
                        pltpu.VMEM((ts, D_LRU), F32), pltpu.VMEM((ts, D_LRU), F32),
                        pltpu.VMEM((SUBLANES, D_LRU), F32)],
        compiler_params=_params("arbitrary", "arbitrary"),
        name="lru_seq",
    )(xl, gl, cw, cb, wg, bg, lam)


def _lru_dec(xl, gl, hist, h0rep, cw, cb, wg, bg, lam, tm=256):
    t = xl.shape[0]
    blk = pl.BlockSpec((tm, D_LRU), lambda i: (i, 0))

    def const(shape):
        return pl.BlockSpec(shape, lambda i: (0,) * len(shape))

    body = functools.partial(_lru_dec_body, tm=tm)
    return pl.pallas_call(
        body,
        grid=(t // tm,),
        in_specs=[blk, blk, blk, blk] + _lru_common_specs(const),
        out_specs=[blk, blk],
        out_shape=[jax.ShapeDtypeStruct((t, D_LRU), F32)] * 2,
        compiler_params=_params("parallel"),
        name="lru_dec",
    )(xl, gl, hist, h0rep, cw, cb, wg, bg, lam)


def _mixout_body(x_ref, ys_ref, yl_ref, wglu_ref, bglu_ref, wo_ref, gxa_ref, wq_ref, x1_ref, q_ref):
    ys = jax.nn.gelu(ys_ref[...])
    gate = jax.nn.sigmoid(_bdot(ys.astype(BF16), wglu_ref[...]) + bglu_ref[...])
    s5 = (ys * gate).astype(BF16)
    x1 = (x_ref[...] + _bdot(s5, wo_ref[0:D_S5, :]) + _bdot(yl_ref[...].astype(BF16), wo_ref[D_S5:, :]))
    x1_ref[...] = x1
    q_ref[...] = _bdot(_rms(x1, gxa_ref[...]).astype(BF16), wq_ref[...]).astype(BF16)


def _mixout(x2d, ys, yl, w_glu, b_glu, w_out, g_xa, w_q, tm=512):
    t = x2d.shape[0]
    row = pl.BlockSpec((tm, D_MODEL), lambda i: (i, 0))
    half = pl.BlockSpec((tm, D_S5), lambda i: (i, 0))

    def const(shape):
        return pl.BlockSpec(shape, lambda i: (0,) * len(shape))

    return pl.pallas_call(
        _mixout_body,
        grid=(t // tm,),
        in_specs=[row, half, half, const((D_S5, D_S5)), const((1, D_S5)), const((D_MODEL, D_MODEL)),
                  const((1, D_MODEL)), const((D_MODEL, D_MODEL))],
        out_specs=[row, row],
        out_shape=[jax.ShapeDtypeStruct((t, D_MODEL), F32), jax.ShapeDtypeStruct((t, D_MODEL), BF16)],
        compiler_params=_params("parallel"),
        name="mixout",
    )(x2d, ys, yl, w_glu, b_glu, w_out, g_xa, w_q)


def _attn_body(q_ref, x1_ref, k_ref, v_ref, wo_ref, x2_ref, *, nb):
    scale = XA_HEAD_DIM ** -0.5
    for i in range(nb):
        q = q_ref[i]
        kb = k_ref[i].astype(BF16)
        vb = v_ref[i].astype(BF16)
        heads = []
        for h in range(XA_HEADS):
            cols = slice(h * XA_HEAD_DIM, (h + 1) * XA_HEAD_DIM)
            sc = lax.dot_general(q[:, cols], kb[:, cols], (((1,), (1,)), ((), ())),
                                 preferred_element_type=F32) * scale
            sc = sc - jnp.max(sc, axis=-1, keepdims=True)
            p = jnp.exp(sc)
            p = p / jnp.sum(p, axis=-1, keepdims=True)
            heads.append(_bdot(p.astype(BF16), vb[:, cols]))
        o = jnp.concatenate(heads, axis=1).astype(BF16)
        x2_ref[i] = x1_ref[i] + _bdot(o, wo_ref[...])


def _attn(q3, x13, k3, v3, w_o, nb, tq):
    bsz, seq, _ = q3.shape
    blk = pl.BlockSpec((nb, tq, D_MODEL), lambda b, t: (b, t, 0))
    kv = pl.BlockSpec((nb, N_MEM, D_MODEL), lambda b, t: (b, 0, 0))
    body = functools.partial(_attn_body, nb=nb)
    return pl.pallas_call(
        body,
        grid=(bsz // nb, seq // tq),
        in_specs=[blk, blk, kv, kv, pl.BlockSpec((D_MODEL, D_MODEL), lambda b, t: (0, 0))],
        out_specs=blk,
        out_shape=jax.ShapeDtypeStruct((bsz, seq, D_MODEL), F32),
        compiler_params=_params("parallel", "parallel"),
        name="attn",
    )(q3, x13, k3, v3, w_o)


def _router(hm, wr_hi_ref, wr_lo_ref):
    a_hi = hm.astype(BF16)
    a_lo = (hm - a_hi.astype(F32)).astype(BF16)
    logits = _bdot(a_hi, wr_hi_ref[...]) + (_bdot(a_hi, wr_lo_ref[...]) + _bdot(a_lo, wr_hi_ref[...]))
    lane_i = lax.broadcasted_iota(jnp.int32, logits.shape, 1)
    lane = lane_i.astype(F32)
    neg = -jnp.inf
    big = float(ROUTER_LANES)
    is_g = lane_i < N_EXPERT_GROUPS
    glog = jnp.where(is_g, logits, neg)
    gmax = jnp.max(glog, axis=-1, keepdims=True)
    gsel = jnp.min(jnp.where(glog == gmax, lane, big), axis=-1, keepdims=True)
    pg_sel = 1.0 / jnp.sum(jnp.where(is_g, jnp.exp(logits - gmax), 0.0), axis=-1, keepdims=True)
    eidx = lane_i - N_EXPERT_GROUPS
    in_group = (eidx >= 0) & (eidx < N_EXPERTS) & ((eidx >> 3).astype(F32) == gsel)
    el = jnp.where(in_group, logits, neg)
    v1 = jnp.max(el, axis=-1, keepdims=True)
    i1 = jnp.min(jnp.where(el == v1, lane, big), axis=-1, keepdims=True)
    el2 = jnp.where(lane == i1, neg, el)
    v2 = jnp.max(el2, axis=-1, keepdims=True)
    i2 = jnp.min(jnp.where(el2 == v2, lane, big), axis=-1, keepdims=True)
    e2 = jnp.exp(v2 - v1)
    w1 = pg_sel / (1.0 + e2)
    w2 = pg_sel * e2 / (1.0 + e2)
    return i1, i2, w1, w2


def _route_body(x2_ref, gm_ref, wr_hi_ref, wr_lo_ref, info_ref):
    hm = _rms(x2_ref[...], gm_ref[...])
    i1, i2, w1, w2 = _router(hm, wr_hi_ref, wr_lo_ref)
    lane = lax.broadcasted_iota(jnp.int32, info_ref.shape, 1)
    info_ref[...] = jnp.where(lane == 0, i1, jnp.where(lane == 1, i2, jnp.where(lane == 2, w1,
                                                                               jnp.where(lane == 3, w2, 0.0))))


def _route(x2, g_moe, wr_hi, wr_lo, tm=512):
    t = x2.shape[0]

    def const(shape):
        return pl.BlockSpec(shape, lambda i: (0,) * len(shape))

    return pl.pallas_call(
        _route_body,
        grid=(t // tm,),
        in_specs=[pl.BlockSpec((tm, D_MODEL), lambda i: (i, 0)), const((1, D_MODEL)),
                  const((D_MODEL, ROUTER_LANES)), const((D_MODEL, ROUTER_LANES))],
        out_specs=pl.BlockSpec((tm, ROUTER_LANES), lambda i: (i, 0)),
        out_shape=jax.ShapeDtypeStruct((t, ROUTER_LANES), F32),
        compiler_params=_params("parallel"),
        name="route",
    )(x2, g_moe, wr_hi, wr_lo)


def _route_meta(info, tme):
    t = info.shape[0]
    flat_e = (info[:, 0:2].astype(jnp.int32) - N_EXPERT_GROUPS).reshape(-1)
    gate = info[:, 2:4].reshape(-1)
    onehot = (flat_e[:, None] == jnp.arange(N_EXPERTS, dtype=jnp.int32)[None, :]).astype(jnp.int32)
    csum = jnp.cumsum(onehot, axis=0)
    rank = jnp.take_along_axis(csum, flat_e[:, None], axis=1)[:, 0] - 1
    counts = csum[-1]
    padded = ((counts + tme - 1) // tme) * tme
    ends = jnp.cumsum(padded)
    slot = (ends - padded)[flat_e] + rank
    n_tiles = (2 * t + N_EXPERTS * (tme - 1)) // tme + 1
    src = jnp.zeros((n_tiles * tme,), jnp.int32).at[slot].set(jnp.arange(2 * t, dtype=jnp.int32) // 2)
    gsort = jnp.zeros((n_tiles * tme,), F32).at[slot].set(gate)
    tile_start = jnp.arange(n_tiles, dtype=jnp.int32) * tme
    te = jnp.minimum(jnp.searchsorted(ends, tile_start, side="right"), N_EXPERTS - 1).astype(jnp.int32)
    tv = (tile_start < ends[-1]).astype(jnp.int32)
    return slot.astype(jnp.int32), src, gsort[:, None], te, tv


def _row_copy(src_hbm, row, dst_vmem, r, sem):
    return pltpu.make_async_copy(src_hbm.at[pl.ds(row, 1), :], dst_vmem.at[pl.ds(r, 1), :], sem)


def _expert_body(te_ref, tv_ref, src_ref, x2_hbm, gate_ref, gm_ref, wg_ref, wu_ref, wd_ref, y_ref, xbuf, sem,
                 *, tme):
    i = pl.program_id(0)

    @pl.when(tv_ref[i] == 1)
    def _():
        def issue(r, c):
            _row_copy(x2_hbm, src_ref[r], xbuf, r, sem).start()
            return c

        lax.fori_loop(0, tme, issue, 0, unroll=8)

        def drain(r, c):
            _row_copy(x2_hbm, 0, xbuf, r, sem).wait()
            return c

        lax.fori_loop(0, tme, drain, 0, unroll=8)
        hb = _rms(xbuf[...], gm_ref[...]).astype(BF16)
        act = jax.nn.silu(_bdot(hb, wg_ref[0])) * _bdot(hb, wu_ref[0])
        y_ref[...] = _bdot((act * gate_ref[...]).astype(BF16), wd_ref[0])

    @pl.when(tv_ref[i] == 0)
    def _():
        y_ref[...] = jnp.zeros_like(y_ref)


def _experts(te, tv, src, x2, gsort, g_moe, wg, wu, wd, tme):
    n_tiles = te.shape[0]
    body = functools.partial(_expert_body, tme=tme)
    grid_spec = pltpu.PrefetchScalarGridSpec(
        num_scalar_prefetch=2,
        grid=(n_tiles,),
        in_specs=[pl.BlockSpec((tme,), lambda i, te, tv: (i,), memory_space=pltpu.SMEM),
                  pl.BlockSpec(memory_space=pl.ANY),
                  pl.BlockSpec((tme, 1), lambda i, te, tv: (i, 0)),
                  pl.BlockSpec((1, D_MODEL), lambda i, te, tv: (0, 0)),
                  pl.BlockSpec((1, D_MODEL, D_FF_EXPERT), lambda i, te, tv: (te[i], 0, 0)),
                  pl.BlockSpec((1, D_MODEL, D_FF_EXPERT), lambda i, te, tv: (te[i], 0, 0)),
                  pl.BlockSpec((1, D_FF_EXPERT, D_MODEL), lambda i, te, tv: (te[i], 0, 0))],
        out_specs=pl.BlockSpec((tme, D_MODEL), lambda i, te, tv: (i, 0)),
        scratch_shapes=[pltpu.VMEM((tme, D_MODEL), F32), pltpu.SemaphoreType.DMA(())],
    )
    return pl.pallas_call(
        body,
        grid_spec=grid_spec,
        out_shape=jax.ShapeDtypeStruct((n_tiles * tme, D_MODEL), F32),
        compiler_params=_params("arbitrary"),
        name="experts",
    )(te, tv, src, x2, gsort, g_moe, wg, wu, wd)


def _combine_body(slot_ref, x2_ref, ys_hbm, gf_ref, y_ref, b0, b1, sem, *, tm):
    def issue(r, c):
        _row_copy(ys_hbm, slot_ref[2 * r], b0, r, sem).start()
        _row_copy(ys_hbm, slot_ref[2 * r + 1], b1, r, sem).start()
        return c

    lax.fori_loop(0, tm, issue, 0, unroll=8)

    def drain(r, c):
        _row_copy(ys_hbm, 0, b0, r, sem).wait()
        _row_copy(ys_hbm, 0, b1, r, sem).wait()
        return c

    lax.fori_loop(0, tm, drain, 0, unroll=8)
    y_ref[...] = _rms(x2_ref[...] + (b0[...] + b1[...]), gf_ref[...])


def _combine(slot, x2, ysort, g_final, tm=256):
    t = x2.shape[0]
    body = functools.partial(_combine_body, tm=tm)
    return pl.pallas_call(
        body,
        grid=(t // tm,),
        in_specs=[pl.BlockSpec((2 * tm,), lambda i: (i,), memory_space=pltpu.SMEM),
                  pl.BlockSpec((tm, D_MODEL), lambda i: (i, 0)),
                  pl.BlockSpec(memory_space=pl.ANY),
                  pl.BlockSpec((1, D_MODEL), lambda i: (0, 0))],
        out_specs=pl.BlockSpec((tm, D_MODEL), lambda i: (i, 0)),
        out_shape=jax.ShapeDtypeStruct((t, D_MODEL), F32),
        scratch_shapes=[pltpu.VMEM((tm, D_MODEL), F32), pltpu.VMEM((tm, D_MODEL), F32),
                        pltpu.SemaphoreType.DMA(())],
        compiler_params=_params("arbitrary"),
        name="combine",
    )(slot, x2, ysort, g_final)


def _moe(x2, w, tme):
    info = _route(x2, w["g_moe"], w["wr_hi"], w["wr_lo"])
    slot, src, gsort, te, tv = _route_meta(info, tme)
    ysort = _experts(te, tv, src, x2, gsort, w["g_moe"], w["moe_wg"], w["moe_wu"], w["moe_wd"], tme)
    return _combine(slot, x2, ysort, w["g_final"])


def _layer(x3d, k3, v3, s5_h0, lru_h0, lru_conv, w, chunk, nb, tq, tme):
    bsz, seq, _ = x3d.shape
    x2d = x3d.reshape(bsz * seq, D_MODEL)
    u, xl, gl = _mixin(x2d, w["g_mix"], w["w_in"])

    ys, s5_re, s5_im = _s5_group(u, bsz, seq, chunk, s5_h0[0], s5_h0[1], w["s5"][chunk])

    lru_w = (w["conv_w"], w["conv_b"], w["lru_wg"], w["lru_bg"], w["lru_lam"])
    if lru_h0 is None:
        yl, hl = _lru_seq(xl, gl, *lru_w, bsz, seq)
        lru_h = hl.reshape(bsz, SUBLANES, D_LRU)[:, 0]
    else:
        hist = jnp.pad(lru_conv, ((0, 0), (SUBLANES - (CONV_W - 1), 0), (0, 0))).reshape(bsz * seq, D_LRU)
        h0rep = jnp.repeat(lru_h0, seq, axis=0)
        yl, hall = _lru_dec(xl, gl, hist, h0rep, *lru_w)
        lru_h = hall.reshape(bsz, seq, D_LRU)[:, seq - 1]
    conv_new = xl.reshape(bsz, seq, D_LRU)[:, seq - (CONV_W - 1):]

    x1, q = _mixout(x2d, ys, yl, w["w_glu"], w["b_glu"], w["w_out"], w["g_xa"], w["w_q"])
    x2 = _attn(q.reshape(bsz, seq, D_MODEL), x1.reshape(bsz, seq, D_MODEL), k3, v3, w["w_o"], nb, tq)
    y = _moe(x2.reshape(bsz * seq, D_MODEL), w, tme)
    return y.reshape(bsz, seq, D_MODEL), s5_re, s5_im, lru_h, conv_new


def kernel(x_prompt, x_sample, mem_prompt, cache_mem_k, cache_mem_v, state_s5_re, state_s5_im, state_lru_h, state_lru_conv, g_mix, w_in, s5_lam_re, s5_lam_im, s5_log_dt, s5_b_re, s5_b_im, s5_c_re, s5_c_im, s5_d, s5_w_glu, s5_b_glu, lru_conv_w, lru_conv_b, lru_w_a, lru_b_a, lru_w_x, lru_b_x, lru_lam, w_out, g_xa, g_mem, xa_w_q, xa_w_k, xa_w_v, xa_w_o, g_moe, moe_w_group, moe_w_expert, moe_w_gate, moe_w_up, moe_w_down, g_final):
    depth = g_mix.shape[0]
    assert depth == 1, "single-layer step"
    l = 0
    bsz, seq, _ = x_prompt.shape
    dbsz, dseq, _ = x_sample.shape
    chunk_p, chunk_s = 16, dseq
    assert seq % chunk_p == 0 and dseq == SUBLANES

    s5_args = (s5_lam_re[l], s5_lam_im[l], s5_log_dt[l], s5_b_re[l], s5_b_im[l], s5_c_re[l], s5_c_im[l], s5_d[l])
    wg, bg = _lru_weights(lru_w_a[l], lru_w_x[l], lru_b_a[l], lru_b_x[l])
    wr = jnp.concatenate([moe_w_group[l], moe_w_expert[l],
                          jnp.zeros((D_MODEL, ROUTER_LANES - N_EXPERT_GROUPS - N_EXPERTS), F32)], axis=1)
    wr_hi = wr.astype(BF16)
    wr_lo = (wr - wr_hi.astype(F32)).astype(BF16)
    w = {
        "g_mix": g_mix[l][None], "w_in": w_in[l].astype(BF16),
        "s5": {c: _s5_weights(c, *s5_args) for c in {chunk_p, chunk_s}},
        "conv_w": lru_conv_w[l], "conv_b": lru_conv_b[l][None], "lru_wg": wg, "lru_bg": bg,
        "lru_lam": lru_lam[l][None],
        "w_glu": s5_w_glu[l].astype(BF16), "b_glu": s5_b_glu[l][None], "w_out": w_out[l].astype(BF16),
        "g_xa": g_xa[l][None], "w_q": xa_w_q[l].astype(BF16), "w_o": xa_w_o[l].astype(BF16),
        "g_moe": g_moe[l][None], "wr_hi": wr_hi, "wr_lo": wr_lo,
        "moe_wg": moe_w_gate[l].astype(BF16), "moe_wu": moe_w_up[l].astype(BF16),
        "moe_wd": moe_w_down[l].astype(BF16), "g_final": g_final[None],
    }

    mk, mv = _memkv(mem_prompt.reshape(bsz * N_MEM, D_MODEL), g_mem[l][None],
                    xa_w_k[l].astype(BF16), xa_w_v[l].astype(BF16))
    mk3 = mk.reshape(bsz, N_MEM, D_MODEL)
    mv3 = mv.reshape(bsz, N_MEM, D_MODEL)

    zeros_state = jnp.zeros((bsz, S5_GROUPS, S5_STATE), F32)
    yp, p_re, p_im, p_h, p_conv = _layer(x_prompt, mk3, mv3, (zeros_state, zeros_state), None, None, w,
                                         chunk_p, nb=1, tq=512, tme=256)
    ck = cache_mem_k[l].reshape(dbsz, N_MEM, D_MODEL)
    cv = cache_mem_v[l].reshape(dbsz, N_MEM, D_MODEL)
    ysmp, s_re, s_im, s_h, s_conv = _layer(x_sample, ck, cv, (state_s5_re[l], state_s5_im[l]), state_lru_h[l],
                                           state_lru_conv[l], w, chunk_s, nb=4, tq=dseq, tme=128)

    return (yp, ysmp,
            mk.reshape(1, bsz, N_MEM, XA_HEADS, XA_HEAD_DIM), mv.reshape(1, bsz, N_MEM, XA_HEADS, XA_HEAD_DIM),
            p_re[None], p_im[None], p_h[None], p_conv[None],
            s_re[None], s_im[None], s_h[None], s_conv[None])
```

```python
import functools

import jax
import jax.numpy as jnp
import numpy as np
from jax import lax
from jax.experimental import pallas as pl
from jax.experimental.pallas import tpu as pltpu

F32 = jnp.float32
BF16 = jnp.bfloat16

D_MODEL = 1024
D_S5 = 512
D_LRU = 512
S5_GROUP = 16
S5_GROUPS = 32
S5_STATE = 64
LRU_HEADS = 8
LRU_HEAD_DIM = 64
CONV_W = 4
LRU_C = 8.0
N_MEM = 256
XA_HEADS = 4
XA_HEAD_DIM = 256
N_EXPERT_GROUPS = 4
EXPERTS_PER_GROUP = 8
N_EXPERTS = 32
D_FF_EXPERT = 256
EPS = 1e-6

SUBLANES = 8
LANES = 128
ROUTER_LANES = 128
VMEM_LIMIT = 48 * 1024 * 1024


def _params(*sem):
    return pltpu.CompilerParams(dimension_semantics=sem, vmem_limit_bytes=VMEM_LIMIT)


def _rms(x, g):
    ms = jnp.mean(x * x, axis=-1, keepdims=True)
    return x * lax.rsqrt(ms + EPS) * g


def _bdot(a, b):
    return jnp.dot(a, b, preferred_element_type=F32)


def _memkv_body(m_ref, g_ref, wk_ref, wv_ref, k_ref, v_ref):
    hb = _rms(m_ref[...], g_ref[...]).astype(BF16)
    k_ref[...] = _bdot(hb, wk_ref[...])
    v_ref[...] = _bdot(hb, wv_ref[...])


def _memkv(mem2d, g_mem, wk, wv, tm=512):
    t = mem2d.shape[0]
    row = pl.BlockSpec((tm, D_MODEL), lambda i: (i, 0))
    full = pl.BlockSpec((D_MODEL, D_MODEL), lambda i: (0, 0))
    return pl.pallas_call(
        _memkv_body,
        grid=(t // tm,),
        in_specs=[row, pl.BlockSpec((1, D_MODEL), lambda i: (0, 0)), full, full],
        out_specs=[row, row],
        out_shape=[jax.ShapeDtypeStruct((t, D_MODEL), F32)] * 2,
        compiler_params=_params("parallel"),
        name="memkv",
    )(mem2d, g_mem, wk, wv)


def _mixin_body(x_ref, g_ref, w_ref, u_ref, xl_ref, gl_ref):
    hb = _rms(x_ref[...], g_ref[...]).astype(BF16)
    proj = _bdot(hb, w_ref[...])
    u_ref[...] = proj[:, :D_S5]
    xl_ref[...] = proj[:, D_S5:D_S5 + D_LRU]
    gl_ref[...] = proj[:, D_S5 + D_LRU:]


def _mixin(x2d, g_mix, w_in, tm=512):
    t = x2d.shape[0]
    half = pl.BlockSpec((tm, D_S5), lambda i: (i, 0))
    return pl.pallas_call(
        _mixin_body,
        grid=(t // tm,),
        in_specs=[pl.BlockSpec((tm, D_MODEL), lambda i: (i, 0)),
                  pl.BlockSpec((1, D_MODEL), lambda i: (0, 0)),
                  pl.BlockSpec((D_MODEL, D_S5 + 2 * D_LRU), lambda i: (0, 0))],
        out_specs=[half, half, half],
        out_shape=[jax.ShapeDtypeStruct((t, D_S5), F32)] * 3,
        compiler_params=_params("parallel"),
        name="mixin",
    )(x2d, g_mix, w_in)


GROUPS_PER_BLOCK = LANES // S5_GROUP
PAIRS_PER_BLOCK = GROUPS_PER_BLOCK // 2
STATE_BLOCK = GROUPS_PER_BLOCK * S5_STATE
PW_ROWS = 24


def _s5_body(u_ref, wa_ref, wh_ref, pw_ref, d_ref, h0r_ref, h0i_ref, y_ref, hfr_ref, hfi_ref, hpr_scr, hpi_scr,
             *, chunk, nrow, scan):
    nh = chunk // SUBLANES
    lc = chunk * S5_GROUP
    slot = lax.broadcasted_iota(jnp.int32, (nrow, LANES), 1) // S5_GROUP
    in_slot = [slot == s for s in range(GROUPS_PER_BLOCK)]

    def pick(src, sel):
        out = src[sel(0)]
        for s in range(1, GROUPS_PER_BLOCK):
            out = jnp.where(in_slot[s], src[sel(s)], out)
        return out

    nat, skew = [], []
    for t in range(chunk):
        a = u_ref[pl.ds(t, nrow, stride=chunk), :]
        nat.append(a)
        s = t % GROUPS_PER_BLOCK
        skew.append(pltpu.roll(a, s * S5_GROUP, axis=1) if s else a)
    z = [[pick(skew, lambda s, hh=hh, m=m: SUBLANES * hh + (s - m) % GROUPS_PER_BLOCK) for hh in range(nh)]
         for m in range(GROUPS_PER_BLOCK)]

    ys, er, ei = [], [], []
    for p in range(PAIRS_PER_BLOCK):
        lhs = jnp.concatenate(z[2 * p] + z[2 * p + 1], axis=1).astype(BF16)
        full = _bdot(lhs, wa_ref[p])
        ys.append(full[:, :2 * lc])
        er.append(full[:, 2 * lc:2 * lc + LANES])
        ei.append(full[:, 2 * lc + LANES:])
    er = jnp.concatenate(er, axis=1)
    ei = jnp.concatenate(ei, axis=1)

    if scan:
        row = lax.broadcasted_iota(jnp.int32, er.shape, 0) % SUBLANES
        xr, xi = er, ei
        for k, d in enumerate((1, 2, 4)):
            pr, pi = pw_ref[0, k:k + 1, :], pw_ref[0, 4 + k:5 + k, :]
            sr, si = pltpu.roll(xr, d, axis=0), pltpu.roll(xi, d, axis=0)
            m = row >= d
            xr, xi = jnp.where(m, xr + pr * sr - pi * si, xr), jnp.where(m, xi + pr * si + pi * sr, xi)
        sxr = jnp.where(row >= 1, pltpu.roll(xr, 1, axis=0), 0.0)
        sxi = jnp.where(row >= 1, pltpu.roll(xi, 1, axis=0), 0.0)
        p8r, p8i = pw_ref[0, 3:4, :], pw_ref[0, 7:8, :]
        qr, qi = pw_ref[0, 8:16, :], pw_ref[0, 16:24, :]
        hr, hi = h0r_ref[0], h0i_ref[0]
        for t in range(nrow // SUBLANES):
            rows = slice(SUBLANES * t, SUBLANES * (t + 1))
            hpr_scr[rows, :] = sxr[rows] + qr * hr - qi * hi
            hpi_scr[rows, :] = sxi[rows] + qr * hi + qi * hr
            last = SUBLANES * (t + 1) - 1
            hr, hi = (xr[last:last + 1] + p8r * hr - p8i * hi, xi[last:last + 1] + p8r * hi + p8i * hr)
        hfr_ref[0] = hr
        hfi_ref[0] = hi
        hpr, hpi = hpr_scr[...], hpi_scr[...]
    else:
        hpr, hpi = h0r_ref[...], h0i_ref[...]
        p1r, p1i = pw_ref[0, 0:1, :], pw_ref[0, 4:5, :]
        hfr_ref[...] = er + p1r * hpr - p1i * hpi
        hfi_ref[...] = ei + p1r * hpi + p1i * hpr

    yg = []
    for p in range(PAIRS_PER_BLOCK):
        lanes = slice(LANES * p, LANES * (p + 1))
        hp = jnp.concatenate([hpr[:, lanes], hpi[:, lanes]], axis=1).astype(BF16)
        out = ys[p] + _bdot(hp, wh_ref[p])
        for half in range(2):
            yg.append([out[:, half * lc + hh * LANES:half * lc + (hh + 1) * LANES] for hh in range(nh)])
    d = d_ref[...]
    for t in range(chunk):
        hh, s = divmod(t, GROUPS_PER_BLOCK)
        c = pick([g[hh] for g in yg], lambda sl, s=s: (sl - s) % GROUPS_PER_BLOCK)
        if s:
            c = pltpu.roll(c, LANES - s * S5_GROUP, axis=1)
        y_ref[pl.ds(t, nrow, stride=chunk), :] = c + d * nat[t]


def _s5(u2d, weights, h0r, h0i, chunk, nrow, scan):
    wa, wh, pw, dvec = weights
    t = u2d.shape[0]
    lc = chunk * S5_GROUP
    rows = nrow * chunk
    nblk = D_S5 // LANES
    body = functools.partial(_s5_body, chunk=chunk, nrow=nrow, scan=scan)
    ublk = pl.BlockSpec((rows, LANES), lambda i, j: (i, j))
    if scan:
        hblk = pl.BlockSpec((1, 1, STATE_BLOCK), lambda i, j: (i, 0, j))
        hshape = jax.ShapeDtypeStruct((t // rows, 1, S5_GROUPS * S5_STATE), F32)
    else:
        hblk = pl.BlockSpec((nrow, STATE_BLOCK), lambda i, j: (i, j))
        hshape = jax.ShapeDtypeStruct((t // chunk, S5_GROUPS * S5_STATE), F32)
    return pl.pallas_call(
        body,
        grid=(t // rows, nblk),
        in_specs=[ublk,
                  pl.BlockSpec((PAIRS_PER_BLOCK, 2 * lc, 2 * lc + 2 * LANES), lambda i, j: (j, 0, 0)),
                  pl.BlockSpec((PAIRS_PER_BLOCK, 2 * LANES, 2 * lc), lambda i, j: (j, 0, 0)),
                  pl.BlockSpec((1, PW_ROWS, STATE_BLOCK), lambda i, j: (j, 0, 0)),
                  pl.BlockSpec((1, LANES), lambda i, j: (0, j)),
                  hblk, hblk],
        out_specs=[ublk, hblk, hblk],
        out_shape=[jax.ShapeDtypeStruct((t, D_S5), F32), hshape, hshape],
        scratch_shapes=[pltpu.VMEM((nrow, STATE_BLOCK), F32), pltpu.VMEM((nrow, STATE_BLOCK), F32)],
        compiler_params=_params("parallel", "parallel"),
        name="s5",
    )(u2d, wa, wh, pw, dvec, h0r, h0i)


def _s5_weights(chunk, lam_re, lam_im, log_dt, b_re, b_im, c_re, c_im, d_skip):
    hi = lax.Precision.HIGHEST
    dt = jnp.exp(log_dt)[:, None]
    mag = jnp.exp(lam_re * dt)
    ab_re = mag * jnp.cos(lam_im * dt)
    ab_im = mag * jnp.sin(lam_im * dt)
    den = lam_re * lam_re + lam_im * lam_im
    nr = ab_re - 1.0
    f_re = (nr * lam_re + ab_im * lam_im) / den
    f_im = (ab_im * lam_re - nr * lam_im) / den
    bb_re = f_re[..., None] * b_re - f_im[..., None] * b_im
    bb_im = f_re[..., None] * b_im + f_im[..., None] * b_re
    pr = [jnp.ones_like(ab_re)]
    pi = [jnp.zeros_like(ab_im)]
    for _ in range(chunk):
        pr.append(pr[-1] * ab_re - pi[-1] * ab_im)
        pi.append(pr[-2] * ab_im + pi[-1] * ab_re)
    pw_re = jnp.stack(pr)
    pw_im = jnp.stack(pi)
    pb_re = pw_re[:chunk, ..., None] * bb_re - pw_im[:chunk, ..., None] * bb_im
    pb_im = pw_re[:chunk, ..., None] * bb_im + pw_im[:chunk, ..., None] * bb_re
    kk = (jnp.einsum("gon,kgni->kgoi", c_re, pb_re, precision=hi)
          - jnp.einsum("gon,kgni->kgoi", c_im, pb_im, precision=hi))
    tin = jnp.arange(chunk)[:, None]
    tout = jnp.arange(chunk)[None, :]
    lag = tout - tin
    toep = jnp.where((lag >= 0)[:, :, None, None, None], kk[jnp.clip(lag, 0, chunk - 1)], 0.0)
    lc = chunk * S5_GROUP
    toep = toep.transpose(2, 0, 4, 1, 3).reshape(S5_GROUPS, lc, lc)
    ue_re = pb_re[::-1].transpose(1, 0, 3, 2).reshape(S5_GROUPS, lc, S5_STATE)
    ue_im = pb_im[::-1].transpose(1, 0, 3, 2).reshape(S5_GROUPS, lc, S5_STATE)
    gr = c_re[None] * pw_re[1:, :, None, :] - c_im[None] * pw_im[1:, :, None, :]
    gi = c_re[None] * pw_im[1:, :, None, :] + c_im[None] * pw_re[1:, :, None, :]
    hy_re = gr.transpose(1, 3, 0, 2).reshape(S5_GROUPS, S5_STATE, lc)
    hy_im = -gi.transpose(1, 3, 0, 2).reshape(S5_GROUPS, S5_STATE, lc)
    perm = jnp.asarray(_skew_perm(chunk))
    toep = jnp.take_along_axis(jnp.take_along_axis(toep, perm[:, :, None], axis=1), perm[:, None, :], axis=2)
    ue_re = jnp.take_along_axis(ue_re, perm[:, :, None], axis=1)
    ue_im = jnp.take_along_axis(ue_im, perm[:, :, None], axis=1)
    hy_re = jnp.take_along_axis(hy_re, perm[:, None, :], axis=2)
    hy_im = jnp.take_along_axis(hy_im, perm[:, None, :], axis=2)
    ev, od = slice(0, None, 2), slice(1, None, 2)
    npair = S5_GROUPS // 2
    zc = jnp.zeros((npair, lc, lc), F32)
    zs = jnp.zeros((npair, lc, S5_STATE), F32)
    wa = jnp.concatenate([jnp.concatenate([toep[ev], zc, ue_re[ev], zs, ue_im[ev], zs], axis=-1),
                          jnp.concatenate([zc, toep[od], zs, ue_re[od], zs, ue_im[od]], axis=-1)], axis=1)
    zr = jnp.zeros((npair, S5_STATE, lc), F32)
    wh = jnp.concatenate([jnp.concatenate([hy_re[ev], zr], axis=-1), jnp.concatenate([zr, hy_re[od]], axis=-1),
                          jnp.concatenate([hy_im[ev], zr], axis=-1), jnp.concatenate([zr, hy_im[od]], axis=-1)],
                         axis=1)
    qr = [jnp.ones_like(ab_re)]
    qi = [jnp.zeros_like(ab_im)]
    for _ in range(SUBLANES):
        qr.append(qr[-1] * pw_re[chunk] - qi[-1] * pw_im[chunk])
        qi.append(qr[-2] * pw_im[chunk] + qi[-1] * pw_re[chunk])
    rows = ([qr[k] for k in (1, 2, 4, 8)] + [qi[k] for k in (1, 2, 4, 8)] + qr[:SUBLANES] + qi[:SUBLANES])
    nblk = D_S5 // LANES
    pw = jnp.stack(rows).reshape(PW_ROWS, nblk, STATE_BLOCK).transpose(1, 0, 2)
    return wa.astype(BF16), wh.astype(BF16), pw, d_skip.reshape(1, D_S5)


def _skew_perm(chunk):
    hh, sl, c = np.meshgrid(np.arange(chunk // SUBLANES), np.arange(GROUPS_PER_BLOCK), np.arange(S5_GROUP),
                            indexing="ij")
    g = np.arange(S5_GROUPS)[:, None, None, None]
    t = SUBLANES * hh[None] + (sl[None] - g) % GROUPS_PER_BLOCK
    return (t * S5_GROUP + c[None]).reshape(S5_GROUPS, chunk * S5_GROUP).astype(np.int32)


def _lru_ab(xc, wg_ref, bg_ref, lam_ref):
    xb = xc.astype(BF16)
    half = D_LRU // 2
    g0 = _bdot(xb[:, :half], wg_ref[0]) + bg_ref[0]
    g1 = _bdot(xb[:, half:], wg_ref[1]) + bg_ref[1]
    r = jax.nn.sigmoid(jnp.concatenate([g0[:, :half], g1[:, :half]], axis=1))
    ig = jax.nn.sigmoid(jnp.concatenate([g0[:, half:], g1[:, half:]], axis=1))
    zl = -lam_ref[...]
    softplus = jnp.maximum(zl, 0.0) + jnp.log1p(jnp.exp(-jnp.abs(zl)))
    log_a = -LRU_C * r * softplus
    a = jnp.exp(log_a)
    b = jnp.sqrt(-jnp.tanh(log_a) * (a * a + 1.0)) * (ig * xc)
    return a, b


def _tile_scan(a, b):
    row = lax.broadcasted_iota(jnp.int32, a.shape, 0) % SUBLANES
    for d in (1, 2, 4):
        a_prev = pltpu.roll(a, d, axis=0)
        b_prev = pltpu.roll(b, d, axis=0)
        m = row >= d
        b = jnp.where(m, b + a * b_prev, b)
        a = jnp.where(m, a * a_prev, a)
    return a, b


def _lru_seq_body(xl_ref, gl_ref, cw_ref, cb_ref, wg_ref, bg_ref, lam_ref, y_ref, hl_ref,
                  xp_scr, a_scr, b_scr, h_scr, hc_scr, *, ts):
    ti = pl.program_id(1)

    @pl.when(ti == 0)
    def _():
        xp_scr[0:SUBLANES, :] = jnp.zeros((SUBLANES, D_LRU), F32)
        hc_scr[...] = jnp.zeros((SUBLANES, D_LRU), F32)

    xl = xl_ref[...]
    xp_scr[SUBLANES:SUBLANES + ts, :] = xl
    xc = cb_ref[...] + xl * cw_ref[CONV_W - 1:CONV_W, :]
    for j in range(1, CONV_W):
        xc = xc + xp_scr[SUBLANES - j:SUBLANES - j + ts, :] * cw_ref[CONV_W - 1 - j:CONV_W - j, :]
    xp_scr[0:SUBLANES, :] = xl[ts - SUBLANES:, :]
    a, b = _lru_ab(xc, wg_ref, bg_ref, lam_ref)
    a, b = _tile_scan(a, b)
    a_scr[...] = a
    b_scr[...] = b

    def step(i, hin):
        rows = pl.ds(pl.multiple_of(i * SUBLANES, SUBLANES), SUBLANES)
        h = b_scr[rows, :] + a_scr[rows, :] * hin
        h_scr[rows, :] = h
        return h[SUBLANES - 1:SUBLANES, :]

    hlast = lax.fori_loop(0, ts // SUBLANES, step, hc_scr[0:1, :], unroll=4)
    hc_scr[...] = jnp.broadcast_to(hlast, (SUBLANES, D_LRU))
    hl_ref[...] = hc_scr[...]
    y_ref[...] = h_scr[...] * jax.nn.gelu(gl_ref[...])


def _lru_dec_body(xl_ref, gl_ref, hist_ref, h0_ref, cw_ref, cb_ref, wg_ref, bg_ref, lam_ref, y_ref, hl_ref, *, tm):
    xl = xl_ref[...]
    hist = hist_ref[...]
    row = lax.broadcasted_iota(jnp.int32, xl.shape, 0) % SUBLANES
    xc = cb_ref[...] + xl * cw_ref[CONV_W - 1:CONV_W, :]
    for j in range(1, CONV_W):
        prev = jnp.where(row >= j, pltpu.roll(xl, j, axis=0), pltpu.roll(hist, tm - SUBLANES + j, axis=0))
        xc = xc + prev * cw_ref[CONV_W - 1 - j:CONV_W - j, :]
    a, b = _lru_ab(xc, wg_ref, bg_ref, lam_ref)
    a, b = _tile_scan(a, b)
    h = b + a * h0_ref[...]
    hl_ref[...] = h
    y_ref[...] = h * jax.nn.gelu(gl_ref[...])


def _lru_weights(w_a, w_x, b_a, b_x):
    eye = jnp.eye(LRU_HEADS, dtype=F32)
    bd_a = jnp.einsum("hij,hg->higj", w_a, eye).reshape(D_LRU, D_LRU)
    bd_x = jnp.einsum("hij,hg->higj", w_x, eye).reshape(D_LRU, D_LRU)
    half = D_LRU // 2
    wg = jnp.stack([jnp.concatenate([bd_a[:half, :half], bd_x[:half, :half]], axis=1),
                    jnp.concatenate([bd_a[half:, half:], bd_x[half:, half:]], axis=1)]).astype(BF16)
    ba = b_a.reshape(1, D_LRU)
    bx = b_x.reshape(1, D_LRU)
    bg = jnp.stack([jnp.concatenate([ba[:, :half], bx[:, :half]], axis=1),
                    jnp.concatenate([ba[:, half:], bx[:, half:]], axis=1)])
    return wg, bg


def _lru_common_specs(const):
    return [const((CONV_W, D_LRU)), const((1, D_LRU)), const((2, D_LRU // 2, D_LRU)),
            const((2, 1, D_LRU)), const((1, D_LRU))]


def _lru_seq(xl, gl, cw, cb, wg, bg, lam, bsz, seq, ts=512):
    nt = seq // ts
    blk = pl.BlockSpec((ts, D_LRU), lambda b, t: (b * nt + t, 0))

    def const(shape):
        return pl.BlockSpec(shape, lambda b, t: (0,) * len(shape))

    body = functools.partial(_lru_seq_body, ts=ts)
    return pl.pallas_call(
        body,
        grid=(bsz, nt),
        in_specs=[blk, blk] + _lru_common_specs(const),
        out_specs=[blk, pl.BlockSpec((SUBLANES, D_LRU), lambda b, t: (b, 0))],
        out_shape=[jax.ShapeDtypeStruct((bsz * seq, D_LRU), F32),
                   jax.ShapeDtypeStruct((bsz * SUBLANES, D_LRU), F32)],
        scratch_shapes=[pltpu.VMEM((ts + SUBLANES, D_LRU), F32), pltpu.VMEM((ts, D_LRU), F32),
                        pltpu.VMEM((ts, D_LRU), F32), pltpu.VMEM((ts, D_LRU), F32),
                        pltpu.VMEM((SUBLANES, D_LRU), F32)],
        compiler_params=_params("arbitrary", "arbitrary"),
        name="lru_seq",
    )(xl, gl, cw, cb, wg, bg, lam)


def _lru_dec(xl, gl, hist, h0rep, cw, cb, wg, bg, lam, tm=256):
    t = xl.shape[0]
    blk = pl.BlockSpec((tm, D_LRU), lambda i: (i, 0))

    def const(shape):
        return pl.BlockSpec(shape, lambda i: (0,) * len(shape))

    body = functools.partial(_lru_dec_body, tm=tm)
    return pl.pallas_call(
        body,
        grid=(t // tm,),
        in_specs=[blk, blk, blk, blk] + _lru_common_specs(const),
        out_specs=[blk, blk],
        out_shape=[jax.ShapeDtypeStruct((t, D_LRU), F32)] * 2,
        compiler_params=_params("parallel"),
        name="lru_dec",
    )(xl, gl, hist, h0rep, cw, cb, wg, bg, lam)


def _mixout_body(x_ref, ys_ref, yl_ref, wglu_ref, bglu_ref, wo_ref, gxa_ref, wq_ref, x1_ref, q_ref):
    ys = jax.nn.gelu(ys_ref[...])
    gate = jax.nn.sigmoid(_bdot(ys.astype(BF16), wglu_ref[...]) + bglu_ref[...])
    s5 = (ys * gate).astype(BF16)
    x1 = (x_ref[...] + _bdot(s5, wo_ref[0:D_S5, :]) + _bdot(yl_ref[...].astype(BF16), wo_ref[D_S5:, :]))
    x1_ref[...] = x1
    q_ref[...] = _bdot(_rms(x1, gxa_ref[...]).astype(BF16), wq_ref[...]).astype(BF16)


def _mixout(x2d, ys, yl, w_glu, b_glu, w_out, g_xa, w_q, tm=512):
    t = x2d.shape[0]
    row = pl.BlockSpec((tm, D_MODEL), lambda i: (i, 0))
    half = pl.BlockSpec((tm, D_S5), lambda i: (i, 0))

    def const(shape):
        return pl.BlockSpec(shape, lambda i: (0,) * len(shape))

    return pl.pallas_call(
        _mixout_body,
        grid=(t // tm,),
        in_specs=[row, half, half, const((D_S5, D_S5)), const((1, D_S5)), const((D_MODEL, D_MODEL)),
                  const((1, D_MODEL)), const((D_MODEL, D_MODEL))],
        out_specs=[row, row],
        out_shape=[jax.ShapeDtypeStruct((t, D_MODEL), F32), jax.ShapeDtypeStruct((t, D_MODEL), BF16)],
        compiler_params=_params("parallel"),
        name="mixout",
    )(x2d, ys, yl, w_glu, b_glu, w_out, g_xa, w_q)


def _attn_body(q_ref, x1_ref, k_ref, v_ref, wo_ref, x2_ref, *, nb):
    scale = XA_HEAD_DIM ** -0.5
    for i in range(nb):
        q = q_ref[i]
        kb = k_ref[i].astype(BF16)
        vb = v_ref[i].astype(BF16)
        heads = []
        for h in range(XA_HEADS):
            cols = slice(h * XA_HEAD_DIM, (h + 1) * XA_HEAD_DIM)
            sc = lax.dot_general(q[:, cols], kb[:, cols], (((1,), (1,)), ((), ())),
                                 preferred_element_type=F32) * scale
            sc = sc - jnp.max(sc, axis=-1, keepdims=True)
            p = jnp.exp(sc)
            p = p / jnp.sum(p, axis=-1, keepdims=True)
            heads.append(_bdot(p.astype(BF16), vb[:, cols]))
        o = jnp.concatenate(heads, axis=1).astype(BF16)
        x2_ref[i] = x1_ref[i] + _bdot(o, wo_ref[...])


def _attn(q3, x13, k3, v3, w_o, nb, tq):
    bsz, seq, _ = q3.shape
    blk = pl.BlockSpec((nb, tq, D_MODEL), lambda b, t: (b, t, 0))
    kv = pl.BlockSpec((nb, N_MEM, D_MODEL), lambda b, t: (b, 0, 0))
    body = functools.partial(_attn_body, nb=nb)
    return pl.pallas_call(
        body,
        grid=(bsz // nb, seq // tq),
        in_specs=[blk, blk, kv, kv, pl.BlockSpec((D_MODEL, D_MODEL), lambda b, t: (0, 0))],
        out_specs=blk,
        out_shape=jax.ShapeDtypeStruct((bsz, seq, D_MODEL), F32),
        compiler_params=_params("parallel", "parallel"),
        name="attn",
    )(q3, x13, k3, v3, w_o)


def _router(hm, wr_hi_ref, wr_lo_ref):
    a_hi = hm.astype(BF16)
    a_lo = (hm - a_hi.astype(F32)).astype(BF16)
    logits = _bdot(a_hi, wr_hi_ref[...]) + (_bdot(a_hi, wr_lo_ref[...]) + _bdot(a_lo, wr_hi_ref[...]))
    lane_i = lax.broadcasted_iota(jnp.int32, logits.shape, 1)
    lane = lane_i.astype(F32)
    neg = -jnp.inf
    big = float(ROUTER_LANES)
    is_g = lane_i < N_EXPERT_GROUPS
    glog = jnp.where(is_g, logits, neg)
    gmax = jnp.max(glog, axis=-1, keepdims=True)
    gsel = jnp.min(jnp.where(glog == gmax, lane, big), axis=-1, keepdims=True)
    pg_sel = 1.0 / jnp.sum(jnp.where(is_g, jnp.exp(logits - gmax), 0.0), axis=-1, keepdims=True)
    eidx = lane_i - N_EXPERT_GROUPS
    in_group = (eidx >= 0) & (eidx < N_EXPERTS) & ((eidx >> 3).astype(F32) == gsel)
    el = jnp.where(in_group, logits, neg)
    v1 = jnp.max(el, axis=-1, keepdims=True)
    i1 = jnp.min(jnp.where(el == v1, lane, big), axis=-1, keepdims=True)
    el2 = jnp.where(lane == i1, neg, el)
    v2 = jnp.max(el2, axis=-1, keepdims=True)
    i2 = jnp.min(jnp.where(el2 == v2, lane, big), axis=-1, keepdims=True)
    e2 = jnp.exp(v2 - v1)
    w1 = pg_sel / (1.0 + e2)
    w2 = pg_sel * e2 / (1.0 + e2)
    return i1, i2, w1, w2


def _route_body(x2_ref, gm_ref, wr_hi_ref, wr_lo_ref, info_ref):
    hm = _rms(x2_ref[...], gm_ref[...])
    i1, i2, w1, w2 = _router(hm, wr_hi_ref, wr_lo_ref)
    lane = lax.broadcasted_iota(jnp.int32, info_ref.shape, 1)
    info_ref[...] = jnp.where(lane == 0, i1, jnp.where(lane == 1, i2, jnp.where(lane == 2, w1,
                                                                               jnp.where(lane == 3, w2, 0.0))))


def _route(x2, g_moe, wr_hi, wr_lo, tm=512):
    t = x2.shape[0]

    def const(shape):
        return pl.BlockSpec(shape, lambda i: (0,) * len(shape))

    return pl.pallas_call(
        _route_body,
        grid=(t // tm,),
        in_specs=[pl.BlockSpec((tm, D_MODEL), lambda i: (i, 0)), const((1, D_MODEL)),
                  const((D_MODEL, ROUTER_LANES)), const((D_MODEL, ROUTER_LANES))],
        out_specs=pl.BlockSpec((tm, ROUTER_LANES), lambda i: (i, 0)),
        out_shape=jax.ShapeDtypeStruct((t, ROUTER_LANES), F32),
        compiler_params=_params("parallel"),
        name="route",
    )(x2, g_moe, wr_hi, wr_lo)


def _route_meta(info, tme):
    t = info.shape[0]
    flat_e = (info[:, 0:2].astype(jnp.int32) - N_EXPERT_GROUPS).reshape(-1)
    gate = info[:, 2:4].reshape(-1)
    onehot = (flat_e[:, None] == jnp.arange(N_EXPERTS, dtype=jnp.int32)[None, :]).astype(jnp.int32)
    csum = jnp.cumsum(onehot, axis=0)
    rank = jnp.take_along_axis(csum, flat_e[:, None], axis=1)[:, 0] - 1
    counts = csum[-1]
    padded = ((counts + tme - 1) // tme) * tme
    ends = jnp.cumsum(padded)
    slot = (ends - padded)[flat_e] + rank
    n_tiles = (2 * t + N_EXPERTS * (tme - 1)) // tme + 1
    src = jnp.zeros((n_tiles * tme,), jnp.int32).at[slot].set(jnp.arange(2 * t, dtype=jnp.int32) // 2)
    gsort = jnp.zeros((n_tiles * tme,), F32).at[slot].set(gate)
    tile_start = jnp.arange(n_tiles, dtype=jnp.int32) * tme
    te = jnp.minimum(jnp.sum((tile_start[:, None] >= ends[None, :]).astype(jnp.int32), axis=1), N_EXPERTS - 1)
    tv = (tile_start < ends[-1]).astype(jnp.int32)
    return slot.astype(jnp.int32), src, gsort[:, None], te, tv


def _row_copy(src_hbm, row, dst_vmem, r, sem):
    return pltpu.make_async_copy(src_hbm.at[pl.ds(row, 1), :], dst_vmem.at[pl.ds(r, 1), :], sem)


def _expert_body(te_ref, tv_ref, src_ref, x2_hbm, gate_ref, gm_ref, wg_ref, wu_ref, wd_ref, y_ref, xbuf, sem,
                 *, tme):
    i = pl.program_id(0)

    @pl.when(tv_ref[i] == 1)
    def _():
        def issue(r, c):
            _row_copy(x2_hbm, src_ref[r], xbuf, r, sem).start()
            return c

        lax.fori_loop(0, tme, issue, 0, unroll=8)

        def drain(r, c):
            _row_copy(x2_hbm, 0, xbuf, r, sem).wait()
            return c

        lax.fori_loop(0, tme, drain, 0, unroll=8)
        hb = _rms(xbuf[...], gm_ref[...]).astype(BF16)
        act = jax.nn.silu(_bdot(hb, wg_ref[0])) * _bdot(hb, wu_ref[0])
        y_ref[...] = _bdot((act * gate_ref[...]).astype(BF16), wd_ref[0])

    @pl.when(tv_ref[i] == 0)
    def _():
        y_ref[...] = jnp.zeros_like(y_ref)


def _experts(te, tv, src, x2, gsort, g_moe, wg, wu, wd, tme):
    n_tiles = te.shape[0]
    body = functools.partial(_expert_body, tme=tme)
    grid_spec = pltpu.PrefetchScalarGridSpec(
        num_scalar_prefetch=2,
        grid=(n_tiles,),
        in_specs=[pl.BlockSpec((tme,), lambda i, te, tv: (i,), memory_space=pltpu.SMEM),
                  pl.BlockSpec(memory_space=pl.ANY),
                  pl.BlockSpec((tme, 1), lambda i, te, tv: (i, 0)),
                  pl.BlockSpec((1, D_MODEL), lambda i, te, tv: (0, 0)),
                  pl.BlockSpec((1, D_MODEL, D_FF_EXPERT), lambda i, te, tv: (te[i], 0, 0)),
                  pl.BlockSpec((1, D_MODEL, D_FF_EXPERT), lambda i, te, tv: (te[i], 0, 0)),
                  pl.BlockSpec((1, D_FF_EXPERT, D_MODEL), lambda i, te, tv: (te[i], 0, 0))],
        out_specs=pl.BlockSpec((tme, D_MODEL), lambda i, te, tv: (i, 0)),
        scratch_shapes=[pltpu.VMEM((tme, D_MODEL), F32), pltpu.SemaphoreType.DMA(())],
    )
    return pl.pallas_call(
        body,
        grid_spec=grid_spec,
        out_shape=jax.ShapeDtypeStruct((n_tiles * tme, D_MODEL), F32),
        compiler_params=_params("arbitrary"),
        name="experts",
    )(te, tv, src, x2, gsort, g_moe, wg, wu, wd)


def _combine_body(slot_ref, x2_ref, ys_hbm, gf_ref, y_ref, b0, b1, sem, *, tm):
    def issue(r, c):
        _row_copy(ys_hbm, slot_ref[2 * r], b0, r, sem).start()
        _row_copy(ys_hbm, slot_ref[2 * r + 1], b1, r, sem).start()
        return c

    lax.fori_loop(0, tm, issue, 0, unroll=8)

    def drain(r, c):
        _row_copy(ys_hbm, 0, b0, r, sem).wait()
        _row_copy(ys_hbm, 0, b1, r, sem).wait()
        return c

    lax.fori_loop(0, tm, drain, 0, unroll=8)
    y_ref[...] = _rms(x2_ref[...] + (b0[...] + b1[...]), gf_ref[...])


def _combine(slot, x2, ysort, g_final, tm=256):
    t = x2.shape[0]
    body = functools.partial(_combine_body, tm=tm)
    return pl.pallas_call(
        body,
        grid=(t // tm,),
        in_specs=[pl.BlockSpec((2 * tm,), lambda i: (i,), memory_space=pltpu.SMEM),
                  pl.BlockSpec((tm, D_MODEL), lambda i: (i, 0)),
                  pl.BlockSpec(memory_space=pl.ANY),
                  pl.BlockSpec((1, D_MODEL), lambda i: (0, 0))],
        out_specs=pl.BlockSpec((tm, D_MODEL), lambda i: (i, 0)),
        out_shape=jax.ShapeDtypeStruct((t, D_MODEL), F32),
        scratch_shapes=[pltpu.VMEM((tm, D_MODEL), F32), pltpu.VMEM((tm, D_MODEL), F32),
                        pltpu.SemaphoreType.DMA(())],
        compiler_params=_params("arbitrary"),
        name="combine",
    )(slot, x2, ysort, g_final)


def _moe(x2, w, tme):
    info = _route(x2, w["g_moe"], w["wr_hi"], w["wr_lo"])
    slot, src, gsort, te, tv = _route_meta(info, tme)
    ysort = _experts(te, tv, src, x2, gsort, w["g_moe"], w["moe_wg"], w["moe_wu"], w["moe_wd"], tme)
    return _combine(slot, x2, ysort, w["g_final"])


def _layer(x3d, k3, v3, s5_h0, lru_h0, lru_conv, w, chunk, nb, tq, tme):
    bsz, seq, _ = x3d.shape
    x2d = x3d.reshape(bsz * seq, D_MODEL)
    u, xl, gl = _mixin(x2d, w["g_mix"], w["w_in"])

    nstate = S5_GROUPS * S5_STATE
    if s5_h0 is None:
        zero = jnp.zeros((bsz, 1, nstate), F32)
        ys, s5_re, s5_im = _s5(u, w["s5"][chunk], zero, zero, chunk, seq // chunk, True)
    else:
        ys, s5_re, s5_im = _s5(u, w["s5"][chunk], s5_h0[0].reshape(bsz, nstate), s5_h0[1].reshape(bsz, nstate),
                               chunk, bsz, False)
    s5_re = s5_re.reshape(bsz, S5_GROUPS, S5_STATE)
    s5_im = s5_im.reshape(bsz, S5_GROUPS, S5_STATE)

    lru_w = (w["conv_w"], w["conv_b"], w["lru_wg"], w["lru_bg"], w["lru_lam"])
    if lru_h0 is None:
        yl, hl = _lru_seq(xl, gl, *lru_w, bsz, seq)
        lru_h = hl.reshape(bsz, SUBLANES, D_LRU)[:, 0]
    else:
        hist = jnp.pad(lru_conv, ((0, 0), (SUBLANES - (CONV_W - 1), 0), (0, 0))).reshape(bsz * seq, D_LRU)
        h0rep = jnp.repeat(lru_h0, seq, axis=0)
        yl, hall = _lru_dec(xl, gl, hist, h0rep, *lru_w)
        lru_h = hall.reshape(bsz, seq, D_LRU)[:, seq - 1]
    conv_new = xl.reshape(bsz, seq, D_LRU)[:, seq - (CONV_W - 1):]

    x1, q = _mixout(x2d, ys, yl, w["w_glu"], w["b_glu"], w["w_out"], w["g_xa"], w["w_q"])
    x2 = _attn(q.reshape(bsz, seq, D_MODEL), x1.reshape(bsz, seq, D_MODEL), k3, v3, w["w_o"], nb, tq)
    y = _moe(x2.reshape(bsz * seq, D_MODEL), w, tme)
    return y.reshape(bsz, seq, D_MODEL), s5_re, s5_im, lru_h, conv_new


def kernel(x_prompt, x_sample, mem_prompt, cache_mem_k, cache_mem_v, state_s5_re, state_s5_im, state_lru_h, state_lru_conv, g_mix, w_in, s5_lam_re, s5_lam_im, s5_log_dt, s5_b_re, s5_b_im, s5_c_re, s5_c_im, s5_d, s5_w_glu, s5_b_glu, lru_conv_w, lru_conv_b, lru_w_a, lru_b_a, lru_w_x, lru_b_x, lru_lam, w_out, g_xa, g_mem, xa_w_q, xa_w_k, xa_w_v, xa_w_o, g_moe, moe_w_group, moe_w_expert, moe_w_gate, moe_w_up, moe_w_down, g_final):
    depth = g_mix.shape[0]
    assert depth == 1, "single-layer step"
    l = 0
    bsz, seq, _ = x_prompt.shape
    dbsz, dseq, _ = x_sample.shape
    chunk_p, chunk_s = 16, dseq
    assert seq % chunk_p == 0 and dseq == SUBLANES

    s5_args = (s5_lam_re[l], s5_lam_im[l], s5_log_dt[l], s5_b_re[l], s5_b_im[l], s5_c_re[l], s5_c_im[l], s5_d[l])
    wg, bg = _lru_weights(lru_w_a[l], lru_w_x[l], lru_b_a[l], lru_b_x[l])
    wr = jnp.concatenate([moe_w_group[l], moe_w_expert[l],
                          jnp.zeros((D_MODEL, ROUTER_LANES - N_EXPERT_GROUPS - N_EXPERTS), F32)], axis=1)
    wr_hi = wr.astype(BF16)
    wr_lo = (wr - wr_hi.astype(F32)).astype(BF16)
    w = {
        "g_mix": g_mix[l][None], "w_in": w_in[l].astype(BF16),
        "s5": {c: _s5_weights(c, *s5_args) for c in {chunk_p, chunk_s}},
        "conv_w": lru_conv_w[l], "conv_b": lru_conv_b[l][None], "lru_wg": wg, "lru_bg": bg,
        "lru_lam": lru_lam[l][None],
        "w_glu": s5_w_glu[l].astype(BF16), "b_glu": s5_b_glu[l][None], "w_out": w_out[l].astype(BF16),
        "g_xa": g_xa[l][None], "w_q": xa_w_q[l].astype(BF16), "w_o": xa_w_o[l].astype(BF16),
        "g_moe": g_moe[l][None], "wr_hi": wr_hi, "wr_lo": wr_lo,
        "moe_wg": moe_w_gate[l].astype(BF16), "moe_wu": moe_w_up[l].astype(BF16),
        "moe_wd": moe_w_down[l].astype(BF16), "g_final": g_final[None],
    }

    mk, mv = _memkv(mem_prompt.reshape(bsz * N_MEM, D_MODEL), g_mem[l][None],
                    xa_w_k[l].astype(BF16), xa_w_v[l].astype(BF16))
    mk3 = mk.reshape(bsz, N_MEM, D_MODEL)
    mv3 = mv.reshape(bsz, N_MEM, D_MODEL)

    yp, p_re, p_im, p_h, p_conv = _layer(x_prompt, mk3, mv3, None, None, None, w,
                                         chunk_p, nb=1, tq=512, tme=256)
    ck = cache_mem_k[l].reshape(dbsz, N_MEM, D_MODEL)
    cv = cache_mem_v[l].reshape(dbsz, N_MEM, D_MODEL)
    ysmp, s_re, s_im, s_h, s_conv = _layer(x_sample, ck, cv, (state_s5_re[l], state_s5_im[l]), state_lru_h[l],
                                           state_lru_conv[l], w, chunk_s, nb=4, tq=dseq, tme=128)

    return (yp, ysmp,
            mk.reshape(1, bsz, N_MEM, XA_HEADS, XA_HEAD_DIM), mv.reshape(1, bsz, N_MEM, XA_HEADS, XA_HEAD_DIM),
            p_re[None], p_im[None], p_h[None], p_conv[None],
            s_re[None], s_im[None], s_h[None], s_conv[None])
```

```python
import functools

import jax
import jax.numpy as jnp
import numpy as np
from jax import lax
from jax.experimental import pallas as pl
from jax.experimental.pallas import tpu as pltpu

F32 = jnp.float32
BF16 = jnp.bfloat16

D_MODEL = 1024
D_S5 = 512
D_LRU = 512
S5_GROUP = 16
S5_GROUPS = 32
S5_STATE = 64
LRU_HEADS = 8
LRU_HEAD_DIM = 64
CONV_W = 4
LRU_C = 8.0
N_MEM = 256
XA_HEADS = 4
XA_HEAD_DIM = 256
N_EXPERT_GROUPS = 4
EXPERTS_PER_GROUP = 8
N_EXPERTS = 32
D_FF_EXPERT = 256
EPS = 1e-6

SUBLANES = 8
LANES = 128
ROUTER_LANES = 128
VMEM_LIMIT = 48 * 1024 * 1024


def _params(*sem):
    return pltpu.CompilerParams(dimension_semantics=sem, vmem_limit_bytes=VMEM_LIMIT)


def _rms(x, g):
    ms = jnp.mean(x * x, axis=-1, keepdims=True)
    return x * lax.rsqrt(ms + EPS) * g


def _bdot(a, b):
    return jnp.dot(a, b, preferred_element_type=F32)


def _memkv_body(m_ref, g_ref, wk_ref, wv_ref, k_ref, v_ref):
    hb = _rms(m_ref[...], g_ref[...]).astype(BF16)
    k_ref[...] = _bdot(hb, wk_ref[...])
    v_ref[...] = _bdot(hb, wv_ref[...])


def _memkv(mem2d, g_mem, wk, wv, tm=512):
    t = mem2d.shape[0]
    row = pl.BlockSpec((tm, D_MODEL), lambda i: (i, 0))
    full = pl.BlockSpec((D_MODEL, D_MODEL), lambda i: (0, 0))
    return pl.pallas_call(
        _memkv_body,
        grid=(t // tm,),
        in_specs=[row, pl.BlockSpec((1, D_MODEL), lambda i: (0, 0)), full, full],
        out_specs=[row, row],
        out_shape=[jax.ShapeDtypeStruct((t, D_MODEL), F32)] * 2,
        compiler_params=_params("parallel"),
        name="memkv",
    )(mem2d, g_mem, wk, wv)


def _mixin_body(x_ref, g_ref, w_ref, u_ref, xl_ref, gl_ref):
    hb = _rms(x_ref[...], g_ref[...]).astype(BF16)
    proj = _bdot(hb, w_ref[...])
    u_ref[...] = proj[:, :D_S5]
    xl_ref[...] = proj[:, D_S5:D_S5 + D_LRU]
    gl_ref[...] = proj[:, D_S5 + D_LRU:]


def _mixin(x2d, g_mix, w_in, tm=512):
    t = x2d.shape[0]
    half = pl.BlockSpec((tm, D_S5), lambda i: (i, 0))
    return pl.pallas_call(
        _mixin_body,
        grid=(t // tm,),
        in_specs=[pl.BlockSpec((tm, D_MODEL), lambda i: (i, 0)),
                  pl.BlockSpec((1, D_MODEL), lambda i: (0, 0)),
                  pl.BlockSpec((D_MODEL, D_S5 + 2 * D_LRU), lambda i: (0, 0))],
        out_specs=[half, half, half],
        out_shape=[jax.ShapeDtypeStruct((t, D_S5), F32)] * 3,
        compiler_params=_params("parallel"),
        name="mixin",
    )(x2d, g_mix, w_in)


GROUPS_PER_BLOCK = LANES // S5_GROUP
PAIRS_PER_BLOCK = GROUPS_PER_BLOCK // 2
STATE_BLOCK = GROUPS_PER_BLOCK * S5_STATE
PW_ROWS = 24


def _s5_body(u_ref, wa_ref, wh_ref, pw_ref, d_ref, h0r_ref, h0i_ref, y_ref, hfr_ref, hfi_ref, hpr_scr, hpi_scr,
             *, chunk, nrow, scan):
    nh = chunk // SUBLANES
    lc = chunk * S5_GROUP
    slot = lax.broadcasted_iota(jnp.int32, (nrow, LANES), 1) // S5_GROUP
    in_slot = [slot == s for s in range(GROUPS_PER_BLOCK)]

    def pick(src, sel):
        out = src[sel(0)]
        for s in range(1, GROUPS_PER_BLOCK):
            out = jnp.where(in_slot[s], src[sel(s)], out)
        return out

    nat, skew = [], []
    for t in range(chunk):
        a = u_ref[pl.ds(t, nrow, stride=chunk), :]
        nat.append(a)
        s = t % GROUPS_PER_BLOCK
        skew.append(pltpu.roll(a, s * S5_GROUP, axis=1) if s else a)
    def lane_roll(x, slots):
        slots %= GROUPS_PER_BLOCK
        return pltpu.roll(x, slots * S5_GROUP, axis=1) if slots else x

    z = [[lane_roll(pick(skew, lambda s, hh=hh, m=m: SUBLANES * hh + (s - m) % GROUPS_PER_BLOCK), -m)
          for hh in range(nh)] for m in range(GROUPS_PER_BLOCK)]

    ys, er, ei = [], [], []
    for p in range(PAIRS_PER_BLOCK):
        lhs = jnp.concatenate(z[2 * p] + z[2 * p + 1], axis=1).astype(BF16)
        full = _bdot(lhs, wa_ref[p])
        ys.append(full[:, :2 * lc])
        er.append(full[:, 2 * lc:2 * lc + LANES])
        ei.append(full[:, 2 * lc + LANES:])
    er = jnp.concatenate(er, axis=1)
    ei = jnp.concatenate(ei, axis=1)

    if scan:
        row = lax.broadcasted_iota(jnp.int32, er.shape, 0) % SUBLANES
        xr, xi = er, ei
        for k, d in enumerate((1, 2, 4)):
            pr, pi = pw_ref[0, k:k + 1, :], pw_ref[0, 4 + k:5 + k, :]
            sr, si = pltpu.roll(xr, d, axis=0), pltpu.roll(xi, d, axis=0)
            m = row >= d
            xr, xi = jnp.where(m, xr + pr * sr - pi * si, xr), jnp.where(m, xi + pr * si + pi * sr, xi)
        sxr = jnp.where(row >= 1, pltpu.roll(xr, 1, axis=0), 0.0)
        sxi = jnp.where(row >= 1, pltpu.roll(xi, 1, axis=0), 0.0)
        p8r, p8i = pw_ref[0, 3:4, :], pw_ref[0, 7:8, :]
        qr, qi = pw_ref[0, 8:16, :], pw_ref[0, 16:24, :]
        hr, hi = h0r_ref[0], h0i_ref[0]
        for t in range(nrow // SUBLANES):
            rows = slice(SUBLANES * t, SUBLANES * (t + 1))
            hpr_scr[rows, :] = sxr[rows] + qr * hr - qi * hi
            hpi_scr[rows, :] = sxi[rows] + qr * hi + qi * hr
            last = SUBLANES * (t + 1) - 1
            hr, hi = (xr[last:last + 1] + p8r * hr - p8i * hi, xi[last:last + 1] + p8r * hi + p8i * hr)
        hfr_ref[0] = hr
        hfi_ref[0] = hi
        hpr, hpi = hpr_scr[...], hpi_scr[...]
    else:
        hpr, hpi = h0r_ref[...], h0i_ref[...]
        p1r, p1i = pw_ref[0, 0:1, :], pw_ref[0, 4:5, :]
        hfr_ref[...] = er + p1r * hpr - p1i * hpi
        hfi_ref[...] = ei + p1r * hpi + p1i * hpr

    yg = []
    for p in range(PAIRS_PER_BLOCK):
        lanes = slice(LANES * p, LANES * (p + 1))
        hp = jnp.concatenate([hpr[:, lanes], hpi[:, lanes]], axis=1).astype(BF16)
        out = ys[p] + _bdot(hp, wh_ref[p])
        for half in range(2):
            q = 2 * p + half
            yg.append([lane_roll(out[:, half * lc + hh * LANES:half * lc + (hh + 1) * LANES], q)
                       for hh in range(nh)])
    d = d_ref[...]
    for t in range(chunk):
        hh, s = divmod(t, GROUPS_PER_BLOCK)
        c = lane_roll(pick([g[hh] for g in yg], lambda sl, s=s: (sl - s) % GROUPS_PER_BLOCK), -s)
        y_ref[pl.ds(t, nrow, stride=chunk), :] = c + d * nat[t]


def _s5(u2d, weights, h0r, h0i, chunk, nrow, scan):
    wa, wh, pw, dvec = weights
    t = u2d.shape[0]
    lc = chunk * S5_GROUP
    rows = nrow * chunk
    nblk = D_S5 // LANES
    body = functools.partial(_s5_body, chunk=chunk, nrow=nrow, scan=scan)
    ublk = pl.BlockSpec((rows, LANES), lambda i, j: (i, j))
    if scan:
        hblk = pl.BlockSpec((1, 1, STATE_BLOCK), lambda i, j: (i, 0, j))
        hshape = jax.ShapeDtypeStruct((t // rows, 1, S5_GROUPS * S5_STATE), F32)
    else:
        hblk = pl.BlockSpec((nrow, STATE_BLOCK), lambda i, j: (i, j))
        hshape = jax.ShapeDtypeStruct((t // chunk, S5_GROUPS * S5_STATE), F32)
    return pl.pallas_call(
        body,
        grid=(t // rows, nblk),
        in_specs=[ublk,
                  pl.BlockSpec((PAIRS_PER_BLOCK, 2 * lc, 2 * lc + 2 * LANES), lambda i, j: (j, 0, 0)),
                  pl.BlockSpec((PAIRS_PER_BLOCK, 2 * LANES, 2 * lc), lambda i, j: (j, 0, 0)),
                  pl.BlockSpec((1, PW_ROWS, STATE_BLOCK), lambda i, j: (j, 0, 0)),
                  pl.BlockSpec((1, LANES), lambda i, j: (0, j)),
                  hblk, hblk],
        out_specs=[ublk, hblk, hblk],
        out_shape=[jax.ShapeDtypeStruct((t, D_S5), F32), hshape, hshape],
        scratch_shapes=[pltpu.VMEM((nrow, STATE_BLOCK), F32), pltpu.VMEM((nrow, STATE_BLOCK), F32)],
        compiler_params=_params("parallel", "parallel"),
        name="s5",
    )(u2d, wa, wh, pw, dvec, h0r, h0i)


def _s5_weights(chunk, lam_re, lam_im, log_dt, b_re, b_im, c_re, c_im, d_skip):
    hi = lax.Precision.HIGHEST
    dt = jnp.exp(log_dt)[:, None]
    mag = jnp.exp(lam_re * dt)
    ab_re = mag * jnp.cos(lam_im * dt)
    ab_im = mag * jnp.sin(lam_im * dt)
    den = lam_re * lam_re + lam_im * lam_im
    nr = ab_re - 1.0
    f_re = (nr * lam_re + ab_im * lam_im) / den
    f_im = (ab_im * lam_re - nr * lam_im) / den
    bb_re = f_re[..., None] * b_re - f_im[..., None] * b_im
    bb_im = f_re[..., None] * b_im + f_im[..., None] * b_re
    def powers(ks):
        k = jnp.asarray(np.asarray(ks, np.float32))[:, None, None]
        m = jnp.exp(k * (lam_re * dt))
        return m * jnp.cos(k * (lam_im * dt)), m * jnp.sin(k * (lam_im * dt))

    pw_re, pw_im = powers(range(chunk + 1))
    pb_re = pw_re[:chunk, ..., None] * bb_re - pw_im[:chunk, ..., None] * bb_im
    pb_im = pw_re[:chunk, ..., None] * bb_im + pw_im[:chunk, ..., None] * bb_re
    kk = (jnp.einsum("gon,kgni->kgoi", c_re, pb_re, precision=hi)
          - jnp.einsum("gon,kgni->kgoi", c_im, pb_im, precision=hi))
    toep = jnp.stack([jnp.concatenate([jnp.zeros((ti,) + kk.shape[1:], F32), kk[:chunk - ti]], axis=0)
                      for ti in range(chunk)])
    lc = chunk * S5_GROUP
    toep = toep.transpose(2, 0, 4, 1, 3).reshape(S5_GROUPS, lc, lc)
    ue_re = pb_re[::-1].transpose(1, 0, 3, 2).reshape(S5_GROUPS, lc, S5_STATE)
    ue_im = pb_im[::-1].transpose(1, 0, 3, 2).reshape(S5_GROUPS, lc, S5_STATE)
    gr = c_re[None] * pw_re[1:, :, None, :] - c_im[None] * pw_im[1:, :, None, :]
    gi = c_re[None] * pw_im[1:, :, None, :] + c_im[None] * pw_re[1:, :, None, :]
    hy_re = gr.transpose(1, 3, 0, 2).reshape(S5_GROUPS, S5_STATE, lc)
    hy_im = -gi.transpose(1, 3, 0, 2).reshape(S5_GROUPS, S5_STATE, lc)
    ev, od = slice(0, None, 2), slice(1, None, 2)
    npair = S5_GROUPS // 2
    zc = jnp.zeros((npair, lc, lc), F32)
    zs = jnp.zeros((npair, lc, S5_STATE), F32)
    wa = jnp.concatenate([jnp.concatenate([toep[ev], zc, ue_re[ev], zs, ue_im[ev], zs], axis=-1),
                          jnp.concatenate([zc, toep[od], zs, ue_re[od], zs, ue_im[od]], axis=-1)], axis=1)
    zr = jnp.zeros((npair, S5_STATE, lc), F32)
    wh = jnp.concatenate([jnp.concatenate([hy_re[ev], zr], axis=-1), jnp.concatenate([zr, hy_re[od]], axis=-1),
                          jnp.concatenate([hy_im[ev], zr], axis=-1), jnp.concatenate([zr, hy_im[od]], axis=-1)],
                         axis=1)
    tab = [1, 2, 4, 8] + list(range(SUBLANES))
    qr, qi = powers([chunk * k for k in tab])
    nblk = D_S5 // LANES
    pw = jnp.concatenate([qr[:4], qi[:4], qr[4:], qi[4:]]).reshape(PW_ROWS, nblk, STATE_BLOCK).transpose(1, 0, 2)
    return wa.astype(BF16), wh.astype(BF16), pw, d_skip.reshape(1, D_S5)


def _lru_ab(xc, wg_ref, bg_ref, lam_ref):
    xb = xc.astype(BF16)
    half = D_LRU // 2
    g0 = _bdot(xb[:, :half], wg_ref[0]) + bg_ref[0]
    g1 = _bdot(xb[:, half:], wg_ref[1]) + bg_ref[1]
    r = jax.nn.sigmoid(jnp.concatenate([g0[:, :half], g1[:, :half]], axis=1))
    ig = jax.nn.sigmoid(jnp.concatenate([g0[:, half:], g1[:, half:]], axis=1))
    zl = -lam_ref[...]
    softplus = jnp.maximum(zl, 0.0) + jnp.log1p(jnp.exp(-jnp.abs(zl)))
    log_a = -LRU_C * r * softplus
    a = jnp.exp(log_a)
    b = jnp.sqrt(-jnp.tanh(log_a) * (a * a + 1.0)) * (ig * xc)
    return a, b


def _tile_scan(a, b):
    row = lax.broadcasted_iota(jnp.int32, a.shape, 0) % SUBLANES
    for d in (1, 2, 4):
        a_prev = pltpu.roll(a, d, axis=0)
        b_prev = pltpu.roll(b, d, axis=0)
        m = row >= d
        b = jnp.where(m, b + a * b_prev, b)
        a = jnp.where(m, a * a_prev, a)
    return a, b


def _lru_seq_body(xl_ref, gl_ref, cw_ref, cb_ref, wg_ref, bg_ref, lam_ref, y_ref, hl_ref,
                  xp_scr, a_scr, b_scr, h_scr, hc_scr, *, ts):
    ti = pl.program_id(1)

    @pl.when(ti == 0)
    def _():
        xp_scr[0:SUBLANES, :] = jnp.zeros((SUBLANES, D_LRU), F32)
        hc_scr[...] = jnp.zeros((SUBLANES, D_LRU), F32)

    xl = xl_ref[...]
    xp_scr[SUBLANES:SUBLANES + ts, :] = xl
    xc = cb_ref[...] + xl * cw_ref[CONV_W - 1:CONV_W, :]
    for j in range(1, CONV_W):
        xc = xc + xp_scr[SUBLANES - j:SUBLANES - j + ts, :] * cw_ref[CONV_W - 1 - j:CONV_W - j, :]
    xp_scr[0:SUBLANES, :] = xl[ts - SUBLANES:, :]
    a, b = _lru_ab(xc, wg_ref, bg_ref, lam_ref)
    a, b = _tile_scan(a, b)
    a_scr[...] = a
    b_scr[...] = b

    def step(i, hin):
        rows = pl.ds(pl.multiple_of(i * SUBLANES, SUBLANES), SUBLANES)
        h = b_scr[rows, :] + a_scr[rows, :] * hin
        h_scr[rows, :] = h
        return h[SUBLANES - 1:SUBLANES, :]

    hlast = lax.fori_loop(0, ts // SUBLANES, step, hc_scr[0:1, :], unroll=4)
    hc_scr[...] = jnp.broadcast_to(hlast, (SUBLANES, D_LRU))
    hl_ref[...] = hc_scr[...]
    y_ref[...] = h_scr[...] * jax.nn.gelu(gl_ref[...])


def _lru_dec_body(xl_ref, gl_ref, hist_ref, h0_ref, cw_ref, cb_ref, wg_ref, bg_ref, lam_ref, y_ref, hl_ref, *, tm):
    xl = xl_ref[...]
    hist = hist_ref[...]
    row = lax.broadcasted_iota(jnp.int32, xl.shape, 0) % SUBLANES
    xc = cb_ref[...] + xl * cw_ref[CONV_W - 1:CONV_W, :]
    for j in range(1, CONV_W):
        prev = jnp.where(row >= j, pltpu.roll(xl, j, axis=0), pltpu.roll(hist, tm - SUBLANES + j, axis=0))
        xc = xc + prev * cw_ref[CONV_W - 1 - j:CONV_W - j, :]
    a, b = _lru_ab(xc, wg_ref, bg_ref, lam_ref)
    a, b = _tile_scan(a, b)
    h = b + a * h0_ref[...]
    hl_ref[...] = h
    y_ref[...] = h * jax.nn.gelu(gl_ref[...])


def _lru_weights(w_a, w_x, b_a, b_x):
    eye = jnp.eye(LRU_HEADS, dtype=F32)
    bd_a = jnp.einsum("hij,hg->higj", w_a, eye).reshape(D_LRU, D_LRU)
    bd_x = jnp.einsum("hij,hg->higj", w_x, eye).reshape(D_LRU, D_LRU)
    half = D_LRU // 2
    wg = jnp.stack([jnp.concatenate([bd_a[:half, :half], bd_x[:half, :half]], axis=1),
                    jnp.concatenate([bd_a[half:, half:], bd_x[half:, half:]], axis=1)]).astype(BF16)
    ba = b_a.reshape(1, D_LRU)
    bx = b_x.reshape(1, D_LRU)
    bg = jnp.stack([jnp.concatenate([ba[:, :half], bx[:, :half]], axis=1),
                    jnp.concatenate([ba[:, half:], bx[:, half:]], axis=1)])
    return wg, bg


def _lru_common_specs(const):
    return [const((CONV_W, D_LRU)), const((1, D_LRU)), const((2, D_LRU // 2, D_LRU)),
            const((2, 1, D_LRU)), const((1, D_LRU))]


def _lru_seq(xl, gl, cw, cb, wg, bg, lam, bsz, seq, ts=512):
    nt = seq // ts
    blk = pl.BlockSpec((ts, D_LRU), lambda b, t: (b * nt + t, 0))

    def const(shape):
        return pl.BlockSpec(shape, lambda b, t: (0,) * len(shape))

    body = functools.partial(_lru_seq_body, ts=ts)
    return pl.pallas_call(
        body,
        grid=(bsz, nt),
        in_specs=[blk, blk] + _lru_common_specs(const),
        out_specs=[blk, pl.BlockSpec((SUBLANES, D_LRU), lambda b, t: (b, 0))],
        out_shape=[jax.ShapeDtypeStruct((bsz * seq, D_LRU), F32),
                   jax.ShapeDtypeStruct((bsz * SUBLANES, D_LRU), F32)],
        scratch_shapes=[pltpu.VMEM((ts + SUBLANES, D_LRU), F32), pltpu.VMEM((ts, D_LRU), F32),
                        pltpu.VMEM((ts, D_LRU), F32), pltpu.VMEM((ts, D_LRU), F32),
                        pltpu.VMEM((SUBLANES, D_LRU), F32)],
        compiler_params=_params("arbitrary", "arbitrary"),
        name="lru_seq",
    )(xl, gl, cw, cb, wg, bg, lam)


def _lru_dec(xl, gl, hist, h0rep, cw, cb, wg, bg, lam, tm=256):
    t = xl.shape[0]
    blk = pl.BlockSpec((tm, D_LRU), lambda i: (i, 0))

    def const(shape):
        return pl.BlockSpec(shape, lambda i: (0,) * len(shape))

    body = functools.partial(_lru_dec_body, tm=tm)
    return pl.pallas_call(
        body,
        grid=(t // tm,),
        in_specs=[blk, blk, blk, blk] + _lru_common_specs(const),
        out_specs=[blk, blk],
        out_shape=[jax.ShapeDtypeStruct((t, D_LRU), F32)] * 2,
        compiler_params=_params("parallel"),
        name="lru_dec",
    )(xl, gl, hist, h0rep, cw, cb, wg, bg, lam)


def _mixout_body(x_ref, ys_ref, yl_ref, wglu_ref, bglu_ref, wo_ref, gxa_ref, wq_ref, x1_ref, q_ref):
    ys = jax.nn.gelu(ys_ref[...])
    gate = jax.nn.sigmoid(_bdot(ys.astype(BF16), wglu_ref[...]) + bglu_ref[...])
    s5 = (ys * gate).astype(BF16)
    x1 = (x_ref[...] + _bdot(s5, wo_ref[0:D_S5, :]) + _bdot(yl_ref[...].astype(BF16), wo_ref[D_S5:, :]))
    x1_ref[...] = x1
    q_ref[...] = _bdot(_rms(x1, gxa_ref[...]).astype(BF16), wq_ref[...]).astype(BF16)


def _mixout(x2d, ys, yl, w_glu, b_glu, w_out, g_xa, w_q, tm=512):
    t = x2d.shape[0]
    row = pl.BlockSpec((tm, D_MODEL), lambda i: (i, 0))
    half = pl.BlockSpec((tm, D_S5), lambda i: (i, 0))

    def const(shape):
        return pl.BlockSpec(shape, lambda i: (0,) * len(shape))

    return pl.pallas_call(
        _mixout_body,
        grid=(t // tm,),
        in_specs=[row, half, half, const((D_S5, D_S5)), const((1, D_S5)), const((D_MODEL, D_MODEL)),
                  const((1, D_MODEL)), const((D_MODEL, D_MODEL))],
        out_specs=[row, row],
        out_shape=[jax.ShapeDtypeStruct((t, D_MODEL), F32), jax.ShapeDtypeStruct((t, D_MODEL), BF16)],
        compiler_params=_params("parallel"),
        name="mixout",
    )(x2d, ys, yl, w_glu, b_glu, w_out, g_xa, w_q)


def _attn_body(q_ref, x1_ref, k_ref, v_ref, wo_ref, x2t_ref, *, nb, tq):
    scale = XA_HEAD_DIM ** -0.5
    for i in range(nb):
        q = q_ref[i]
        kb = k_ref[i].astype(BF16)
        vb = v_ref[i].astype(BF16)
        heads = []
        for h in range(XA_HEADS):
            cols = slice(h * XA_HEAD_DIM, (h + 1) * XA_HEAD_DIM)
            sc = lax.dot_general(q[:, cols], kb[:, cols], (((1,), (1,)), ((), ())),
                                 preferred_element_type=F32) * scale
            sc = sc - jnp.max(sc, axis=-1, keepdims=True)
            p = jnp.exp(sc)
            p = p / jnp.sum(p, axis=-1, keepdims=True)
            heads.append(_bdot(p.astype(BF16), vb[:, cols]))
        o = jnp.concatenate(heads, axis=1).astype(BF16)
        _rows_to_tiles(x2t_ref, x1_ref[i] + _bdot(o, wo_ref[...]), tq, base=i * tq * TILE_ROWS)


def _attn(q3, x13, k3, v3, w_o, nb, tq):
    bsz, seq, _ = q3.shape
    nt = seq // tq
    blk = pl.BlockSpec((nb, tq, D_MODEL), lambda b, t: (b, t, 0))
    kv = pl.BlockSpec((nb, N_MEM, D_MODEL), lambda b, t: (b, 0, 0))
    body = functools.partial(_attn_body, nb=nb, tq=tq)
    return pl.pallas_call(
        body,
        grid=(bsz // nb, nt),
        in_specs=[blk, blk, kv, kv, pl.BlockSpec((D_MODEL, D_MODEL), lambda b, t: (0, 0))],
        out_specs=pl.BlockSpec((nb * tq * TILE_ROWS, LANES), lambda b, t: (b * nt + t, 0)),
        out_shape=jax.ShapeDtypeStruct((bsz * seq * TILE_ROWS, LANES), F32),
        compiler_params=_params("parallel", "parallel"),
        name="attn",
    )(q3, x13, k3, v3, w_o)


def _router(hm, wr_hi_ref, wr_lo_ref):
    a_hi = hm.astype(BF16)
    a_lo = (hm - a_hi.astype(F32)).astype(BF16)
    logits = _bdot(a_hi, wr_hi_ref[...]) + (_bdot(a_hi, wr_lo_ref[...]) + _bdot(a_lo, wr_hi_ref[...]))
    lane_i = lax.broadcasted_iota(jnp.int32, logits.shape, 1)
    lane = lane_i.astype(F32)
    neg = -jnp.inf
    big = float(ROUTER_LANES)
    is_g = lane_i < N_EXPERT_GROUPS
    glog = jnp.where(is_g, logits, neg)
    gmax = jnp.max(glog, axis=-1, keepdims=True)
    gsel = jnp.min(jnp.where(glog == gmax, lane, big), axis=-1, keepdims=True)
    pg_sel = 1.0 / jnp.sum(jnp.where(is_g, jnp.exp(logits - gmax), 0.0), axis=-1, keepdims=True)
    eidx = lane_i - N_EXPERT_GROUPS
    in_group = (eidx >= 0) & (eidx < N_EXPERTS) & ((eidx >> 3).astype(F32) == gsel)
    el = jnp.where(in_group, logits, neg)
    v1 = jnp.max(el, axis=-1, keepdims=True)
    i1 = jnp.min(jnp.where(el == v1, lane, big), axis=-1, keepdims=True)
    el2 = jnp.where(lane == i1, neg, el)
    v2 = jnp.max(el2, axis=-1, keepdims=True)
    i2 = jnp.min(jnp.where(el2 == v2, lane, big), axis=-1, keepdims=True)
    e2 = jnp.exp(v2 - v1)
    w1 = pg_sel / (1.0 + e2)
    w2 = pg_sel * e2 / (1.0 + e2)
    return i1, i2, w1, w2


TILE_ROWS = D_MODEL // LANES


def _tiles_to_rows(ref, n, base=0):
    return jnp.concatenate([ref[pl.ds(base + s, n, stride=TILE_ROWS), :] for s in range(TILE_ROWS)], axis=1)


def _rows_to_tiles(ref, val, n, base=0):
    for s in range(TILE_ROWS):
        ref[pl.ds(base + s, n, stride=TILE_ROWS), :] = val[:, s * LANES:(s + 1) * LANES]


def _route_body(x2t_ref, gm_ref, wr_hi_ref, wr_lo_ref, info_ref, cnt_ref, cnt_scr, *, tm):
    @pl.when(pl.program_id(0) == 0)
    def _():
        cnt_scr[...] = jnp.zeros_like(cnt_scr)

    hm = _rms(_tiles_to_rows(x2t_ref, tm), gm_ref[...])
    i1, i2, w1, w2 = _router(hm, wr_hi_ref, wr_lo_ref)
    lane_i = lax.broadcasted_iota(jnp.int32, (tm, ROUTER_LANES), 1)
    lane = lane_i.astype(F32)
    chosen = ((lane == i1) | (lane == i2)).astype(F32)
    earlier = (lax.broadcasted_iota(jnp.int32, (tm, tm), 1) < lax.broadcasted_iota(jnp.int32, (tm, tm), 0))
    before = _bdot(earlier.astype(BF16), chosen.astype(BF16)) + cnt_scr[0:1, :]
    r1 = jnp.sum(jnp.where(lane == i1, before, 0.0), axis=-1, keepdims=True)
    r2 = jnp.sum(jnp.where(lane == i2, before, 0.0), axis=-1, keepdims=True)
    cols = (i1, i2, w1, w2, r1, r2)
    info = jnp.zeros((tm, ROUTER_LANES), F32)
    for j, col in enumerate(cols):
        info = jnp.where(lane_i == j, col, info)
    info_ref[...] = info
    cnt_scr[...] = cnt_scr[...] + jnp.sum(chosen, axis=0, keepdims=True)
    cnt_ref[...] = cnt_scr[...]


def _route(x2t, g_moe, wr_hi, wr_lo, tm=512):
    t = x2t.shape[0] // TILE_ROWS

    def const(shape):
        return pl.BlockSpec(shape, lambda i: (0,) * len(shape))

    body = functools.partial(_route_body, tm=tm)
    return pl.pallas_call(
        body,
        grid=(t // tm,),
        in_specs=[pl.BlockSpec((tm * TILE_ROWS, LANES), lambda i: (i, 0)), const((1, D_MODEL)),
                  const((D_MODEL, ROUTER_LANES)), const((D_MODEL, ROUTER_LANES))],
        out_specs=[pl.BlockSpec((tm, ROUTER_LANES), lambda i: (i, 0)), const((SUBLANES, ROUTER_LANES))],
        out_shape=[jax.ShapeDtypeStruct((t, ROUTER_LANES), F32),
                   jax.ShapeDtypeStruct((SUBLANES, ROUTER_LANES), F32)],
        scratch_shapes=[pltpu.VMEM((SUBLANES, ROUTER_LANES), F32)],
        compiler_params=_params("arbitrary"),
        name="route",
    )(x2t, g_moe, wr_hi, wr_lo)


def _route_meta(info, cnt, tme):
    t = info.shape[0]
    n_tiles = 2 * t // tme
    counts = cnt[0, N_EXPERT_GROUPS:N_EXPERT_GROUPS + N_EXPERTS].astype(jnp.int32)
    ends = jnp.cumsum(counts)
    starts = ends - counts
    e = info[:, 0:2].astype(jnp.int32) - N_EXPERT_GROUPS
    onehot = e[:, :, None] == jnp.arange(N_EXPERTS, dtype=jnp.int32)
    slot = (jnp.sum(jnp.where(onehot, starts, 0), axis=-1) + info[:, 4:6].astype(jnp.int32)).reshape(-1)
    pts = jnp.concatenate([jnp.arange(n_tiles, dtype=jnp.int32) * tme, starts])
    n = pts.shape[0]
    idx = jnp.arange(n, dtype=jnp.int32)
    pos = jnp.sum((pts[None, :] < pts[:, None]) | ((pts[None, :] == pts[:, None]) & (idx[None, :] < idx[:, None])),
                  axis=1)
    lo_abs = jnp.sum(jnp.where(pos[:, None] == idx[None, :], pts[:, None], 0), axis=0)
    hi_abs = jnp.concatenate([lo_abs[1:], jnp.full((1,), 2 * t, jnp.int32)])
    tile = jnp.minimum(lo_abs // tme, n_tiles - 1)
    expert = jnp.clip(jnp.sum(starts[None, :] <= lo_abs[:, None], axis=1) - 1, 0, N_EXPERTS - 1)
    first = jnp.concatenate([jnp.ones((1,), jnp.int32), (tile[1:] != tile[:-1]).astype(jnp.int32)])
    last = jnp.concatenate([(tile[1:] != tile[:-1]).astype(jnp.int32), jnp.ones((1,), jnp.int32)])
    work = [a.astype(jnp.int32) for a in (tile, expert, lo_abs - tile * tme, hi_abs - tile * tme, first, last)]
    return slot.astype(jnp.int32), work


def _tile_copy(src, src_tok, dst, dst_tok, sem):
    return pltpu.make_async_copy(src.at[pl.ds(pl.multiple_of(src_tok * TILE_ROWS, TILE_ROWS), TILE_ROWS), :],
                                 dst.at[pl.ds(pl.multiple_of(dst_tok * TILE_ROWS, TILE_ROWS), TILE_ROWS), :], sem)


def _scatter_body(slot_ref, x2t_ref, xs_hbm, sem, *, tm):
    def issue(r, c):
        _tile_copy(x2t_ref, r, xs_hbm, slot_ref[2 * r], sem).start()
        _tile_copy(x2t_ref, r, xs_hbm, slot_ref[2 * r + 1], sem).start()
        return c

    lax.fori_loop(0, tm, issue, 0, unroll=8)

    def drain(r, c):
        _tile_copy(x2t_ref, r, xs_hbm, 0, sem).wait()
        _tile_copy(x2t_ref, r, xs_hbm, 0, sem).wait()
        return c

    lax.fori_loop(0, tm, drain, 0, unroll=8)


def _scatter(slot, x2t, tm=256):
    t = x2t.shape[0] // TILE_ROWS
    body = functools.partial(_scatter_body, tm=tm)
    return pl.pallas_call(
        body,
        grid=(t // tm,),
        in_specs=[pl.BlockSpec((2 * tm,), lambda i: (i,), memory_space=pltpu.SMEM),
                  pl.BlockSpec((tm * TILE_ROWS, LANES), lambda i: (i, 0))],
        out_specs=pl.BlockSpec(memory_space=pl.ANY),
        out_shape=jax.ShapeDtypeStruct((2 * t * TILE_ROWS, LANES), F32),
        scratch_shapes=[pltpu.SemaphoreType.DMA(())],
        compiler_params=_params("arbitrary"),
        name="scatter",
    )(slot, x2t)


def _expert_body(tile_ref, exp_ref, lo_ref, hi_ref, first_ref, last_ref, xs_ref, gm_ref, wg_ref, wu_ref, wd_ref,
                 ys_ref, acc_scr, *, tme):
    k = pl.program_id(0)

    @pl.when(first_ref[k] == 1)
    def _():
        acc_scr[...] = jnp.zeros_like(acc_scr)

    @pl.when(hi_ref[k] > lo_ref[k])
    def _():
        hb = _rms(_tiles_to_rows(xs_ref, tme), gm_ref[...]).astype(BF16)
        act = jax.nn.silu(_bdot(hb, wg_ref[0])) * _bdot(hb, wu_ref[0])
        y = _bdot(act.astype(BF16), wd_ref[0])
        row = lax.broadcasted_iota(jnp.int32, (tme, 1), 0)
        acc_scr[...] += jnp.where((row >= lo_ref[k]) & (row < hi_ref[k]), y, 0.0)

    @pl.when(last_ref[k] == 1)
    def _():
        _rows_to_tiles(ys_ref, acc_scr[...], tme)


def _experts(work, xs, g_moe, wg, wu, wd, tme):
    n_work = work[0].shape[0]
    body = functools.partial(_expert_body, tme=tme)
    tile_blk = pl.BlockSpec((tme * TILE_ROWS, LANES), lambda k, tile, *_: (tile[k], 0))

    def wspec(shape):
        return pl.BlockSpec(shape, lambda k, tile, exp, *_: (exp[k], 0, 0))

    grid_spec = pltpu.PrefetchScalarGridSpec(
        num_scalar_prefetch=len(work),
        grid=(n_work,),
        in_specs=[tile_blk, pl.BlockSpec((1, D_MODEL), lambda k, *_: (0, 0)),
                  wspec((1, D_MODEL, D_FF_EXPERT)), wspec((1, D_MODEL, D_FF_EXPERT)),
                  wspec((1, D_FF_EXPERT, D_MODEL))],
        out_specs=tile_blk,
        scratch_shapes=[pltpu.VMEM((tme, D_MODEL), F32)],
    )
    return pl.pallas_call(
        body,
        grid_spec=grid_spec,
        out_shape=jax.ShapeDtypeStruct(xs.shape, F32),
        compiler_params=_params("arbitrary"),
        name="experts",
    )(*work, xs, g_moe, wg, wu, wd)


def _combine_body(slot_ref, x2t_ref, info_ref, ys_hbm, gf_ref, y_ref, b0, b1, sem, *, tm):
    def issue(r, c):
        _tile_copy(ys_hbm, slot_ref[2 * r], b0, r, sem).start()
        _tile_copy(ys_hbm, slot_ref[2 * r + 1], b1, r, sem).start()
        return c

    lax.fori_loop(0, tm, issue, 0, unroll=8)

    def drain(r, c):
        _tile_copy(ys_hbm, 0, b0, r, sem).wait()
        _tile_copy(ys_hbm, 0, b1, r, sem).wait()
        return c

    lax.fori_loop(0, tm, drain, 0, unroll=8)
    info = info_ref[...]
    moe = info[:, 2:3] * _tiles_to_rows(b0, tm) + info[:, 3:4] * _tiles_to_rows(b1, tm)
    y_ref[...] = _rms(_tiles_to_rows(x2t_ref, tm) + moe, gf_ref[...])


def _combine(slot, x2t, info, ys, g_final, tm=256):
    t = info.shape[0]
    body = functools.partial(_combine_body, tm=tm)
    return pl.pallas_call(
        body,
        grid=(t // tm,),
        in_specs=[pl.BlockSpec((2 * tm,), lambda i: (i,), memory_space=pltpu.SMEM),
                  pl.BlockSpec((tm * TILE_ROWS, LANES), lambda i: (i, 0)),
                  pl.BlockSpec((tm, ROUTER_LANES), lambda i: (i, 0)),
                  pl.BlockSpec(memory_space=pl.ANY),
                  pl.BlockSpec((1, D_MODEL), lambda i: (0, 0))],
        out_specs=pl.BlockSpec((tm, D_MODEL), lambda i: (i, 0)),
        out_shape=jax.ShapeDtypeStruct((t, D_MODEL), F32),
        scratch_shapes=[pltpu.VMEM((tm * TILE_ROWS, LANES), F32), pltpu.VMEM((tm * TILE_ROWS, LANES), F32),
                        pltpu.SemaphoreType.DMA(())],
        compiler_params=_params("arbitrary"),
        name="combine",
    )(slot, x2t, info, ys, g_final)


def _moe(x2t, w, tme):
    info, cnt = _route(x2t, w["g_moe"], w["wr_hi"], w["wr_lo"])
    slot, work = _route_meta(info, cnt, tme)
    xs = _scatter(slot, x2t)
    ys = _experts(work, xs, w["g_moe"], w["moe_wg"], w["moe_wu"], w["moe_wd"], tme)
    return _combine(slot, x2t, info, ys, w["g_final"])


def _layer(x3d, k3, v3, s5_h0, lru_h0, lru_conv, w, chunk, nb, tq, tme):
    bsz, seq, _ = x3d.shape
    x2d = x3d.reshape(bsz * seq, D_MODEL)
    u, xl, gl = _mixin(x2d, w["g_mix"], w["w_in"])

    nstate = S5_GROUPS * S5_STATE
    if s5_h0 is None:
        zero = jnp.zeros((bsz, 1, nstate), F32)
        ys, s5_re, s5_im = _s5(u, w["s5"][chunk], zero, zero, chunk, seq // chunk, True)
    else:
        ys, s5_re, s5_im = _s5(u, w["s5"][chunk], s5_h0[0].reshape(bsz, nstate), s5_h0[1].reshape(bsz, nstate),
                               chunk, bsz, False)
    s5_re = s5_re.reshape(bsz, S5_GROUPS, S5_STATE)
    s5_im = s5_im.reshape(bsz, S5_GROUPS, S5_STATE)

    lru_w = (w["conv_w"], w["conv_b"], w["lru_wg"], w["lru_bg"], w["lru_lam"])
    if lru_h0 is None:
        yl, hl = _lru_seq(xl, gl, *lru_w, bsz, seq)
        lru_h = hl.reshape(bsz, SUBLANES, D_LRU)[:, 0]
    else:
        hist = jnp.pad(lru_conv, ((0, 0), (SUBLANES - (CONV_W - 1), 0), (0, 0))).reshape(bsz * seq, D_LRU)
        h0rep = jnp.repeat(lru_h0, seq, axis=0)
        yl, hall = _lru_dec(xl, gl, hist, h0rep, *lru_w)
        lru_h = hall.reshape(bsz, seq, D_LRU)[:, seq - 1]
    conv_new = xl.reshape(bsz, seq, D_LRU)[:, seq - (CONV_W - 1):]

    x1, q = _mixout(x2d, ys, yl, w["w_glu"], w["b_glu"], w["w_out"], w["g_xa"], w["w_q"])
    x2t = _attn(q.reshape(bsz, seq, D_MODEL), x1.reshape(bsz, seq, D_MODEL), k3, v3, w["w_o"], nb, tq)
    y = _moe(x2t, w, tme)
    return y.reshape(bsz, seq, D_MODEL), s5_re, s5_im, lru_h, conv_new


def kernel(x_prompt, x_sample, mem_prompt, cache_mem_k, cache_mem_v, state_s5_re, state_s5_im, state_lru_h, state_lru_conv, g_mix, w_in, s5_lam_re, s5_lam_im, s5_log_dt, s5_b_re, s5_b_im, s5_c_re, s5_c_im, s5_d, s5_w_glu, s5_b_glu, lru_conv_w, lru_conv_b, lru_w_a, lru_b_a, lru_w_x, lru_b_x, lru_lam, w_out, g_xa, g_mem, xa_w_q, xa_w_k, xa_w_v, xa_w_o, g_moe, moe_w_group, moe_w_expert, moe_w_gate, moe_w_up, moe_w_down, g_final):
    depth = g_mix.shape[0]
    assert depth == 1, "single-layer step"
    l = 0
    bsz, seq, _ = x_prompt.shape
    dbsz, dseq, _ = x_sample.shape
    chunk_p, chunk_s = 16, dseq
    assert seq % chunk_p == 0 and dseq == SUBLANES

    s5_args = (s5_lam_re[l], s5_lam_im[l], s5_log_dt[l], s5_b_re[l], s5_b_im[l], s5_c_re[l], s5_c_im[l], s5_d[l])
    wg, bg = _lru_weights(lru_w_a[l], lru_w_x[l], lru_b_a[l], lru_b_x[l])
    wr = jnp.concatenate([moe_w_group[l], moe_w_expert[l],
                          jnp.zeros((D_MODEL, ROUTER_LANES - N_EXPERT_GROUPS - N_EXPERTS), F32)], axis=1)
    wr_hi = wr.astype(BF16)
    wr_lo = (wr - wr_hi.astype(F32)).astype(BF16)
    w = {
        "g_mix": g_mix[l][None], "w_in": w_in[l].astype(BF16),
        "s5": {c: _s5_weights(c, *s5_args) for c in {chunk_p, chunk_s}},
        "conv_w": lru_conv_w[l], "conv_b": lru_conv_b[l][None], "lru_wg": wg, "lru_bg": bg,
        "lru_lam": lru_lam[l][None],
        "w_glu": s5_w_glu[l].astype(BF16), "b_glu": s5_b_glu[l][None], "w_out": w_out[l].astype(BF16),
        "g_xa": g_xa[l][None], "w_q": xa_w_q[l].astype(BF16), "w_o": xa_w_o[l].astype(BF16),
        "g_moe": g_moe[l][None], "wr_hi": wr_hi, "wr_lo": wr_lo,
        "moe_wg": moe_w_gate[l].astype(BF16), "moe_wu": moe_w_up[l].astype(BF16),
        "moe_wd": moe_w_down[l].astype(BF16), "g_final": g_final[None],
    }

    mk, mv = _memkv(mem_prompt.reshape(bsz * N_MEM, D_MODEL), g_mem[l][None],
                    xa_w_k[l].astype(BF16), xa_w_v[l].astype(BF16))
    mk3 = mk.reshape(bsz, N_MEM, D_MODEL)
    mv3 = mv.reshape(bsz, N_MEM, D_MODEL)

    yp, p_re, p_im, p_h, p_conv = _layer(x_prompt, mk3, mv3, None, None, None, w,
                                         chunk_p, nb=1, tq=512, tme=256)
    ck = cache_mem_k[l].reshape(dbsz, N_MEM, D_MODEL)
    cv = cache_mem_v[l].reshape(dbsz, N_MEM, D_MODEL)
    ysmp, s_re, s_im, s_h, s_conv = _layer(x_sample, ck, cv, (state_s5_re[l], state_s5_im[l]), state_lru_h[l],
                                           state_lru_conv[l], w, chunk_s, nb=4, tq=dseq, tme=128)

    return (yp, ysmp,
            mk.reshape(1, bsz, N_MEM, XA_HEADS, XA_HEAD_DIM), mv.reshape(1, bsz, N_MEM, XA_HEADS, XA_HEAD_DIM),
            p_re[None], p_im[None], p_h[None], p_conv[None],
            s_re[None], s_im[None], s_h[None], s_conv[None])
```

```python
import functools

import jax
import jax.numpy as jnp
import numpy as np
from jax import lax
from jax.experimental import pallas as pl
from jax.experimental.pallas import tpu as pltpu

F32 = jnp.float32
BF16 = jnp.bfloat16

D_MODEL = 1024
D_S5 = 512
D_LRU = 512
S5_GROUP = 16
S5_GROUPS = 32
S5_STATE = 64
LRU_HEADS = 8
LRU_HEAD_DIM = 64
CONV_W = 4
LRU_C = 8.0
N_MEM = 256
XA_HEADS = 4
XA_HEAD_DIM = 256
N_EXPERT_GROUPS = 4
EXPERTS_PER_GROUP = 8
N_EXPERTS = 32
D_FF_EXPERT = 256
EPS = 1e-6

SUBLANES = 8
LANES = 128
ROUTER_LANES = 128
VMEM_LIMIT = 48 * 1024 * 1024


def _params(*sem):
    return pltpu.CompilerParams(dimension_semantics=sem, vmem_limit_bytes=VMEM_LIMIT)


def _rms(x, g):
    ms = jnp.mean(x * x, axis=-1, keepdims=True)
    return x * lax.rsqrt(ms + EPS) * g


def _bdot(a, b):
    return jnp.dot(a, b, preferred_element_type=F32)


def _memkv_body(m_ref, g_ref, wk_ref, wv_ref, k_ref, v_ref):
    hb = _rms(m_ref[...], g_ref[...]).astype(BF16)
    k_ref[...] = _bdot(hb, wk_ref[...])
    v_ref[...] = _bdot(hb, wv_ref[...])


def _memkv(mem2d, g_mem, wk, wv, tm=512):
    t = mem2d.shape[0]
    row = pl.BlockSpec((tm, D_MODEL), lambda i: (i, 0))
    full = pl.BlockSpec((D_MODEL, D_MODEL), lambda i: (0, 0))
    return pl.pallas_call(
        _memkv_body,
        grid=(t // tm,),
        in_specs=[row, pl.BlockSpec((1, D_MODEL), lambda i: (0, 0)), full, full],
        out_specs=[row, row],
        out_shape=[jax.ShapeDtypeStruct((t, D_MODEL), F32)] * 2,
        compiler_params=_params("parallel"),
        name="memkv",
    )(mem2d, g_mem, wk, wv)


def _mixin_body(x_ref, g_ref, w_ref, u_ref, xl_ref, gl_ref):
    hb = _rms(x_ref[...], g_ref[...]).astype(BF16)
    proj = _bdot(hb, w_ref[...])
    u_ref[...] = proj[:, :D_S5]
    xl_ref[...] = proj[:, D_S5:D_S5 + D_LRU]
    gl_ref[...] = proj[:, D_S5 + D_LRU:]


def _mixin(x2d, g_mix, w_in, tm=512):
    t = x2d.shape[0]
    half = pl.BlockSpec((tm, D_S5), lambda i: (i, 0))
    return pl.pallas_call(
        _mixin_body,
        grid=(t // tm,),
        in_specs=[pl.BlockSpec((tm, D_MODEL), lambda i: (i, 0)),
                  pl.BlockSpec((1, D_MODEL), lambda i: (0, 0)),
                  pl.BlockSpec((D_MODEL, D_S5 + 2 * D_LRU), lambda i: (0, 0))],
        out_specs=[half, half, half],
        out_shape=[jax.ShapeDtypeStruct((t, D_S5), F32)] * 3,
        compiler_params=_params("parallel"),
        name="mixin",
    )(x2d, g_mix, w_in)


GROUPS_PER_BLOCK = LANES // S5_GROUP
PAIRS_PER_BLOCK = GROUPS_PER_BLOCK // 2
STATE_BLOCK = GROUPS_PER_BLOCK * S5_STATE
PW_ROWS = 24


def _s5_body(u_ref, wa_ref, wh_ref, pw_ref, d_ref, h0r_ref, h0i_ref, y_ref, hfr_ref, hfi_ref, hpr_scr, hpi_scr,
             *, chunk, nrow, scan):
    nh = chunk // SUBLANES
    lc = chunk * S5_GROUP
    slot = lax.broadcasted_iota(jnp.int32, (nrow, LANES), 1) // S5_GROUP
    in_slot = [slot == s for s in range(GROUPS_PER_BLOCK)]

    def pick(src, sel):
        out = src[sel(0)]
        for s in range(1, GROUPS_PER_BLOCK):
            out = jnp.where(in_slot[s], src[sel(s)], out)
        return out

    nat, skew = [], []
    for t in range(chunk):
        a = u_ref[pl.ds(t, nrow, stride=chunk), :]
        nat.append(a)
        s = t % GROUPS_PER_BLOCK
        skew.append(pltpu.roll(a, s * S5_GROUP, axis=1) if s else a)
    def lane_roll(x, slots):
        slots %= GROUPS_PER_BLOCK
        return pltpu.roll(x, slots * S5_GROUP, axis=1) if slots else x

    z = [[lane_roll(pick(skew, lambda s, hh=hh, m=m: SUBLANES * hh + (s - m) % GROUPS_PER_BLOCK), -m)
          for hh in range(nh)] for m in range(GROUPS_PER_BLOCK)]

    ys, er, ei = [], [], []
    for p in range(PAIRS_PER_BLOCK):
        lhs = jnp.concatenate(z[2 * p] + z[2 * p + 1], axis=1).astype(BF16)
        full = _bdot(lhs, wa_ref[p])
        ys.append(full[:, :2 * lc])
        er.append(full[:, 2 * lc:2 * lc + LANES])
        ei.append(full[:, 2 * lc + LANES:])
    er = jnp.concatenate(er, axis=1)
    ei = jnp.concatenate(ei, axis=1)

    if scan:
        row = lax.broadcasted_iota(jnp.int32, er.shape, 0) % SUBLANES
        xr, xi = er, ei
        for k, d in enumerate((1, 2, 4)):
            pr, pi = pw_ref[0, k:k + 1, :], pw_ref[0, 4 + k:5 + k, :]
            sr, si = pltpu.roll(xr, d, axis=0), pltpu.roll(xi, d, axis=0)
            m = row >= d
            xr, xi = jnp.where(m, xr + pr * sr - pi * si, xr), jnp.where(m, xi + pr * si + pi * sr, xi)
        sxr = jnp.where(row >= 1, pltpu.roll(xr, 1, axis=0), 0.0)
        sxi = jnp.where(row >= 1, pltpu.roll(xi, 1, axis=0), 0.0)
        p8r, p8i = pw_ref[0, 3:4, :], pw_ref[0, 7:8, :]
        qr, qi = pw_ref[0, 8:16, :], pw_ref[0, 16:24, :]
        hr, hi = h0r_ref[0], h0i_ref[0]
        for t in range(nrow // SUBLANES):
            rows = slice(SUBLANES * t, SUBLANES * (t + 1))
            hpr_scr[rows, :] = sxr[rows] + qr * hr - qi * hi
            hpi_scr[rows, :] = sxi[rows] + qr * hi + qi * hr
            last = SUBLANES * (t + 1) - 1
            hr, hi = (xr[last:last + 1] + p8r * hr - p8i * hi, xi[last:last + 1] + p8r * hi + p8i * hr)
        hfr_ref[0] = hr
        hfi_ref[0] = hi
        hpr, hpi = hpr_scr[...], hpi_scr[...]
    else:
        hpr, hpi = h0r_ref[...], h0i_ref[...]
        p1r, p1i = pw_ref[0, 0:1, :], pw_ref[0, 4:5, :]
        hfr_ref[...] = er + p1r * hpr - p1i * hpi
        hfi_ref[...] = ei + p1r * hpi + p1i * hpr

    yg = []
    for p in range(PAIRS_PER_BLOCK):
        lanes = slice(LANES * p, LANES * (p + 1))
        hp = jnp.concatenate([hpr[:, lanes], hpi[:, lanes]], axis=1).astype(BF16)
        out = ys[p] + _bdot(hp, wh_ref[p])
        for half in range(2):
            q = 2 * p + half
            yg.append([lane_roll(out[:, half * lc + hh * LANES:half * lc + (hh + 1) * LANES], q)
                       for hh in range(nh)])
    d = d_ref[...]
    for t in range(chunk):
        hh, s = divmod(t, GROUPS_PER_BLOCK)
        c = lane_roll(pick([g[hh] for g in yg], lambda sl, s=s: (sl - s) % GROUPS_PER_BLOCK), -s)
        y_ref[pl.ds(t, nrow, stride=chunk), :] = c + d * nat[t]


def _s5(u2d, weights, h0r, h0i, chunk, nrow, scan):
    wa, wh, pw, dvec = weights
    t = u2d.shape[0]
    lc = chunk * S5_GROUP
    rows = nrow * chunk
    nblk = D_S5 // LANES
    body = functools.partial(_s5_body, chunk=chunk, nrow=nrow, scan=scan)
    ublk = pl.BlockSpec((rows, LANES), lambda i, j: (i, j))
    if scan:
        hblk = pl.BlockSpec((1, 1, STATE_BLOCK), lambda i, j: (i, 0, j))
        hshape = jax.ShapeDtypeStruct((t // rows, 1, S5_GROUPS * S5_STATE), F32)
    else:
        hblk = pl.BlockSpec((nrow, STATE_BLOCK), lambda i, j: (i, j))
        hshape = jax.ShapeDtypeStruct((t // chunk, S5_GROUPS * S5_STATE), F32)
    return pl.pallas_call(
        body,
        grid=(t // rows, nblk),
        in_specs=[ublk,
                  pl.BlockSpec((PAIRS_PER_BLOCK, 2 * lc, 2 * lc + 2 * LANES), lambda i, j: (j, 0, 0)),
                  pl.BlockSpec((PAIRS_PER_BLOCK, 2 * LANES, 2 * lc), lambda i, j: (j, 0, 0)),
                  pl.BlockSpec((1, PW_ROWS, STATE_BLOCK), lambda i, j: (j, 0, 0)),
                  pl.BlockSpec((1, LANES), lambda i, j: (0, j)),
                  hblk, hblk],
        out_specs=[ublk, hblk, hblk],
        out_shape=[jax.ShapeDtypeStruct((t, D_S5), F32), hshape, hshape],
        scratch_shapes=[pltpu.VMEM((nrow, STATE_BLOCK), F32), pltpu.VMEM((nrow, STATE_BLOCK), F32)],
        compiler_params=_params("parallel", "parallel"),
        name="s5",
    )(u2d, wa, wh, pw, dvec, h0r, h0i)


def _s5_weights(chunk, lam_re, lam_im, log_dt, b_re, b_im, c_re, c_im, d_skip):
    hi = lax.Precision.HIGHEST
    dt = jnp.exp(log_dt)[:, None]
    mag = jnp.exp(lam_re * dt)
    ab_re = mag * jnp.cos(lam_im * dt)
    ab_im = mag * jnp.sin(lam_im * dt)
    den = lam_re * lam_re + lam_im * lam_im
    nr = ab_re - 1.0
    f_re = (nr * lam_re + ab_im * lam_im) / den
    f_im = (ab_im * lam_re - nr * lam_im) / den
    bb_re = f_re[..., None] * b_re - f_im[..., None] * b_im
    bb_im = f_re[..., None] * b_im + f_im[..., None] * b_re
    def powers(ks):
        k = jnp.asarray(np.asarray(ks, np.float32))[:, None, None]
        m = jnp.exp(k * (lam_re * dt))
        return m * jnp.cos(k * (lam_im * dt)), m * jnp.sin(k * (lam_im * dt))

    pw_re, pw_im = powers(range(chunk + 1))
    pb_re = pw_re[:chunk, ..., None] * bb_re - pw_im[:chunk, ..., None] * bb_im
    pb_im = pw_re[:chunk, ..., None] * bb_im + pw_im[:chunk, ..., None] * bb_re
    kk = (jnp.einsum("gon,kgni->kgoi", c_re, pb_re, precision=hi)
          - jnp.einsum("gon,kgni->kgoi", c_im, pb_im, precision=hi))
    npair = S5_GROUPS // 2
    lc = chunk * S5_GROUP
    t_idx = np.arange(chunk)
    lagsel = (t_idx[None, None, :] - t_idx[None, :, None] == t_idx[:, None, None]).astype(np.float32)
    rev = np.eye(chunk, dtype=np.float32)[::-1]
    eye2 = np.eye(2, dtype=np.float32)

    def pairs(x):
        return x.reshape(x.shape[:1] + (npair, 2) + x.shape[2:])

    conv = jnp.einsum("kab,kpeoi,ef->peaifbo", lagsel, pairs(kk), eye2, precision=hi).reshape(npair, 2 * lc, 2 * lc)
    ends = [jnp.einsum("ak,kpeni,ef->peaifn", rev, pairs(x), eye2, precision=hi).reshape(npair, 2 * lc, 2 * S5_STATE)
            for x in (pb_re, pb_im)]
    wa = jnp.concatenate([conv] + ends, axis=-1)
    gr = c_re[None] * pw_re[1:, :, None, :] - c_im[None] * pw_im[1:, :, None, :]
    gi = c_re[None] * pw_im[1:, :, None, :] + c_im[None] * pw_re[1:, :, None, :]
    wh = jnp.concatenate([jnp.einsum("bpeon,ef->penfbo", pairs(x), eye2, precision=hi)
                          .reshape(npair, 2 * S5_STATE, 2 * lc) for x in (gr, -gi)], axis=1)
    tab = [1, 2, 4, 8] + list(range(SUBLANES))
    qr, qi = powers([chunk * k for k in tab])
    nblk = D_S5 // LANES
    pw = jnp.concatenate([qr[:4], qi[:4], qr[4:], qi[4:]]).reshape(PW_ROWS, nblk, STATE_BLOCK).transpose(1, 0, 2)
    return wa.astype(BF16), wh.astype(BF16), pw, d_skip.reshape(1, D_S5)


def _lru_ab(xc, wg_ref, bg_ref, lam_ref):
    xb = xc.astype(BF16)
    half = D_LRU // 2
    g0 = _bdot(xb[:, :half], wg_ref[0]) + bg_ref[0]
    g1 = _bdot(xb[:, half:], wg_ref[1]) + bg_ref[1]
    r = jax.nn.sigmoid(jnp.concatenate([g0[:, :half], g1[:, :half]], axis=1))
    ig = jax.nn.sigmoid(jnp.concatenate([g0[:, half:], g1[:, half:]], axis=1))
    zl = -lam_ref[...]
    softplus = jnp.maximum(zl, 0.0) + jnp.log1p(jnp.exp(-jnp.abs(zl)))
    log_a = -LRU_C * r * softplus
    a = jnp.exp(log_a)
    b = jnp.sqrt(-jnp.tanh(log_a) * (a * a + 1.0)) * (ig * xc)
    return a, b


def _tile_scan(a, b):
    row = lax.broadcasted_iota(jnp.int32, a.shape, 0) % SUBLANES
    for d in (1, 2, 4):
        a_prev = pltpu.roll(a, d, axis=0)
        b_prev = pltpu.roll(b, d, axis=0)
        m = row >= d
        b = jnp.where(m, b + a * b_prev, b)
        a = jnp.where(m, a * a_prev, a)
    return a, b


def _lru_seq_body(xl_ref, gl_ref, cw_ref, cb_ref, wg_ref, bg_ref, lam_ref, y_ref, hl_ref,
                  xp_scr, a_scr, b_scr, h_scr, hc_scr, *, ts):
    ti = pl.program_id(1)

    @pl.when(ti == 0)
    def _():
        xp_scr[0:SUBLANES, :] = jnp.zeros((SUBLANES, D_LRU), F32)
        hc_scr[...] = jnp.zeros((SUBLANES, D_LRU), F32)

    xl = xl_ref[...]
    xp_scr[SUBLANES:SUBLANES + ts, :] = xl
    xc = cb_ref[...] + xl * cw_ref[CONV_W - 1:CONV_W, :]
    for j in range(1, CONV_W):
        xc = xc + xp_scr[SUBLANES - j:SUBLANES - j + ts, :] * cw_ref[CONV_W - 1 - j:CONV_W - j, :]
    xp_scr[0:SUBLANES, :] = xl[ts - SUBLANES:, :]
    a, b = _lru_ab(xc, wg_ref, bg_ref, lam_ref)
    a, b = _tile_scan(a, b)
    a_scr[...] = a
    b_scr[...] = b

    def step(i, hin):
        rows = pl.ds(pl.multiple_of(i * SUBLANES, SUBLANES), SUBLANES)
        h = b_scr[rows, :] + a_scr[rows, :] * hin
        h_scr[rows, :] = h
        return h[SUBLANES - 1:SUBLANES, :]

    hlast = lax.fori_loop(0, ts // SUBLANES, step, hc_scr[0:1, :], unroll=4)
    hc_scr[...] = jnp.broadcast_to(hlast, (SUBLANES, D_LRU))
    hl_ref[...] = hc_scr[...]
    y_ref[...] = h_scr[...] * jax.nn.gelu(gl_ref[...])


def _lru_dec_body(xl_ref, gl_ref, hist_ref, h0_ref, cw_ref, cb_ref, wg_ref, bg_ref, lam_ref, y_ref, hl_ref, *, tm):
    xl = xl_ref[...]
    hist = hist_ref[...]
    row = lax.broadcasted_iota(jnp.int32, xl.shape, 0) % SUBLANES
    xc = cb_ref[...] + xl * cw_ref[CONV_W - 1:CONV_W, :]
    for j in range(1, CONV_W):
        prev = jnp.where(row >= j, pltpu.roll(xl, j, axis=0), pltpu.roll(hist, tm - SUBLANES + j, axis=0))
        xc = xc + prev * cw_ref[CONV_W - 1 - j:CONV_W - j, :]
    a, b = _lru_ab(xc, wg_ref, bg_ref, lam_ref)
    a, b = _tile_scan(a, b)
    h = b + a * h0_ref[...]
    hl_ref[...] = h
    y_ref[...] = h * jax.nn.gelu(gl_ref[...])


def _lru_weights(w_a, w_x, b_a, b_x):
    eye = jnp.eye(LRU_HEADS, dtype=F32)
    bd_a = jnp.einsum("hij,hg->higj", w_a, eye).reshape(D_LRU, D_LRU)
    bd_x = jnp.einsum("hij,hg->higj", w_x, eye).reshape(D_LRU, D_LRU)
    half = D_LRU // 2
    wg = jnp.stack([jnp.concatenate([bd_a[:half, :half], bd_x[:half, :half]], axis=1),
                    jnp.concatenate([bd_a[half:, half:], bd_x[half:, half:]], axis=1)]).astype(BF16)
    ba = b_a.reshape(1, D_LRU)
    bx = b_x.reshape(1, D_LRU)
    bg = jnp.stack([jnp.concatenate([ba[:, :half], bx[:, :half]], axis=1),
                    jnp.concatenate([ba[:, half:], bx[:, half:]], axis=1)])
    return wg, bg


def _lru_common_specs(const):
    return [const((CONV_W, D_LRU)), const((1, D_LRU)), const((2, D_LRU // 2, D_LRU)),
            const((2, 1, D_LRU)), const((1, D_LRU))]


def _lru_seq(xl, gl, cw, cb, wg, bg, lam, bsz, seq, ts=512):
    nt = seq // ts
    blk = pl.BlockSpec((ts, D_LRU), lambda b, t: (b * nt + t, 0))

    def const(shape):
        return pl.BlockSpec(shape, lambda b, t: (0,) * len(shape))

    body = functools.partial(_lru_seq_body, ts=ts)
    return pl.pallas_call(
        body,
        grid=(bsz, nt),
        in_specs=[blk, blk] + _lru_common_specs(const),
        out_specs=[blk, pl.BlockSpec((SUBLANES, D_LRU), lambda b, t: (b, 0))],
        out_shape=[jax.ShapeDtypeStruct((bsz * seq, D_LRU), F32),
                   jax.ShapeDtypeStruct((bsz * SUBLANES, D_LRU), F32)],
        scratch_shapes=[pltpu.VMEM((ts + SUBLANES, D_LRU), F32), pltpu.VMEM((ts, D_LRU), F32),
                        pltpu.VMEM((ts, D_LRU), F32), pltpu.VMEM((ts, D_LRU), F32),
                        pltpu.VMEM((SUBLANES, D_LRU), F32)],
        compiler_params=_params("arbitrary", "arbitrary"),
        name="lru_seq",
    )(xl, gl, cw, cb, wg, bg, lam)


def _lru_dec(xl, gl, hist, h0rep, cw, cb, wg, bg, lam, tm=256):
    t = xl.shape[0]
    blk = pl.BlockSpec((tm, D_LRU), lambda i: (i, 0))

    def const(shape):
        return pl.BlockSpec(shape, lambda i: (0,) * len(shape))

    body = functools.partial(_lru_dec_body, tm=tm)
    return pl.pallas_call(
        body,
        grid=(t // tm,),
        in_specs=[blk, blk, blk, blk] + _lru_common_specs(const),
        out_specs=[blk, blk],
        out_shape=[jax.ShapeDtypeStruct((t, D_LRU), F32)] * 2,
        compiler_params=_params("parallel"),
        name="lru_dec",
    )(xl, gl, hist, h0rep, cw, cb, wg, bg, lam)


def _mixout_body(x_ref, ys_ref, yl_ref, wglu_ref, bglu_ref, wo_ref, gxa_ref, wq_ref, x1_ref, q_ref):
    ys = jax.nn.gelu(ys_ref[...])
    gate = jax.nn.sigmoid(_bdot(ys.astype(BF16), wglu_ref[...]) + bglu_ref[...])
    s5 = (ys * gate).astype(BF16)
    x1 = (x_ref[...] + _bdot(s5, wo_ref[0:D_S5, :]) + _bdot(yl_ref[...].astype(BF16), wo_ref[D_S5:, :]))
    x1_ref[...] = x1
    q_ref[...] = _bdot(_rms(x1, gxa_ref[...]).astype(BF16), wq_ref[...]).astype(BF16)


def _mixout(x2d, ys, yl, w_glu, b_glu, w_out, g_xa, w_q, tm=512):
    t = x2d.shape[0]
    row = pl.BlockSpec((tm, D_MODEL), lambda i: (i, 0))
    half = pl.BlockSpec((tm, D_S5), lambda i: (i, 0))

    def const(shape):
        return pl.BlockSpec(shape, lambda i: (0,) * len(shape))

    return pl.pallas_call(
        _mixout_body,
        grid=(t // tm,),
        in_specs=[row, half, half, const((D_S5, D_S5)), const((1, D_S5)), const((D_MODEL, D_MODEL)),
                  const((1, D_MODEL)), const((D_MODEL, D_MODEL))],
        out_specs=[row, row],
        out_shape=[jax.ShapeDtypeStruct((t, D_MODEL), F32), jax.ShapeDtypeStruct((t, D_MODEL), BF16)],
        compiler_params=_params("parallel"),
        name="mixout",
    )(x2d, ys, yl, w_glu, b_glu, w_out, g_xa, w_q)


def _softmax(sc):
    p = jnp.exp(sc - jnp.max(sc, axis=-1, keepdims=True))
    return p / jnp.sum(p, axis=-1, keepdims=True)


def _attn_body(q_ref, x1_ref, k_ref, v_ref, wo_ref, x2t_ref, *, nb, tq):
    scale = XA_HEAD_DIM ** -0.5
    for i in range(nb):
        q = q_ref[i]
        heads = [q[:, h * XA_HEAD_DIM:(h + 1) * XA_HEAD_DIM] for h in range(XA_HEADS)]
        if len(k_ref.shape) == 4:
            kf = k_ref[i].reshape(N_MEM * XA_HEADS, XA_HEAD_DIM).astype(BF16)
            vf = v_ref[i].reshape(N_MEM * XA_HEADS, XA_HEAD_DIM).astype(BF16)
            sc = lax.dot_general(jnp.concatenate(heads, axis=0), kf, (((1,), (1,)), ((), ())),
                                 preferred_element_type=F32) * scale
            own = (lax.broadcasted_iota(jnp.int32, sc.shape, 1) % XA_HEADS
                   == lax.broadcasted_iota(jnp.int32, sc.shape, 0) // tq)
            oh = _bdot(_softmax(jnp.where(own, sc, -jnp.inf)).astype(BF16), vf)
            outs = [oh[h * tq:(h + 1) * tq] for h in range(XA_HEADS)]
        else:
            kb = k_ref[i].astype(BF16)
            vb = v_ref[i].astype(BF16)
            outs = []
            for h in range(XA_HEADS):
                cols = slice(h * XA_HEAD_DIM, (h + 1) * XA_HEAD_DIM)
                sc = lax.dot_general(heads[h], kb[:, cols], (((1,), (1,)), ((), ())),
                                     preferred_element_type=F32) * scale
                outs.append(_bdot(_softmax(sc).astype(BF16), vb[:, cols]))
        o = jnp.concatenate(outs, axis=1).astype(BF16)
        _rows_to_tiles(x2t_ref, x1_ref[i] + _bdot(o, wo_ref[...]), tq, base=i * tq * TILE_ROWS)


def _attn(q3, x13, k4, v4, w_o, nb, tq):
    bsz, seq, _ = q3.shape
    nt = seq // tq
    blk = pl.BlockSpec((nb, tq, D_MODEL), lambda b, t: (b, t, 0))
    kv = pl.BlockSpec((nb,) + k4.shape[1:], lambda b, t: (b,) + (0,) * (k4.ndim - 1))
    body = functools.partial(_attn_body, nb=nb, tq=tq)
    return pl.pallas_call(
        body,
        grid=(bsz // nb, nt),
        in_specs=[blk, blk, kv, kv, pl.BlockSpec((D_MODEL, D_MODEL), lambda b, t: (0, 0))],
        out_specs=pl.BlockSpec((nb * tq * TILE_ROWS, LANES), lambda b, t: (b * nt + t, 0)),
        out_shape=jax.ShapeDtypeStruct((bsz * seq * TILE_ROWS, LANES), F32),
        compiler_params=_params("parallel", "parallel"),
        name="attn",
    )(q3, x13, k4, v4, w_o)


def _router(hm, wr_hi_ref, wr_lo_ref):
    a_hi = hm.astype(BF16)
    a_lo = (hm - a_hi.astype(F32)).astype(BF16)
    logits = _bdot(a_hi, wr_hi_ref[...]) + (_bdot(a_hi, wr_lo_ref[...]) + _bdot(a_lo, wr_hi_ref[...]))
    lane_i = lax.broadcasted_iota(jnp.int32, logits.shape, 1)
    lane = lane_i.astype(F32)
    neg = -jnp.inf
    big = float(ROUTER_LANES)
    is_g = lane_i < N_EXPERT_GROUPS
    glog = jnp.where(is_g, logits, neg)
    gmax = jnp.max(glog, axis=-1, keepdims=True)
    gsel = jnp.min(jnp.where(glog == gmax, lane, big), axis=-1, keepdims=True)
    pg_sel = 1.0 / jnp.sum(jnp.where(is_g, jnp.exp(logits - gmax), 0.0), axis=-1, keepdims=True)
    eidx = lane_i - N_EXPERT_GROUPS
    in_group = (eidx >= 0) & (eidx < N_EXPERTS) & ((eidx >> 3).astype(F32) == gsel)
    el = jnp.where(in_group, logits, neg)
    v1 = jnp.max(el, axis=-1, keepdims=True)
    i1 = jnp.min(jnp.where(el == v1, lane, big), axis=-1, keepdims=True)
    el2 = jnp.where(lane == i1, neg, el)
    v2 = jnp.max(el2, axis=-1, keepdims=True)
    i2 = jnp.min(jnp.where(el2 == v2, lane, big), axis=-1, keepdims=True)
    e2 = jnp.exp(v2 - v1)
    w1 = pg_sel / (1.0 + e2)
    w2 = pg_sel * e2 / (1.0 + e2)
    return i1, i2, w1, w2


TILE_ROWS = D_MODEL // LANES


def _tiles_to_rows(ref, n, base=0):
    return jnp.concatenate([ref[pl.ds(base + s, n, stride=TILE_ROWS), :] for s in range(TILE_ROWS)], axis=1)


def _rows_to_tiles(ref, val, n, base=0):
    for s in range(TILE_ROWS):
        ref[pl.ds(base + s, n, stride=TILE_ROWS), :] = val[:, s * LANES:(s + 1) * LANES]


def _route_body(x2t_ref, gm_ref, wr_hi_ref, wr_lo_ref, info_ref, cnt_ref, cnt_scr, *, tm):
    @pl.when(pl.program_id(0) == 0)
    def _():
        cnt_scr[...] = jnp.zeros_like(cnt_scr)

    hm = _rms(_tiles_to_rows(x2t_ref, tm), gm_ref[...])
    i1, i2, w1, w2 = _router(hm, wr_hi_ref, wr_lo_ref)
    lane_i = lax.broadcasted_iota(jnp.int32, (tm, ROUTER_LANES), 1)
    lane = lane_i.astype(F32)
    chosen = ((lane == i1) | (lane == i2)).astype(F32)
    earlier = (lax.broadcasted_iota(jnp.int32, (tm, tm), 1) < lax.broadcasted_iota(jnp.int32, (tm, tm), 0))
    before = _bdot(earlier.astype(BF16), chosen.astype(BF16)) + cnt_scr[0:1, :]
    r1 = jnp.sum(jnp.where(lane == i1, before, 0.0), axis=-1, keepdims=True)
    r2 = jnp.sum(jnp.where(lane == i2, before, 0.0), axis=-1, keepdims=True)
    cols = (i1, i2, w1, w2, r1, r2)
    info = jnp.zeros((tm, ROUTER_LANES), F32)
    for j, col in enumerate(cols):
        info = jnp.where(lane_i == j, col, info)
    info_ref[...] = info
    cnt_scr[...] = cnt_scr[...] + jnp.sum(chosen, axis=0, keepdims=True)
    cnt_ref[...] = cnt_scr[...]


def _route(x2t, g_moe, wr_hi, wr_lo, tm=512):
    t = x2t.shape[0] // TILE_ROWS

    def const(shape):
        return pl.BlockSpec(shape, lambda i: (0,) * len(shape))

    body = functools.partial(_route_body, tm=tm)
    return pl.pallas_call(
        body,
        grid=(t // tm,),
        in_specs=[pl.BlockSpec((tm * TILE_ROWS, LANES), lambda i: (i, 0)), const((1, D_MODEL)),
                  const((D_MODEL, ROUTER_LANES)), const((D_MODEL, ROUTER_LANES))],
        out_specs=[pl.BlockSpec((tm, ROUTER_LANES), lambda i: (i, 0)), const((SUBLANES, ROUTER_LANES))],
        out_shape=[jax.ShapeDtypeStruct((t, ROUTER_LANES), F32),
                   jax.ShapeDtypeStruct((SUBLANES, ROUTER_LANES), F32)],
        scratch_shapes=[pltpu.VMEM((SUBLANES, ROUTER_LANES), F32)],
        compiler_params=_params("arbitrary"),
        name="route",
    )(x2t, g_moe, wr_hi, wr_lo)


def _route_meta(info, cnt, tme):
    t = info.shape[0]
    n_tiles = 2 * t // tme
    counts = cnt[0, N_EXPERT_GROUPS:N_EXPERT_GROUPS + N_EXPERTS].astype(jnp.int32)
    ends = jnp.cumsum(counts)
    starts = ends - counts
    e = info[:, 0:2].astype(jnp.int32) - N_EXPERT_GROUPS
    onehot = e[:, :, None] == jnp.arange(N_EXPERTS, dtype=jnp.int32)
    slot = (jnp.sum(jnp.where(onehot, starts, 0), axis=-1) + info[:, 4:6].astype(jnp.int32)).reshape(-1)
    pts = jnp.concatenate([jnp.arange(n_tiles, dtype=jnp.int32) * tme, starts])
    n = pts.shape[0]
    idx = jnp.arange(n, dtype=jnp.int32)
    pos = jnp.sum((pts[None, :] < pts[:, None]) | ((pts[None, :] == pts[:, None]) & (idx[None, :] < idx[:, None])),
                  axis=1)
    lo_abs = jnp.sum(jnp.where(pos[:, None] == idx[None, :], pts[:, None], 0), axis=0)
    hi_abs = jnp.concatenate([lo_abs[1:], jnp.full((1,), 2 * t, jnp.int32)])
    tile = jnp.minimum(lo_abs // tme, n_tiles - 1)
    expert = jnp.clip(jnp.sum(starts[None, :] <= lo_abs[:, None], axis=1) - 1, 0, N_EXPERTS - 1)
    first = jnp.concatenate([jnp.ones((1,), jnp.int32), (tile[1:] != tile[:-1]).astype(jnp.int32)])
    last = jnp.concatenate([(tile[1:] != tile[:-1]).astype(jnp.int32), jnp.ones((1,), jnp.int32)])
    work = [a.astype(jnp.int32) for a in (tile, expert, lo_abs - tile * tme, hi_abs - tile * tme, first, last)]
    return slot.astype(jnp.int32), work


def _tile_copy(src, src_tok, dst, dst_tok, sem):
    return pltpu.make_async_copy(src.at[pl.ds(pl.multiple_of(src_tok * TILE_ROWS, TILE_ROWS), TILE_ROWS), :],
                                 dst.at[pl.ds(pl.multiple_of(dst_tok * TILE_ROWS, TILE_ROWS), TILE_ROWS), :], sem)


def _scatter_body(slot_ref, x2t_ref, xs_hbm, sem, *, tm):
    def issue(r, c):
        _tile_copy(x2t_ref, r, xs_hbm, slot_ref[2 * r], sem).start(priority=0)
        _tile_copy(x2t_ref, r, xs_hbm, slot_ref[2 * r + 1], sem).start(priority=1)
        return c

    lax.fori_loop(0, tm, issue, 0, unroll=8)

    def drain(r, c):
        _tile_copy(x2t_ref, r, xs_hbm, 0, sem).wait()
        _tile_copy(x2t_ref, r, xs_hbm, 0, sem).wait()
        return c

    lax.fori_loop(0, tm, drain, 0, unroll=8)


def _scatter(slot, x2t, tm=256):
    t = x2t.shape[0] // TILE_ROWS
    body = functools.partial(_scatter_body, tm=tm)
    return pl.pallas_call(
        body,
        grid=(t // tm,),
        in_specs=[pl.BlockSpec((2 * tm,), lambda i: (i,), memory_space=pltpu.SMEM),
                  pl.BlockSpec((tm * TILE_ROWS, LANES), lambda i: (i, 0))],
        out_specs=pl.BlockSpec(memory_space=pl.ANY),
        out_shape=jax.ShapeDtypeStruct((2 * t * TILE_ROWS, LANES), F32),
        scratch_shapes=[pltpu.SemaphoreType.DMA(())],
        compiler_params=_params("arbitrary"),
        name="scatter",
    )(slot, x2t)


def _expert_body(tile_ref, exp_ref, lo_ref, hi_ref, first_ref, last_ref, xs_ref, gm_ref, wg_ref, wu_ref, wd_ref,
                 ys_ref, acc_scr, *, tme):
    k = pl.program_id(0)

    @pl.when(first_ref[k] == 1)
    def _():
        acc_scr[...] = jnp.zeros_like(acc_scr)

    @pl.when(hi_ref[k] > lo_ref[k])
    def _():
        hb = _rms(_tiles_to_rows(xs_ref, tme), gm_ref[...]).astype(BF16)
        act = jax.nn.silu(_bdot(hb, wg_ref[0])) * _bdot(hb, wu_ref[0])
        y = _bdot(act.astype(BF16), wd_ref[0])
        row = lax.broadcasted_iota(jnp.int32, (tme, 1), 0)
        acc_scr[...] += jnp.where((row >= lo_ref[k]) & (row < hi_ref[k]), y, 0.0)

    @pl.when(last_ref[k] == 1)
    def _():
        _rows_to_tiles(ys_ref, acc_scr[...], tme)


def _experts(work, xs, g_moe, wg, wu, wd, tme):
    n_work = work[0].shape[0]
    body = functools.partial(_expert_body, tme=tme)
    tile_blk = pl.BlockSpec((tme * TILE_ROWS, LANES), lambda k, tile, *_: (tile[k], 0))

    def wspec(shape):
        return pl.BlockSpec(shape, lambda k, tile, exp, *_: (exp[k], 0, 0))

    grid_spec = pltpu.PrefetchScalarGridSpec(
        num_scalar_prefetch=len(work),
        grid=(n_work,),
        in_specs=[tile_blk, pl.BlockSpec((1, D_MODEL), lambda k, *_: (0, 0)),
                  wspec((1, D_MODEL, D_FF_EXPERT)), wspec((1, D_MODEL, D_FF_EXPERT)),
                  wspec((1, D_FF_EXPERT, D_MODEL))],
        out_specs=tile_blk,
        scratch_shapes=[pltpu.VMEM((tme, D_MODEL), F32)],
    )
    return pl.pallas_call(
        body,
        grid_spec=grid_spec,
        out_shape=jax.ShapeDtypeStruct(xs.shape, F32),
        compiler_params=_params("arbitrary"),
        name="experts",
    )(*work, xs, g_moe, wg, wu, wd)


def _combine_body(slot_ref, x2t_ref, info_ref, ys_hbm, gf_ref, y_ref, b0, b1, sem, *, tm):
    def issue(r, c):
        _tile_copy(ys_hbm, slot_ref[2 * r], b0, r, sem).start(priority=0)
        _tile_copy(ys_hbm, slot_ref[2 * r + 1], b1, r, sem).start(priority=1)
        return c

    lax.fori_loop(0, tm, issue, 0, unroll=8)

    def drain(r, c):
        _tile_copy(ys_hbm, 0, b0, r, sem).wait()
        _tile_copy(ys_hbm, 0, b1, r, sem).wait()
        return c

    lax.fori_loop(0, tm, drain, 0, unroll=8)
    info = info_ref[...]
    moe = info[:, 2:3] * _tiles_to_rows(b0, tm) + info[:, 3:4] * _tiles_to_rows(b1, tm)
    y_ref[...] = _rms(_tiles_to_rows(x2t_ref, tm) + moe, gf_ref[...])


def _combine(slot, x2t, info, ys, g_final, tm=256):
    t = info.shape[0]
    body = functools.partial(_combine_body, tm=tm)
    return pl.pallas_call(
        body,
        grid=(t // tm,),
        in_specs=[pl.BlockSpec((2 * tm,), lambda i: (i,), memory_space=pltpu.SMEM),
                  pl.BlockSpec((tm * TILE_ROWS, LANES), lambda i: (i, 0)),
                  pl.BlockSpec((tm, ROUTER_LANES), lambda i: (i, 0)),
                  pl.BlockSpec(memory_space=pl.ANY),
                  pl.BlockSpec((1, D_MODEL), lambda i: (0, 0))],
        out_specs=pl.BlockSpec((tm, D_MODEL), lambda i: (i, 0)),
        out_shape=jax.ShapeDtypeStruct((t, D_MODEL), F32),
        scratch_shapes=[pltpu.VMEM((tm * TILE_ROWS, LANES), F32), pltpu.VMEM((tm * TILE_ROWS, LANES), F32),
                        pltpu.SemaphoreType.DMA(())],
        compiler_params=_params("arbitrary"),
        name="combine",
    )(slot, x2t, info, ys, g_final)


def _moe(x2t, w, tme):
    info, cnt = _route(x2t, w["g_moe"], w["wr_hi"], w["wr_lo"])
    slot, work = _route_meta(info, cnt, tme)
    xs = _scatter(slot, x2t)
    ys = _experts(work, xs, w["g_moe"], w["moe_wg"], w["moe_wu"], w["moe_wd"], tme)
    return _combine(slot, x2t, info, ys, w["g_final"])


def _layer(x3d, k4, v4, s5_h0, lru_h0, lru_conv, w, chunk, nb, tq, tme):
    bsz, seq, _ = x3d.shape
    x2d = x3d.reshape(bsz * seq, D_MODEL)
    u, xl, gl = _mixin(x2d, w["g_mix"], w["w_in"])

    nstate = S5_GROUPS * S5_STATE
    if s5_h0 is None:
        zero = jnp.zeros((bsz, 1, nstate), F32)
        ys, s5_re, s5_im = _s5(u, w["s5"][chunk], zero, zero, chunk, seq // chunk, True)
    else:
        ys, s5_re, s5_im = _s5(u, w["s5"][chunk], s5_h0[0].reshape(bsz, nstate), s5_h0[1].reshape(bsz, nstate),
                               chunk, bsz, False)
    s5_re = s5_re.reshape(bsz, S5_GROUPS, S5_STATE)
    s5_im = s5_im.reshape(bsz, S5_GROUPS, S5_STATE)

    lru_w = (w["conv_w"], w["conv_b"], w["lru_wg"], w["lru_bg"], w["lru_lam"])
    if lru_h0 is None:
        yl, hl = _lru_seq(xl, gl, *lru_w, bsz, seq)
        lru_h = hl.reshape(bsz, SUBLANES, D_LRU)[:, 0]
    else:
        hist = jnp.pad(lru_conv, ((0, 0), (SUBLANES - (CONV_W - 1), 0), (0, 0))).reshape(bsz * seq, D_LRU)
        h0rep = jnp.repeat(lru_h0, seq, axis=0)
        yl, hall = _lru_dec(xl, gl, hist, h0rep, *lru_w)
        lru_h = hall.reshape(bsz, seq, D_LRU)[:, seq - 1]
    conv_new = xl.reshape(bsz, seq, D_LRU)[:, seq - (CONV_W - 1):]

    x1, q = _mixout(x2d, ys, yl, w["w_glu"], w["b_glu"], w["w_out"], w["g_xa"], w["w_q"])
    x2t = _attn(q.reshape(bsz, seq, D_MODEL), x1.reshape(bsz, seq, D_MODEL), k4, v4, w["w_o"], nb, tq)
    y = _moe(x2t, w, tme)
    return y.reshape(bsz, seq, D_MODEL), s5_re, s5_im, lru_h, conv_new


def kernel(x_prompt, x_sample, mem_prompt, cache_mem_k, cache_mem_v, state_s5_re, state_s5_im, state_lru_h, state_lru_conv, g_mix, w_in, s5_lam_re, s5_lam_im, s5_log_dt, s5_b_re, s5_b_im, s5_c_re, s5_c_im, s5_d, s5_w_glu, s5_b_glu, lru_conv_w, lru_conv_b, lru_w_a, lru_b_a, lru_w_x, lru_b_x, lru_lam, w_out, g_xa, g_mem, xa_w_q, xa_w_k, xa_w_v, xa_w_o, g_moe, moe_w_group, moe_w_expert, moe_w_gate, moe_w_up, moe_w_down, g_final):
    depth = g_mix.shape[0]
    assert depth == 1, "single-layer step"
    l = 0
    bsz, seq, _ = x_prompt.shape
    dbsz, dseq, _ = x_sample.shape
    chunk_p, chunk_s = 16, dseq
    assert seq % chunk_p == 0 and dseq == SUBLANES

    s5_args = (s5_lam_re[l], s5_lam_im[l], s5_log_dt[l], s5_b_re[l], s5_b_im[l], s5_c_re[l], s5_c_im[l], s5_d[l])
    wg, bg = _lru_weights(lru_w_a[l], lru_w_x[l], lru_b_a[l], lru_b_x[l])
    wr = jnp.concatenate([moe_w_group[l], moe_w_expert[l],
                          jnp.zeros((D_MODEL, ROUTER_LANES - N_EXPERT_GROUPS - N_EXPERTS), F32)], axis=1)
    wr_hi = wr.astype(BF16)
    wr_lo = (wr - wr_hi.astype(F32)).astype(BF16)
    w = {
        "g_mix": g_mix[l][None], "w_in": w_in[l].astype(BF16),
        "s5": {c: _s5_weights(c, *s5_args) for c in {chunk_p, chunk_s}},
        "conv_w": lru_conv_w[l], "conv_b": lru_conv_b[l][None], "lru_wg": wg, "lru_bg": bg,
        "lru_lam": lru_lam[l][None],
        "w_glu": s5_w_glu[l].astype(BF16), "b_glu": s5_b_glu[l][None], "w_out": w_out[l].astype(BF16),
        "g_xa": g_xa[l][None], "w_q": xa_w_q[l].astype(BF16), "w_o": xa_w_o[l].astype(BF16),
        "g_moe": g_moe[l][None], "wr_hi": wr_hi, "wr_lo": wr_lo,
        "moe_wg": moe_w_gate[l].astype(BF16), "moe_wu": moe_w_up[l].astype(BF16),
        "moe_wd": moe_w_down[l].astype(BF16), "g_final": g_final[None],
    }

    mk, mv = _memkv(mem_prompt.reshape(bsz * N_MEM, D_MODEL), g_mem[l][None],
                    xa_w_k[l].astype(BF16), xa_w_v[l].astype(BF16))
    yp, p_re, p_im, p_h, p_conv = _layer(x_prompt, mk.reshape(bsz, N_MEM, D_MODEL), mv.reshape(bsz, N_MEM, D_MODEL),
                                         None, None, None, w, chunk_p, nb=1, tq=512, tme=256)
    ysmp, s_re, s_im, s_h, s_conv = _layer(x_sample, cache_mem_k[l], cache_mem_v[l],
                                           (state_s5_re[l], state_s5_im[l]), state_lru_h[l],
                                           state_lru_conv[l], w, chunk_s, nb=4, tq=dseq, tme=128)

    return (yp, ysmp,
            mk.reshape(1, bsz, N_MEM, XA_HEADS, XA_HEAD_DIM), mv.reshape(1, bsz, N_MEM, XA_HEADS, XA_HEAD_DIM),
            p_re[None], p_im[None], p_h[None], p_conv[None],
            s_re[None], s_im[None], s_h[None], s_conv[None])
```

```python
import functools

import jax
import jax.numpy as jnp
import numpy as np
from jax import lax
from jax.experimental import pallas as pl
from jax.experimental.pallas import tpu as pltpu

F32 = jnp.float32
BF16 = jnp.bfloat16

D_MODEL = 1024
D_S5 = 512
D_LRU = 512
S5_GROUP = 16
S5_GROUPS = 32
S5_STATE = 64
LRU_HEADS = 8
LRU_HEAD_DIM = 64
CONV_W = 4
LRU_C = 8.0
N_MEM = 256
XA_HEADS = 4
XA_HEAD_DIM = 256
N_EXPERT_GROUPS = 4
EXPERTS_PER_GROUP = 8
N_EXPERTS = 32
D_FF_EXPERT = 256
EPS = 1e-6

SUBLANES = 8
LANES = 128
ROUTER_LANES = 128
VMEM_LIMIT = 48 * 1024 * 1024


def _params(*sem):
    return pltpu.CompilerParams(dimension_semantics=sem, vmem_limit_bytes=VMEM_LIMIT)


def _rms(x, g):
    ms = jnp.mean(x * x, axis=-1, keepdims=True)
    return x * lax.rsqrt(ms + EPS) * g


def _bdot(a, b):
    return jnp.dot(a, b, preferred_element_type=F32)


def _memkv_body(m_ref, g_ref, wk_ref, wv_ref, k_ref, v_ref):
    hb = _rms(m_ref[...], g_ref[...]).astype(BF16)
    k_ref[...] = _bdot(hb, wk_ref[...])
    v_ref[...] = _bdot(hb, wv_ref[...])


def _memkv(mem2d, g_mem, wk, wv, tm=512):
    t = mem2d.shape[0]
    row = pl.BlockSpec((tm, D_MODEL), lambda i: (i, 0))
    full = pl.BlockSpec((D_MODEL, D_MODEL), lambda i: (0, 0))
    return pl.pallas_call(
        _memkv_body,
        grid=(t // tm,),
        in_specs=[row, pl.BlockSpec((1, D_MODEL), lambda i: (0, 0)), full, full],
        out_specs=[row, row],
        out_shape=[jax.ShapeDtypeStruct((t, D_MODEL), F32)] * 2,
        compiler_params=_params("parallel"),
        name="memkv",
    )(mem2d, g_mem, wk, wv)


def _mixin_body(x_ref, g_ref, w_ref, u_ref, xl_ref, gl_ref):
    hb = _rms(x_ref[...], g_ref[...]).astype(BF16)
    proj = _bdot(hb, w_ref[...])
    u_ref[...] = proj[:, :D_S5]
    xl_ref[...] = proj[:, D_S5:D_S5 + D_LRU]
    gl_ref[...] = proj[:, D_S5 + D_LRU:]


def _mixin(x2d, g_mix, w_in, tm=512):
    t = x2d.shape[0]
    half = pl.BlockSpec((tm, D_S5), lambda i: (i, 0))
    return pl.pallas_call(
        _mixin_body,
        grid=(t // tm,),
        in_specs=[pl.BlockSpec((tm, D_MODEL), lambda i: (i, 0)),
                  pl.BlockSpec((1, D_MODEL), lambda i: (0, 0)),
                  pl.BlockSpec((D_MODEL, D_S5 + 2 * D_LRU), lambda i: (0, 0))],
        out_specs=[half, half, half],
        out_shape=[jax.ShapeDtypeStruct((t, D_S5), F32)] * 3,
        compiler_params=_params("parallel"),
        name="mixin",
    )(x2d, g_mix, w_in)


GROUPS_PER_BLOCK = LANES // S5_GROUP
PAIRS_PER_BLOCK = GROUPS_PER_BLOCK // 2
STATE_BLOCK = GROUPS_PER_BLOCK * S5_STATE
PW_ROWS = 24


def _s5_body(u_ref, wa_ref, wh_ref, pw_ref, d_ref, h0r_ref, h0i_ref, y_ref, hfr_ref, hfi_ref, hpr_scr, hpi_scr,
             *, chunk, nrow, scan):
    nh = chunk // SUBLANES
    lc = chunk * S5_GROUP
    slot = lax.broadcasted_iota(jnp.int32, (nrow, LANES), 1) // S5_GROUP
    in_slot = [slot == s for s in range(GROUPS_PER_BLOCK)]

    def pick(src, sel):
        out = src[sel(0)]
        for s in range(1, GROUPS_PER_BLOCK):
            out = jnp.where(in_slot[s], src[sel(s)], out)
        return out

    nat, skew = [], []
    for t in range(chunk):
        a = u_ref[pl.ds(t, nrow, stride=chunk), :]
        nat.append(a)
        s = t % GROUPS_PER_BLOCK
        skew.append(pltpu.roll(a, s * S5_GROUP, axis=1) if s else a)
    def lane_roll(x, slots):
        slots %= GROUPS_PER_BLOCK
        return pltpu.roll(x, slots * S5_GROUP, axis=1) if slots else x

    z = [[lane_roll(pick(skew, lambda s, hh=hh, m=m: SUBLANES * hh + (s - m) % GROUPS_PER_BLOCK), -m)
          for hh in range(nh)] for m in range(GROUPS_PER_BLOCK)]

    ys, er, ei = [], [], []
    for p in range(PAIRS_PER_BLOCK):
        lhs = jnp.concatenate(z[2 * p] + z[2 * p + 1], axis=1).astype(BF16)
        full = _bdot(lhs, wa_ref[p])
        ys.append(full[:, :2 * lc])
        er.append(full[:, 2 * lc:2 * lc + LANES])
        ei.append(full[:, 2 * lc + LANES:])
    er = jnp.concatenate(er, axis=1)
    ei = jnp.concatenate(ei, axis=1)

    if scan:
        ntile = nrow // SUBLANES
        xr = er.reshape(ntile, SUBLANES, STATE_BLOCK)
        xi = ei.reshape(ntile, SUBLANES, STATE_BLOCK)
        row = lax.broadcasted_iota(jnp.int32, xr.shape, 1)
        for k, d in enumerate((1, 2, 4)):
            pr, pi = pw_ref[0, k:k + 1, :], pw_ref[0, 4 + k:5 + k, :]
            sr, si = pltpu.roll(xr, d, axis=1), pltpu.roll(xi, d, axis=1)
            m = row >= d
            xr, xi = jnp.where(m, xr + pr * sr - pi * si, xr), jnp.where(m, xi + pr * si + pi * sr, xi)
        sxr = jnp.where(row >= 1, pltpu.roll(xr, 1, axis=1), 0.0)
        sxi = jnp.where(row >= 1, pltpu.roll(xi, 1, axis=1), 0.0)
        p8r, p8i = pw_ref[0, 3:4, :], pw_ref[0, 7:8, :]
        qr, qi = pw_ref[0, 8:16, :], pw_ref[0, 16:24, :]
        hr, hi = h0r_ref[0], h0i_ref[0]
        for t in range(ntile):
            rows = slice(SUBLANES * t, SUBLANES * (t + 1))
            hpr_scr[rows, :] = sxr[t] + qr * hr - qi * hi
            hpi_scr[rows, :] = sxi[t] + qr * hi + qi * hr
            hr, hi = (xr[t, SUBLANES - 1:] + p8r * hr - p8i * hi, xi[t, SUBLANES - 1:] + p8r * hi + p8i * hr)
        hfr_ref[0] = hr
        hfi_ref[0] = hi
        hpr, hpi = hpr_scr[...], hpi_scr[...]
    else:
        hpr, hpi = h0r_ref[...], h0i_ref[...]
        p1r, p1i = pw_ref[0, 0:1, :], pw_ref[0, 4:5, :]
        hfr_ref[...] = er + p1r * hpr - p1i * hpi
        hfi_ref[...] = ei + p1r * hpi + p1i * hpr

    yg = []
    for p in range(PAIRS_PER_BLOCK):
        lanes = slice(LANES * p, LANES * (p + 1))
        hp = jnp.concatenate([hpr[:, lanes], hpi[:, lanes]], axis=1).astype(BF16)
        out = ys[p] + _bdot(hp, wh_ref[p])
        for half in range(2):
            q = 2 * p + half
            yg.append([lane_roll(out[:, half * lc + hh * LANES:half * lc + (hh + 1) * LANES], q)
                       for hh in range(nh)])
    d = d_ref[...]
    for t in range(chunk):
        hh, s = divmod(t, GROUPS_PER_BLOCK)
        c = lane_roll(pick([g[hh] for g in yg], lambda sl, s=s: (sl - s) % GROUPS_PER_BLOCK), -s)
        y_ref[pl.ds(t, nrow, stride=chunk), :] = c + d * nat[t]


def _s5(u2d, weights, h0r, h0i, chunk, nrow, scan):
    wa, wh, pw, dvec = weights
    t = u2d.shape[0]
    lc = chunk * S5_GROUP
    rows = nrow * chunk
    nblk = D_S5 // LANES
    body = functools.partial(_s5_body, chunk=chunk, nrow=nrow, scan=scan)
    ublk = pl.BlockSpec((rows, LANES), lambda i, j: (i, j))
    if scan:
        hblk = pl.BlockSpec((1, 1, STATE_BLOCK), lambda i, j: (i, 0, j))
        hshape = jax.ShapeDtypeStruct((t // rows, 1, S5_GROUPS * S5_STATE), F32)
    else:
        hblk = pl.BlockSpec((nrow, STATE_BLOCK), lambda i, j: (i, j))
        hshape = jax.ShapeDtypeStruct((t // chunk, S5_GROUPS * S5_STATE), F32)
    return pl.pallas_call(
        body,
        grid=(t // rows, nblk),
        in_specs=[ublk,
                  pl.BlockSpec((PAIRS_PER_BLOCK, 2 * lc, 2 * lc + 2 * LANES), lambda i, j: (j, 0, 0)),
                  pl.BlockSpec((PAIRS_PER_BLOCK, 2 * LANES, 2 * lc), lambda i, j: (j, 0, 0)),
                  pl.BlockSpec((1, PW_ROWS, STATE_BLOCK), lambda i, j: (j, 0, 0)),
                  pl.BlockSpec((1, LANES), lambda i, j: (0, j)),
                  hblk, hblk],
        out_specs=[ublk, hblk, hblk],
        out_shape=[jax.ShapeDtypeStruct((t, D_S5), F32), hshape, hshape],
        scratch_shapes=[pltpu.VMEM((nrow, STATE_BLOCK), F32), pltpu.VMEM((nrow, STATE_BLOCK), F32)],
        compiler_params=_params("parallel", "parallel"),
        name="s5",
    )(u2d, wa, wh, pw, dvec, h0r, h0i)


def _s5_weights(chunk, lam_re, lam_im, log_dt, b_re, b_im, c_re, c_im, d_skip):
    hi = lax.Precision.HIGHEST
    dt = jnp.exp(log_dt)[:, None]
    mag = jnp.exp(lam_re * dt)
    ab_re = mag * jnp.cos(lam_im * dt)
    ab_im = mag * jnp.sin(lam_im * dt)
    den = lam_re * lam_re + lam_im * lam_im
    nr = ab_re - 1.0
    f_re = (nr * lam_re + ab_im * lam_im) / den
    f_im = (ab_im * lam_re - nr * lam_im) / den
    bb_re = f_re[..., None] * b_re - f_im[..., None] * b_im
    bb_im = f_re[..., None] * b_im + f_im[..., None] * b_re
    def powers(ks):
        k = jnp.asarray(np.asarray(ks, np.float32))[:, None, None]
        m = jnp.exp(k * (lam_re * dt))
        return m * jnp.cos(k * (lam_im * dt)), m * jnp.sin(k * (lam_im * dt))

    pw_re, pw_im = powers(range(chunk + 1))
    pb_re = pw_re[:chunk, ..., None] * bb_re - pw_im[:chunk, ..., None] * bb_im
    pb_im = pw_re[:chunk, ..., None] * bb_im + pw_im[:chunk, ..., None] * bb_re
    kk = (jnp.einsum("gon,kgni->kgoi", c_re, pb_re, precision=hi)
          - jnp.einsum("gon,kgni->kgoi", c_im, pb_im, precision=hi))
    npair = S5_GROUPS // 2
    lc = chunk * S5_GROUP
    row0 = kk.transpose(1, 3, 0, 2).reshape(S5_GROUPS, S5_GROUP, lc)
    row0 = jnp.pad(row0, ((0, 0), (0, 0), (lc, 0)))
    conv = jnp.stack([row0[:, :, lc - S5_GROUP * ti:2 * lc - S5_GROUP * ti] for ti in range(chunk)], axis=1)
    conv = conv.reshape(npair, 2, lc, lc)
    ends = [x[::-1].transpose(1, 0, 3, 2).reshape(npair, 2, lc, S5_STATE) for x in (pb_re, pb_im)]
    zc = jnp.zeros((npair, lc, lc), F32)
    zs = jnp.zeros((npair, lc, S5_STATE), F32)
    wa = jnp.concatenate([
        jnp.concatenate([conv[:, 0], zc, ends[0][:, 0], zs, ends[1][:, 0], zs], axis=-1),
        jnp.concatenate([zc, conv[:, 1], zs, ends[0][:, 1], zs, ends[1][:, 1]], axis=-1)], axis=1)
    gr = c_re[None] * pw_re[1:, :, None, :] - c_im[None] * pw_im[1:, :, None, :]
    gi = c_re[None] * pw_im[1:, :, None, :] + c_im[None] * pw_re[1:, :, None, :]
    zr = jnp.zeros((npair, S5_STATE, lc), F32)
    blocks = []
    for x in (gr, -gi):
        hy = x.transpose(1, 3, 0, 2).reshape(npair, 2, S5_STATE, lc)
        blocks += [jnp.concatenate([hy[:, 0], zr], axis=-1), jnp.concatenate([zr, hy[:, 1]], axis=-1)]
    wh = jnp.concatenate(blocks, axis=1)
    tab = [1, 2, 4, 8] + list(range(SUBLANES))
    qr, qi = powers([chunk * k for k in tab])
    nblk = D_S5 // LANES
    pw = jnp.concatenate([qr[:4], qi[:4], qr[4:], qi[4:]]).reshape(PW_ROWS, nblk, STATE_BLOCK).transpose(1, 0, 2)
    return wa.astype(BF16), wh.astype(BF16), pw, d_skip.reshape(1, D_S5)


def _lru_ab(xc, wg_ref, bg_ref, lam_ref):
    xb = xc.astype(BF16)
    half = D_LRU // 2
    g0 = _bdot(xb[:, :half], wg_ref[0]) + bg_ref[0]
    g1 = _bdot(xb[:, half:], wg_ref[1]) + bg_ref[1]
    r = jax.nn.sigmoid(jnp.concatenate([g0[:, :half], g1[:, :half]], axis=1))
    ig = jax.nn.sigmoid(jnp.concatenate([g0[:, half:], g1[:, half:]], axis=1))
    zl = -lam_ref[...]
    softplus = jnp.maximum(zl, 0.0) + jnp.log1p(jnp.exp(-jnp.abs(zl)))
    log_a = -LRU_C * r * softplus
    a = jnp.exp(log_a)
    b = jnp.sqrt(-jnp.tanh(log_a) * (a * a + 1.0)) * (ig * xc)
    return a, b


def _tile_scan(a, b):
    shape = a.shape
    a = a.reshape(shape[0] // SUBLANES, SUBLANES, shape[1])
    b = b.reshape(a.shape)
    row = lax.broadcasted_iota(jnp.int32, a.shape, 1)
    for d in (1, 2, 4):
        a_prev = pltpu.roll(a, d, axis=1)
        b_prev = pltpu.roll(b, d, axis=1)
        m = row >= d
        b = jnp.where(m, b + a * b_prev, b)
        a = jnp.where(m, a * a_prev, a)
    return a.reshape(shape), b.reshape(shape)


def _lru_seq_body(xl_ref, gl_ref, cw_ref, cb_ref, wg_ref, bg_ref, lam_ref, y_ref, hl_ref,
                  xp_scr, a_scr, b_scr, h_scr, hc_scr, *, ts):
    ti = pl.program_id(1)

    @pl.when(ti == 0)
    def _():
        xp_scr[0:SUBLANES, :] = jnp.zeros((SUBLANES, D_LRU), F32)
        hc_scr[...] = jnp.zeros((SUBLANES, D_LRU), F32)

    xl = xl_ref[...]
    xp_scr[SUBLANES:SUBLANES + ts, :] = xl
    xc = cb_ref[...] + xl * cw_ref[CONV_W - 1:CONV_W, :]
    for j in range(1, CONV_W):
        xc = xc + xp_scr[SUBLANES - j:SUBLANES - j + ts, :] * cw_ref[CONV_W - 1 - j:CONV_W - j, :]
    xp_scr[0:SUBLANES, :] = xl[ts - SUBLANES:, :]
    a, b = _lru_ab(xc, wg_ref, bg_ref, lam_ref)
    a, b = _tile_scan(a, b)
    a_scr[...] = a
    b_scr[...] = b

    def step(i, hin):
        rows = pl.ds(pl.multiple_of(i * SUBLANES, SUBLANES), SUBLANES)
        h = b_scr[rows, :] + a_scr[rows, :] * hin
        h_scr[rows, :] = h
        return h[SUBLANES - 1:SUBLANES, :]

    hlast = lax.fori_loop(0, ts // SUBLANES, step, hc_scr[0:1, :], unroll=4)
    hc_scr[...] = jnp.broadcast_to(hlast, (SUBLANES, D_LRU))
    hl_ref[...] = hc_scr[...]
    y_ref[...] = h_scr[...] * jax.nn.gelu(gl_ref[...])


def _lru_dec_body(xl_ref, gl_ref, hist_ref, h0_ref, cw_ref, cb_ref, wg_ref, bg_ref, lam_ref, y_ref, hl_ref, *, tm):
    xl = xl_ref[...]
    hist = hist_ref[...]
    row = lax.broadcasted_iota(jnp.int32, xl.shape, 0) % SUBLANES
    xc = cb_ref[...] + xl * cw_ref[CONV_W - 1:CONV_W, :]
    for j in range(1, CONV_W):
        prev = jnp.where(row >= j, pltpu.roll(xl, j, axis=0), pltpu.roll(hist, tm - SUBLANES + j, axis=0))
        xc = xc + prev * cw_ref[CONV_W - 1 - j:CONV_W - j, :]
    a, b = _lru_ab(xc, wg_ref, bg_ref, lam_ref)
    a, b = _tile_scan(a, b)
    h = b + a * h0_ref[...]
    hl_ref[...] = h
    y_ref[...] = h * jax.nn.gelu(gl_ref[...])


def _lru_weights(w_a, w_x, b_a, b_x):
    eye = jnp.eye(LRU_HEADS, dtype=F32)
    bd_a = jnp.einsum("hij,hg->higj", w_a, eye).reshape(D_LRU, D_LRU)
    bd_x = jnp.einsum("hij,hg->higj", w_x, eye).reshape(D_LRU, D_LRU)
    half = D_LRU // 2
    wg = jnp.stack([jnp.concatenate([bd_a[:half, :half], bd_x[:half, :half]], axis=1),
                    jnp.concatenate([bd_a[half:, half:], bd_x[half:, half:]], axis=1)]).astype(BF16)
    ba = b_a.reshape(1, D_LRU)
    bx = b_x.reshape(1, D_LRU)
    bg = jnp.stack([jnp.concatenate([ba[:, :half], bx[:, :half]], axis=1),
                    jnp.concatenate([ba[:, half:], bx[:, half:]], axis=1)])
    return wg, bg


def _lru_common_specs(const):
    return [const((CONV_W, D_LRU)), const((1, D_LRU)), const((2, D_LRU // 2, D_LRU)),
            const((2, 1, D_LRU)), const((1, D_LRU))]


def _lru_seq(xl, gl, cw, cb, wg, bg, lam, bsz, seq, ts=512):
    nt = seq // ts
    blk = pl.BlockSpec((ts, D_LRU), lambda b, t: (b * nt + t, 0))

    def const(shape):
        return pl.BlockSpec(shape, lambda b, t: (0,) * len(shape))

    body = functools.partial(_lru_seq_body, ts=ts)
    return pl.pallas_call(
        body,
        grid=(bsz, nt),
        in_specs=[blk, blk] + _lru_common_specs(const),
        out_specs=[blk, pl.BlockSpec((SUBLANES, D_LRU), lambda b, t: (b, 0))],
        out_shape=[jax.ShapeDtypeStruct((bsz * seq, D_LRU), F32),
                   jax.ShapeDtypeStruct((bsz * SUBLANES, D_LRU), F32)],
        scratch_shapes=[pltpu.VMEM((ts + SUBLANES, D_LRU), F32), pltpu.VMEM((ts, D_LRU), F32),
                        pltpu.VMEM((ts, D_LRU), F32), pltpu.VMEM((ts, D_LRU), F32),
                        pltpu.VMEM((SUBLANES, D_LRU), F32)],
        compiler_params=_params("arbitrary", "arbitrary"),
        name="lru_seq",
    )(xl, gl, cw, cb, wg, bg, lam)


def _lru_dec(xl, gl, hist, h0rep, cw, cb, wg, bg, lam, tm=256):
    t = xl.shape[0]
    blk = pl.BlockSpec((tm, D_LRU), lambda i: (i, 0))

    def const(shape):
        return pl.BlockSpec(shape, lambda i: (0,) * len(shape))

    body = functools.partial(_lru_dec_body, tm=tm)
    return pl.pallas_call(
        body,
        grid=(t // tm,),
        in_specs=[blk, blk, blk, blk] + _lru_common_specs(const),
        out_specs=[blk, blk],
        out_shape=[jax.ShapeDtypeStruct((t, D_LRU), F32)] * 2,
        compiler_params=_params("parallel"),
        name="lru_dec",
    )(xl, gl, hist, h0rep, cw, cb, wg, bg, lam)


def _mixout_body(x_ref, ys_ref, yl_ref, wglu_ref, bglu_ref, wo_ref, gxa_ref, wq_ref, x1_ref, q_ref):
    ys = jax.nn.gelu(ys_ref[...])
    gate = jax.nn.sigmoid(_bdot(ys.astype(BF16), wglu_ref[...]) + bglu_ref[...])
    s5 = (ys * gate).astype(BF16)
    x1 = (x_ref[...] + _bdot(s5, wo_ref[0:D_S5, :]) + _bdot(yl_ref[...].astype(BF16), wo_ref[D_S5:, :]))
    x1_ref[...] = x1
    q_ref[...] = _bdot(_rms(x1, gxa_ref[...]).astype(BF16), wq_ref[...]).astype(BF16)


def _mixout(x2d, ys, yl, w_glu, b_glu, w_out, g_xa, w_q, tm=512):
    t = x2d.shape[0]
    row = pl.BlockSpec((tm, D_MODEL), lambda i: (i, 0))
    half = pl.BlockSpec((tm, D_S5), lambda i: (i, 0))

    def const(shape):
        return pl.BlockSpec(shape, lambda i: (0,) * len(shape))

    return pl.pallas_call(
        _mixout_body,
        grid=(t // tm,),
        in_specs=[row, half, half, const((D_S5, D_S5)), const((1, D_S5)), const((D_MODEL, D_MODEL)),
                  const((1, D_MODEL)), const((D_MODEL, D_MODEL))],
        out_specs=[row, row],
        out_shape=[jax.ShapeDtypeStruct((t, D_MODEL), F32), jax.ShapeDtypeStruct((t, D_MODEL), BF16)],
        compiler_params=_params("parallel"),
        name="mixout",
    )(x2d, ys, yl, w_glu, b_glu, w_out, g_xa, w_q)


def _softmax(sc):
    p = jnp.exp(sc - jnp.max(sc, axis=-1, keepdims=True))
    return p / jnp.sum(p, axis=-1, keepdims=True)


def _attn_body(q_ref, x1_ref, k_ref, v_ref, wo_ref, x2t_ref, *, nb, tq):
    scale = XA_HEAD_DIM ** -0.5
    for i in range(nb):
        q = q_ref[i]
        heads = [q[:, h * XA_HEAD_DIM:(h + 1) * XA_HEAD_DIM] for h in range(XA_HEADS)]
        if len(k_ref.shape) == 4:
            kf = k_ref[i].reshape(N_MEM * XA_HEADS, XA_HEAD_DIM).astype(BF16)
            vf = v_ref[i].reshape(N_MEM * XA_HEADS, XA_HEAD_DIM).astype(BF16)
            sc = lax.dot_general(jnp.concatenate(heads, axis=0), kf, (((1,), (1,)), ((), ())),
                                 preferred_element_type=F32) * scale
            own = (lax.broadcasted_iota(jnp.int32, sc.shape, 1) % XA_HEADS
                   == lax.broadcasted_iota(jnp.int32, sc.shape, 0) // tq)
            oh = _bdot(_softmax(jnp.where(own, sc, -jnp.inf)).astype(BF16), vf)
            outs = [oh[h * tq:(h + 1) * tq] for h in range(XA_HEADS)]
        else:
            kb = k_ref[i].astype(BF16)
            vb = v_ref[i].astype(BF16)
            outs = []
            for h in range(XA_HEADS):
                cols = slice(h * XA_HEAD_DIM, (h + 1) * XA_HEAD_DIM)
                sc = lax.dot_general(heads[h], kb[:, cols], (((1,), (1,)), ((), ())),
                                     preferred_element_type=F32) * scale
                outs.append(_bdot(_softmax(sc).astype(BF16), vb[:, cols]))
        o = jnp.concatenate(outs, axis=1).astype(BF16)
        _rows_to_tiles(x2t_ref, x1_ref[i] + _bdot(o, wo_ref[...]), tq, base=i * tq * TILE_ROWS)


def _attn(q3, x13, k4, v4, w_o, nb, tq):
    bsz, seq, _ = q3.shape
    nt = seq // tq
    blk = pl.BlockSpec((nb, tq, D_MODEL), lambda b, t: (b, t, 0))
    kv = pl.BlockSpec((nb,) + k4.shape[1:], lambda b, t: (b,) + (0,) * (k4.ndim - 1))
    body = functools.partial(_attn_body, nb=nb, tq=tq)
    return pl.pallas_call(
        body,
        grid=(bsz // nb, nt),
        in_specs=[blk, blk, kv, kv, pl.BlockSpec((D_MODEL, D_MODEL), lambda b, t: (0, 0))],
        out_specs=pl.BlockSpec((nb * tq * TILE_ROWS, LANES), lambda b, t: (b * nt + t, 0)),
        out_shape=jax.ShapeDtypeStruct((bsz * seq * TILE_ROWS, LANES), F32),
        compiler_params=_params("parallel", "parallel"),
        name="attn",
    )(q3, x13, k4, v4, w_o)


def _router(hm, wr_hi_ref, wr_lo_ref):
    a_hi = hm.astype(BF16)
    a_lo = (hm - a_hi.astype(F32)).astype(BF16)
    logits = _bdot(a_hi, wr_hi_ref[...]) + (_bdot(a_hi, wr_lo_ref[...]) + _bdot(a_lo, wr_hi_ref[...]))
    lane_i = lax.broadcasted_iota(jnp.int32, logits.shape, 1)
    lane = lane_i.astype(F32)
    neg = -jnp.inf
    big = float(ROUTER_LANES)
    is_g = lane_i < N_EXPERT_GROUPS
    glog = jnp.where(is_g, logits, neg)
    gmax = jnp.max(glog, axis=-1, keepdims=True)
    gsel = jnp.min(jnp.where(glog == gmax, lane, big), axis=-1, keepdims=True)
    pg_sel = 1.0 / jnp.sum(jnp.where(is_g, jnp.exp(logits - gmax), 0.0), axis=-1, keepdims=True)
    eidx = lane_i - N_EXPERT_GROUPS
    in_group = (eidx >= 0) & (eidx < N_EXPERTS) & ((eidx >> 3).astype(F32) == gsel)
    el = jnp.where(in_group, logits, neg)
    v1 = jnp.max(el, axis=-1, keepdims=True)
    i1 = jnp.min(jnp.where(el == v1, lane, big), axis=-1, keepdims=True)
    el2 = jnp.where(lane == i1, neg, el)
    v2 = jnp.max(el2, axis=-1, keepdims=True)
    i2 = jnp.min(jnp.where(el2 == v2, lane, big), axis=-1, keepdims=True)
    e2 = jnp.exp(v2 - v1)
    w1 = pg_sel / (1.0 + e2)
    w2 = pg_sel * e2 / (1.0 + e2)
    return i1, i2, w1, w2


TILE_ROWS = D_MODEL // LANES


def _tiles_to_rows(ref, n, base=0):
    return jnp.concatenate([ref[pl.ds(base + s, n, stride=TILE_ROWS), :] for s in range(TILE_ROWS)], axis=1)


def _rows_to_tiles(ref, val, n, base=0):
    for s in range(TILE_ROWS):
        ref[pl.ds(base + s, n, stride=TILE_ROWS), :] = val[:, s * LANES:(s + 1) * LANES]


def _route_body(x2t_ref, gm_ref, wr_hi_ref, wr_lo_ref, info_ref, cnt_ref, cnt_scr, *, tm):
    @pl.when(pl.program_id(0) == 0)
    def _():
        cnt_scr[...] = jnp.zeros_like(cnt_scr)

    hm = _rms(_tiles_to_rows(x2t_ref, tm), gm_ref[...])
    i1, i2, w1, w2 = _router(hm, wr_hi_ref, wr_lo_ref)
    lane_i = lax.broadcasted_iota(jnp.int32, (tm, ROUTER_LANES), 1)
    lane = lane_i.astype(F32)
    chosen = ((lane == i1) | (lane == i2)).astype(F32)
    earlier = (lax.broadcasted_iota(jnp.int32, (tm, tm), 1) < lax.broadcasted_iota(jnp.int32, (tm, tm), 0))
    before = _bdot(earlier.astype(BF16), chosen.astype(BF16)) + cnt_scr[0:1, :]
    r1 = jnp.sum(jnp.where(lane == i1, before, 0.0), axis=-1, keepdims=True)
    r2 = jnp.sum(jnp.where(lane == i2, before, 0.0), axis=-1, keepdims=True)
    cols = (i1, i2, w1, w2, r1, r2)
    info = jnp.zeros((tm, ROUTER_LANES), F32)
    for j, col in enumerate(cols):
        info = jnp.where(lane_i == j, col, info)
    info_ref[...] = info
    cnt_scr[...] = cnt_scr[...] + jnp.sum(chosen, axis=0, keepdims=True)
    cnt_ref[...] = cnt_scr[...]


def _route(x2t, g_moe, wr_hi, wr_lo, tm=512):
    t = x2t.shape[0] // TILE_ROWS

    def const(shape):
        return pl.BlockSpec(shape, lambda i: (0,) * len(shape))

    body = functools.partial(_route_body, tm=tm)
    return pl.pallas_call(
        body,
        grid=(t // tm,),
        in_specs=[pl.BlockSpec((tm * TILE_ROWS, LANES), lambda i: (i, 0)), const((1, D_MODEL)),
                  const((D_MODEL, ROUTER_LANES)), const((D_MODEL, ROUTER_LANES))],
        out_specs=[pl.BlockSpec((tm, ROUTER_LANES), lambda i: (i, 0)), const((SUBLANES, ROUTER_LANES))],
        out_shape=[jax.ShapeDtypeStruct((t, ROUTER_LANES), F32),
                   jax.ShapeDtypeStruct((SUBLANES, ROUTER_LANES), F32)],
        scratch_shapes=[pltpu.VMEM((SUBLANES, ROUTER_LANES), F32)],
        compiler_params=_params("arbitrary"),
        name="route",
    )(x2t, g_moe, wr_hi, wr_lo)


def _route_meta(info, cnt, tme):
    t = info.shape[0]
    n_tiles = 2 * t // tme
    counts = cnt[0, N_EXPERT_GROUPS:N_EXPERT_GROUPS + N_EXPERTS].astype(jnp.int32)
    ends = jnp.cumsum(counts)
    starts = ends - counts
    e = info[:, 0:2].astype(jnp.int32) - N_EXPERT_GROUPS
    onehot = e[:, :, None] == jnp.arange(N_EXPERTS, dtype=jnp.int32)
    slot = (jnp.sum(jnp.where(onehot, starts, 0), axis=-1) + info[:, 4:6].astype(jnp.int32)).reshape(-1)
    pts = jnp.concatenate([jnp.arange(n_tiles, dtype=jnp.int32) * tme, starts])
    n = pts.shape[0]
    idx = jnp.arange(n, dtype=jnp.int32)
    pos = jnp.sum((pts[None, :] < pts[:, None]) | ((pts[None, :] == pts[:, None]) & (idx[None, :] < idx[:, None])),
                  axis=1)
    lo_abs = jnp.sum(jnp.where(pos[:, None] == idx[None, :], pts[:, None], 0), axis=0)
    hi_abs = jnp.concatenate([lo_abs[1:], jnp.full((1,), 2 * t, jnp.int32)])
    tile = jnp.minimum(lo_abs // tme, n_tiles - 1)
    expert = jnp.clip(jnp.sum(starts[None, :] <= lo_abs[:, None], axis=1) - 1, 0, N_EXPERTS - 1)
    first = jnp.concatenate([jnp.ones((1,), jnp.int32), (tile[1:] != tile[:-1]).astype(jnp.int32)])
    last = jnp.concatenate([(tile[1:] != tile[:-1]).astype(jnp.int32), jnp.ones((1,), jnp.int32)])
    work = [a.astype(jnp.int32) for a in (tile, expert, lo_abs - tile * tme, hi_abs - tile * tme, first, last)]
    return slot.astype(jnp.int32), work


def _tile_copy(src, src_tok, dst, dst_tok, sem):
    return pltpu.make_async_copy(src.at[pl.ds(pl.multiple_of(src_tok * TILE_ROWS, TILE_ROWS), TILE_ROWS), :],
                                 dst.at[pl.ds(pl.multiple_of(dst_tok * TILE_ROWS, TILE_ROWS), TILE_ROWS), :], sem)


def _scatter_body(slot_ref, x2t_ref, xs_hbm, sem, *, tm):
    def issue(r, c):
        _tile_copy(x2t_ref, r, xs_hbm, slot_ref[2 * r], sem).start(priority=0)
        _tile_copy(x2t_ref, r, xs_hbm, slot_ref[2 * r + 1], sem).start(priority=1)
        return c

    lax.fori_loop(0, tm, issue, 0, unroll=8)

    def drain(r, c):
        _tile_copy(x2t_ref, r, xs_hbm, 0, sem).wait()
        _tile_copy(x2t_ref, r, xs_hbm, 0, sem).wait()
        return c

    lax.fori_loop(0, tm, drain, 0, unroll=8)


def _scatter(slot, x2t, tm=256):
    t = x2t.shape[0] // TILE_ROWS
    body = functools.partial(_scatter_body, tm=tm)
    return pl.pallas_call(
        body,
        grid=(t // tm,),
        in_specs=[pl.BlockSpec((2 * tm,), lambda i: (i,), memory_space=pltpu.SMEM),
                  pl.BlockSpec((tm * TILE_ROWS, LANES), lambda i: (i, 0))],
        out_specs=pl.BlockSpec(memory_space=pl.ANY),
        out_shape=jax.ShapeDtypeStruct((2 * t * TILE_ROWS, LANES), F32),
        scratch_shapes=[pltpu.SemaphoreType.DMA(())],
        compiler_params=_params("arbitrary"),
        name="scatter",
    )(slot, x2t)


def _expert_body(tile_ref, exp_ref, lo_ref, hi_ref, first_ref, last_ref, xs_ref, gm_ref, wg_ref, wu_ref, wd_ref,
                 ys_ref, acc_scr, *, tme):
    k = pl.program_id(0)

    @pl.when(first_ref[k] == 1)
    def _():
        acc_scr[...] = jnp.zeros_like(acc_scr)

    @pl.when(hi_ref[k] > lo_ref[k])
    def _():
        hb = _rms(_tiles_to_rows(xs_ref, tme), gm_ref[...]).astype(BF16)
        act = jax.nn.silu(_bdot(hb, wg_ref[0])) * _bdot(hb, wu_ref[0])
        y = _bdot(act.astype(BF16), wd_ref[0])
        row = lax.broadcasted_iota(jnp.int32, (tme, 1), 0)
        acc_scr[...] += jnp.where((row >= lo_ref[k]) & (row < hi_ref[k]), y, 0.0)

    @pl.when(last_ref[k] == 1)
    def _():
        _rows_to_tiles(ys_ref, acc_scr[...], tme)


def _experts(work, xs, g_moe, wg, wu, wd, tme):
    n_work = work[0].shape[0]
    body = functools.partial(_expert_body, tme=tme)
    tile_blk = pl.BlockSpec((tme * TILE_ROWS, LANES), lambda k, tile, *_: (tile[k], 0))

    def wspec(shape):
        return pl.BlockSpec(shape, lambda k, tile, exp, *_: (exp[k], 0, 0))

    grid_spec = pltpu.PrefetchScalarGridSpec(
        num_scalar_prefetch=len(work),
        grid=(n_work,),
        in_specs=[tile_blk, pl.BlockSpec((1, D_MODEL), lambda k, *_: (0, 0)),
                  wspec((1, D_MODEL, D_FF_EXPERT)), wspec((1, D_MODEL, D_FF_EXPERT)),
                  wspec((1, D_FF_EXPERT, D_MODEL))],
        out_specs=tile_blk,
        scratch_shapes=[pltpu.VMEM((tme, D_MODEL), F32)],
    )
    return pl.pallas_call(
        body,
        grid_spec=grid_spec,
        out_shape=jax.ShapeDtypeStruct(xs.shape, F32),
        compiler_params=_params("arbitrary"),
        name="experts",
    )(*work, xs, g_moe, wg, wu, wd)


def _combine_body(slot_ref, x2t_ref, info_ref, ys_hbm, gf_ref, y_ref, b0, b1, sem, *, tm):
    def issue(r, c):
        _tile_copy(ys_hbm, slot_ref[2 * r], b0, r, sem).start(priority=0)
        _tile_copy(ys_hbm, slot_ref[2 * r + 1], b1, r, sem).start(priority=1)
        return c

    lax.fori_loop(0, tm, issue, 0, unroll=8)

    def drain(r, c):
        _tile_copy(ys_hbm, 0, b0, r, sem).wait()
        _tile_copy(ys_hbm, 0, b1, r, sem).wait()
        return c

    lax.fori_loop(0, tm, drain, 0, unroll=8)
    info = info_ref[...]
    moe = info[:, 2:3] * _tiles_to_rows(b0, tm) + info[:, 3:4] * _tiles_to_rows(b1, tm)
    y_ref[...] = _rms(_tiles_to_rows(x2t_ref, tm) + moe, gf_ref[...])


def _combine(slot, x2t, info, ys, g_final, tm=256):
    t = info.shape[0]
    body = functools.partial(_combine_body, tm=tm)
    return pl.pallas_call(
        body,
        grid=(t // tm,),
        in_specs=[pl.BlockSpec((2 * tm,), lambda i: (i,), memory_space=pltpu.SMEM),
                  pl.BlockSpec((tm * TILE_ROWS, LANES), lambda i: (i, 0)),
                  pl.BlockSpec((tm, ROUTER_LANES), lambda i: (i, 0)),
                  pl.BlockSpec(memory_space=pl.ANY),
                  pl.BlockSpec((1, D_MODEL), lambda i: (0, 0))],
        out_specs=pl.BlockSpec((tm, D_MODEL), lambda i: (i, 0)),
        out_shape=jax.ShapeDtypeStruct((t, D_MODEL), F32),
        scratch_shapes=[pltpu.VMEM((tm * TILE_ROWS, LANES), F32), pltpu.VMEM((tm * TILE_ROWS, LANES), F32),
                        pltpu.SemaphoreType.DMA(())],
        compiler_params=_params("arbitrary"),
        name="combine",
    )(slot, x2t, info, ys, g_final)


def _moe(x2t, w, tme):
    info, cnt = _route(x2t, w["g_moe"], w["wr_hi"], w["wr_lo"])
    slot, work = _route_meta(info, cnt, tme)
    xs = _scatter(slot, x2t)
    ys = _experts(work, xs, w["g_moe"], w["moe_wg"], w["moe_wu"], w["moe_wd"], tme)
    return _combine(slot, x2t, info, ys, w["g_final"])


def _layer(x3d, k4, v4, s5_h0, lru_h0, lru_conv, w, chunk, nb, tq, tme):
    bsz, seq, _ = x3d.shape
    x2d = x3d.reshape(bsz * seq, D_MODEL)
    u, xl, gl = _mixin(x2d, w["g_mix"], w["w_in"])

    nstate = S5_GROUPS * S5_STATE
    if s5_h0 is None:
        zero = jnp.zeros((bsz, 1, nstate), F32)
        ys, s5_re, s5_im = _s5(u, w["s5"][chunk], zero, zero, chunk, seq // chunk, True)
    else:
        ys, s5_re, s5_im = _s5(u, w["s5"][chunk], s5_h0[0].reshape(bsz, nstate), s5_h0[1].reshape(bsz, nstate),
                               chunk, bsz, False)
    s5_re = s5_re.reshape(bsz, S5_GROUPS, S5_STATE)
    s5_im = s5_im.reshape(bsz, S5_GROUPS, S5_STATE)

    lru_w = (w["conv_w"], w["conv_b"], w["lru_wg"], w["lru_bg"], w["lru_lam"])
    if lru_h0 is None:
        yl, hl = _lru_seq(xl, gl, *lru_w, bsz, seq)
        lru_h = hl.reshape(bsz, SUBLANES, D_LRU)[:, 0]
    else:
        hist = jnp.pad(lru_conv, ((0, 0), (SUBLANES - (CONV_W - 1), 0), (0, 0))).reshape(bsz * seq, D_LRU)
        h0rep = jnp.repeat(lru_h0, seq, axis=0)
        yl, hall = _lru_dec(xl, gl, hist, h0rep, *lru_w)
        lru_h = hall.reshape(bsz, seq, D_LRU)[:, seq - 1]
    conv_new = xl.reshape(bsz, seq, D_LRU)[:, seq - (CONV_W - 1):]

    x1, q = _mixout(x2d, ys, yl, w["w_glu"], w["b_glu"], w["w_out"], w["g_xa"], w["w_q"])
    x2t = _attn(q.reshape(bsz, seq, D_MODEL), x1.reshape(bsz, seq, D_MODEL), k4, v4, w["w_o"], nb, tq)
    y = _moe(x2t, w, tme)
    return y.reshape(bsz, seq, D_MODEL), s5_re, s5_im, lru_h, conv_new


def kernel(x_prompt, x_sample, mem_prompt, cache_mem_k, cache_mem_v, state_s5_re, state_s5_im, state_lru_h, state_lru_conv, g_mix, w_in, s5_lam_re, s5_lam_im, s5_log_dt, s5_b_re, s5_b_im, s5_c_re, s5_c_im, s5_d, s5_w_glu, s5_b_glu, lru_conv_w, lru_conv_b, lru_w_a, lru_b_a, lru_w_x, lru_b_x, lru_lam, w_out, g_xa, g_mem, xa_w_q, xa_w_k, xa_w_v, xa_w_o, g_moe, moe_w_group, moe_w_expert, moe_w_gate, moe_w_up, moe_w_down, g_final):
    depth = g_mix.shape[0]
    assert depth == 1, "single-layer step"
    l = 0
    bsz, seq, _ = x_prompt.shape
    dbsz, dseq, _ = x_sample.shape
    chunk_p, chunk_s = 16, dseq
    assert seq % chunk_p == 0 and dseq == SUBLANES

    s5_args = (s5_lam_re[l], s5_lam_im[l], s5_log_dt[l], s5_b_re[l], s5_b_im[l], s5_c_re[l], s5_c_im[l], s5_d[l])
    wg, bg = _lru_weights(lru_w_a[l], lru_w_x[l], lru_b_a[l], lru_b_x[l])
    wr = jnp.concatenate([moe_w_group[l], moe_w_expert[l],
                          jnp.zeros((D_MODEL, ROUTER_LANES - N_EXPERT_GROUPS - N_EXPERTS), F32)], axis=1)
    wr_hi = wr.astype(BF16)
    wr_lo = (wr - wr_hi.astype(F32)).astype(BF16)
    w = {
        "g_mix": g_mix[l][None], "w_in": w_in[l].astype(BF16),
        "s5": {c: _s5_weights(c, *s5_args) for c in {chunk_p, chunk_s}},
        "conv_w": lru_conv_w[l], "conv_b": lru_conv_b[l][None], "lru_wg": wg, "lru_bg": bg,
        "lru_lam": lru_lam[l][None],
        "w_glu": s5_w_glu[l].astype(BF16), "b_glu": s5_b_glu[l][None], "w_out": w_out[l].astype(BF16),
        "g_xa": g_xa[l][None], "w_q": xa_w_q[l].astype(BF16), "w_o": xa_w_o[l].astype(BF16),
        "g_moe": g_moe[l][None], "wr_hi": wr_hi, "wr_lo": wr_lo,
        "moe_wg": moe_w_gate[l].astype(BF16), "moe_wu": moe_w_up[l].astype(BF16),
        "moe_wd": moe_w_down[l].astype(BF16), "g_final": g_final[None],
    }

    mk, mv = _memkv(mem_prompt.reshape(bsz * N_MEM, D_MODEL), g_mem[l][None],
                    xa_w_k[l].astype(BF16), xa_w_v[l].astype(BF16))
    yp, p_re, p_im, p_h, p_conv = _layer(x_prompt, mk.reshape(bsz, N_MEM, D_MODEL), mv.reshape(bsz, N_MEM, D_MODEL),
                                         None, None, None, w, chunk_p, nb=1, tq=512, tme=512)
    ysmp, s_re, s_im, s_h, s_conv = _layer(x_sample, cache_mem_k[l], cache_mem_v[l],
                                           (state_s5_re[l], state_s5_im[l]), state_lru_h[l],
                                           state_lru_conv[l], w, chunk_s, nb=4, tq=dseq, tme=128)

    return (yp, ysmp,
            mk.reshape(1, bsz, N_MEM, XA_HEADS, XA_HEAD_DIM), mv.reshape(1, bsz, N_MEM, XA_HEADS, XA_HEAD_DIM),
            p_re[None], p_im[None], p_h[None], p_conv[None],
            s_re[None], s_im[None], s_h[None], s_conv[None])
```

```python
import functools

import jax
import jax.numpy as jnp
import numpy as np
from jax import lax
from jax.experimental import pallas as pl
from jax.experimental.pallas import tpu as pltpu

F32 = jnp.float32
BF16 = jnp.bfloat16

D_MODEL = 1024
D_S5 = 512
D_LRU = 512
S5_GROUP = 16
S5_GROUPS = 32
S5_STATE = 64
LRU_HEADS = 8
LRU_HEAD_DIM = 64
CONV_W = 4
LRU_C = 8.0
N_MEM = 256
XA_HEADS = 4
XA_HEAD_DIM = 256
N_EXPERT_GROUPS = 4
EXPERTS_PER_GROUP = 8
N_EXPERTS = 32
D_FF_EXPERT = 256
EPS = 1e-6

SUBLANES = 8
LANES = 128
ROUTER_LANES = 128
VMEM_LIMIT = 48 * 1024 * 1024


def _params(*sem):
    return pltpu.CompilerParams(dimension_semantics=sem, vmem_limit_bytes=VMEM_LIMIT)


def _rms(x, g):
    ms = jnp.mean(x * x, axis=-1, keepdims=True)
    return x * lax.rsqrt(ms + EPS) * g


def _bdot(a, b):
    return jnp.dot(a, b, preferred_element_type=F32)


def _memkv_body(m_ref, g_ref, wk_ref, wv_ref, k_ref, v_ref):
    hb = _rms(m_ref[...], g_ref[...]).astype(BF16)
    k_ref[...] = _bdot(hb, wk_ref[...])
    v_ref[...] = _bdot(hb, wv_ref[...])


def _memkv(mem2d, g_mem, wk, wv, tm=512):
    t = mem2d.shape[0]
    row = pl.BlockSpec((tm, D_MODEL), lambda i: (i, 0))
    full = pl.BlockSpec((D_MODEL, D_MODEL), lambda i: (0, 0))
    return pl.pallas_call(
        _memkv_body,
        grid=(t // tm,),
        in_specs=[row, pl.BlockSpec((1, D_MODEL), lambda i: (0, 0)), full, full],
        out_specs=[row, row],
        out_shape=[jax.ShapeDtypeStruct((t, D_MODEL), F32)] * 2,
        compiler_params=_params("parallel"),
        name="memkv",
    )(mem2d, g_mem, wk, wv)


def _mixin_body(x_ref, g_ref, w_ref, u_ref, xl_ref, gl_ref):
    hb = _rms(x_ref[...], g_ref[...]).astype(BF16)
    proj = _bdot(hb, w_ref[...])
    u_ref[...] = proj[:, :D_S5]
    xl_ref[...] = proj[:, D_S5:D_S5 + D_LRU]
    gl_ref[...] = proj[:, D_S5 + D_LRU:]


def _mixin(x2d, g_mix, w_in, tm=512):
    t = x2d.shape[0]
    half = pl.BlockSpec((tm, D_S5), lambda i: (i, 0))
    return pl.pallas_call(
        _mixin_body,
        grid=(t // tm,),
        in_specs=[pl.BlockSpec((tm, D_MODEL), lambda i: (i, 0)),
                  pl.BlockSpec((1, D_MODEL), lambda i: (0, 0)),
                  pl.BlockSpec((D_MODEL, D_S5 + 2 * D_LRU), lambda i: (0, 0))],
        out_specs=[half, half, half],
        out_shape=[jax.ShapeDtypeStruct((t, D_S5), F32)] * 3,
        compiler_params=_params("parallel"),
        name="mixin",
    )(x2d, g_mix, w_in)


GROUPS_PER_BLOCK = LANES // S5_GROUP
PAIRS_PER_BLOCK = GROUPS_PER_BLOCK // 2
STATE_BLOCK = GROUPS_PER_BLOCK * S5_STATE
PW_ROWS = 24


def _s5_body(u_ref, wa_ref, wh_ref, pw_ref, d_ref, h0r_ref, h0i_ref, y_ref, hfr_ref, hfi_ref, hpr_scr, hpi_scr,
             *, chunk, nrow, scan):
    nh = chunk // SUBLANES
    lc = chunk * S5_GROUP
    slot = lax.broadcasted_iota(jnp.int32, (nrow, LANES), 1) // S5_GROUP
    in_slot = [slot == s for s in range(GROUPS_PER_BLOCK)]

    def pick(src, sel):
        out = src[sel(0)]
        for s in range(1, GROUPS_PER_BLOCK):
            out = jnp.where(in_slot[s], src[sel(s)], out)
        return out

    nat, skew = [], []
    for t in range(chunk):
        a = u_ref[pl.ds(t, nrow, stride=chunk), :]
        nat.append(a)
        s = t % GROUPS_PER_BLOCK
        skew.append(pltpu.roll(a, s * S5_GROUP, axis=1) if s else a)
    def lane_roll(x, slots):
        slots %= GROUPS_PER_BLOCK
        return pltpu.roll(x, slots * S5_GROUP, axis=1) if slots else x

    z = [[lane_roll(pick(skew, lambda s, hh=hh, m=m: SUBLANES * hh + (s - m) % GROUPS_PER_BLOCK), -m)
          for hh in range(nh)] for m in range(GROUPS_PER_BLOCK)]

    ys, er, ei = [], [], []
    for p in range(PAIRS_PER_BLOCK):
        lhs = jnp.concatenate(z[2 * p] + z[2 * p + 1], axis=1).astype(BF16)
        full = _bdot(lhs, wa_ref[p])
        ys.append(full[:, :2 * lc])
        er.append(full[:, 2 * lc:2 * lc + LANES])
        ei.append(full[:, 2 * lc + LANES:])
    er = jnp.concatenate(er, axis=1)
    ei = jnp.concatenate(ei, axis=1)

    if scan:
        ntile = nrow // SUBLANES
        xr = er.reshape(ntile, SUBLANES, STATE_BLOCK)
        xi = ei.reshape(ntile, SUBLANES, STATE_BLOCK)
        row = lax.broadcasted_iota(jnp.int32, xr.shape, 1)
        for k, d in enumerate((1, 2, 4)):
            pr, pi = pw_ref[0, k:k + 1, :], pw_ref[0, 4 + k:5 + k, :]
            sr, si = pltpu.roll(xr, d, axis=1), pltpu.roll(xi, d, axis=1)
            m = row >= d
            xr, xi = jnp.where(m, xr + pr * sr - pi * si, xr), jnp.where(m, xi + pr * si + pi * sr, xi)
        sxr = jnp.where(row >= 1, pltpu.roll(xr, 1, axis=1), 0.0)
        sxi = jnp.where(row >= 1, pltpu.roll(xi, 1, axis=1), 0.0)
        p8r, p8i = pw_ref[0, 3:4, :], pw_ref[0, 7:8, :]
        qr, qi = pw_ref[0, 8:16, :], pw_ref[0, 16:24, :]
        hr, hi = h0r_ref[0], h0i_ref[0]
        for t in range(ntile):
            rows = slice(SUBLANES * t, SUBLANES * (t + 1))
            hpr_scr[rows, :] = sxr[t] + qr * hr - qi * hi
            hpi_scr[rows, :] = sxi[t] + qr * hi + qi * hr
            hr, hi = (xr[t, SUBLANES - 1:] + p8r * hr - p8i * hi, xi[t, SUBLANES - 1:] + p8r * hi + p8i * hr)
        hfr_ref[0] = hr
        hfi_ref[0] = hi
        hpr, hpi = hpr_scr[...], hpi_scr[...]
    else:
        hpr, hpi = h0r_ref[...], h0i_ref[...]
        p1r, p1i = pw_ref[0, 0:1, :], pw_ref[0, 4:5, :]
        hfr_ref[...] = er + p1r * hpr - p1i * hpi
        hfi_ref[...] = ei + p1r * hpi + p1i * hpr

    yg = []
    for p in range(PAIRS_PER_BLOCK):
        lanes = slice(LANES * p, LANES * (p + 1))
        hp = jnp.concatenate([hpr[:, lanes], hpi[:, lanes]], axis=1).astype(BF16)
        out = ys[p] + _bdot(hp, wh_ref[p])
        for half in range(2):
            q = 2 * p + half
            yg.append([lane_roll(out[:, half * lc + hh * LANES:half * lc + (hh + 1) * LANES], q)
                       for hh in range(nh)])
    d = d_ref[...]
    for t in range(chunk):
        hh, s = divmod(t, GROUPS_PER_BLOCK)
        c = lane_roll(pick([g[hh] for g in yg], lambda sl, s=s: (sl - s) % GROUPS_PER_BLOCK), -s)
        y_ref[pl.ds(t, nrow, stride=chunk), :] = c + d * nat[t]


def _s5(u2d, weights, h0r, h0i, chunk, nrow, scan):
    wa, wh, pw, dvec = weights
    t = u2d.shape[0]
    lc = chunk * S5_GROUP
    rows = nrow * chunk
    nblk = D_S5 // LANES
    body = functools.partial(_s5_body, chunk=chunk, nrow=nrow, scan=scan)
    ublk = pl.BlockSpec((rows, LANES), lambda i, j: (i, j))
    if scan:
        hblk = pl.BlockSpec((1, 1, STATE_BLOCK), lambda i, j: (i, 0, j))
        hshape = jax.ShapeDtypeStruct((t // rows, 1, S5_GROUPS * S5_STATE), F32)
    else:
        hblk = pl.BlockSpec((nrow, STATE_BLOCK), lambda i, j: (i, j))
        hshape = jax.ShapeDtypeStruct((t // chunk, S5_GROUPS * S5_STATE), F32)
    return pl.pallas_call(
        body,
        grid=(t // rows, nblk),
        in_specs=[ublk,
                  pl.BlockSpec((PAIRS_PER_BLOCK, 2 * lc, 2 * lc + 2 * LANES), lambda i, j: (j, 0, 0)),
                  pl.BlockSpec((PAIRS_PER_BLOCK, 2 * LANES, 2 * lc), lambda i, j: (j, 0, 0)),
                  pl.BlockSpec((1, PW_ROWS, STATE_BLOCK), lambda i, j: (j, 0, 0)),
                  pl.BlockSpec((1, LANES), lambda i, j: (0, j)),
                  hblk, hblk],
        out_specs=[ublk, hblk, hblk],
        out_shape=[jax.ShapeDtypeStruct((t, D_S5), F32), hshape, hshape],
        scratch_shapes=[pltpu.VMEM((nrow, STATE_BLOCK), F32), pltpu.VMEM((nrow, STATE_BLOCK), F32)],
        compiler_params=_params("parallel", "parallel"),
        name="s5",
    )(u2d, wa, wh, pw, dvec, h0r, h0i)


def _s5_weights(chunk, lam_re, lam_im, log_dt, b_re, b_im, c_re, c_im, d_skip):
    hi = lax.Precision.HIGHEST
    dt = jnp.exp(log_dt)[:, None]
    mag = jnp.exp(lam_re * dt)
    ab_re = mag * jnp.cos(lam_im * dt)
    ab_im = mag * jnp.sin(lam_im * dt)
    den = lam_re * lam_re + lam_im * lam_im
    nr = ab_re - 1.0
    f_re = (nr * lam_re + ab_im * lam_im) / den
    f_im = (ab_im * lam_re - nr * lam_im) / den
    bb_re = f_re[..., None] * b_re - f_im[..., None] * b_im
    bb_im = f_re[..., None] * b_im + f_im[..., None] * b_re
    def powers(ks):
        k = jnp.asarray(np.asarray(ks, np.float32))[:, None, None]
        m = jnp.exp(k * (lam_re * dt))
        return m * jnp.cos(k * (lam_im * dt)), m * jnp.sin(k * (lam_im * dt))

    pw_re, pw_im = powers(range(chunk + 1))
    pb_re = pw_re[:chunk, ..., None] * bb_re - pw_im[:chunk, ..., None] * bb_im
    pb_im = pw_re[:chunk, ..., None] * bb_im + pw_im[:chunk, ..., None] * bb_re
    kk = (jnp.einsum("gon,kgni->kgoi", c_re, pb_re, precision=hi)
          - jnp.einsum("gon,kgni->kgoi", c_im, pb_im, precision=hi))
    npair = S5_GROUPS // 2
    lc = chunk * S5_GROUP
    row0 = kk.transpose(1, 3, 0, 2).reshape(S5_GROUPS, S5_GROUP, lc)
    row0 = jnp.pad(row0, ((0, 0), (0, 0), (lc, 0)))
    conv = jnp.stack([row0[:, :, lc - S5_GROUP * ti:2 * lc - S5_GROUP * ti] for ti in range(chunk)], axis=1)
    conv = conv.reshape(npair, 2, lc, lc)
    ends = [x[::-1].transpose(1, 0, 3, 2).reshape(npair, 2, lc, S5_STATE) for x in (pb_re, pb_im)]
    zc = jnp.zeros((npair, lc, lc), F32)
    zs = jnp.zeros((npair, lc, S5_STATE), F32)
    wa = jnp.concatenate([
        jnp.concatenate([conv[:, 0], zc, ends[0][:, 0], zs, ends[1][:, 0], zs], axis=-1),
        jnp.concatenate([zc, conv[:, 1], zs, ends[0][:, 1], zs, ends[1][:, 1]], axis=-1)], axis=1)
    gr = c_re[None] * pw_re[1:, :, None, :] - c_im[None] * pw_im[1:, :, None, :]
    gi = c_re[None] * pw_im[1:, :, None, :] + c_im[None] * pw_re[1:, :, None, :]
    zr = jnp.zeros((npair, S5_STATE, lc), F32)
    blocks = []
    for x in (gr, -gi):
        hy = x.transpose(1, 3, 0, 2).reshape(npair, 2, S5_STATE, lc)
        blocks += [jnp.concatenate([hy[:, 0], zr], axis=-1), jnp.concatenate([zr, hy[:, 1]], axis=-1)]
    wh = jnp.concatenate(blocks, axis=1)
    tab = [1, 2, 4, 8] + list(range(SUBLANES))
    qr, qi = powers([chunk * k for k in tab])
    nblk = D_S5 // LANES
    pw = jnp.concatenate([qr[:4], qi[:4], qr[4:], qi[4:]]).reshape(PW_ROWS, nblk, STATE_BLOCK).transpose(1, 0, 2)
    return wa.astype(BF16), wh.astype(BF16), pw, d_skip.reshape(1, D_S5)


def _lru_ab(xc, wg_ref, bg_ref, lam_ref):
    xb = xc.astype(BF16)
    half = D_LRU // 2
    g0 = _bdot(xb[:, :half], wg_ref[0]) + bg_ref[0]
    g1 = _bdot(xb[:, half:], wg_ref[1]) + bg_ref[1]
    r = jax.nn.sigmoid(jnp.concatenate([g0[:, :half], g1[:, :half]], axis=1))
    ig = jax.nn.sigmoid(jnp.concatenate([g0[:, half:], g1[:, half:]], axis=1))
    zl = -lam_ref[...]
    softplus = jnp.maximum(zl, 0.0) + jnp.log1p(jnp.exp(-jnp.abs(zl)))
    log_a = -LRU_C * r * softplus
    a = jnp.exp(log_a)
    b = jnp.sqrt(-jnp.tanh(log_a) * (a * a + 1.0)) * (ig * xc)
    return a, b


def _tile_scan(a, b):
    shape = a.shape
    a = a.reshape(shape[0] // SUBLANES, SUBLANES, shape[1])
    b = b.reshape(a.shape)
    row = lax.broadcasted_iota(jnp.int32, a.shape, 1)
    for d in (1, 2, 4):
        a_prev = pltpu.roll(a, d, axis=1)
        b_prev = pltpu.roll(b, d, axis=1)
        m = row >= d
        b = jnp.where(m, b + a * b_prev, b)
        a = jnp.where(m, a * a_prev, a)
    return a.reshape(shape), b.reshape(shape)


def _lru_seq_body(xl_ref, gl_ref, cw_ref, cb_ref, wg_ref, bg_ref, lam_ref, y_ref, hl_ref,
                  xp_scr, a_scr, b_scr, h_scr, hc_scr, *, ts):
    ti = pl.program_id(1)

    @pl.when(ti == 0)
    def _():
        xp_scr[0:SUBLANES, :] = jnp.zeros((SUBLANES, D_LRU), F32)
        hc_scr[...] = jnp.zeros((SUBLANES, D_LRU), F32)

    xl = xl_ref[...]
    xp_scr[SUBLANES:SUBLANES + ts, :] = xl
    xc = cb_ref[...] + xl * cw_ref[CONV_W - 1:CONV_W, :]
    for j in range(1, CONV_W):
        xc = xc + xp_scr[SUBLANES - j:SUBLANES - j + ts, :] * cw_ref[CONV_W - 1 - j:CONV_W - j, :]
    xp_scr[0:SUBLANES, :] = xl[ts - SUBLANES:, :]
    a, b = _lru_ab(xc, wg_ref, bg_ref, lam_ref)
    a, b = _tile_scan(a, b)
    a_scr[...] = a
    b_scr[...] = b

    def step(i, hin):
        rows = pl.ds(pl.multiple_of(i * SUBLANES, SUBLANES), SUBLANES)
        h = b_scr[rows, :] + a_scr[rows, :] * hin
        h_scr[rows, :] = h
        return h[SUBLANES - 1:SUBLANES, :]

    hlast = lax.fori_loop(0, ts // SUBLANES, step, hc_scr[0:1, :], unroll=4)
    hc_scr[...] = jnp.broadcast_to(hlast, (SUBLANES, D_LRU))
    hl_ref[...] = hc_scr[...]
    y_ref[...] = h_scr[...] * jax.nn.gelu(gl_ref[...])


def _lru_dec_body(xl_ref, gl_ref, hist_ref, h0_ref, cw_ref, cb_ref, wg_ref, bg_ref, lam_ref, y_ref, hl_ref, *, tm):
    xl = xl_ref[...]
    hist = hist_ref[...]
    row = lax.broadcasted_iota(jnp.int32, xl.shape, 0) % SUBLANES
    xc = cb_ref[...] + xl * cw_ref[CONV_W - 1:CONV_W, :]
    for j in range(1, CONV_W):
        prev = jnp.where(row >= j, pltpu.roll(xl, j, axis=0), pltpu.roll(hist, tm - SUBLANES + j, axis=0))
        xc = xc + prev * cw_ref[CONV_W - 1 - j:CONV_W - j, :]
    a, b = _lru_ab(xc, wg_ref, bg_ref, lam_ref)
    a, b = _tile_scan(a, b)
    h = b + a * h0_ref[...]
    hl_ref[...] = h
    y_ref[...] = h * jax.nn.gelu(gl_ref[...])


def _lru_weights(w_a, w_x, b_a, b_x):
    eye = jnp.eye(LRU_HEADS, dtype=F32)
    bd_a = jnp.einsum("hij,hg->higj", w_a, eye).reshape(D_LRU, D_LRU)
    bd_x = jnp.einsum("hij,hg->higj", w_x, eye).reshape(D_LRU, D_LRU)
    half = D_LRU // 2
    wg = jnp.stack([jnp.concatenate([bd_a[:half, :half], bd_x[:half, :half]], axis=1),
                    jnp.concatenate([bd_a[half:, half:], bd_x[half:, half:]], axis=1)]).astype(BF16)
    ba = b_a.reshape(1, D_LRU)
    bx = b_x.reshape(1, D_LRU)
    bg = jnp.stack([jnp.concatenate([ba[:, :half], bx[:, :half]], axis=1),
                    jnp.concatenate([ba[:, half:], bx[:, half:]], axis=1)])
    return wg, bg


def _lru_common_specs(const):
    return [const((CONV_W, D_LRU)), const((1, D_LRU)), const((2, D_LRU // 2, D_LRU)),
            const((2, 1, D_LRU)), const((1, D_LRU))]


def _lru_seq(xl, gl, cw, cb, wg, bg, lam, bsz, seq, ts=512):
    nt = seq // ts
    blk = pl.BlockSpec((ts, D_LRU), lambda b, t: (b * nt + t, 0))

    def const(shape):
        return pl.BlockSpec(shape, lambda b, t: (0,) * len(shape))

    body = functools.partial(_lru_seq_body, ts=ts)
    return pl.pallas_call(
        body,
        grid=(bsz, nt),
        in_specs=[blk, blk] + _lru_common_specs(const),
        out_specs=[blk, pl.BlockSpec((SUBLANES, D_LRU), lambda b, t: (b, 0))],
        out_shape=[jax.ShapeDtypeStruct((bsz * seq, D_LRU), F32),
                   jax.ShapeDtypeStruct((bsz * SUBLANES, D_LRU), F32)],
        scratch_shapes=[pltpu.VMEM((ts + SUBLANES, D_LRU), F32), pltpu.VMEM((ts, D_LRU), F32),
                        pltpu.VMEM((ts, D_LRU), F32), pltpu.VMEM((ts, D_LRU), F32),
                        pltpu.VMEM((SUBLANES, D_LRU), F32)],
        compiler_params=_params("arbitrary", "arbitrary"),
        name="lru_seq",
    )(xl, gl, cw, cb, wg, bg, lam)


def _lru_dec(xl, gl, hist, h0rep, cw, cb, wg, bg, lam, tm=256):
    t = xl.shape[0]
    blk = pl.BlockSpec((tm, D_LRU), lambda i: (i, 0))

    def const(shape):
        return pl.BlockSpec(shape, lambda i: (0,) * len(shape))

    body = functools.partial(_lru_dec_body, tm=tm)
    return pl.pallas_call(
        body,
        grid=(t // tm,),
        in_specs=[blk, blk, blk, blk] + _lru_common_specs(const),
        out_specs=[blk, blk],
        out_shape=[jax.ShapeDtypeStruct((t, D_LRU), F32)] * 2,
        compiler_params=_params("parallel"),
        name="lru_dec",
    )(xl, gl, hist, h0rep, cw, cb, wg, bg, lam)


def _mixout_body(x_ref, ys_ref, yl_ref, wglu_ref, bglu_ref, wo_ref, gxa_ref, wq_ref, x1_ref, q_ref):
    ys = jax.nn.gelu(ys_ref[...])
    gate = jax.nn.sigmoid(_bdot(ys.astype(BF16), wglu_ref[...]) + bglu_ref[...])
    s5 = (ys * gate).astype(BF16)
    x1 = (x_ref[...] + _bdot(s5, wo_ref[0:D_S5, :]) + _bdot(yl_ref[...].astype(BF16), wo_ref[D_S5:, :]))
    x1_ref[...] = x1
    q_ref[...] = _bdot(_rms(x1, gxa_ref[...]).astype(BF16), wq_ref[...]).astype(BF16)


def _mixout(x2d, ys, yl, w_glu, b_glu, w_out, g_xa, w_q, tm=512):
    t = x2d.shape[0]
    row = pl.BlockSpec((tm, D_MODEL), lambda i: (i, 0))
    half = pl.BlockSpec((tm, D_S5), lambda i: (i, 0))

    def const(shape):
        return pl.BlockSpec(shape, lambda i: (0,) * len(shape))

    return pl.pallas_call(
        _mixout_body,
        grid=(t // tm,),
        in_specs=[row, half, half, const((D_S5, D_S5)), const((1, D_S5)), const((D_MODEL, D_MODEL)),
                  const((1, D_MODEL)), const((D_MODEL, D_MODEL))],
        out_specs=[row, row],
        out_shape=[jax.ShapeDtypeStruct((t, D_MODEL), F32), jax.ShapeDtypeStruct((t, D_MODEL), BF16)],
        compiler_params=_params("parallel"),
        name="mixout",
    )(x2d, ys, yl, w_glu, b_glu, w_out, g_xa, w_q)


def _softmax(sc):
    p = jnp.exp(sc - jnp.max(sc, axis=-1, keepdims=True))
    return p / jnp.sum(p, axis=-1, keepdims=True)


def _attn_body(q_ref, x1_ref, k_ref, v_ref, wo_ref, x2t_ref, *, nb, tq):
    scale = XA_HEAD_DIM ** -0.5
    for i in range(nb):
        q = q_ref[i]
        heads = [q[:, h * XA_HEAD_DIM:(h + 1) * XA_HEAD_DIM] for h in range(XA_HEADS)]
        if len(k_ref.shape) == 4:
            kf = k_ref[i].reshape(N_MEM * XA_HEADS, XA_HEAD_DIM).astype(BF16)
            vf = v_ref[i].reshape(N_MEM * XA_HEADS, XA_HEAD_DIM).astype(BF16)
            sc = lax.dot_general(jnp.concatenate(heads, axis=0), kf, (((1,), (1,)), ((), ())),
                                 preferred_element_type=F32) * scale
            own = (lax.broadcasted_iota(jnp.int32, sc.shape, 1) % XA_HEADS
                   == lax.broadcasted_iota(jnp.int32, sc.shape, 0) // tq)
            oh = _bdot(_softmax(jnp.where(own, sc, -jnp.inf)).astype(BF16), vf)
            outs = [oh[h * tq:(h + 1) * tq] for h in range(XA_HEADS)]
        else:
            kb = k_ref[i].astype(BF16)
            vb = v_ref[i].astype(BF16)
            outs = []
            for h in range(XA_HEADS):
                cols = slice(h * XA_HEAD_DIM, (h + 1) * XA_HEAD_DIM)
                sc = lax.dot_general(heads[h], kb[:, cols], (((1,), (1,)), ((), ())),
                                     preferred_element_type=F32) * scale
                outs.append(_bdot(_softmax(sc).astype(BF16), vb[:, cols]))
        o = jnp.concatenate(outs, axis=1).astype(BF16)
        _rows_to_tiles(x2t_ref, x1_ref[i] + _bdot(o, wo_ref[...]), tq, base=i * tq * TILE_ROWS)


def _attn(q3, x13, k4, v4, w_o, nb, tq):
    bsz, seq, _ = q3.shape
    nt = seq // tq
    blk = pl.BlockSpec((nb, tq, D_MODEL), lambda b, t: (b, t, 0))
    kv = pl.BlockSpec((nb,) + k4.shape[1:], lambda b, t: (b,) + (0,) * (k4.ndim - 1))
    body = functools.partial(_attn_body, nb=nb, tq=tq)
    return pl.pallas_call(
        body,
        grid=(bsz // nb, nt),
        in_specs=[blk, blk, kv, kv, pl.BlockSpec((D_MODEL, D_MODEL), lambda b, t: (0, 0))],
        out_specs=pl.BlockSpec((nb * tq * TILE_ROWS, LANES), lambda b, t: (b * nt + t, 0)),
        out_shape=jax.ShapeDtypeStruct((bsz * seq * TILE_ROWS, LANES), F32),
        compiler_params=_params("parallel", "parallel"),
        name="attn",
    )(q3, x13, k4, v4, w_o)


def _router(hm, wr_hi_ref, wr_lo_ref):
    a_hi = hm.astype(BF16)
    a_lo = (hm - a_hi.astype(F32)).astype(BF16)
    logits = _bdot(a_hi, wr_hi_ref[...]) + (_bdot(a_hi, wr_lo_ref[...]) + _bdot(a_lo, wr_hi_ref[...]))
    lane_i = lax.broadcasted_iota(jnp.int32, logits.shape, 1)
    lane = lane_i.astype(F32)
    neg = -jnp.inf
    big = float(ROUTER_LANES)
    is_g = lane_i < N_EXPERT_GROUPS
    glog = jnp.where(is_g, logits, neg)
    gmax = jnp.max(glog, axis=-1, keepdims=True)
    gsel = jnp.min(jnp.where(glog == gmax, lane, big), axis=-1, keepdims=True)
    pg_sel = 1.0 / jnp.sum(jnp.where(is_g, jnp.exp(logits - gmax), 0.0), axis=-1, keepdims=True)
    eidx = lane_i - N_EXPERT_GROUPS
    in_group = (eidx >= 0) & (eidx < N_EXPERTS) & ((eidx >> 3).astype(F32) == gsel)
    el = jnp.where(in_group, logits, neg)
    v1 = jnp.max(el, axis=-1, keepdims=True)
    i1 = jnp.min(jnp.where(el == v1, lane, big), axis=-1, keepdims=True)
    el2 = jnp.where(lane == i1, neg, el)
    v2 = jnp.max(el2, axis=-1, keepdims=True)
    i2 = jnp.min(jnp.where(el2 == v2, lane, big), axis=-1, keepdims=True)
    e2 = jnp.exp(v2 - v1)
    w1 = pg_sel / (1.0 + e2)
    w2 = pg_sel * e2 / (1.0 + e2)
    return i1, i2, w1, w2


TILE_ROWS = D_MODEL // LANES


def _tiles_to_rows(ref, n, base=0):
    return jnp.concatenate([ref[pl.ds(base + s, n, stride=TILE_ROWS), :] for s in range(TILE_ROWS)], axis=1)


def _rows_to_tiles(ref, val, n, base=0):
    for s in range(TILE_ROWS):
        ref[pl.ds(base + s, n, stride=TILE_ROWS), :] = val[:, s * LANES:(s + 1) * LANES]


def _route_body(x2t_ref, gm_ref, wr_hi_ref, wr_lo_ref, info_ref, cnt_ref, cnt_scr, *, tm):
    @pl.when(pl.program_id(0) == 0)
    def _():
        cnt_scr[...] = jnp.zeros_like(cnt_scr)

    hm = _rms(_tiles_to_rows(x2t_ref, tm), gm_ref[...])
    i1, i2, w1, w2 = _router(hm, wr_hi_ref, wr_lo_ref)
    lane_i = lax.broadcasted_iota(jnp.int32, (tm, ROUTER_LANES), 1)
    lane = lane_i.astype(F32)
    chosen = ((lane == i1) | (lane == i2)).astype(F32)
    earlier = (lax.broadcasted_iota(jnp.int32, (tm, tm), 1) < lax.broadcasted_iota(jnp.int32, (tm, tm), 0))
    before = _bdot(earlier.astype(BF16), chosen.astype(BF16)) + cnt_scr[0:1, :]
    r1 = jnp.sum(jnp.where(lane == i1, before, 0.0), axis=-1, keepdims=True)
    r2 = jnp.sum(jnp.where(lane == i2, before, 0.0), axis=-1, keepdims=True)
    cols = (i1, i2, w1, w2, r1, r2)
    info = jnp.zeros((tm, ROUTER_LANES), F32)
    for j, col in enumerate(cols):
        info = jnp.where(lane_i == j, col, info)
    info_ref[...] = info
    cnt_scr[...] = cnt_scr[...] + jnp.sum(chosen, axis=0, keepdims=True)
    cnt_ref[...] = cnt_scr[...]


def _route(x2t, g_moe, wr_hi, wr_lo, tm=512):
    t = x2t.shape[0] // TILE_ROWS

    def const(shape):
        return pl.BlockSpec(shape, lambda i: (0,) * len(shape))

    body = functools.partial(_route_body, tm=tm)
    return pl.pallas_call(
        body,
        grid=(t // tm,),
        in_specs=[pl.BlockSpec((tm * TILE_ROWS, LANES), lambda i: (i, 0)), const((1, D_MODEL)),
                  const((D_MODEL, ROUTER_LANES)), const((D_MODEL, ROUTER_LANES))],
        out_specs=[pl.BlockSpec((tm, ROUTER_LANES), lambda i: (i, 0)), const((SUBLANES, ROUTER_LANES))],
        out_shape=[jax.ShapeDtypeStruct((t, ROUTER_LANES), F32),
                   jax.ShapeDtypeStruct((SUBLANES, ROUTER_LANES), F32)],
        scratch_shapes=[pltpu.VMEM((SUBLANES, ROUTER_LANES), F32)],
        compiler_params=_params("arbitrary"),
        name="route",
    )(x2t, g_moe, wr_hi, wr_lo)


def _route_meta(info, cnt, tme):
    t = info.shape[0]
    n_tiles = 2 * t // tme
    counts = cnt[0, N_EXPERT_GROUPS:N_EXPERT_GROUPS + N_EXPERTS].astype(jnp.int32)
    ends = jnp.cumsum(counts)
    starts = ends - counts
    e = info[:, 0:2].astype(jnp.int32) - N_EXPERT_GROUPS
    onehot = e[:, :, None] == jnp.arange(N_EXPERTS, dtype=jnp.int32)
    slot = (jnp.sum(jnp.where(onehot, starts, 0), axis=-1) + info[:, 4:6].astype(jnp.int32)).reshape(-1)
    pts = jnp.concatenate([jnp.arange(n_tiles, dtype=jnp.int32) * tme, starts])
    n = pts.shape[0]
    idx = jnp.arange(n, dtype=jnp.int32)
    pos = jnp.sum((pts[None, :] < pts[:, None]) | ((pts[None, :] == pts[:, None]) & (idx[None, :] < idx[:, None])),
                  axis=1)
    lo_abs = jnp.sum(jnp.where(pos[:, None] == idx[None, :], pts[:, None], 0), axis=0)
    hi_abs = jnp.concatenate([lo_abs[1:], jnp.full((1,), 2 * t, jnp.int32)])
    tile = jnp.minimum(lo_abs // tme, n_tiles - 1)
    expert = jnp.clip(jnp.sum(starts[None, :] <= lo_abs[:, None], axis=1) - 1, 0, N_EXPERTS - 1)
    first = jnp.concatenate([jnp.ones((1,), jnp.int32), (tile[1:] != tile[:-1]).astype(jnp.int32)])
    last = jnp.concatenate([(tile[1:] != tile[:-1]).astype(jnp.int32), jnp.ones((1,), jnp.int32)])
    work = [a.astype(jnp.int32) for a in (tile, expert, lo_abs - tile * tme, hi_abs - tile * tme, first, last)]
    return slot.astype(jnp.int32), work


def _tile_copy(src, src_tok, dst, dst_tok, sem):
    return pltpu.make_async_copy(src.at[pl.ds(pl.multiple_of(src_tok * TILE_ROWS, TILE_ROWS), TILE_ROWS), :],
                                 dst.at[pl.ds(pl.multiple_of(dst_tok * TILE_ROWS, TILE_ROWS), TILE_ROWS), :], sem)


def _scatter_body(slot_ref, x2t_ref, xs_hbm, sem, *, tm):
    def issue(r, c):
        _tile_copy(x2t_ref, r, xs_hbm, slot_ref[2 * r], sem).start(priority=0)
        _tile_copy(x2t_ref, r, xs_hbm, slot_ref[2 * r + 1], sem).start(priority=1)
        return c

    lax.fori_loop(0, tm, issue, 0, unroll=8)

    def drain(r, c):
        _tile_copy(x2t_ref, r, xs_hbm, 0, sem).wait()
        _tile_copy(x2t_ref, r, xs_hbm, 0, sem).wait()
        return c

    lax.fori_loop(0, tm, drain, 0, unroll=8)


def _scatter(slot, x2t, tm=512):
    t = x2t.shape[0] // TILE_ROWS
    body = functools.partial(_scatter_body, tm=tm)
    return pl.pallas_call(
        body,
        grid=(t // tm,),
        in_specs=[pl.BlockSpec((2 * tm,), lambda i: (i,), memory_space=pltpu.SMEM),
                  pl.BlockSpec((tm * TILE_ROWS, LANES), lambda i: (i, 0))],
        out_specs=pl.BlockSpec(memory_space=pl.ANY),
        out_shape=jax.ShapeDtypeStruct((2 * t * TILE_ROWS, LANES), F32),
        scratch_shapes=[pltpu.SemaphoreType.DMA(())],
        compiler_params=_params("arbitrary"),
        name="scatter",
    )(slot, x2t)


def _expert_body(tile_ref, exp_ref, lo_ref, hi_ref, first_ref, last_ref, xs_ref, gm_ref, wg_ref, wu_ref, wd_ref,
                 ys_ref, acc_scr, *, tme):
    k = pl.program_id(0)

    @pl.when(first_ref[k] == 1)
    def _():
        acc_scr[...] = jnp.zeros_like(acc_scr)

    @pl.when(hi_ref[k] > lo_ref[k])
    def _():
        hb = _rms(_tiles_to_rows(xs_ref, tme), gm_ref[...]).astype(BF16)
        act = jax.nn.silu(_bdot(hb, wg_ref[0])) * _bdot(hb, wu_ref[0])
        y = _bdot(act.astype(BF16), wd_ref[0])
        row = lax.broadcasted_iota(jnp.int32, (tme, 1), 0)
        acc_scr[...] += jnp.where((row >= lo_ref[k]) & (row < hi_ref[k]), y, 0.0)

    @pl.when(last_ref[k] == 1)
    def _():
        _rows_to_tiles(ys_ref, acc_scr[...], tme)


def _experts(work, xs, g_moe, wg, wu, wd, tme):
    n_work = work[0].shape[0]
    body = functools.partial(_expert_body, tme=tme)
    tile_blk = pl.BlockSpec((tme * TILE_ROWS, LANES), lambda k, tile, *_: (tile[k], 0))

    def wspec(shape):
        return pl.BlockSpec(shape, lambda k, tile, exp, *_: (exp[k], 0, 0))

    grid_spec = pltpu.PrefetchScalarGridSpec(
        num_scalar_prefetch=len(work),
        grid=(n_work,),
        in_specs=[tile_blk, pl.BlockSpec((1, D_MODEL), lambda k, *_: (0, 0)),
                  wspec((1, D_MODEL, D_FF_EXPERT)), wspec((1, D_MODEL, D_FF_EXPERT)),
                  wspec((1, D_FF_EXPERT, D_MODEL))],
        out_specs=tile_blk,
        scratch_shapes=[pltpu.VMEM((tme, D_MODEL), F32)],
    )
    return pl.pallas_call(
        body,
        grid_spec=grid_spec,
        out_shape=jax.ShapeDtypeStruct(xs.shape, F32),
        compiler_params=_params("arbitrary"),
        name="experts",
    )(*work, xs, g_moe, wg, wu, wd)


def _combine_body(slot_ref, slot_next_ref, x2t_ref, info_ref, ys_hbm, gf_ref, y_ref, b0, b1, sem, *, tm, nstep):
    i = pl.program_id(0)
    cur = i % 2

    def gather(slots, buf):
        def issue(r, c):
            _tile_copy(ys_hbm, slots[2 * r], b0.at[buf], r, sem.at[buf]).start(priority=0)
            _tile_copy(ys_hbm, slots[2 * r + 1], b1.at[buf], r, sem.at[buf]).start(priority=1)
            return c

        lax.fori_loop(0, tm, issue, 0, unroll=8)

    @pl.when(i == 0)
    def _():
        gather(slot_ref, 0)

    @pl.when(i + 1 < nstep)
    def _():
        gather(slot_next_ref, 1 - cur)

    def drain(r, c):
        _tile_copy(ys_hbm, 0, b0.at[cur], r, sem.at[cur]).wait()
        _tile_copy(ys_hbm, 0, b1.at[cur], r, sem.at[cur]).wait()
        return c

    lax.fori_loop(0, tm, drain, 0, unroll=8)
    info = info_ref[...]
    moe = info[:, 2:3] * _tiles_to_rows(b0.at[cur], tm) + info[:, 3:4] * _tiles_to_rows(b1.at[cur], tm)
    y_ref[...] = _rms(_tiles_to_rows(x2t_ref, tm) + moe, gf_ref[...])


def _combine(slot, x2t, info, ys, g_final, tm=256):
    t = info.shape[0]
    nstep = t // tm
    body = functools.partial(_combine_body, tm=tm, nstep=nstep)
    buf = pltpu.VMEM((2, tm * TILE_ROWS, LANES), F32)
    return pl.pallas_call(
        body,
        grid=(nstep,),
        in_specs=[pl.BlockSpec((2 * tm,), lambda i: (i,), memory_space=pltpu.SMEM),
                  pl.BlockSpec((2 * tm,), lambda i: (jnp.minimum(i + 1, nstep - 1),), memory_space=pltpu.SMEM),
                  pl.BlockSpec((tm * TILE_ROWS, LANES), lambda i: (i, 0)),
                  pl.BlockSpec((tm, ROUTER_LANES), lambda i: (i, 0)),
                  pl.BlockSpec(memory_space=pl.ANY),
                  pl.BlockSpec((1, D_MODEL), lambda i: (0, 0))],
        out_specs=pl.BlockSpec((tm, D_MODEL), lambda i: (i, 0)),
        out_shape=jax.ShapeDtypeStruct((t, D_MODEL), F32),
        scratch_shapes=[buf, buf, pltpu.SemaphoreType.DMA((2,))],
        compiler_params=_params("arbitrary"),
        name="combine",
    )(slot, slot, x2t, info, ys, g_final)


def _moe(x2t, w, tme):
    info, cnt = _route(x2t, w["g_moe"], w["wr_hi"], w["wr_lo"])
    slot, work = _route_meta(info, cnt, tme)
    xs = _scatter(slot, x2t)
    ys = _experts(work, xs, w["g_moe"], w["moe_wg"], w["moe_wu"], w["moe_wd"], tme)
    return _combine(slot, x2t, info, ys, w["g_final"])


def _layer(x3d, k4, v4, s5_h0, lru_h0, lru_conv, w, chunk, nb, tq, tme):
    bsz, seq, _ = x3d.shape
    x2d = x3d.reshape(bsz * seq, D_MODEL)
    u, xl, gl = _mixin(x2d, w["g_mix"], w["w_in"])

    nstate = S5_GROUPS * S5_STATE
    if s5_h0 is None:
        zero = jnp.zeros((bsz, 1, nstate), F32)
        ys, s5_re, s5_im = _s5(u, w["s5"][chunk], zero, zero, chunk, seq // chunk, True)
    else:
        ys, s5_re, s5_im = _s5(u, w["s5"][chunk], s5_h0[0].reshape(bsz, nstate), s5_h0[1].reshape(bsz, nstate),
                               chunk, bsz, False)
    s5_re = s5_re.reshape(bsz, S5_GROUPS, S5_STATE)
    s5_im = s5_im.reshape(bsz, S5_GROUPS, S5_STATE)

    lru_w = (w["conv_w"], w["conv_b"], w["lru_wg"], w["lru_bg"], w["lru_lam"])
    if lru_h0 is None:
        yl, hl = _lru_seq(xl, gl, *lru_w, bsz, seq)
        lru_h = hl.reshape(bsz, SUBLANES, D_LRU)[:, 0]
    else:
        hist = jnp.pad(lru_conv, ((0, 0), (SUBLANES - (CONV_W - 1), 0), (0, 0))).reshape(bsz * seq, D_LRU)
        h0rep = jnp.repeat(lru_h0, seq, axis=0)
        yl, hall = _lru_dec(xl, gl, hist, h0rep, *lru_w)
        lru_h = hall.reshape(bsz, seq, D_LRU)[:, seq - 1]
    conv_new = xl.reshape(bsz, seq, D_LRU)[:, seq - (CONV_W - 1):]

    x1, q = _mixout(x2d, ys, yl, w["w_glu"], w["b_glu"], w["w_out"], w["g_xa"], w["w_q"])
    x2t = _attn(q.reshape(bsz, seq, D_MODEL), x1.reshape(bsz, seq, D_MODEL), k4, v4, w["w_o"], nb, tq)
    y = _moe(x2t, w, tme)
    return y.reshape(bsz, seq, D_MODEL), s5_re, s5_im, lru_h, conv_new


def kernel(x_prompt, x_sample, mem_prompt, cache_mem_k, cache_mem_v, state_s5_re, state_s5_im, state_lru_h, state_lru_conv, g_mix, w_in, s5_lam_re, s5_lam_im, s5_log_dt, s5_b_re, s5_b_im, s5_c_re, s5_c_im, s5_d, s5_w_glu, s5_b_glu, lru_conv_w, lru_conv_b, lru_w_a, lru_b_a, lru_w_x, lru_b_x, lru_lam, w_out, g_xa, g_mem, xa_w_q, xa_w_k, xa_w_v, xa_w_o, g_moe, moe_w_group, moe_w_expert, moe_w_gate, moe_w_up, moe_w_down, g_final):
    depth = g_mix.shape[0]
    assert depth == 1, "single-layer step"
    l = 0
    bsz, seq, _ = x_prompt.shape
    dbsz, dseq, _ = x_sample.shape
    chunk_p, chunk_s = 16, dseq
    assert seq % chunk_p == 0 and dseq == SUBLANES

    s5_args = (s5_lam_re[l], s5_lam_im[l], s5_log_dt[l], s5_b_re[l], s5_b_im[l], s5_c_re[l], s5_c_im[l], s5_d[l])
    wg, bg = _lru_weights(lru_w_a[l], lru_w_x[l], lru_b_a[l], lru_b_x[l])
    wr = jnp.concatenate([moe_w_group[l], moe_w_expert[l],
                          jnp.zeros((D_MODEL, ROUTER_LANES - N_EXPERT_GROUPS - N_EXPERTS), F32)], axis=1)
    wr_hi = wr.astype(BF16)
    wr_lo = (wr - wr_hi.astype(F32)).astype(BF16)
    w = {
        "g_mix": g_mix[l][None], "w_in": w_in[l].astype(BF16),
        "s5": {c: _s5_weights(c, *s5_args) for c in {chunk_p, chunk_s}},
        "conv_w": lru_conv_w[l], "conv_b": lru_conv_b[l][None], "lru_wg": wg, "lru_bg": bg,
        "lru_lam": lru_lam[l][None],
        "w_glu": s5_w_glu[l].astype(BF16), "b_glu": s5_b_glu[l][None], "w_out": w_out[l].astype(BF16),
        "g_xa": g_xa[l][None], "w_q": xa_w_q[l].astype(BF16), "w_o": xa_w_o[l].astype(BF16),
        "g_moe": g_moe[l][None], "wr_hi": wr_hi, "wr_lo": wr_lo,
        "moe_wg": moe_w_gate[l].astype(BF16), "moe_wu": moe_w_up[l].astype(BF16),
        "moe_wd": moe_w_down[l].astype(BF16), "g_final": g_final[None],
    }

    mk, mv = _memkv(mem_prompt.reshape(bsz * N_MEM, D_MODEL), g_mem[l][None],
                    xa_w_k[l].astype(BF16), xa_w_v[l].astype(BF16))
    yp, p_re, p_im, p_h, p_conv = _layer(x_prompt, mk.reshape(bsz, N_MEM, D_MODEL), mv.reshape(bsz, N_MEM, D_MODEL),
                                         None, None, None, w, chunk_p, nb=1, tq=512, tme=512)
    ysmp, s_re, s_im, s_h, s_conv = _layer(x_sample, cache_mem_k[l], cache_mem_v[l],
                                           (state_s5_re[l], state_s5_im[l]), state_lru_h[l],
                                           state_lru_conv[l], w, chunk_s, nb=4, tq=dseq, tme=256)

    return (yp, ysmp,
            mk.reshape(1, bsz, N_MEM, XA_HEADS, XA_HEAD_DIM), mv.reshape(1, bsz, N_MEM, XA_HEADS, XA_HEAD_DIM),
            p_re[None], p_im[None], p_h[None], p_conv[None],
            s_re[None], s_im[None], s_h[None], s_conv[None])
```

```python
import functools

import jax
import jax.numpy as jnp
import numpy as np
from jax import lax
from jax.experimental import pallas as pl
from jax.experimental.pallas import tpu as pltpu

F32 = jnp.float32
BF16 = jnp.bfloat16

D_MODEL = 1024
D_S5 = 512
D_LRU = 512
S5_GROUP = 16
S5_GROUPS = 32
S5_STATE = 64
LRU_HEADS = 8
LRU_HEAD_DIM = 64
CONV_W = 4
LRU_C = 8.0
N_MEM = 256
XA_HEADS = 4
XA_HEAD_DIM = 256
N_EXPERT_GROUPS = 4
EXPERTS_PER_GROUP = 8
N_EXPERTS = 32
D_FF_EXPERT = 256
EPS = 1e-6

SUBLANES = 8
LANES = 128
ROUTER_LANES = 128
VMEM_LIMIT = 48 * 1024 * 1024


def _params(*sem):
    return pltpu.CompilerParams(dimension_semantics=sem, vmem_limit_bytes=VMEM_LIMIT)


def _rms(x, g):
    ms = jnp.mean(x * x, axis=-1, keepdims=True)
    return x * lax.rsqrt(ms + EPS) * g


def _bdot(a, b):
    return jnp.dot(a, b, preferred_element_type=F32)


def _memkv_body(m_ref, g_ref, wk_ref, wv_ref, k_ref, v_ref):
    hb = _rms(m_ref[...], g_ref[...]).astype(BF16)
    k_ref[...] = _bdot(hb, wk_ref[...])
    v_ref[...] = _bdot(hb, wv_ref[...])


def _memkv(mem2d, g_mem, wk, wv, tm=512):
    t = mem2d.shape[0]
    row = pl.BlockSpec((tm, D_MODEL), lambda i: (i, 0))
    full = pl.BlockSpec((D_MODEL, D_MODEL), lambda i: (0, 0))
    return pl.pallas_call(
        _memkv_body,
        grid=(t // tm,),
        in_specs=[row, pl.BlockSpec((1, D_MODEL), lambda i: (0, 0)), full, full],
        out_specs=[row, row],
        out_shape=[jax.ShapeDtypeStruct((t, D_MODEL), F32)] * 2,
        compiler_params=_params("parallel"),
        name="memkv",
    )(mem2d, g_mem, wk, wv)


def _mixin_body(x_ref, g_ref, w_ref, u_ref, xl_ref, gl_ref):
    hb = _rms(x_ref[...], g_ref[...]).astype(BF16)
    proj = _bdot(hb, w_ref[...])
    u_ref[...] = proj[:, :D_S5]
    xl_ref[...] = proj[:, D_S5:D_S5 + D_LRU]
    gl_ref[...] = proj[:, D_S5 + D_LRU:]


def _mixin(x2d, g_mix, w_in, tm=512):
    t = x2d.shape[0]
    half = pl.BlockSpec((tm, D_S5), lambda i: (i, 0))
    return pl.pallas_call(
        _mixin_body,
        grid=(t // tm,),
        in_specs=[pl.BlockSpec((tm, D_MODEL), lambda i: (i, 0)),
                  pl.BlockSpec((1, D_MODEL), lambda i: (0, 0)),
                  pl.BlockSpec((D_MODEL, D_S5 + 2 * D_LRU), lambda i: (0, 0))],
        out_specs=[half, half, half],
        out_shape=[jax.ShapeDtypeStruct((t, D_S5), F32)] * 3,
        compiler_params=_params("parallel"),
        name="mixin",
    )(x2d, g_mix, w_in)


GROUPS_PER_BLOCK = LANES // S5_GROUP
PAIRS_PER_BLOCK = GROUPS_PER_BLOCK // 2
STATE_BLOCK = GROUPS_PER_BLOCK * S5_STATE
PW_ROWS = 24


def _s5_body(u_ref, wa_ref, wh_ref, pw_ref, d_ref, h0r_ref, h0i_ref, y_ref, hfr_ref, hfi_ref, hpr_scr, hpi_scr,
             *, chunk, nrow, scan):
    nh = chunk // SUBLANES
    lc = chunk * S5_GROUP
    slot = lax.broadcasted_iota(jnp.int32, (nrow, LANES), 1) // S5_GROUP
    in_slot = [slot == s for s in range(GROUPS_PER_BLOCK)]

    def pick(src, sel):
        out = src[sel(0)]
        for s in range(1, GROUPS_PER_BLOCK):
            out = jnp.where(in_slot[s], src[sel(s)], out)
        return out

    nat, skew = [], []
    for t in range(chunk):
        a = u_ref[pl.ds(t, nrow, stride=chunk), :]
        nat.append(a)
        s = t % GROUPS_PER_BLOCK
        skew.append(pltpu.roll(a, s * S5_GROUP, axis=1) if s else a)
    def lane_roll(x, slots):
        slots %= GROUPS_PER_BLOCK
        return pltpu.roll(x, slots * S5_GROUP, axis=1) if slots else x

    z = [[lane_roll(pick(skew, lambda s, hh=hh, m=m: SUBLANES * hh + (s - m) % GROUPS_PER_BLOCK), -m)
          for hh in range(nh)] for m in range(GROUPS_PER_BLOCK)]

    ys, er, ei = [], [], []
    for p in range(PAIRS_PER_BLOCK):
        lhs = jnp.concatenate(z[2 * p] + z[2 * p + 1], axis=1).astype(BF16)
        full = _bdot(lhs, wa_ref[p])
        ys.append(full[:, :2 * lc])
        er.append(full[:, 2 * lc:2 * lc + LANES])
        ei.append(full[:, 2 * lc + LANES:])
    er = jnp.concatenate(er, axis=1)
    ei = jnp.concatenate(ei, axis=1)

    if scan:
        ntile = nrow // SUBLANES
        xr = er.reshape(ntile, SUBLANES, STATE_BLOCK)
        xi = ei.reshape(ntile, SUBLANES, STATE_BLOCK)
        row = lax.broadcasted_iota(jnp.int32, xr.shape, 1)
        for k, d in enumerate((1, 2, 4)):
            pr, pi = pw_ref[0, k:k + 1, :], pw_ref[0, 4 + k:5 + k, :]
            sr, si = pltpu.roll(xr, d, axis=1), pltpu.roll(xi, d, axis=1)
            m = row >= d
            xr, xi = jnp.where(m, xr + pr * sr - pi * si, xr), jnp.where(m, xi + pr * si + pi * sr, xi)
        sxr = jnp.where(row >= 1, pltpu.roll(xr, 1, axis=1), 0.0)
        sxi = jnp.where(row >= 1, pltpu.roll(xi, 1, axis=1), 0.0)
        p8r, p8i = pw_ref[0, 3:4, :], pw_ref[0, 7:8, :]
        qr, qi = pw_ref[0, 8:16, :], pw_ref[0, 16:24, :]
        hr, hi = h0r_ref[0], h0i_ref[0]
        for t in range(ntile):
            rows = slice(SUBLANES * t, SUBLANES * (t + 1))
            hpr_scr[rows, :] = sxr[t] + qr * hr - qi * hi
            hpi_scr[rows, :] = sxi[t] + qr * hi + qi * hr
            hr, hi = (xr[t, SUBLANES - 1:] + p8r * hr - p8i * hi, xi[t, SUBLANES - 1:] + p8r * hi + p8i * hr)
        hfr_ref[0] = hr
        hfi_ref[0] = hi
        hpr, hpi = hpr_scr[...], hpi_scr[...]
    else:
        hpr, hpi = h0r_ref[...], h0i_ref[...]
        p1r, p1i = pw_ref[0, 0:1, :], pw_ref[0, 4:5, :]
        hfr_ref[...] = er + p1r * hpr - p1i * hpi
        hfi_ref[...] = ei + p1r * hpi + p1i * hpr

    yg = []
    for p in range(PAIRS_PER_BLOCK):
        lanes = slice(LANES * p, LANES * (p + 1))
        hp = jnp.concatenate([hpr[:, lanes], hpi[:, lanes]], axis=1).astype(BF16)
        out = ys[p] + _bdot(hp, wh_ref[p])
        for half in range(2):
            q = 2 * p + half
            yg.append([lane_roll(out[:, half * lc + hh * LANES:half * lc + (hh + 1) * LANES], q)
                       for hh in range(nh)])
    d = d_ref[...]
    for t in range(chunk):
        hh, s = divmod(t, GROUPS_PER_BLOCK)
        c = lane_roll(pick([g[hh] for g in yg], lambda sl, s=s: (sl - s) % GROUPS_PER_BLOCK), -s)
        y_ref[pl.ds(t, nrow, stride=chunk), :] = c + d * nat[t]


def _s5(u2d, weights, h0r, h0i, chunk, nrow, scan):
    wa, wh, pw, dvec = weights
    t = u2d.shape[0]
    lc = chunk * S5_GROUP
    rows = nrow * chunk
    nblk = D_S5 // LANES
    body = functools.partial(_s5_body, chunk=chunk, nrow=nrow, scan=scan)
    ublk = pl.BlockSpec((rows, LANES), lambda i, j: (i, j))
    if scan:
        hblk = pl.BlockSpec((1, 1, STATE_BLOCK), lambda i, j: (i, 0, j))
        hshape = jax.ShapeDtypeStruct((t // rows, 1, S5_GROUPS * S5_STATE), F32)
    else:
        hblk = pl.BlockSpec((nrow, STATE_BLOCK), lambda i, j: (i, j))
        hshape = jax.ShapeDtypeStruct((t // chunk, S5_GROUPS * S5_STATE), F32)
    return pl.pallas_call(
        body,
        grid=(t // rows, nblk),
        in_specs=[ublk,
                  pl.BlockSpec((PAIRS_PER_BLOCK, 2 * lc, 2 * lc + 2 * LANES), lambda i, j: (j, 0, 0)),
                  pl.BlockSpec((PAIRS_PER_BLOCK, 2 * LANES, 2 * lc), lambda i, j: (j, 0, 0)),
                  pl.BlockSpec((1, PW_ROWS, STATE_BLOCK), lambda i, j: (j, 0, 0)),
                  pl.BlockSpec((1, LANES), lambda i, j: (0, j)),
                  hblk, hblk],
        out_specs=[ublk, hblk, hblk],
        out_shape=[jax.ShapeDtypeStruct((t, D_S5), F32), hshape, hshape],
        scratch_shapes=[pltpu.VMEM((nrow, STATE_BLOCK), F32), pltpu.VMEM((nrow, STATE_BLOCK), F32)],
        compiler_params=_params("parallel", "parallel"),
        name="s5",
    )(u2d, wa, wh, pw, dvec, h0r, h0i)


def _s5_weights(chunk, lam_re, lam_im, log_dt, b_re, b_im, c_re, c_im, d_skip):
    hi = lax.Precision.HIGHEST
    dt = jnp.exp(log_dt)[:, None]
    mag = jnp.exp(lam_re * dt)
    ab_re = mag * jnp.cos(lam_im * dt)
    ab_im = mag * jnp.sin(lam_im * dt)
    den = lam_re * lam_re + lam_im * lam_im
    nr = ab_re - 1.0
    f_re = (nr * lam_re + ab_im * lam_im) / den
    f_im = (ab_im * lam_re - nr * lam_im) / den
    bb_re = f_re[..., None] * b_re - f_im[..., None] * b_im
    bb_im = f_re[..., None] * b_im + f_im[..., None] * b_re
    def powers(ks):
        k = jnp.asarray(np.asarray(ks, np.float32))[:, None, None]
        m = jnp.exp(k * (lam_re * dt))
        return m * jnp.cos(k * (lam_im * dt)), m * jnp.sin(k * (lam_im * dt))

    pw_re, pw_im = powers(range(chunk + 1))
    pb_re = pw_re[:chunk, ..., None] * bb_re - pw_im[:chunk, ..., None] * bb_im
    pb_im = pw_re[:chunk, ..., None] * bb_im + pw_im[:chunk, ..., None] * bb_re
    kk = (jnp.einsum("gon,kgni->kgoi", c_re, pb_re, precision=hi)
          - jnp.einsum("gon,kgni->kgoi", c_im, pb_im, precision=hi))
    npair = S5_GROUPS // 2
    lc = chunk * S5_GROUP
    row0 = kk.transpose(1, 3, 0, 2).reshape(S5_GROUPS, S5_GROUP, lc)
    row0 = jnp.pad(row0, ((0, 0), (0, 0), (lc, 0)))
    conv = jnp.stack([row0[:, :, lc - S5_GROUP * ti:2 * lc - S5_GROUP * ti] for ti in range(chunk)], axis=1)
    conv = conv.reshape(npair, 2, lc, lc)
    ends = [x[::-1].transpose(1, 0, 3, 2).reshape(npair, 2, lc, S5_STATE) for x in (pb_re, pb_im)]
    zc = jnp.zeros((npair, lc, lc), F32)
    zs = jnp.zeros((npair, lc, S5_STATE), F32)
    wa = jnp.concatenate([
        jnp.concatenate([conv[:, 0], zc, ends[0][:, 0], zs, ends[1][:, 0], zs], axis=-1),
        jnp.concatenate([zc, conv[:, 1], zs, ends[0][:, 1], zs, ends[1][:, 1]], axis=-1)], axis=1)
    gr = c_re[None] * pw_re[1:, :, None, :] - c_im[None] * pw_im[1:, :, None, :]
    gi = c_re[None] * pw_im[1:, :, None, :] + c_im[None] * pw_re[1:, :, None, :]
    zr = jnp.zeros((npair, S5_STATE, lc), F32)
    blocks = []
    for x in (gr, -gi):
        hy = x.transpose(1, 3, 0, 2).reshape(npair, 2, S5_STATE, lc)
        blocks += [jnp.concatenate([hy[:, 0], zr], axis=-1), jnp.concatenate([zr, hy[:, 1]], axis=-1)]
    wh = jnp.concatenate(blocks, axis=1)
    tab = [1, 2, 4, 8] + list(range(SUBLANES))
    qr, qi = powers([chunk * k for k in tab])
    nblk = D_S5 // LANES
    pw = jnp.concatenate([qr[:4], qi[:4], qr[4:], qi[4:]]).reshape(PW_ROWS, nblk, STATE_BLOCK).transpose(1, 0, 2)
    return wa.astype(BF16), wh.astype(BF16), pw, d_skip.reshape(1, D_S5)


def _lru_ab(xc, wg_ref, bg_ref, lam_ref):
    xb = xc.astype(BF16)
    half = D_LRU // 2
    g0 = _bdot(xb[:, :half], wg_ref[0]) + bg_ref[0]
    g1 = _bdot(xb[:, half:], wg_ref[1]) + bg_ref[1]
    r = jax.nn.sigmoid(jnp.concatenate([g0[:, :half], g1[:, :half]], axis=1))
    ig = jax.nn.sigmoid(jnp.concatenate([g0[:, half:], g1[:, half:]], axis=1))
    zl = -lam_ref[...]
    softplus = jnp.maximum(zl, 0.0) + jnp.log1p(jnp.exp(-jnp.abs(zl)))
    log_a = -LRU_C * r * softplus
    a = jnp.exp(log_a)
    b = jnp.sqrt(-jnp.tanh(log_a) * (a * a + 1.0)) * (ig * xc)
    return a, b


def _tile_scan(a, b):
    shape = a.shape
    a = a.reshape(shape[0] // SUBLANES, SUBLANES, shape[1])
    b = b.reshape(a.shape)
    row = lax.broadcasted_iota(jnp.int32, a.shape, 1)
    for d in (1, 2, 4):
        a_prev = pltpu.roll(a, d, axis=1)
        b_prev = pltpu.roll(b, d, axis=1)
        m = row >= d
        b = jnp.where(m, b + a * b_prev, b)
        a = jnp.where(m, a * a_prev, a)
    return a.reshape(shape), b.reshape(shape)


def _lru_seq_body(xl_ref, gl_ref, cw_ref, cb_ref, wg_ref, bg_ref, lam_ref, y_ref, hl_ref,
                  xp_scr, a_scr, b_scr, h_scr, hc_scr, *, ts):
    ti = pl.program_id(1)

    @pl.when(ti == 0)
    def _():
        xp_scr[0:SUBLANES, :] = jnp.zeros((SUBLANES, D_LRU), F32)
        hc_scr[...] = jnp.zeros((SUBLANES, D_LRU), F32)

    xl = xl_ref[...]
    xp_scr[SUBLANES:SUBLANES + ts, :] = xl
    xc = cb_ref[...] + xl * cw_ref[CONV_W - 1:CONV_W, :]
    for j in range(1, CONV_W):
        xc = xc + xp_scr[SUBLANES - j:SUBLANES - j + ts, :] * cw_ref[CONV_W - 1 - j:CONV_W - j, :]
    xp_scr[0:SUBLANES, :] = xl[ts - SUBLANES:, :]
    a, b = _lru_ab(xc, wg_ref, bg_ref, lam_ref)
    a, b = _tile_scan(a, b)
    a_scr[...] = a
    b_scr[...] = b

    def step(i, hin):
        rows = pl.ds(pl.multiple_of(i * SUBLANES, SUBLANES), SUBLANES)
        h = b_scr[rows, :] + a_scr[rows, :] * hin
        h_scr[rows, :] = h
        return h[SUBLANES - 1:SUBLANES, :]

    hlast = lax.fori_loop(0, ts // SUBLANES, step, hc_scr[0:1, :], unroll=4)
    hc_scr[...] = jnp.broadcast_to(hlast, (SUBLANES, D_LRU))
    hl_ref[...] = hc_scr[...]
    y_ref[...] = h_scr[...] * jax.nn.gelu(gl_ref[...])


def _lru_dec_body(xl_ref, gl_ref, hist_ref, h0_ref, cw_ref, cb_ref, wg_ref, bg_ref, lam_ref, y_ref, hl_ref, *, tm):
    xl = xl_ref[...]
    hist = hist_ref[...]
    row = lax.broadcasted_iota(jnp.int32, xl.shape, 0) % SUBLANES
    xc = cb_ref[...] + xl * cw_ref[CONV_W - 1:CONV_W, :]
    for j in range(1, CONV_W):
        prev = jnp.where(row >= j, pltpu.roll(xl, j, axis=0), pltpu.roll(hist, tm - SUBLANES + j, axis=0))
        xc = xc + prev * cw_ref[CONV_W - 1 - j:CONV_W - j, :]
    a, b = _lru_ab(xc, wg_ref, bg_ref, lam_ref)
    a, b = _tile_scan(a, b)
    h = b + a * h0_ref[...]
    hl_ref[...] = h
    y_ref[...] = h * jax.nn.gelu(gl_ref[...])


def _lru_weights(w_a, w_x, b_a, b_x):
    eye = jnp.eye(LRU_HEADS, dtype=F32)
    bd_a = jnp.einsum("hij,hg->higj", w_a, eye).reshape(D_LRU, D_LRU)
    bd_x = jnp.einsum("hij,hg->higj", w_x, eye).reshape(D_LRU, D_LRU)
    half = D_LRU // 2
    wg = jnp.stack([jnp.concatenate([bd_a[:half, :half], bd_x[:half, :half]], axis=1),
                    jnp.concatenate([bd_a[half:, half:], bd_x[half:, half:]], axis=1)]).astype(BF16)
    ba = b_a.reshape(1, D_LRU)
    bx = b_x.reshape(1, D_LRU)
    bg = jnp.stack([jnp.concatenate([ba[:, :half], bx[:, :half]], axis=1),
                    jnp.concatenate([ba[:, half:], bx[:, half:]], axis=1)])
    return wg, bg


def _lru_common_specs(const):
    return [const((CONV_W, D_LRU)), const((1, D_LRU)), const((2, D_LRU // 2, D_LRU)),
            const((2, 1, D_LRU)), const((1, D_LRU))]


def _lru_seq(xl, gl, cw, cb, wg, bg, lam, bsz, seq, ts=512):
    nt = seq // ts
    blk = pl.BlockSpec((ts, D_LRU), lambda b, t: (b * nt + t, 0))

    def const(shape):
        return pl.BlockSpec(shape, lambda b, t: (0,) * len(shape))

    body = functools.partial(_lru_seq_body, ts=ts)
    return pl.pallas_call(
        body,
        grid=(bsz, nt),
        in_specs=[blk, blk] + _lru_common_specs(const),
        out_specs=[blk, pl.BlockSpec((SUBLANES, D_LRU), lambda b, t: (b, 0))],
        out_shape=[jax.ShapeDtypeStruct((bsz * seq, D_LRU), F32),
                   jax.ShapeDtypeStruct((bsz * SUBLANES, D_LRU), F32)],
        scratch_shapes=[pltpu.VMEM((ts + SUBLANES, D_LRU), F32), pltpu.VMEM((ts, D_LRU), F32),
                        pltpu.VMEM((ts, D_LRU), F32), pltpu.VMEM((ts, D_LRU), F32),
                        pltpu.VMEM((SUBLANES, D_LRU), F32)],
        compiler_params=_params("arbitrary", "arbitrary"),
        name="lru_seq",
    )(xl, gl, cw, cb, wg, bg, lam)


def _lru_dec(xl, gl, hist, h0rep, cw, cb, wg, bg, lam, tm=256):
    t = xl.shape[0]
    blk = pl.BlockSpec((tm, D_LRU), lambda i: (i, 0))

    def const(shape):
        return pl.BlockSpec(shape, lambda i: (0,) * len(shape))

    body = functools.partial(_lru_dec_body, tm=tm)
    return pl.pallas_call(
        body,
        grid=(t // tm,),
        in_specs=[blk, blk, blk, blk] + _lru_common_specs(const),
        out_specs=[blk, blk],
        out_shape=[jax.ShapeDtypeStruct((t, D_LRU), F32)] * 2,
        compiler_params=_params("parallel"),
        name="lru_dec",
    )(xl, gl, hist, h0rep, cw, cb, wg, bg, lam)


def _mixout_body(x_ref, ys_ref, yl_ref, wglu_ref, bglu_ref, wo_ref, gxa_ref, wq_ref, x1_ref, q_ref):
    ys = jax.nn.gelu(ys_ref[...])
    gate = jax.nn.sigmoid(_bdot(ys.astype(BF16), wglu_ref[...]) + bglu_ref[...])
    s5 = (ys * gate).astype(BF16)
    x1 = (x_ref[...] + _bdot(s5, wo_ref[0:D_S5, :]) + _bdot(yl_ref[...].astype(BF16), wo_ref[D_S5:, :]))
    x1_ref[...] = x1
    q_ref[...] = _bdot(_rms(x1, gxa_ref[...]).astype(BF16), wq_ref[...]).astype(BF16)


def _mixout(x2d, ys, yl, w_glu, b_glu, w_out, g_xa, w_q, tm=512):
    t = x2d.shape[0]
    row = pl.BlockSpec((tm, D_MODEL), lambda i: (i, 0))
    half = pl.BlockSpec((tm, D_S5), lambda i: (i, 0))

    def const(shape):
        return pl.BlockSpec(shape, lambda i: (0,) * len(shape))

    return pl.pallas_call(
        _mixout_body,
        grid=(t // tm,),
        in_specs=[row, half, half, const((D_S5, D_S5)), const((1, D_S5)), const((D_MODEL, D_MODEL)),
                  const((1, D_MODEL)), const((D_MODEL, D_MODEL))],
        out_specs=[row, row],
        out_shape=[jax.ShapeDtypeStruct((t, D_MODEL), F32), jax.ShapeDtypeStruct((t, D_MODEL), BF16)],
        compiler_params=_params("parallel"),
        name="mixout",
    )(x2d, ys, yl, w_glu, b_glu, w_out, g_xa, w_q)


def _softmax(sc):
    p = jnp.exp(sc - jnp.max(sc, axis=-1, keepdims=True))
    return p / jnp.sum(p, axis=-1, keepdims=True)


def _attn_body(q_ref, x1_ref, k_ref, v_ref, wo_ref, x2t_ref, *, nb, tq):
    scale = XA_HEAD_DIM ** -0.5
    for i in range(nb):
        q = q_ref[i]
        heads = [q[:, h * XA_HEAD_DIM:(h + 1) * XA_HEAD_DIM] for h in range(XA_HEADS)]
        if len(k_ref.shape) == 4:
            kf = k_ref[i].reshape(N_MEM * XA_HEADS, XA_HEAD_DIM).astype(BF16)
            vf = v_ref[i].reshape(N_MEM * XA_HEADS, XA_HEAD_DIM).astype(BF16)
            sc = lax.dot_general(jnp.concatenate(heads, axis=0), kf, (((1,), (1,)), ((), ())),
                                 preferred_element_type=F32) * scale
            own = (lax.broadcasted_iota(jnp.int32, sc.shape, 1) % XA_HEADS
                   == lax.broadcasted_iota(jnp.int32, sc.shape, 0) // tq)
            oh = _bdot(_softmax(jnp.where(own, sc, -jnp.inf)).astype(BF16), vf)
            outs = [oh[h * tq:(h + 1) * tq] for h in range(XA_HEADS)]
        else:
            kb = k_ref[i].astype(BF16)
            vb = v_ref[i].astype(BF16)
            outs = []
            for h in range(XA_HEADS):
                cols = slice(h * XA_HEAD_DIM, (h + 1) * XA_HEAD_DIM)
                sc = lax.dot_general(heads[h], kb[:, cols], (((1,), (1,)), ((), ())),
                                     preferred_element_type=F32) * scale
                outs.append(_bdot(_softmax(sc).astype(BF16), vb[:, cols]))
        o = jnp.concatenate(outs, axis=1).astype(BF16)
        _rows_to_tiles(x2t_ref, x1_ref[i] + _bdot(o, wo_ref[...]), tq, base=i * tq * TILE_ROWS)


def _attn(q3, x13, k4, v4, w_o, nb, tq):
    bsz, seq, _ = q3.shape
    nt = seq // tq
    blk = pl.BlockSpec((nb, tq, D_MODEL), lambda b, t: (b, t, 0))
    kv = pl.BlockSpec((nb,) + k4.shape[1:], lambda b, t: (b,) + (0,) * (k4.ndim - 1))
    body = functools.partial(_attn_body, nb=nb, tq=tq)
    return pl.pallas_call(
        body,
        grid=(bsz // nb, nt),
        in_specs=[blk, blk, kv, kv, pl.BlockSpec((D_MODEL, D_MODEL), lambda b, t: (0, 0))],
        out_specs=pl.BlockSpec((nb * tq * TILE_ROWS, LANES), lambda b, t: (b * nt + t, 0)),
        out_shape=jax.ShapeDtypeStruct((bsz * seq * TILE_ROWS, LANES), F32),
        compiler_params=_params("parallel", "parallel"),
        name="attn",
    )(q3, x13, k4, v4, w_o)


def _router(hm, wr_hi_ref, wr_lo_ref):
    a_hi = hm.astype(BF16)
    a_lo = (hm - a_hi.astype(F32)).astype(BF16)
    logits = _bdot(a_hi, wr_hi_ref[...]) + (_bdot(a_hi, wr_lo_ref[...]) + _bdot(a_lo, wr_hi_ref[...]))
    lane_i = lax.broadcasted_iota(jnp.int32, logits.shape, 1)
    lane = lane_i.astype(F32)
    neg = -jnp.inf
    big = float(ROUTER_LANES)
    is_g = lane_i < N_EXPERT_GROUPS
    glog = jnp.where(is_g, logits, neg)
    gmax = jnp.max(glog, axis=-1, keepdims=True)
    gsel = jnp.min(jnp.where(glog == gmax, lane, big), axis=-1, keepdims=True)
    pg_sel = 1.0 / jnp.sum(jnp.where(is_g, jnp.exp(logits - gmax), 0.0), axis=-1, keepdims=True)
    eidx = lane_i - N_EXPERT_GROUPS
    in_group = (eidx >= 0) & (eidx < N_EXPERTS) & ((eidx >> 3).astype(F32) == gsel)
    el = jnp.where(in_group, logits, neg)
    v1 = jnp.max(el, axis=-1, keepdims=True)
    i1 = jnp.min(jnp.where(el == v1, lane, big), axis=-1, keepdims=True)
    el2 = jnp.where(lane == i1, neg, el)
    v2 = jnp.max(el2, axis=-1, keepdims=True)
    i2 = jnp.min(jnp.where(el2 == v2, lane, big), axis=-1, keepdims=True)
    e2 = jnp.exp(v2 - v1)
    w1 = pg_sel / (1.0 + e2)
    w2 = pg_sel * e2 / (1.0 + e2)
    return i1, i2, w1, w2


TILE_ROWS = D_MODEL // LANES


def _tiles_to_rows(ref, n, base=0):
    return jnp.concatenate([ref[pl.ds(base + s, n, stride=TILE_ROWS), :] for s in range(TILE_ROWS)], axis=1)


def _rows_to_tiles(ref, val, n, base=0):
    for s in range(TILE_ROWS):
        ref[pl.ds(base + s, n, stride=TILE_ROWS), :] = val[:, s * LANES:(s + 1) * LANES]


def _route_body(x2t_ref, gm_ref, wr_hi_ref, wr_lo_ref, info_ref, cnt_ref, cnt_scr, *, tm):
    @pl.when(pl.program_id(0) == 0)
    def _():
        cnt_scr[...] = jnp.zeros_like(cnt_scr)

    hm = _rms(_tiles_to_rows(x2t_ref, tm), gm_ref[...])
    i1, i2, w1, w2 = _router(hm, wr_hi_ref, wr_lo_ref)
    lane_i = lax.broadcasted_iota(jnp.int32, (tm, ROUTER_LANES), 1)
    lane = lane_i.astype(F32)
    chosen = ((lane == i1) | (lane == i2)).astype(F32)
    earlier = (lax.broadcasted_iota(jnp.int32, (tm, tm), 1) < lax.broadcasted_iota(jnp.int32, (tm, tm), 0))
    before = _bdot(earlier.astype(BF16), chosen.astype(BF16)) + cnt_scr[0:1, :]
    r1 = jnp.sum(jnp.where(lane == i1, before, 0.0), axis=-1, keepdims=True)
    r2 = jnp.sum(jnp.where(lane == i2, before, 0.0), axis=-1, keepdims=True)
    cols = (i1, i2, w1, w2, r1, r2)
    info = jnp.zeros((tm, ROUTER_LANES), F32)
    for j, col in enumerate(cols):
        info = jnp.where(lane_i == j, col, info)
    info_ref[...] = info
    cnt_scr[...] = cnt_scr[...] + jnp.sum(chosen, axis=0, keepdims=True)
    cnt_ref[...] = cnt_scr[...]


def _route(x2t, g_moe, wr_hi, wr_lo, tm=512):
    t = x2t.shape[0] // TILE_ROWS

    def const(shape):
        return pl.BlockSpec(shape, lambda i: (0,) * len(shape))

    body = functools.partial(_route_body, tm=tm)
    return pl.pallas_call(
        body,
        grid=(t // tm,),
        in_specs=[pl.BlockSpec((tm * TILE_ROWS, LANES), lambda i: (i, 0)), const((1, D_MODEL)),
                  const((D_MODEL, ROUTER_LANES)), const((D_MODEL, ROUTER_LANES))],
        out_specs=[pl.BlockSpec((tm, ROUTER_LANES), lambda i: (i, 0)), const((SUBLANES, ROUTER_LANES))],
        out_shape=[jax.ShapeDtypeStruct((t, ROUTER_LANES), F32),
                   jax.ShapeDtypeStruct((SUBLANES, ROUTER_LANES), F32)],
        scratch_shapes=[pltpu.VMEM((SUBLANES, ROUTER_LANES), F32)],
        compiler_params=_params("arbitrary"),
        name="route",
    )(x2t, g_moe, wr_hi, wr_lo)


def _route_meta(info, cnt, tme):
    t = info.shape[0]
    n_tiles = 2 * t // tme
    counts = cnt[0, N_EXPERT_GROUPS:N_EXPERT_GROUPS + N_EXPERTS].astype(jnp.int32)
    ends = jnp.cumsum(counts)
    starts = ends - counts
    e = info[:, 0:2].astype(jnp.int32) - N_EXPERT_GROUPS
    onehot = e[:, :, None] == jnp.arange(N_EXPERTS, dtype=jnp.int32)
    slot = (jnp.sum(jnp.where(onehot, starts, 0), axis=-1) + info[:, 4:6].astype(jnp.int32)).reshape(-1)
    pts = jnp.concatenate([jnp.arange(n_tiles, dtype=jnp.int32) * tme, starts])
    n = pts.shape[0]
    idx = jnp.arange(n, dtype=jnp.int32)
    pos = jnp.sum((pts[None, :] < pts[:, None]) | ((pts[None, :] == pts[:, None]) & (idx[None, :] < idx[:, None])),
                  axis=1)
    lo_abs = jnp.sum(jnp.where(pos[:, None] == idx[None, :], pts[:, None], 0), axis=0)
    hi_abs = jnp.concatenate([lo_abs[1:], jnp.full((1,), 2 * t, jnp.int32)])
    tile = jnp.minimum(lo_abs // tme, n_tiles - 1)
    expert = jnp.clip(jnp.sum(starts[None, :] <= lo_abs[:, None], axis=1) - 1, 0, N_EXPERTS - 1)
    first = jnp.concatenate([jnp.ones((1,), jnp.int32), (tile[1:] != tile[:-1]).astype(jnp.int32)])
    last = jnp.concatenate([(tile[1:] != tile[:-1]).astype(jnp.int32), jnp.ones((1,), jnp.int32)])
    work = [a.astype(jnp.int32) for a in (tile, expert, lo_abs - tile * tme, hi_abs - tile * tme, first, last)]
    return slot.astype(jnp.int32), work


def _tile_copy(src, src_tok, dst, dst_tok, sem):
    return pltpu.make_async_copy(src.at[pl.ds(pl.multiple_of(src_tok * TILE_ROWS, TILE_ROWS), TILE_ROWS), :],
                                 dst.at[pl.ds(pl.multiple_of(dst_tok * TILE_ROWS, TILE_ROWS), TILE_ROWS), :], sem)


def _scatter_body(slot_ref, x2t_ref, xs_hbm, sem, *, tm):
    def issue(r, c):
        _tile_copy(x2t_ref, r, xs_hbm, slot_ref[2 * r], sem).start(priority=0)
        _tile_copy(x2t_ref, r, xs_hbm, slot_ref[2 * r + 1], sem).start(priority=1)
        return c

    lax.fori_loop(0, tm, issue, 0, unroll=8)

    def drain(r, c):
        _tile_copy(x2t_ref, r, xs_hbm, 0, sem).wait()
        _tile_copy(x2t_ref, r, xs_hbm, 0, sem).wait()
        return c

    lax.fori_loop(0, tm, drain, 0, unroll=8)


def _scatter(slot, x2t, tm=512):
    t = x2t.shape[0] // TILE_ROWS
    body = functools.partial(_scatter_body, tm=tm)
    return pl.pallas_call(
        body,
        grid=(t // tm,),
        in_specs=[pl.BlockSpec((2 * tm,), lambda i: (i,), memory_space=pltpu.SMEM),
                  pl.BlockSpec((tm * TILE_ROWS, LANES), lambda i: (i, 0))],
        out_specs=pl.BlockSpec(memory_space=pl.ANY),
        out_shape=jax.ShapeDtypeStruct((2 * t * TILE_ROWS, LANES), F32),
        scratch_shapes=[pltpu.SemaphoreType.DMA(())],
        compiler_params=_params("arbitrary"),
        name="scatter",
    )(slot, x2t)


def _expert_body(tile_ref, exp_ref, lo_ref, hi_ref, first_ref, last_ref, xs_ref, gm_ref, wg_ref, wu_ref, wd_ref,
                 ys_ref, hb_scr, acc_scr, *, tme):
    k = pl.program_id(0)
    first = first_ref[k] == 1
    last = last_ref[k] == 1
    nonempty = hi_ref[k] > lo_ref[k]

    @pl.when(first)
    def _():
        hb_scr[...] = _rms(_tiles_to_rows(xs_ref, tme), gm_ref[...]).astype(BF16)

    @pl.when(first & jnp.logical_not(nonempty))
    def _():
        acc_scr[...] = jnp.zeros_like(acc_scr)

    @pl.when(nonempty)
    def _():
        hb = hb_scr[...]
        act = jax.nn.silu(_bdot(hb, wg_ref[0])) * _bdot(hb, wu_ref[0])
        row = lax.broadcasted_iota(jnp.int32, (tme, 1), 0)
        y = jnp.where((row >= lo_ref[k]) & (row < hi_ref[k]), _bdot(act.astype(BF16), wd_ref[0]), 0.0)

        @pl.when(first & last)
        def _():
            _rows_to_tiles(ys_ref, y, tme)

        @pl.when(first & jnp.logical_not(last))
        def _():
            acc_scr[...] = y

        @pl.when(jnp.logical_not(first))
        def _():
            acc_scr[...] += y

    @pl.when(last & jnp.logical_not(first & nonempty))
    def _():
        _rows_to_tiles(ys_ref, acc_scr[...], tme)


def _experts(work, xs, g_moe, wg, wu, wd, tme):
    n_work = work[0].shape[0]
    body = functools.partial(_expert_body, tme=tme)
    tile_blk = pl.BlockSpec((tme * TILE_ROWS, LANES), lambda k, tile, *_: (tile[k], 0))

    def wspec(shape):
        return pl.BlockSpec(shape, lambda k, tile, exp, *_: (exp[k], 0, 0))

    grid_spec = pltpu.PrefetchScalarGridSpec(
        num_scalar_prefetch=len(work),
        grid=(n_work,),
        in_specs=[tile_blk, pl.BlockSpec((1, D_MODEL), lambda k, *_: (0, 0)),
                  wspec((1, D_MODEL, D_FF_EXPERT)), wspec((1, D_MODEL, D_FF_EXPERT)),
                  wspec((1, D_FF_EXPERT, D_MODEL))],
        out_specs=tile_blk,
        scratch_shapes=[pltpu.VMEM((tme, D_MODEL), BF16), pltpu.VMEM((tme, D_MODEL), F32)],
    )
    return pl.pallas_call(
        body,
        grid_spec=grid_spec,
        out_shape=jax.ShapeDtypeStruct(xs.shape, F32),
        compiler_params=_params("arbitrary"),
        name="experts",
    )(*work, xs, g_moe, wg, wu, wd)


def _combine_body(slot_ref, slot_next_ref, x2t_ref, info_ref, ys_hbm, gf_ref, y_ref, b0, b1, sem, *, tm, nstep):
    i = pl.program_id(0)
    cur = i % 2

    def gather(slots, buf):
        def issue(r, c):
            _tile_copy(ys_hbm, slots[2 * r], b0.at[buf], r, sem.at[buf]).start(priority=0)
            _tile_copy(ys_hbm, slots[2 * r + 1], b1.at[buf], r, sem.at[buf]).start(priority=1)
            return c

        lax.fori_loop(0, tm, issue, 0, unroll=8)

    @pl.when(i == 0)
    def _():
        gather(slot_ref, 0)

    @pl.when(i + 1 < nstep)
    def _():
        gather(slot_next_ref, 1 - cur)

    def drain(r, c):
        _tile_copy(ys_hbm, 0, b0.at[cur], r, sem.at[cur]).wait()
        _tile_copy(ys_hbm, 0, b1.at[cur], r, sem.at[cur]).wait()
        return c

    lax.fori_loop(0, tm, drain, 0, unroll=8)
    info = info_ref[...]
    moe = info[:, 2:3] * _tiles_to_rows(b0.at[cur], tm) + info[:, 3:4] * _tiles_to_rows(b1.at[cur], tm)
    y_ref[...] = _rms(_tiles_to_rows(x2t_ref, tm) + moe, gf_ref[...])


def _combine(slot, x2t, info, ys, g_final, tm=256):
    t = info.shape[0]
    nstep = t // tm
    body = functools.partial(_combine_body, tm=tm, nstep=nstep)
    buf = pltpu.VMEM((2, tm * TILE_ROWS, LANES), F32)
    return pl.pallas_call(
        body,
        grid=(nstep,),
        in_specs=[pl.BlockSpec((2 * tm,), lambda i: (i,), memory_space=pltpu.SMEM),
                  pl.BlockSpec((2 * tm,), lambda i: (jnp.minimum(i + 1, nstep - 1),), memory_space=pltpu.SMEM),
                  pl.BlockSpec((tm * TILE_ROWS, LANES), lambda i: (i, 0)),
                  pl.BlockSpec((tm, ROUTER_LANES), lambda i: (i, 0)),
                  pl.BlockSpec(memory_space=pl.ANY),
                  pl.BlockSpec((1, D_MODEL), lambda i: (0, 0))],
        out_specs=pl.BlockSpec((tm, D_MODEL), lambda i: (i, 0)),
        out_shape=jax.ShapeDtypeStruct((t, D_MODEL), F32),
        scratch_shapes=[buf, buf, pltpu.SemaphoreType.DMA((2,))],
        compiler_params=_params("arbitrary"),
        name="combine",
    )(slot, slot, x2t, info, ys, g_final)


def _moe(x2t, w, tme):
    info, cnt = _route(x2t, w["g_moe"], w["wr_hi"], w["wr_lo"])
    slot, work = _route_meta(info, cnt, tme)
    xs = _scatter(slot, x2t)
    ys = _experts(work, xs, w["g_moe"], w["moe_wg"], w["moe_wu"], w["moe_wd"], tme)
    return _combine(slot, x2t, info, ys, w["g_final"])


def _layer(x3d, k4, v4, s5_h0, lru_h0, lru_conv, w, chunk, nb, tq, tme):
    bsz, seq, _ = x3d.shape
    x2d = x3d.reshape(bsz * seq, D_MODEL)
    u, xl, gl = _mixin(x2d, w["g_mix"], w["w_in"])

    nstate = S5_GROUPS * S5_STATE
    if s5_h0 is None:
        zero = jnp.zeros((bsz, 1, nstate), F32)
        ys, s5_re, s5_im = _s5(u, w["s5"][chunk], zero, zero, chunk, seq // chunk, True)
    else:
        ys, s5_re, s5_im = _s5(u, w["s5"][chunk], s5_h0[0].reshape(bsz, nstate), s5_h0[1].reshape(bsz, nstate),
                               chunk, bsz, False)
    s5_re = s5_re.reshape(bsz, S5_GROUPS, S5_STATE)
    s5_im = s5_im.reshape(bsz, S5_GROUPS, S5_STATE)

    lru_w = (w["conv_w"], w["conv_b"], w["lru_wg"], w["lru_bg"], w["lru_lam"])
    if lru_h0 is None:
        yl, hl = _lru_seq(xl, gl, *lru_w, bsz, seq)
        lru_h = hl.reshape(bsz, SUBLANES, D_LRU)[:, 0]
    else:
        hist = jnp.pad(lru_conv, ((0, 0), (SUBLANES - (CONV_W - 1), 0), (0, 0))).reshape(bsz * seq, D_LRU)
        h0rep = jnp.repeat(lru_h0, seq, axis=0)
        yl, hall = _lru_dec(xl, gl, hist, h0rep, *lru_w)
        lru_h = hall.reshape(bsz, seq, D_LRU)[:, seq - 1]
    conv_new = xl.reshape(bsz, seq, D_LRU)[:, seq - (CONV_W - 1):]

    x1, q = _mixout(x2d, ys, yl, w["w_glu"], w["b_glu"], w["w_out"], w["g_xa"], w["w_q"])
    x2t = _attn(q.reshape(bsz, seq, D_MODEL), x1.reshape(bsz, seq, D_MODEL), k4, v4, w["w_o"], nb, tq)
    y = _moe(x2t, w, tme)
    return y.reshape(bsz, seq, D_MODEL), s5_re, s5_im, lru_h, conv_new


def kernel(x_prompt, x_sample, mem_prompt, cache_mem_k, cache_mem_v, state_s5_re, state_s5_im, state_lru_h, state_lru_conv, g_mix, w_in, s5_lam_re, s5_lam_im, s5_log_dt, s5_b_re, s5_b_im, s5_c_re, s5_c_im, s5_d, s5_w_glu, s5_b_glu, lru_conv_w, lru_conv_b, lru_w_a, lru_b_a, lru_w_x, lru_b_x, lru_lam, w_out, g_xa, g_mem, xa_w_q, xa_w_k, xa_w_v, xa_w_o, g_moe, moe_w_group, moe_w_expert, moe_w_gate, moe_w_up, moe_w_down, g_final):
    depth = g_mix.shape[0]
    assert depth == 1, "single-layer step"
    l = 0
    bsz, seq, _ = x_prompt.shape
    dbsz, dseq, _ = x_sample.shape
    chunk = SUBLANES
    chunk_p = chunk_s = chunk
    assert seq % chunk == 0 and dseq == chunk

    s5_args = (s5_lam_re[l], s5_lam_im[l], s5_log_dt[l], s5_b_re[l], s5_b_im[l], s5_c_re[l], s5_c_im[l], s5_d[l])
    wg, bg = _lru_weights(lru_w_a[l], lru_w_x[l], lru_b_a[l], lru_b_x[l])
    wr = jnp.concatenate([moe_w_group[l], moe_w_expert[l],
                          jnp.zeros((D_MODEL, ROUTER_LANES - N_EXPERT_GROUPS - N_EXPERTS), F32)], axis=1)
    wr_hi = wr.astype(BF16)
    wr_lo = (wr - wr_hi.astype(F32)).astype(BF16)
    w = {
        "g_mix": g_mix[l][None], "w_in": w_in[l].astype(BF16),
        "s5": {c: _s5_weights(c, *s5_args) for c in {chunk_p, chunk_s}},
        "conv_w": lru_conv_w[l], "conv_b": lru_conv_b[l][None], "lru_wg": wg, "lru_bg": bg,
        "lru_lam": lru_lam[l][None],
        "w_glu": s5_w_glu[l].astype(BF16), "b_glu": s5_b_glu[l][None], "w_out": w_out[l].astype(BF16),
        "g_xa": g_xa[l][None], "w_q": xa_w_q[l].astype(BF16), "w_o": xa_w_o[l].astype(BF16),
        "g_moe": g_moe[l][None], "wr_hi": wr_hi, "wr_lo": wr_lo,
        "moe_wg": moe_w_gate[l].astype(BF16), "moe_wu": moe_w_up[l].astype(BF16),
        "moe_wd": moe_w_down[l].astype(BF16), "g_final": g_final[None],
    }

    mk, mv = _memkv(mem_prompt.reshape(bsz * N_MEM, D_MODEL), g_mem[l][None],
                    xa_w_k[l].astype(BF16), xa_w_v[l].astype(BF16))
    yp, p_re, p_im, p_h, p_conv = _layer(x_prompt, mk.reshape(bsz, N_MEM, D_MODEL), mv.reshape(bsz, N_MEM, D_MODEL),
                                         None, None, None, w, chunk_p, nb=1, tq=512, tme=512)
    ysmp, s_re, s_im, s_h, s_conv = _layer(x_sample, cache_mem_k[l], cache_mem_v[l],
                                           (state_s5_re[l], state_s5_im[l]), state_lru_h[l],
                                           state_lru_conv[l], w, chunk_s, nb=4, tq=dseq, tme=256)

    return (yp, ysmp,
            mk.reshape(1, bsz, N_MEM, XA_HEADS, XA_HEAD_DIM), mv.reshape(1, bsz, N_MEM, XA_HEADS, XA_HEAD_DIM),
            p_re[None], p_im[None], p_h[None], p_conv[None],
            s_re[None], s_im[None], s_h[None], s_conv[None])
```

```python
import functools

import jax
import jax.numpy as jnp
import numpy as np
from jax import lax
from jax.experimental import pallas as pl
from jax.experimental.pallas import tpu as pltpu

F32 = jnp.float32
BF16 = jnp.bfloat16

D_MODEL = 1024
D_S5 = 512
D_LRU = 512
S5_GROUP = 16
S5_GROUPS = 32
S5_STATE = 64
LRU_HEADS = 8
LRU_HEAD_DIM = 64
CONV_W = 4
LRU_C = 8.0
N_MEM = 256
XA_HEADS = 4
XA_HEAD_DIM = 256
N_EXPERT_GROUPS = 4
EXPERTS_PER_GROUP = 8
N_EXPERTS = 32
D_FF_EXPERT = 256
EPS = 1e-6

SUBLANES = 8
LANES = 128
ROUTER_LANES = 128
VMEM_LIMIT = 56 * 1024 * 1024


def _params(*sem):
    return pltpu.CompilerParams(dimension_semantics=sem, vmem_limit_bytes=VMEM_LIMIT)


def _rms(x, g):
    ms = jnp.mean(x * x, axis=-1, keepdims=True)
    return x * lax.rsqrt(ms + EPS) * g


def _bdot(a, b):
    return jnp.dot(a, b, preferred_element_type=F32)


def _memkv_body(m_ref, g_ref, wk_ref, wv_ref, k_ref, v_ref):
    hb = _rms(m_ref[...], g_ref[...]).astype(BF16)
    k_ref[...] = _bdot(hb, wk_ref[...])
    v_ref[...] = _bdot(hb, wv_ref[...])


def _memkv(mem2d, g_mem, wk, wv, tm=512):
    t = mem2d.shape[0]
    row = pl.BlockSpec((tm, D_MODEL), lambda i: (i, 0))
    full = pl.BlockSpec((D_MODEL, D_MODEL), lambda i: (0, 0))
    return pl.pallas_call(
        _memkv_body,
        grid=(t // tm,),
        in_specs=[row, pl.BlockSpec((1, D_MODEL), lambda i: (0, 0)), full, full],
        out_specs=[row, row],
        out_shape=[jax.ShapeDtypeStruct((t, D_MODEL), F32)] * 2,
        compiler_params=_params("parallel"),
        name="memkv",
    )(mem2d, g_mem, wk, wv)


def _mixin_body(x_ref, g_ref, w_ref, u_ref, xl_ref, gl_ref):
    hb = _rms(x_ref[...], g_ref[...]).astype(BF16)
    proj = _bdot(hb, w_ref[...])
    u_ref[...] = proj[:, :D_S5]
    xl_ref[...] = proj[:, D_S5:D_S5 + D_LRU]
    gl_ref[...] = proj[:, D_S5 + D_LRU:]


def _mixin(x2d, g_mix, w_in, tm=1024):
    t = x2d.shape[0]
    half = pl.BlockSpec((tm, D_S5), lambda i: (i, 0))
    return pl.pallas_call(
        _mixin_body,
        grid=(t // tm,),
        in_specs=[pl.BlockSpec((tm, D_MODEL), lambda i: (i, 0)),
                  pl.BlockSpec((1, D_MODEL), lambda i: (0, 0)),
                  pl.BlockSpec((D_MODEL, D_S5 + 2 * D_LRU), lambda i: (0, 0))],
        out_specs=[half, half, half],
        out_shape=[jax.ShapeDtypeStruct((t, D_S5), F32)] * 3,
        compiler_params=_params("parallel"),
        name="mixin",
    )(x2d, g_mix, w_in)


GROUPS_PER_BLOCK = LANES // S5_GROUP
PAIRS_PER_BLOCK = GROUPS_PER_BLOCK // 2
STATE_BLOCK = GROUPS_PER_BLOCK * S5_STATE
PW_ROWS = 24


def _s5_body(u_ref, wa_ref, wh_ref, pw_ref, d_ref, h0r_ref, h0i_ref, y_ref, hfr_ref, hfi_ref, hpr_scr, hpi_scr,
             *, chunk, nrow, scan):
    nh = chunk // SUBLANES
    lc = chunk * S5_GROUP
    slot = lax.broadcasted_iota(jnp.int32, (nrow, LANES), 1) // S5_GROUP
    in_slot = [slot == s for s in range(GROUPS_PER_BLOCK)]

    def pick(src, sel):
        out = src[sel(0)]
        for s in range(1, GROUPS_PER_BLOCK):
            out = jnp.where(in_slot[s], src[sel(s)], out)
        return out

    nat, skew = [], []
    for t in range(chunk):
        a = u_ref[pl.ds(t, nrow, stride=chunk), :]
        nat.append(a)
        s = t % GROUPS_PER_BLOCK
        skew.append(pltpu.roll(a, s * S5_GROUP, axis=1) if s else a)
    def lane_roll(x, slots):
        slots %= GROUPS_PER_BLOCK
        return pltpu.roll(x, slots * S5_GROUP, axis=1) if slots else x

    z = [[lane_roll(pick(skew, lambda s, hh=hh, m=m: SUBLANES * hh + (s - m) % GROUPS_PER_BLOCK), -m)
          for hh in range(nh)] for m in range(GROUPS_PER_BLOCK)]

    ys, er, ei = [], [], []
    for p in range(PAIRS_PER_BLOCK):
        lhs = jnp.concatenate(z[2 * p] + z[2 * p + 1], axis=1).astype(BF16)
        full = _bdot(lhs, wa_ref[p])
        ys.append(full[:, :2 * lc])
        er.append(full[:, 2 * lc:2 * lc + LANES])
        ei.append(full[:, 2 * lc + LANES:])
    er = jnp.concatenate(er, axis=1)
    ei = jnp.concatenate(ei, axis=1)

    if scan:
        ntile = nrow // SUBLANES
        xr = er.reshape(ntile, SUBLANES, STATE_BLOCK)
        xi = ei.reshape(ntile, SUBLANES, STATE_BLOCK)
        row = lax.broadcasted_iota(jnp.int32, xr.shape, 1)
        for k, d in enumerate((1, 2, 4)):
            pr, pi = pw_ref[0, k:k + 1, :], pw_ref[0, 4 + k:5 + k, :]
            sr, si = pltpu.roll(xr, d, axis=1), pltpu.roll(xi, d, axis=1)
            m = row >= d
            xr, xi = jnp.where(m, xr + pr * sr - pi * si, xr), jnp.where(m, xi + pr * si + pi * sr, xi)
        sxr = jnp.where(row >= 1, pltpu.roll(xr, 1, axis=1), 0.0)
        sxi = jnp.where(row >= 1, pltpu.roll(xi, 1, axis=1), 0.0)
        p8r, p8i = pw_ref[0, 3:4, :], pw_ref[0, 7:8, :]
        qr, qi = pw_ref[0, 8:16, :], pw_ref[0, 16:24, :]
        hr, hi = h0r_ref[0], h0i_ref[0]
        for t in range(ntile):
            rows = slice(SUBLANES * t, SUBLANES * (t + 1))
            hpr_scr[rows, :] = sxr[t] + qr * hr - qi * hi
            hpi_scr[rows, :] = sxi[t] + qr * hi + qi * hr
            hr, hi = (xr[t, SUBLANES - 1:] + p8r * hr - p8i * hi, xi[t, SUBLANES - 1:] + p8r * hi + p8i * hr)
        hfr_ref[0] = hr
        hfi_ref[0] = hi
        hpr, hpi = hpr_scr[...], hpi_scr[...]
    else:
        hpr, hpi = h0r_ref[...], h0i_ref[...]
        p1r, p1i = pw_ref[0, 0:1, :], pw_ref[0, 4:5, :]
        hfr_ref[...] = er + p1r * hpr - p1i * hpi
        hfi_ref[...] = ei + p1r * hpi + p1i * hpr

    yg = []
    for p in range(PAIRS_PER_BLOCK):
        lanes = slice(LANES * p, LANES * (p + 1))
        hp = jnp.concatenate([hpr[:, lanes], hpi[:, lanes]], axis=1).astype(BF16)
        out = ys[p] + _bdot(hp, wh_ref[p])
        for half in range(2):
            q = 2 * p + half
            yg.append([lane_roll(out[:, half * lc + hh * LANES:half * lc + (hh + 1) * LANES], q)
                       for hh in range(nh)])
    d = d_ref[...]
    for t in range(chunk):
        hh, s = divmod(t, GROUPS_PER_BLOCK)
        c = lane_roll(pick([g[hh] for g in yg], lambda sl, s=s: (sl - s) % GROUPS_PER_BLOCK), -s)
        y_ref[pl.ds(t, nrow, stride=chunk), :] = c + d * nat[t]


def _s5(u2d, weights, h0r, h0i, chunk, nrow, scan):
    wa, wh, pw, dvec = weights
    t = u2d.shape[0]
    lc = chunk * S5_GROUP
    rows = nrow * chunk
    nblk = D_S5 // LANES
    body = functools.partial(_s5_body, chunk=chunk, nrow=nrow, scan=scan)
    ublk = pl.BlockSpec((rows, LANES), lambda i, j: (i, j))
    if scan:
        hblk = pl.BlockSpec((1, 1, STATE_BLOCK), lambda i, j: (i, 0, j))
        hshape = jax.ShapeDtypeStruct((t // rows, 1, S5_GROUPS * S5_STATE), F32)
    else:
        hblk = pl.BlockSpec((nrow, STATE_BLOCK), lambda i, j: (i, j))
        hshape = jax.ShapeDtypeStruct((t // chunk, S5_GROUPS * S5_STATE), F32)
    return pl.pallas_call(
        body,
        grid=(t // rows, nblk),
        in_specs=[ublk,
                  pl.BlockSpec((PAIRS_PER_BLOCK, 2 * lc, 2 * lc + 2 * LANES), lambda i, j: (j, 0, 0)),
                  pl.BlockSpec((PAIRS_PER_BLOCK, 2 * LANES, 2 * lc), lambda i, j: (j, 0, 0)),
                  pl.BlockSpec((1, PW_ROWS, STATE_BLOCK), lambda i, j: (j, 0, 0)),
                  pl.BlockSpec((1, LANES), lambda i, j: (0, j)),
                  hblk, hblk],
        out_specs=[ublk, hblk, hblk],
        out_shape=[jax.ShapeDtypeStruct((t, D_S5), F32), hshape, hshape],
        scratch_shapes=[pltpu.VMEM((nrow, STATE_BLOCK), F32), pltpu.VMEM((nrow, STATE_BLOCK), F32)],
        compiler_params=_params("parallel", "parallel"),
        name="s5",
    )(u2d, wa, wh, pw, dvec, h0r, h0i)


def _s5_weights(chunk, lam_re, lam_im, log_dt, b_re, b_im, c_re, c_im, d_skip):
    hi = lax.Precision.HIGHEST
    dt = jnp.exp(log_dt)[:, None]
    mag = jnp.exp(lam_re * dt)
    ab_re = mag * jnp.cos(lam_im * dt)
    ab_im = mag * jnp.sin(lam_im * dt)
    den = lam_re * lam_re + lam_im * lam_im
    nr = ab_re - 1.0
    f_re = (nr * lam_re + ab_im * lam_im) / den
    f_im = (ab_im * lam_re - nr * lam_im) / den
    bb_re = f_re[..., None] * b_re - f_im[..., None] * b_im
    bb_im = f_re[..., None] * b_im + f_im[..., None] * b_re
    def powers(ks):
        k = jnp.asarray(np.asarray(ks, np.float32))[:, None, None]
        m = jnp.exp(k * (lam_re * dt))
        return m * jnp.cos(k * (lam_im * dt)), m * jnp.sin(k * (lam_im * dt))

    pw_re, pw_im = powers(range(chunk + 1))
    pb_re = pw_re[:chunk, ..., None] * bb_re - pw_im[:chunk, ..., None] * bb_im
    pb_im = pw_re[:chunk, ..., None] * bb_im + pw_im[:chunk, ..., None] * bb_re
    kk = (jnp.einsum("gon,kgni->kgoi", c_re, pb_re, precision=hi)
          - jnp.einsum("gon,kgni->kgoi", c_im, pb_im, precision=hi))
    npair = S5_GROUPS // 2
    lc = chunk * S5_GROUP
    row0 = kk.transpose(1, 3, 0, 2).reshape(S5_GROUPS, S5_GROUP, lc)
    row0 = jnp.pad(row0, ((0, 0), (0, 0), (lc, 0)))
    conv = jnp.stack([row0[:, :, lc - S5_GROUP * ti:2 * lc - S5_GROUP * ti] for ti in range(chunk)], axis=1)
    conv = conv.reshape(npair, 2, lc, lc)
    ends = [x[::-1].transpose(1, 0, 3, 2).reshape(npair, 2, lc, S5_STATE) for x in (pb_re, pb_im)]
    zc = jnp.zeros((npair, lc, lc), F32)
    zs = jnp.zeros((npair, lc, S5_STATE), F32)
    wa = jnp.concatenate([
        jnp.concatenate([conv[:, 0], zc, ends[0][:, 0], zs, ends[1][:, 0], zs], axis=-1),
        jnp.concatenate([zc, conv[:, 1], zs, ends[0][:, 1], zs, ends[1][:, 1]], axis=-1)], axis=1)
    gr = c_re[None] * pw_re[1:, :, None, :] - c_im[None] * pw_im[1:, :, None, :]
    gi = c_re[None] * pw_im[1:, :, None, :] + c_im[None] * pw_re[1:, :, None, :]
    zr = jnp.zeros((npair, S5_STATE, lc), F32)
    blocks = []
    for x in (gr, -gi):
        hy = x.transpose(1, 3, 0, 2).reshape(npair, 2, S5_STATE, lc)
        blocks += [jnp.concatenate([hy[:, 0], zr], axis=-1), jnp.concatenate([zr, hy[:, 1]], axis=-1)]
    wh = jnp.concatenate(blocks, axis=1)
    tab = [1, 2, 4, 8] + list(range(SUBLANES))
    qr, qi = powers([chunk * k for k in tab])
    nblk = D_S5 // LANES
    pw = jnp.concatenate([qr[:4], qi[:4], qr[4:], qi[4:]]).reshape(PW_ROWS, nblk, STATE_BLOCK).transpose(1, 0, 2)
    return wa.astype(BF16), wh.astype(BF16), pw, d_skip.reshape(1, D_S5)


def _lru_ab(xc, wg_ref, bg_ref, lam_ref):
    xb = xc.astype(BF16)
    half = D_LRU // 2
    g0 = _bdot(xb[:, :half], wg_ref[0]) + bg_ref[0]
    g1 = _bdot(xb[:, half:], wg_ref[1]) + bg_ref[1]
    r = jax.nn.sigmoid(jnp.concatenate([g0[:, :half], g1[:, :half]], axis=1))
    ig = jax.nn.sigmoid(jnp.concatenate([g0[:, half:], g1[:, half:]], axis=1))
    zl = -lam_ref[...]
    softplus = jnp.maximum(zl, 0.0) + jnp.log1p(jnp.exp(-jnp.abs(zl)))
    log_a = -LRU_C * r * softplus
    a = jnp.exp(log_a)
    b = jnp.sqrt(-jnp.tanh(log_a) * (a * a + 1.0)) * (ig * xc)
    return a, b


def _tile_scan(a, b):
    shape = a.shape
    a = a.reshape(shape[0] // SUBLANES, SUBLANES, shape[1])
    b = b.reshape(a.shape)
    row = lax.broadcasted_iota(jnp.int32, a.shape, 1)
    for d in (1, 2, 4):
        a_prev = pltpu.roll(a, d, axis=1)
        b_prev = pltpu.roll(b, d, axis=1)
        m = row >= d
        b = jnp.where(m, b + a * b_prev, b)
        a = jnp.where(m, a * a_prev, a)
    return a.reshape(shape), b.reshape(shape)


def _lru_seq_body(xl_ref, gl_ref, cw_ref, cb_ref, wg_ref, bg_ref, lam_ref, y_ref, hl_ref,
                  xp_scr, a_scr, b_scr, h_scr, hc_scr, *, ts):
    ti = pl.program_id(1)

    @pl.when(ti == 0)
    def _():
        xp_scr[0:SUBLANES, :] = jnp.zeros((SUBLANES, D_LRU), F32)
        hc_scr[...] = jnp.zeros((SUBLANES, D_LRU), F32)

    xl = xl_ref[...]
    xp_scr[SUBLANES:SUBLANES + ts, :] = xl
    xc = cb_ref[...] + xl * cw_ref[CONV_W - 1:CONV_W, :]
    for j in range(1, CONV_W):
        xc = xc + xp_scr[SUBLANES - j:SUBLANES - j + ts, :] * cw_ref[CONV_W - 1 - j:CONV_W - j, :]
    xp_scr[0:SUBLANES, :] = xl[ts - SUBLANES:, :]
    a, b = _lru_ab(xc, wg_ref, bg_ref, lam_ref)
    a, b = _tile_scan(a, b)
    a_scr[...] = a
    b_scr[...] = b

    def step(i, hin):
        rows = pl.ds(pl.multiple_of(i * SUBLANES, SUBLANES), SUBLANES)
        h = b_scr[rows, :] + a_scr[rows, :] * hin
        h_scr[rows, :] = h
        return h[SUBLANES - 1:SUBLANES, :]

    hlast = lax.fori_loop(0, ts // SUBLANES, step, hc_scr[0:1, :], unroll=4)
    hc_scr[...] = jnp.broadcast_to(hlast, (SUBLANES, D_LRU))
    hl_ref[...] = hc_scr[...]
    y_ref[...] = h_scr[...] * jax.nn.gelu(gl_ref[...])


def _lru_dec_body(xl_ref, gl_ref, hist_ref, h0_ref, cw_ref, cb_ref, wg_ref, bg_ref, lam_ref, y_ref, hl_ref, *, tm):
    xl = xl_ref[...]
    hist = hist_ref[...]
    row = lax.broadcasted_iota(jnp.int32, xl.shape, 0) % SUBLANES
    xc = cb_ref[...] + xl * cw_ref[CONV_W - 1:CONV_W, :]
    for j in range(1, CONV_W):
        prev = jnp.where(row >= j, pltpu.roll(xl, j, axis=0), pltpu.roll(hist, tm - SUBLANES + j, axis=0))
        xc = xc + prev * cw_ref[CONV_W - 1 - j:CONV_W - j, :]
    a, b = _lru_ab(xc, wg_ref, bg_ref, lam_ref)
    a, b = _tile_scan(a, b)
    h = b + a * h0_ref[...]
    hl_ref[...] = h
    y_ref[...] = h * jax.nn.gelu(gl_ref[...])


def _lru_weights(w_a, w_x, b_a, b_x):
    eye = jnp.eye(LRU_HEADS, dtype=F32)
    bd_a = jnp.einsum("hij,hg->higj", w_a, eye).reshape(D_LRU, D_LRU)
    bd_x = jnp.einsum("hij,hg->higj", w_x, eye).reshape(D_LRU, D_LRU)
    half = D_LRU // 2
    wg = jnp.stack([jnp.concatenate([bd_a[:half, :half], bd_x[:half, :half]], axis=1),
                    jnp.concatenate([bd_a[half:, half:], bd_x[half:, half:]], axis=1)]).astype(BF16)
    ba = b_a.reshape(1, D_LRU)
    bx = b_x.reshape(1, D_LRU)
    bg = jnp.stack([jnp.concatenate([ba[:, :half], bx[:, :half]], axis=1),
                    jnp.concatenate([ba[:, half:], bx[:, half:]], axis=1)])
    return wg, bg


def _lru_common_specs(const):
    return [const((CONV_W, D_LRU)), const((1, D_LRU)), const((2, D_LRU // 2, D_LRU)),
            const((2, 1, D_LRU)), const((1, D_LRU))]


def _lru_seq(xl, gl, cw, cb, wg, bg, lam, bsz, seq, ts=512):
    nt = seq // ts
    blk = pl.BlockSpec((ts, D_LRU), lambda b, t: (b * nt + t, 0))

    def const(shape):
        return pl.BlockSpec(shape, lambda b, t: (0,) * len(shape))

    body = functools.partial(_lru_seq_body, ts=ts)
    return pl.pallas_call(
        body,
        grid=(bsz, nt),
        in_specs=[blk, blk] + _lru_common_specs(const),
        out_specs=[blk, pl.BlockSpec((SUBLANES, D_LRU), lambda b, t: (b, 0))],
        out_shape=[jax.ShapeDtypeStruct((bsz * seq, D_LRU), F32),
                   jax.ShapeDtypeStruct((bsz * SUBLANES, D_LRU), F32)],
        scratch_shapes=[pltpu.VMEM((ts + SUBLANES, D_LRU), F32), pltpu.VMEM((ts, D_LRU), F32),
                        pltpu.VMEM((ts, D_LRU), F32), pltpu.VMEM((ts, D_LRU), F32),
                        pltpu.VMEM((SUBLANES, D_LRU), F32)],
        compiler_params=_params("arbitrary", "arbitrary"),
        name="lru_seq",
    )(xl, gl, cw, cb, wg, bg, lam)


def _lru_dec(xl, gl, hist, h0rep, cw, cb, wg, bg, lam, tm=256):
    t = xl.shape[0]
    blk = pl.BlockSpec((tm, D_LRU), lambda i: (i, 0))

    def const(shape):
        return pl.BlockSpec(shape, lambda i: (0,) * len(shape))

    body = functools.partial(_lru_dec_body, tm=tm)
    return pl.pallas_call(
        body,
        grid=(t // tm,),
        in_specs=[blk, blk, blk, blk] + _lru_common_specs(const),
        out_specs=[blk, blk],
        out_shape=[jax.ShapeDtypeStruct((t, D_LRU), F32)] * 2,
        compiler_params=_params("parallel"),
        name="lru_dec",
    )(xl, gl, hist, h0rep, cw, cb, wg, bg, lam)


def _mixout_body(x_ref, ys_ref, yl_ref, wglu_ref, bglu_ref, wo_ref, gxa_ref, wq_ref, x1_ref, q_ref):
    ys = jax.nn.gelu(ys_ref[...])
    gate = jax.nn.sigmoid(_bdot(ys.astype(BF16), wglu_ref[...]) + bglu_ref[...])
    s5 = (ys * gate).astype(BF16)
    x1 = (x_ref[...] + _bdot(s5, wo_ref[0:D_S5, :]) + _bdot(yl_ref[...].astype(BF16), wo_ref[D_S5:, :]))
    x1_ref[...] = x1
    q_ref[...] = _bdot(_rms(x1, gxa_ref[...]).astype(BF16), wq_ref[...]).astype(BF16)


def _mixout(x2d, ys, yl, w_glu, b_glu, w_out, g_xa, w_q, tm=1024):
    t = x2d.shape[0]
    row = pl.BlockSpec((tm, D_MODEL), lambda i: (i, 0))
    half = pl.BlockSpec((tm, D_S5), lambda i: (i, 0))

    def const(shape):
        return pl.BlockSpec(shape, lambda i: (0,) * len(shape))

    return pl.pallas_call(
        _mixout_body,
        grid=(t // tm,),
        in_specs=[row, half, half, const((D_S5, D_S5)), const((1, D_S5)), const((D_MODEL, D_MODEL)),
                  const((1, D_MODEL)), const((D_MODEL, D_MODEL))],
        out_specs=[row, row],
        out_shape=[jax.ShapeDtypeStruct((t, D_MODEL), F32), jax.ShapeDtypeStruct((t, D_MODEL), BF16)],
        compiler_params=_params("parallel"),
        name="mixout",
    )(x2d, ys, yl, w_glu, b_glu, w_out, g_xa, w_q)


def _softmax(sc):
    p = jnp.exp(sc - jnp.max(sc, axis=-1, keepdims=True))
    return p / jnp.sum(p, axis=-1, keepdims=True)


def _attn_body(q_ref, x1_ref, k_ref, v_ref, wo_ref, x2t_ref, *, nb, tq):
    scale = XA_HEAD_DIM ** -0.5
    for i in range(nb):
        q = q_ref[i]
        heads = [q[:, h * XA_HEAD_DIM:(h + 1) * XA_HEAD_DIM] for h in range(XA_HEADS)]
        if len(k_ref.shape) == 4:
            kf = k_ref[i].reshape(N_MEM * XA_HEADS, XA_HEAD_DIM).astype(BF16)
            vf = v_ref[i].reshape(N_MEM * XA_HEADS, XA_HEAD_DIM).astype(BF16)
            sc = lax.dot_general(jnp.concatenate(heads, axis=0), kf, (((1,), (1,)), ((), ())),
                                 preferred_element_type=F32) * scale
            own = (lax.broadcasted_iota(jnp.int32, sc.shape, 1) % XA_HEADS
                   == lax.broadcasted_iota(jnp.int32, sc.shape, 0) // tq)
            oh = _bdot(_softmax(jnp.where(own, sc, -jnp.inf)).astype(BF16), vf)
            outs = [oh[h * tq:(h + 1) * tq] for h in range(XA_HEADS)]
        else:
            kb = k_ref[i].astype(BF16)
            vb = v_ref[i].astype(BF16)
            outs = []
            for h in range(XA_HEADS):
                cols = slice(h * XA_HEAD_DIM, (h + 1) * XA_HEAD_DIM)
                sc = lax.dot_general(heads[h], kb[:, cols], (((1,), (1,)), ((), ())),
                                     preferred_element_type=F32) * scale
                outs.append(_bdot(_softmax(sc).astype(BF16), vb[:, cols]))
        o = jnp.concatenate(outs, axis=1).astype(BF16)
        _rows_to_tiles(x2t_ref, x1_ref[i] + _bdot(o, wo_ref[...]), tq, base=i * tq * TILE_ROWS)


def _attn(q3, x13, k4, v4, w_o, nb, tq):
    bsz, seq, _ = q3.shape
    nt = seq // tq
    blk = pl.BlockSpec((nb, tq, D_MODEL), lambda b, t: (b, t, 0))
    kv = pl.BlockSpec((nb,) + k4.shape[1:], lambda b, t: (b,) + (0,) * (k4.ndim - 1))
    body = functools.partial(_attn_body, nb=nb, tq=tq)
    return pl.pallas_call(
        body,
        grid=(bsz // nb, nt),
        in_specs=[blk, blk, kv, kv, pl.BlockSpec((D_MODEL, D_MODEL), lambda b, t: (0, 0))],
        out_specs=pl.BlockSpec((nb * tq * TILE_ROWS, LANES), lambda b, t: (b * nt + t, 0)),
        out_shape=jax.ShapeDtypeStruct((bsz * seq * TILE_ROWS, LANES), F32),
        compiler_params=_params("parallel", "parallel"),
        name="attn",
    )(q3, x13, k4, v4, w_o)


def _router(hm, wr_hi_ref, wr_lo_ref):
    a_hi = hm.astype(BF16)
    a_lo = (hm - a_hi.astype(F32)).astype(BF16)
    logits = _bdot(a_hi, wr_hi_ref[...]) + (_bdot(a_hi, wr_lo_ref[...]) + _bdot(a_lo, wr_hi_ref[...]))
    lane_i = lax.broadcasted_iota(jnp.int32, logits.shape, 1)
    lane = lane_i.astype(F32)
    neg = -jnp.inf
    big = float(ROUTER_LANES)
    is_g = lane_i < N_EXPERT_GROUPS
    glog = jnp.where(is_g, logits, neg)
    gmax = jnp.max(glog, axis=-1, keepdims=True)
    gsel = jnp.min(jnp.where(glog == gmax, lane, big), axis=-1, keepdims=True)
    pg_sel = 1.0 / jnp.sum(jnp.where(is_g, jnp.exp(logits - gmax), 0.0), axis=-1, keepdims=True)
    eidx = lane_i - N_EXPERT_GROUPS
    in_group = (eidx >= 0) & (eidx < N_EXPERTS) & ((eidx >> 3).astype(F32) == gsel)
    el = jnp.where(in_group, logits, neg)
    v1 = jnp.max(el, axis=-1, keepdims=True)
    i1 = jnp.min(jnp.where(el == v1, lane, big), axis=-1, keepdims=True)
    el2 = jnp.where(lane == i1, neg, el)
    v2 = jnp.max(el2, axis=-1, keepdims=True)
    i2 = jnp.min(jnp.where(el2 == v2, lane, big), axis=-1, keepdims=True)
    e2 = jnp.exp(v2 - v1)
    w1 = pg_sel / (1.0 + e2)
    w2 = pg_sel * e2 / (1.0 + e2)
    return i1, i2, w1, w2


TILE_ROWS = D_MODEL // LANES


def _tiles_to_rows(ref, n, base=0):
    return jnp.concatenate([ref[pl.ds(base + s, n, stride=TILE_ROWS), :] for s in range(TILE_ROWS)], axis=1)


def _rows_to_tiles(ref, val, n, base=0):
    for s in range(TILE_ROWS):
        ref[pl.ds(base + s, n, stride=TILE_ROWS), :] = val[:, s * LANES:(s + 1) * LANES]


def _two_part_specs(block_rows, n_a, n_b):
    return [pl.BlockSpec((block_rows, LANES), lambda i: (jnp.minimum(i, n_a - 1), 0)),
            pl.BlockSpec((block_rows, LANES), lambda i: (jnp.clip(i - n_a, 0, n_b - 1), 0))]


def _on_part(n_a, fn, *ref_pairs):
    i = pl.program_id(0)
    pl.when(i < n_a)(lambda: fn(*[p[0] for p in ref_pairs]))
    pl.when(i >= n_a)(lambda: fn(*[p[1] for p in ref_pairs]))


def _route_body(xa_ref, xb_ref, gm_ref, wr_hi_ref, wr_lo_ref, info_ref, cnt_ref, cnt_scr, *, tm, n_a):
    @pl.when(pl.program_id(0) == 0)
    def _():
        cnt_scr[...] = jnp.zeros_like(cnt_scr)

    _on_part(n_a, functools.partial(_route_step, gm_ref, wr_hi_ref, wr_lo_ref, info_ref, cnt_ref, cnt_scr, tm),
             (xa_ref, xb_ref))


def _route_step(gm_ref, wr_hi_ref, wr_lo_ref, info_ref, cnt_ref, cnt_scr, tm, x2t_ref):
    hm = _rms(_tiles_to_rows(x2t_ref, tm), gm_ref[...])
    i1, i2, w1, w2 = _router(hm, wr_hi_ref, wr_lo_ref)
    lane_i = lax.broadcasted_iota(jnp.int32, (tm, ROUTER_LANES), 1)
    lane = lane_i.astype(F32)
    chosen = ((lane == i1) | (lane == i2)).astype(F32)
    earlier = (lax.broadcasted_iota(jnp.int32, (tm, tm), 1) < lax.broadcasted_iota(jnp.int32, (tm, tm), 0))
    before = _bdot(earlier.astype(BF16), chosen.astype(BF16)) + cnt_scr[0:1, :]
    r1 = jnp.sum(jnp.where(lane == i1, before, 0.0), axis=-1, keepdims=True)
    r2 = jnp.sum(jnp.where(lane == i2, before, 0.0), axis=-1, keepdims=True)
    cols = (i1, i2, w1, w2, r1, r2)
    info = jnp.zeros((tm, ROUTER_LANES), F32)
    for j, col in enumerate(cols):
        info = jnp.where(lane_i == j, col, info)
    info_ref[...] = info
    cnt_scr[...] = cnt_scr[...] + jnp.sum(chosen, axis=0, keepdims=True)
    cnt_ref[...] = cnt_scr[...]


def _route(xa, xb, g_moe, wr_hi, wr_lo, tm=512):
    n_a, n_b = xa.shape[0] // (tm * TILE_ROWS), xb.shape[0] // (tm * TILE_ROWS)
    t = (n_a + n_b) * tm

    def const(shape):
        return pl.BlockSpec(shape, lambda i: (0,) * len(shape))

    body = functools.partial(_route_body, tm=tm, n_a=n_a)
    return pl.pallas_call(
        body,
        grid=(n_a + n_b,),
        in_specs=_two_part_specs(tm * TILE_ROWS, n_a, n_b) + [
            const((1, D_MODEL)), const((D_MODEL, ROUTER_LANES)), const((D_MODEL, ROUTER_LANES))],
        out_specs=[pl.BlockSpec((tm, ROUTER_LANES), lambda i: (i, 0)), const((SUBLANES, ROUTER_LANES))],
        out_shape=[jax.ShapeDtypeStruct((t, ROUTER_LANES), F32),
                   jax.ShapeDtypeStruct((SUBLANES, ROUTER_LANES), F32)],
        scratch_shapes=[pltpu.VMEM((SUBLANES, ROUTER_LANES), F32)],
        compiler_params=_params("arbitrary"),
        name="route",
    )(xa, xb, g_moe, wr_hi, wr_lo)


def _route_meta(info, cnt, tme):
    t = info.shape[0]
    n_tiles = 2 * t // tme
    counts = cnt[0, N_EXPERT_GROUPS:N_EXPERT_GROUPS + N_EXPERTS].astype(jnp.int32)
    ends = jnp.cumsum(counts)
    starts = ends - counts
    e = info[:, 0:2].astype(jnp.int32) - N_EXPERT_GROUPS
    onehot = e[:, :, None] == jnp.arange(N_EXPERTS, dtype=jnp.int32)
    slot = (jnp.sum(jnp.where(onehot, starts, 0), axis=-1) + info[:, 4:6].astype(jnp.int32)).reshape(-1)
    pts = jnp.concatenate([jnp.arange(n_tiles, dtype=jnp.int32) * tme, starts])
    n = pts.shape[0]
    idx = jnp.arange(n, dtype=jnp.int32)
    pos = jnp.sum((pts[None, :] < pts[:, None]) | ((pts[None, :] == pts[:, None]) & (idx[None, :] < idx[:, None])),
                  axis=1)
    lo_abs = jnp.sum(jnp.where(pos[:, None] == idx[None, :], pts[:, None], 0), axis=0)
    hi_abs = jnp.concatenate([lo_abs[1:], jnp.full((1,), 2 * t, jnp.int32)])
    tile = jnp.minimum(lo_abs // tme, n_tiles - 1)
    expert = jnp.clip(jnp.sum(starts[None, :] <= lo_abs[:, None], axis=1) - 1, 0, N_EXPERTS - 1)
    first = jnp.concatenate([jnp.ones((1,), jnp.int32), (tile[1:] != tile[:-1]).astype(jnp.int32)])
    last = jnp.concatenate([(tile[1:] != tile[:-1]).astype(jnp.int32), jnp.ones((1,), jnp.int32)])
    work = [a.astype(jnp.int32) for a in (tile, expert, lo_abs - tile * tme, hi_abs - tile * tme, first, last)]
    return slot.astype(jnp.int32), work


def _tile_copy(src, src_tok, dst, dst_tok, sem):
    return pltpu.make_async_copy(src.at[pl.ds(pl.multiple_of(src_tok * TILE_ROWS, TILE_ROWS), TILE_ROWS), :],
                                 dst.at[pl.ds(pl.multiple_of(dst_tok * TILE_ROWS, TILE_ROWS), TILE_ROWS), :], sem)


def _scatter_body(slot_ref, xa_ref, xb_ref, xs_hbm, sem, *, tm, n_a):
    _on_part(n_a, functools.partial(_scatter_step, slot_ref, xs_hbm, sem, tm), (xa_ref, xb_ref))


def _scatter_step(slot_ref, xs_hbm, sem, tm, x2t_ref):
    def issue(r, c):
        _tile_copy(x2t_ref, r, xs_hbm, slot_ref[2 * r], sem).start(priority=0)
        _tile_copy(x2t_ref, r, xs_hbm, slot_ref[2 * r + 1], sem).start(priority=1)
        return c

    lax.fori_loop(0, tm, issue, 0, unroll=8)

    def drain(r, c):
        _tile_copy(x2t_ref, r, xs_hbm, 0, sem).wait()
        _tile_copy(x2t_ref, r, xs_hbm, 0, sem).wait()
        return c

    lax.fori_loop(0, tm, drain, 0, unroll=8)


def _scatter(slot, xa, xb, tm=512):
    n_a, n_b = xa.shape[0] // (tm * TILE_ROWS), xb.shape[0] // (tm * TILE_ROWS)
    t = (n_a + n_b) * tm
    body = functools.partial(_scatter_body, tm=tm, n_a=n_a)
    return pl.pallas_call(
        body,
        grid=(n_a + n_b,),
        in_specs=[pl.BlockSpec((2 * tm,), lambda i: (i,), memory_space=pltpu.SMEM)]
        + _two_part_specs(tm * TILE_ROWS, n_a, n_b),
        out_specs=pl.BlockSpec(memory_space=pl.ANY),
        out_shape=jax.ShapeDtypeStruct((2 * t * TILE_ROWS, LANES), F32),
        scratch_shapes=[pltpu.SemaphoreType.DMA(())],
        compiler_params=_params("arbitrary"),
        name="scatter",
    )(slot, xa, xb)


def _expert_body(tile_ref, exp_ref, lo_ref, hi_ref, first_ref, last_ref, xs_ref, gm_ref, wg_ref, wu_ref, wd_ref,
                 ys_ref, hb_scr, acc_scr, *, tme):
    k = pl.program_id(0)
    first = first_ref[k] == 1
    last = last_ref[k] == 1
    nonempty = hi_ref[k] > lo_ref[k]

    @pl.when(first)
    def _():
        hb_scr[...] = _rms(_tiles_to_rows(xs_ref, tme), gm_ref[...]).astype(BF16)

    @pl.when(first & jnp.logical_not(nonempty))
    def _():
        acc_scr[...] = jnp.zeros_like(acc_scr)

    @pl.when(nonempty)
    def _():
        hb = hb_scr[...]
        act = jax.nn.silu(_bdot(hb, wg_ref[0])) * _bdot(hb, wu_ref[0])
        row = lax.broadcasted_iota(jnp.int32, (tme, 1), 0)
        y = jnp.where((row >= lo_ref[k]) & (row < hi_ref[k]), _bdot(act.astype(BF16), wd_ref[0]), 0.0)

        @pl.when(first & last)
        def _():
            _rows_to_tiles(ys_ref, y, tme)

        @pl.when(first & jnp.logical_not(last))
        def _():
            acc_scr[...] = y

        @pl.when(jnp.logical_not(first))
        def _():
            acc_scr[...] += y

    @pl.when(last & jnp.logical_not(first & nonempty))
    def _():
        _rows_to_tiles(ys_ref, acc_scr[...], tme)


def _experts(work, xs, g_moe, wg, wu, wd, tme):
    n_work = work[0].shape[0]
    body = functools.partial(_expert_body, tme=tme)
    tile_blk = pl.BlockSpec((tme * TILE_ROWS, LANES), lambda k, tile, *_: (tile[k], 0))

    def wspec(shape):
        return pl.BlockSpec(shape, lambda k, tile, exp, *_: (exp[k], 0, 0))

    grid_spec = pltpu.PrefetchScalarGridSpec(
        num_scalar_prefetch=len(work),
        grid=(n_work,),
        in_specs=[tile_blk, pl.BlockSpec((1, D_MODEL), lambda k, *_: (0, 0)),
                  wspec((1, D_MODEL, D_FF_EXPERT)), wspec((1, D_MODEL, D_FF_EXPERT)),
                  wspec((1, D_FF_EXPERT, D_MODEL))],
        out_specs=tile_blk,
        scratch_shapes=[pltpu.VMEM((tme, D_MODEL), BF16), pltpu.VMEM((tme, D_MODEL), F32)],
    )
    return pl.pallas_call(
        body,
        grid_spec=grid_spec,
        out_shape=jax.ShapeDtypeStruct(xs.shape, F32),
        compiler_params=_params("arbitrary"),
        name="experts",
    )(*work, xs, g_moe, wg, wu, wd)


def _combine_body(slot_ref, slot_next_ref, xa_ref, xb_ref, info_ref, ys_hbm, gf_ref, ya_ref, yb_ref, b0, b1, sem,
                  *, tm, nstep, n_a):
    i = pl.program_id(0)
    cur = i % 2

    def gather(slots, buf):
        def issue(r, c):
            _tile_copy(ys_hbm, slots[2 * r], b0.at[buf], r, sem.at[buf]).start(priority=0)
            _tile_copy(ys_hbm, slots[2 * r + 1], b1.at[buf], r, sem.at[buf]).start(priority=1)
            return c

        lax.fori_loop(0, tm, issue, 0, unroll=8)

    @pl.when(i == 0)
    def _():
        gather(slot_ref, 0)

    @pl.when(i + 1 < nstep)
    def _():
        gather(slot_next_ref, 1 - cur)

    def drain(r, c):
        _tile_copy(ys_hbm, 0, b0.at[cur], r, sem.at[cur]).wait()
        _tile_copy(ys_hbm, 0, b1.at[cur], r, sem.at[cur]).wait()
        return c

    lax.fori_loop(0, tm, drain, 0, unroll=8)
    info = info_ref[...]
    moe = info[:, 2:3] * _tiles_to_rows(b0.at[cur], tm) + info[:, 3:4] * _tiles_to_rows(b1.at[cur], tm)

    def finish(x2t_ref, y_ref):
        y_ref[...] = _rms(_tiles_to_rows(x2t_ref, tm) + moe, gf_ref[...])

    _on_part(n_a, finish, (xa_ref, xb_ref), (ya_ref, yb_ref))


def _combine(slot, xa, xb, info, ys, g_final, tm=256):
    n_a, n_b = xa.shape[0] // (tm * TILE_ROWS), xb.shape[0] // (tm * TILE_ROWS)
    nstep = n_a + n_b
    body = functools.partial(_combine_body, tm=tm, nstep=nstep, n_a=n_a)
    buf = pltpu.VMEM((2, tm * TILE_ROWS, LANES), F32)
    return pl.pallas_call(
        body,
        grid=(nstep,),
        in_specs=[pl.BlockSpec((2 * tm,), lambda i: (i,), memory_space=pltpu.SMEM),
                  pl.BlockSpec((2 * tm,), lambda i: (jnp.minimum(i + 1, nstep - 1),), memory_space=pltpu.SMEM)]
        + _two_part_specs(tm * TILE_ROWS, n_a, n_b) + [
                  pl.BlockSpec((tm, ROUTER_LANES), lambda i: (i, 0)),
                  pl.BlockSpec(memory_space=pl.ANY),
                  pl.BlockSpec((1, D_MODEL), lambda i: (0, 0))],
        out_specs=[pl.BlockSpec((tm, D_MODEL), lambda i: (jnp.minimum(i, n_a - 1), 0)),
                   pl.BlockSpec((tm, D_MODEL), lambda i: (jnp.clip(i - n_a, 0, n_b - 1), 0))],
        out_shape=[jax.ShapeDtypeStruct((n_a * tm, D_MODEL), F32), jax.ShapeDtypeStruct((n_b * tm, D_MODEL), F32)],
        scratch_shapes=[buf, buf, pltpu.SemaphoreType.DMA((2,))],
        compiler_params=_params("arbitrary"),
        name="combine",
    )(slot, slot, xa, xb, info, ys, g_final)


def _moe(xa, xb, w, tme=512):
    info, cnt = _route(xa, xb, w["g_moe"], w["wr_hi"], w["wr_lo"])
    slot, work = _route_meta(info, cnt, tme)
    xs = _scatter(slot, xa, xb)
    ys = _experts(work, xs, w["g_moe"], w["moe_wg"], w["moe_wu"], w["moe_wd"], tme)
    return _combine(slot, xa, xb, info, ys, w["g_final"])


def _layer(x3d, k4, v4, s5_h0, lru_h0, lru_conv, w, chunk, nb, tq):
    bsz, seq, _ = x3d.shape
    x2d = x3d.reshape(bsz * seq, D_MODEL)
    u, xl, gl = _mixin(x2d, w["g_mix"], w["w_in"])

    nstate = S5_GROUPS * S5_STATE
    if s5_h0 is None:
        zero = jnp.zeros((bsz, 1, nstate), F32)
        ys, s5_re, s5_im = _s5(u, w["s5"][chunk], zero, zero, chunk, seq // chunk, True)
    else:
        ys, s5_re, s5_im = _s5(u, w["s5"][chunk], s5_h0[0].reshape(bsz, nstate), s5_h0[1].reshape(bsz, nstate),
                               chunk, bsz, False)
    s5_re = s5_re.reshape(bsz, S5_GROUPS, S5_STATE)
    s5_im = s5_im.reshape(bsz, S5_GROUPS, S5_STATE)

    lru_w = (w["conv_w"], w["conv_b"], w["lru_wg"], w["lru_bg"], w["lru_lam"])
    if lru_h0 is None:
        yl, hl = _lru_seq(xl, gl, *lru_w, bsz, seq)
        lru_h = hl.reshape(bsz, SUBLANES, D_LRU)[:, 0]
    else:
        hist = jnp.pad(lru_conv, ((0, 0), (SUBLANES - (CONV_W - 1), 0), (0, 0))).reshape(bsz * seq, D_LRU)
        h0rep = jnp.repeat(lru_h0, seq, axis=0)
        yl, hall = _lru_dec(xl, gl, hist, h0rep, *lru_w)
        lru_h = hall.reshape(bsz, seq, D_LRU)[:, seq - 1]
    conv_new = xl.reshape(bsz, seq, D_LRU)[:, seq - (CONV_W - 1):]

    x1, q = _mixout(x2d, ys, yl, w["w_glu"], w["b_glu"], w["w_out"], w["g_xa"], w["w_q"])
    x2t = _attn(q.reshape(bsz, seq, D_MODEL), x1.reshape(bsz, seq, D_MODEL), k4, v4, w["w_o"], nb, tq)
    return x2t, s5_re, s5_im, lru_h, conv_new


def kernel(x_prompt, x_sample, mem_prompt, cache_mem_k, cache_mem_v, state_s5_re, state_s5_im, state_lru_h, state_lru_conv, g_mix, w_in, s5_lam_re, s5_lam_im, s5_log_dt, s5_b_re, s5_b_im, s5_c_re, s5_c_im, s5_d, s5_w_glu, s5_b_glu, lru_conv_w, lru_conv_b, lru_w_a, lru_b_a, lru_w_x, lru_b_x, lru_lam, w_out, g_xa, g_mem, xa_w_q, xa_w_k, xa_w_v, xa_w_o, g_moe, moe_w_group, moe_w_expert, moe_w_gate, moe_w_up, moe_w_down, g_final):
    depth = g_mix.shape[0]
    assert depth == 1, "single-layer step"
    l = 0
    bsz, seq, _ = x_prompt.shape
    dbsz, dseq, _ = x_sample.shape
    chunk = SUBLANES
    assert seq % chunk == 0 and dseq == chunk

    s5_args = (s5_lam_re[l], s5_lam_im[l], s5_log_dt[l], s5_b_re[l], s5_b_im[l], s5_c_re[l], s5_c_im[l], s5_d[l])
    wg, bg = _lru_weights(lru_w_a[l], lru_w_x[l], lru_b_a[l], lru_b_x[l])
    wr = jnp.concatenate([moe_w_group[l], moe_w_expert[l],
                          jnp.zeros((D_MODEL, ROUTER_LANES - N_EXPERT_GROUPS - N_EXPERTS), F32)], axis=1)
    wr_hi = wr.astype(BF16)
    wr_lo = (wr - wr_hi.astype(F32)).astype(BF16)
    w = {
        "g_mix": g_mix[l][None], "w_in": w_in[l].astype(BF16),
        "s5": {chunk: _s5_weights(chunk, *s5_args)},
        "conv_w": lru_conv_w[l], "conv_b": lru_conv_b[l][None], "lru_wg": wg, "lru_bg": bg,
        "lru_lam": lru_lam[l][None],
        "w_glu": s5_w_glu[l].astype(BF16), "b_glu": s5_b_glu[l][None], "w_out": w_out[l].astype(BF16),
        "g_xa": g_xa[l][None], "w_q": xa_w_q[l].astype(BF16), "w_o": xa_w_o[l].astype(BF16),
        "g_moe": g_moe[l][None], "wr_hi": wr_hi, "wr_lo": wr_lo,
        "moe_wg": moe_w_gate[l].astype(BF16), "moe_wu": moe_w_up[l].astype(BF16),
        "moe_wd": moe_w_down[l].astype(BF16), "g_final": g_final[None],
    }

    mk, mv = _memkv(mem_prompt.reshape(bsz * N_MEM, D_MODEL), g_mem[l][None],
                    xa_w_k[l].astype(BF16), xa_w_v[l].astype(BF16))
    xp, p_re, p_im, p_h, p_conv = _layer(x_prompt, mk.reshape(bsz, N_MEM, D_MODEL), mv.reshape(bsz, N_MEM, D_MODEL),
                                         None, None, None, w, chunk, nb=1, tq=1024)
    xs, s_re, s_im, s_h, s_conv = _layer(x_sample, cache_mem_k[l], cache_mem_v[l],
                                         (state_s5_re[l], state_s5_im[l]), state_lru_h[l],
                                         state_lru_conv[l], w, chunk, nb=4, tq=dseq)
    yp, ysmp = _moe(xp, xs, w)
    yp = yp.reshape(bsz, seq, D_MODEL)
    ysmp = ysmp.reshape(dbsz, dseq, D_MODEL)

    return (yp, ysmp,
            mk.reshape(1, bsz, N_MEM, XA_HEADS, XA_HEAD_DIM), mv.reshape(1, bsz, N_MEM, XA_HEADS, XA_HEAD_DIM),
            p_re[None], p_im[None], p_h[None], p_conv[None],
            s_re[None], s_im[None], s_h[None], s_conv[None])
```

```python
import functools

import jax
import jax.numpy as jnp
import numpy as np
from jax import lax
from jax.experimental import pallas as pl
from jax.experimental.pallas import tpu as pltpu

F32 = jnp.float32
BF16 = jnp.bfloat16

D_MODEL = 1024
D_S5 = 512
D_LRU = 512
S5_GROUP = 16
S5_GROUPS = 32
S5_STATE = 64
LRU_HEADS = 8
LRU_HEAD_DIM = 64
CONV_W = 4
LRU_C = 8.0
N_MEM = 256
XA_HEADS = 4
XA_HEAD_DIM = 256
N_EXPERT_GROUPS = 4
EXPERTS_PER_GROUP = 8
N_EXPERTS = 32
D_FF_EXPERT = 256
EPS = 1e-6

SUBLANES = 8
LANES = 128
ROUTER_LANES = 128
VMEM_LIMIT = 56 * 1024 * 1024


def _params(*sem):
    return pltpu.CompilerParams(dimension_semantics=sem, vmem_limit_bytes=VMEM_LIMIT)


def _rms(x, g):
    ms = jnp.mean(x * x, axis=-1, keepdims=True)
    return x * lax.rsqrt(ms + EPS) * g


def _bdot(a, b):
    return jnp.dot(a, b, preferred_element_type=F32)


def _memkv_body(m_ref, g_ref, wk_ref, wv_ref, k_ref, v_ref):
    hb = _rms(m_ref[...], g_ref[...]).astype(BF16)
    k_ref[...] = _bdot(hb, wk_ref[...])
    v_ref[...] = _bdot(hb, wv_ref[...])


def _memkv(mem2d, g_mem, wk, wv, tm=512):
    t = mem2d.shape[0]
    row = pl.BlockSpec((tm, D_MODEL), lambda i: (i, 0))
    full = pl.BlockSpec((D_MODEL, D_MODEL), lambda i: (0, 0))
    return pl.pallas_call(
        _memkv_body,
        grid=(t // tm,),
        in_specs=[row, pl.BlockSpec((1, D_MODEL), lambda i: (0, 0)), full, full],
        out_specs=[row, row],
        out_shape=[jax.ShapeDtypeStruct((t, D_MODEL), F32)] * 2,
        compiler_params=_params("parallel"),
        name="memkv",
    )(mem2d, g_mem, wk, wv)


def _mixin_body(x_ref, g_ref, w_ref, u_ref, xl_ref, gl_ref):
    hb = _rms(x_ref[...], g_ref[...]).astype(BF16)
    proj = _bdot(hb, w_ref[...])
    u_ref[...] = proj[:, :D_S5]
    xl_ref[...] = proj[:, D_S5:D_S5 + D_LRU]
    gl_ref[...] = proj[:, D_S5 + D_LRU:]


def _mixin(x2d, g_mix, w_in, tm=1024):
    t = x2d.shape[0]
    half = pl.BlockSpec((tm, D_S5), lambda i: (i, 0))
    return pl.pallas_call(
        _mixin_body,
        grid=(t // tm,),
        in_specs=[pl.BlockSpec((tm, D_MODEL), lambda i: (i, 0)),
                  pl.BlockSpec((1, D_MODEL), lambda i: (0, 0)),
                  pl.BlockSpec((D_MODEL, D_S5 + 2 * D_LRU), lambda i: (0, 0))],
        out_specs=[half, half, half],
        out_shape=[jax.ShapeDtypeStruct((t, D_S5), F32)] * 3,
        compiler_params=_params("parallel"),
        name="mixin",
    )(x2d, g_mix, w_in)


GROUPS_PER_BLOCK = LANES // S5_GROUP
PAIRS_PER_BLOCK = GROUPS_PER_BLOCK // 2
STATE_BLOCK = GROUPS_PER_BLOCK * S5_STATE
PW_ROWS = 24


def _s5_body(u_ref, wa_ref, wh_ref, pw_ref, d_ref, h0r_ref, h0i_ref, y_ref, hfr_ref, hfi_ref, hpr_scr, hpi_scr,
             *, chunk, nrow, scan):
    nh = chunk // SUBLANES
    lc = chunk * S5_GROUP
    slot = lax.broadcasted_iota(jnp.int32, (nrow, LANES), 1) // S5_GROUP
    in_slot = [slot == s for s in range(GROUPS_PER_BLOCK)]

    def pick(src, sel):
        out = src[sel(0)]
        for s in range(1, GROUPS_PER_BLOCK):
            out = jnp.where(in_slot[s], src[sel(s)], out)
        return out

    nat, skew = [], []
    for t in range(chunk):
        a = u_ref[pl.ds(t, nrow, stride=chunk), :]
        nat.append(a)
        s = t % GROUPS_PER_BLOCK
        skew.append(pltpu.roll(a, s * S5_GROUP, axis=1) if s else a)
    def lane_roll(x, slots):
        slots %= GROUPS_PER_BLOCK
        return pltpu.roll(x, slots * S5_GROUP, axis=1) if slots else x

    z = [[lane_roll(pick(skew, lambda s, hh=hh, m=m: SUBLANES * hh + (s - m) % GROUPS_PER_BLOCK), -m)
          for hh in range(nh)] for m in range(GROUPS_PER_BLOCK)]

    ys, er, ei = [], [], []
    for p in range(PAIRS_PER_BLOCK):
        lhs = jnp.concatenate(z[2 * p] + z[2 * p + 1], axis=1).astype(BF16)
        full = _bdot(lhs, wa_ref[p])
        ys.append(full[:, :2 * lc])
        er.append(full[:, 2 * lc:2 * lc + LANES])
        ei.append(full[:, 2 * lc + LANES:])
    er = jnp.concatenate(er, axis=1)
    ei = jnp.concatenate(ei, axis=1)

    if scan:
        ntile = nrow // SUBLANES
        xr = er.reshape(ntile, SUBLANES, STATE_BLOCK)
        xi = ei.reshape(ntile, SUBLANES, STATE_BLOCK)
        row = lax.broadcasted_iota(jnp.int32, xr.shape, 1)
        for k, d in enumerate((1, 2, 4)):
            pr, pi = pw_ref[0, k:k + 1, :], pw_ref[0, 4 + k:5 + k, :]
            sr, si = pltpu.roll(xr, d, axis=1), pltpu.roll(xi, d, axis=1)
            m = row >= d
            xr, xi = jnp.where(m, xr + pr * sr - pi * si, xr), jnp.where(m, xi + pr * si + pi * sr, xi)
        sxr = jnp.where(row >= 1, pltpu.roll(xr, 1, axis=1), 0.0)
        sxi = jnp.where(row >= 1, pltpu.roll(xi, 1, axis=1), 0.0)
        p8r, p8i = pw_ref[0, 3:4, :], pw_ref[0, 7:8, :]
        qr, qi = pw_ref[0, 8:16, :], pw_ref[0, 16:24, :]
        hr, hi = h0r_ref[0], h0i_ref[0]
        for t in range(ntile):
            rows = slice(SUBLANES * t, SUBLANES * (t + 1))
            hpr_scr[rows, :] = sxr[t] + qr * hr - qi * hi
            hpi_scr[rows, :] = sxi[t] + qr * hi + qi * hr
            hr, hi = (xr[t, SUBLANES - 1:] + p8r * hr - p8i * hi, xi[t, SUBLANES - 1:] + p8r * hi + p8i * hr)
        hfr_ref[0] = hr
        hfi_ref[0] = hi
        hpr, hpi = hpr_scr[...], hpi_scr[...]
    else:
        hpr, hpi = h0r_ref[...], h0i_ref[...]
        p1r, p1i = pw_ref[0, 0:1, :], pw_ref[0, 4:5, :]
        hfr_ref[...] = er + p1r * hpr - p1i * hpi
        hfi_ref[...] = ei + p1r * hpi + p1i * hpr

    yg = []
    for p in range(PAIRS_PER_BLOCK):
        lanes = slice(LANES * p, LANES * (p + 1))
        hp = jnp.concatenate([hpr[:, lanes], hpi[:, lanes]], axis=1).astype(BF16)
        out = ys[p] + _bdot(hp, wh_ref[p])
        for half in range(2):
            q = 2 * p + half
            yg.append([lane_roll(out[:, half * lc + hh * LANES:half * lc + (hh + 1) * LANES], q)
                       for hh in range(nh)])
    d = d_ref[...]
    for t in range(chunk):
        hh, s = divmod(t, GROUPS_PER_BLOCK)
        c = lane_roll(pick([g[hh] for g in yg], lambda sl, s=s: (sl - s) % GROUPS_PER_BLOCK), -s)
        y_ref[pl.ds(t, nrow, stride=chunk), :] = c + d * nat[t]


def _s5(u2d, weights, h0r, h0i, chunk, nrow, scan):
    wa, wh, pw, dvec = weights
    t = u2d.shape[0]
    lc = chunk * S5_GROUP
    rows = nrow * chunk
    nblk = D_S5 // LANES
    body = functools.partial(_s5_body, chunk=chunk, nrow=nrow, scan=scan)
    ublk = pl.BlockSpec((rows, LANES), lambda i, j: (i, j))
    if scan:
        hblk = pl.BlockSpec((1, 1, STATE_BLOCK), lambda i, j: (i, 0, j))
        hshape = jax.ShapeDtypeStruct((t // rows, 1, S5_GROUPS * S5_STATE), F32)
    else:
        hblk = pl.BlockSpec((nrow, STATE_BLOCK), lambda i, j: (i, j))
        hshape = jax.ShapeDtypeStruct((t // chunk, S5_GROUPS * S5_STATE), F32)
    return pl.pallas_call(
        body,
        grid=(t // rows, nblk),
        in_specs=[ublk,
                  pl.BlockSpec((PAIRS_PER_BLOCK, 2 * lc, 2 * lc + 2 * LANES), lambda i, j: (j, 0, 0)),
                  pl.BlockSpec((PAIRS_PER_BLOCK, 2 * LANES, 2 * lc), lambda i, j: (j, 0, 0)),
                  pl.BlockSpec((1, PW_ROWS, STATE_BLOCK), lambda i, j: (j, 0, 0)),
                  pl.BlockSpec((1, LANES), lambda i, j: (0, j)),
                  hblk, hblk],
        out_specs=[ublk, hblk, hblk],
        out_shape=[jax.ShapeDtypeStruct((t, D_S5), F32), hshape, hshape],
        scratch_shapes=[pltpu.VMEM((nrow, STATE_BLOCK), F32), pltpu.VMEM((nrow, STATE_BLOCK), F32)],
        compiler_params=_params("parallel", "parallel"),
        name="s5",
    )(u2d, wa, wh, pw, dvec, h0r, h0i)


def _s5_weights(chunk, lam_re, lam_im, log_dt, b_re, b_im, c_re, c_im, d_skip):
    hi = lax.Precision.HIGHEST
    dt = jnp.exp(log_dt)[:, None]
    mag = jnp.exp(lam_re * dt)
    ab_re = mag * jnp.cos(lam_im * dt)
    ab_im = mag * jnp.sin(lam_im * dt)
    den = lam_re * lam_re + lam_im * lam_im
    nr = ab_re - 1.0
    f_re = (nr * lam_re + ab_im * lam_im) / den
    f_im = (ab_im * lam_re - nr * lam_im) / den
    bb_re = f_re[..., None] * b_re - f_im[..., None] * b_im
    bb_im = f_re[..., None] * b_im + f_im[..., None] * b_re
    def powers(ks):
        k = jnp.asarray(np.asarray(ks, np.float32))[:, None, None]
        m = jnp.exp(k * (lam_re * dt))
        return m * jnp.cos(k * (lam_im * dt)), m * jnp.sin(k * (lam_im * dt))

    pw_re, pw_im = powers(range(chunk + 1))
    pb_re = pw_re[:chunk, ..., None] * bb_re - pw_im[:chunk, ..., None] * bb_im
    pb_im = pw_re[:chunk, ..., None] * bb_im + pw_im[:chunk, ..., None] * bb_re
    kk = (jnp.einsum("gon,kgni->kgoi", c_re, pb_re, precision=hi)
          - jnp.einsum("gon,kgni->kgoi", c_im, pb_im, precision=hi))
    npair = S5_GROUPS // 2
    lc = chunk * S5_GROUP
    row0 = kk.transpose(1, 3, 0, 2).reshape(S5_GROUPS, S5_GROUP, lc)
    row0 = jnp.pad(row0, ((0, 0), (0, 0), (lc, 0)))
    conv = jnp.stack([row0[:, :, lc - S5_GROUP * ti:2 * lc - S5_GROUP * ti] for ti in range(chunk)], axis=1)
    conv = conv.reshape(npair, 2, lc, lc)
    ends = [x[::-1].transpose(1, 0, 3, 2).reshape(npair, 2, lc, S5_STATE) for x in (pb_re, pb_im)]
    zc = jnp.zeros((npair, lc, lc), F32)
    zs = jnp.zeros((npair, lc, S5_STATE), F32)
    wa = jnp.concatenate([
        jnp.concatenate([conv[:, 0], zc, ends[0][:, 0], zs, ends[1][:, 0], zs], axis=-1),
        jnp.concatenate([zc, conv[:, 1], zs, ends[0][:, 1], zs, ends[1][:, 1]], axis=-1)], axis=1)
    gr = c_re[None] * pw_re[1:, :, None, :] - c_im[None] * pw_im[1:, :, None, :]
    gi = c_re[None] * pw_im[1:, :, None, :] + c_im[None] * pw_re[1:, :, None, :]
    zr = jnp.zeros((npair, S5_STATE, lc), F32)
    blocks = []
    for x in (gr, -gi):
        hy = x.transpose(1, 3, 0, 2).reshape(npair, 2, S5_STATE, lc)
        blocks += [jnp.concatenate([hy[:, 0], zr], axis=-1), jnp.concatenate([zr, hy[:, 1]], axis=-1)]
    wh = jnp.concatenate(blocks, axis=1)
    tab = [1, 2, 4, 8] + list(range(SUBLANES))
    qr, qi = powers([chunk * k for k in tab])
    nblk = D_S5 // LANES
    pw = jnp.concatenate([qr[:4], qi[:4], qr[4:], qi[4:]]).reshape(PW_ROWS, nblk, STATE_BLOCK).transpose(1, 0, 2)
    return wa.astype(BF16), wh.astype(BF16), pw, d_skip.reshape(1, D_S5)


def _lru_ab(xc, wg_ref, bg_ref, lam_ref):
    xb = xc.astype(BF16)
    half = D_LRU // 2
    g0 = _bdot(xb[:, :half], wg_ref[0]) + bg_ref[0]
    g1 = _bdot(xb[:, half:], wg_ref[1]) + bg_ref[1]
    r = jax.nn.sigmoid(jnp.concatenate([g0[:, :half], g1[:, :half]], axis=1))
    ig = jax.nn.sigmoid(jnp.concatenate([g0[:, half:], g1[:, half:]], axis=1))
    zl = -lam_ref[...]
    softplus = jnp.maximum(zl, 0.0) + jnp.log1p(jnp.exp(-jnp.abs(zl)))
    log_a = -LRU_C * r * softplus
    a = jnp.exp(log_a)
    b = jnp.sqrt(-jnp.tanh(log_a) * (a * a + 1.0)) * (ig * xc)
    return a, b


def _tile_scan(a, b):
    shape = a.shape
    a = a.reshape(shape[0] // SUBLANES, SUBLANES, shape[1])
    b = b.reshape(a.shape)
    row = lax.broadcasted_iota(jnp.int32, a.shape, 1)
    for d in (1, 2, 4):
        a_prev = pltpu.roll(a, d, axis=1)
        b_prev = pltpu.roll(b, d, axis=1)
        m = row >= d
        b = jnp.where(m, b + a * b_prev, b)
        a = jnp.where(m, a * a_prev, a)
    return a.reshape(shape), b.reshape(shape)


def _lru_seq_body(xl_ref, gl_ref, cw_ref, cb_ref, wg_ref, bg_ref, lam_ref, y_ref, hl_ref,
                  xp_scr, a_scr, b_scr, h_scr, hc_scr, *, ts):
    ti = pl.program_id(1)

    @pl.when(ti == 0)
    def _():
        xp_scr[0:SUBLANES, :] = jnp.zeros((SUBLANES, D_LRU), F32)
        hc_scr[...] = jnp.zeros((SUBLANES, D_LRU), F32)

    xl = xl_ref[...]
    xp_scr[SUBLANES:SUBLANES + ts, :] = xl
    xc = cb_ref[...] + xl * cw_ref[CONV_W - 1:CONV_W, :]
    for j in range(1, CONV_W):
        xc = xc + xp_scr[SUBLANES - j:SUBLANES - j + ts, :] * cw_ref[CONV_W - 1 - j:CONV_W - j, :]
    xp_scr[0:SUBLANES, :] = xl[ts - SUBLANES:, :]
    a, b = _lru_ab(xc, wg_ref, bg_ref, lam_ref)
    a, b = _tile_scan(a, b)
    a_scr[...] = a
    b_scr[...] = b

    def step(i, hin):
        rows = pl.ds(pl.multiple_of(i * SUBLANES, SUBLANES), SUBLANES)
        h = b_scr[rows, :] + a_scr[rows, :] * hin
        h_scr[rows, :] = h
        return h[SUBLANES - 1:SUBLANES, :]

    hlast = lax.fori_loop(0, ts // SUBLANES, step, hc_scr[0:1, :], unroll=4)
    hc_scr[...] = jnp.broadcast_to(hlast, (SUBLANES, D_LRU))
    hl_ref[...] = hc_scr[...]
    y_ref[...] = h_scr[...] * jax.nn.gelu(gl_ref[...])


def _lru_dec_body(xl_ref, gl_ref, hist_ref, h0_ref, cw_ref, cb_ref, wg_ref, bg_ref, lam_ref, y_ref, hl_ref, *, tm):
    xl = xl_ref[...]
    hist = hist_ref[...]
    row = lax.broadcasted_iota(jnp.int32, xl.shape, 0) % SUBLANES
    xc = cb_ref[...] + xl * cw_ref[CONV_W - 1:CONV_W, :]
    for j in range(1, CONV_W):
        prev = jnp.where(row >= j, pltpu.roll(xl, j, axis=0), pltpu.roll(hist, tm - SUBLANES + j, axis=0))
        xc = xc + prev * cw_ref[CONV_W - 1 - j:CONV_W - j, :]
    a, b = _lru_ab(xc, wg_ref, bg_ref, lam_ref)
    a, b = _tile_scan(a, b)
    h = b + a * h0_ref[...]
    hl_ref[...] = h
    y_ref[...] = h * jax.nn.gelu(gl_ref[...])


def _lru_weights(w_a, w_x, b_a, b_x):
    eye = jnp.eye(LRU_HEADS, dtype=F32)
    bd_a = jnp.einsum("hij,hg->higj", w_a, eye).reshape(D_LRU, D_LRU)
    bd_x = jnp.einsum("hij,hg->higj", w_x, eye).reshape(D_LRU, D_LRU)
    half = D_LRU // 2
    wg = jnp.stack([jnp.concatenate([bd_a[:half, :half], bd_x[:half, :half]], axis=1),
                    jnp.concatenate([bd_a[half:, half:], bd_x[half:, half:]], axis=1)]).astype(BF16)
    ba = b_a.reshape(1, D_LRU)
    bx = b_x.reshape(1, D_LRU)
    bg = jnp.stack([jnp.concatenate([ba[:, :half], bx[:, :half]], axis=1),
                    jnp.concatenate([ba[:, half:], bx[:, half:]], axis=1)])
    return wg, bg


def _lru_common_specs(const):
    return [const((CONV_W, D_LRU)), const((1, D_LRU)), const((2, D_LRU // 2, D_LRU)),
            const((2, 1, D_LRU)), const((1, D_LRU))]


def _lru_seq(xl, gl, cw, cb, wg, bg, lam, bsz, seq, ts=512):
    nt = seq // ts
    blk = pl.BlockSpec((ts, D_LRU), lambda b, t: (b * nt + t, 0))

    def const(shape):
        return pl.BlockSpec(shape, lambda b, t: (0,) * len(shape))

    body = functools.partial(_lru_seq_body, ts=ts)
    return pl.pallas_call(
        body,
        grid=(bsz, nt),
        in_specs=[blk, blk] + _lru_common_specs(const),
        out_specs=[blk, pl.BlockSpec((SUBLANES, D_LRU), lambda b, t: (b, 0))],
        out_shape=[jax.ShapeDtypeStruct((bsz * seq, D_LRU), F32),
                   jax.ShapeDtypeStruct((bsz * SUBLANES, D_LRU), F32)],
        scratch_shapes=[pltpu.VMEM((ts + SUBLANES, D_LRU), F32), pltpu.VMEM((ts, D_LRU), F32),
                        pltpu.VMEM((ts, D_LRU), F32), pltpu.VMEM((ts, D_LRU), F32),
                        pltpu.VMEM((SUBLANES, D_LRU), F32)],
        compiler_params=_params("arbitrary", "arbitrary"),
        name="lru_seq",
    )(xl, gl, cw, cb, wg, bg, lam)


def _lru_dec(xl, gl, hist, h0rep, cw, cb, wg, bg, lam, tm=256):
    t = xl.shape[0]
    blk = pl.BlockSpec((tm, D_LRU), lambda i: (i, 0))

    def const(shape):
        return pl.BlockSpec(shape, lambda i: (0,) * len(shape))

    body = functools.partial(_lru_dec_body, tm=tm)
    return pl.pallas_call(
        body,
        grid=(t // tm,),
        in_specs=[blk, blk, blk, blk] + _lru_common_specs(const),
        out_specs=[blk, blk],
        out_shape=[jax.ShapeDtypeStruct((t, D_LRU), F32)] * 2,
        compiler_params=_params("parallel"),
        name="lru_dec",
    )(xl, gl, hist, h0rep, cw, cb, wg, bg, lam)


def _mixout_body(x_ref, ys_ref, yl_ref, wglu_ref, bglu_ref, wo_ref, gxa_ref, wq_ref, x1_ref, q_ref):
    ys = jax.nn.gelu(ys_ref[...])
    gate = jax.nn.sigmoid(_bdot(ys.astype(BF16), wglu_ref[...]) + bglu_ref[...])
    s5 = (ys * gate).astype(BF16)
    x1 = (x_ref[...] + _bdot(s5, wo_ref[0:D_S5, :]) + _bdot(yl_ref[...].astype(BF16), wo_ref[D_S5:, :]))
    x1_ref[...] = x1
    q_ref[...] = _bdot(_rms(x1, gxa_ref[...]).astype(BF16), wq_ref[...]).astype(BF16)


def _mixout(x2d, ys, yl, w_glu, b_glu, w_out, g_xa, w_q, tm=1024):
    t = x2d.shape[0]
    row = pl.BlockSpec((tm, D_MODEL), lambda i: (i, 0))
    half = pl.BlockSpec((tm, D_S5), lambda i: (i, 0))

    def const(shape):
        return pl.BlockSpec(shape, lambda i: (0,) * len(shape))

    return pl.pallas_call(
        _mixout_body,
        grid=(t // tm,),
        in_specs=[row, half, half, const((D_S5, D_S5)), const((1, D_S5)), const((D_MODEL, D_MODEL)),
                  const((1, D_MODEL)), const((D_MODEL, D_MODEL))],
        out_specs=[row, row],
        out_shape=[jax.ShapeDtypeStruct((t, D_MODEL), F32), jax.ShapeDtypeStruct((t, D_MODEL), BF16)],
        compiler_params=_params("parallel"),
        name="mixout",
    )(x2d, ys, yl, w_glu, b_glu, w_out, g_xa, w_q)


def _softmax(sc):
    p = jnp.exp(sc - jnp.max(sc, axis=-1, keepdims=True))
    return p / jnp.sum(p, axis=-1, keepdims=True)


def _attn_body(q_ref, x1_ref, k_ref, v_ref, wo_ref, x2t_ref, *, nb, tq):
    scale = XA_HEAD_DIM ** -0.5
    for i in range(nb):
        q = q_ref[i]
        heads = [q[:, h * XA_HEAD_DIM:(h + 1) * XA_HEAD_DIM] for h in range(XA_HEADS)]
        if len(k_ref.shape) == 4:
            kf = k_ref[i].reshape(N_MEM * XA_HEADS, XA_HEAD_DIM).astype(BF16)
            vf = v_ref[i].reshape(N_MEM * XA_HEADS, XA_HEAD_DIM).astype(BF16)
            sc = lax.dot_general(jnp.concatenate(heads, axis=0), kf, (((1,), (1,)), ((), ())),
                                 preferred_element_type=F32) * scale
            own = (lax.broadcasted_iota(jnp.int32, sc.shape, 1) % XA_HEADS
                   == lax.broadcasted_iota(jnp.int32, sc.shape, 0) // tq)
            oh = _bdot(_softmax(jnp.where(own, sc, -jnp.inf)).astype(BF16), vf)
            outs = [oh[h * tq:(h + 1) * tq] for h in range(XA_HEADS)]
        else:
            kb = k_ref[i].astype(BF16)
            vb = v_ref[i].astype(BF16)
            outs = []
            for h in range(XA_HEADS):
                cols = slice(h * XA_HEAD_DIM, (h + 1) * XA_HEAD_DIM)
                sc = lax.dot_general(heads[h], kb[:, cols], (((1,), (1,)), ((), ())),
                                     preferred_element_type=F32) * scale
                outs.append(_bdot(_softmax(sc).astype(BF16), vb[:, cols]))
        o = jnp.concatenate(outs, axis=1).astype(BF16)
        _rows_to_tiles(x2t_ref, x1_ref[i] + _bdot(o, wo_ref[...]), tq, base=i * tq * TILE_ROWS)


def _attn(q3, x13, k4, v4, w_o, nb, tq):
    bsz, seq, _ = q3.shape
    nt = seq // tq
    blk = pl.BlockSpec((nb, tq, D_MODEL), lambda b, t: (b, t, 0))
    kv = pl.BlockSpec((nb,) + k4.shape[1:], lambda b, t: (b,) + (0,) * (k4.ndim - 1))
    body = functools.partial(_attn_body, nb=nb, tq=tq)
    return pl.pallas_call(
        body,
        grid=(bsz // nb, nt),
        in_specs=[blk, blk, kv, kv, pl.BlockSpec((D_MODEL, D_MODEL), lambda b, t: (0, 0))],
        out_specs=pl.BlockSpec((nb * tq * TILE_ROWS, LANES), lambda b, t: (b * nt + t, 0)),
        out_shape=jax.ShapeDtypeStruct((bsz * seq * TILE_ROWS, LANES), F32),
        compiler_params=_params("parallel", "parallel"),
        name="attn",
    )(q3, x13, k4, v4, w_o)


def _router(hm, wr_hi_ref, wr_lo_ref):
    a_hi = hm.astype(BF16)
    a_lo = (hm - a_hi.astype(F32)).astype(BF16)
    logits = _bdot(a_hi, wr_hi_ref[...]) + (_bdot(a_hi, wr_lo_ref[...]) + _bdot(a_lo, wr_hi_ref[...]))
    lane_i = lax.broadcasted_iota(jnp.int32, logits.shape, 1)
    lane = lane_i.astype(F32)
    neg = -jnp.inf
    big = float(ROUTER_LANES)
    is_g = lane_i < N_EXPERT_GROUPS
    glog = jnp.where(is_g, logits, neg)
    gmax = jnp.max(glog, axis=-1, keepdims=True)
    gsel = jnp.min(jnp.where(glog == gmax, lane, big), axis=-1, keepdims=True)
    pg_sel = 1.0 / jnp.sum(jnp.where(is_g, jnp.exp(logits - gmax), 0.0), axis=-1, keepdims=True)
    eidx = lane_i - N_EXPERT_GROUPS
    in_group = (eidx >= 0) & (eidx < N_EXPERTS) & ((eidx >> 3).astype(F32) == gsel)
    el = jnp.where(in_group, logits, neg)
    v1 = jnp.max(el, axis=-1, keepdims=True)
    i1 = jnp.min(jnp.where(el == v1, lane, big), axis=-1, keepdims=True)
    el2 = jnp.where(lane == i1, neg, el)
    v2 = jnp.max(el2, axis=-1, keepdims=True)
    i2 = jnp.min(jnp.where(el2 == v2, lane, big), axis=-1, keepdims=True)
    e2 = jnp.exp(v2 - v1)
    w1 = pg_sel / (1.0 + e2)
    w2 = pg_sel * e2 / (1.0 + e2)
    return i1, i2, w1, w2


TILE_ROWS = D_MODEL // LANES


def _tiles_to_rows(ref, n, base=0, rows=TILE_ROWS):
    return jnp.concatenate([ref[pl.ds(base + s, n, stride=rows), :] for s in range(rows)], axis=1)


def _rows_to_tiles(ref, val, n, base=0, rows=TILE_ROWS):
    for s in range(rows):
        ref[pl.ds(base + s, n, stride=rows), :] = val[:, s * LANES:(s + 1) * LANES]


def _btiles_to_rows(ref, n, tmp):
    tmp[...] = ref[...].astype(F32)
    return _tiles_to_rows(tmp, n)


def _rows_to_btiles(ref, val, n, tmp):
    _rows_to_tiles(tmp, val, n)
    ref[...] = tmp[...].astype(BF16)


def _two_part_specs(block_rows, n_a, n_b):
    return [pl.BlockSpec((block_rows, LANES), lambda i: (jnp.minimum(i, n_a - 1), 0)),
            pl.BlockSpec((block_rows, LANES), lambda i: (jnp.clip(i - n_a, 0, n_b - 1), 0))]


def _on_part(n_a, fn, *ref_pairs):
    i = pl.program_id(0)
    pl.when(i < n_a)(lambda: fn(*[p[0] for p in ref_pairs]))
    pl.when(i >= n_a)(lambda: fn(*[p[1] for p in ref_pairs]))


def _route_body(xa_ref, xb_ref, gm_ref, wr_hi_ref, wr_lo_ref, info_ref, cnt_ref, hp_ref, cnt_scr, tmp_scr,
                *, tm, n_a):
    @pl.when(pl.program_id(0) == 0)
    def _():
        cnt_scr[...] = jnp.zeros_like(cnt_scr)

    _on_part(n_a, functools.partial(_route_step, gm_ref, wr_hi_ref, wr_lo_ref, info_ref, cnt_ref, hp_ref, cnt_scr,
                                    tmp_scr, tm), (xa_ref, xb_ref))


def _route_step(gm_ref, wr_hi_ref, wr_lo_ref, info_ref, cnt_ref, hp_ref, cnt_scr, tmp_scr, tm, x2t_ref):
    hm = _rms(_tiles_to_rows(x2t_ref, tm), gm_ref[...])
    _rows_to_btiles(hp_ref, hm, tm, tmp_scr)
    i1, i2, w1, w2 = _router(hm, wr_hi_ref, wr_lo_ref)
    lane_i = lax.broadcasted_iota(jnp.int32, (tm, ROUTER_LANES), 1)
    lane = lane_i.astype(F32)
    chosen = ((lane == i1) | (lane == i2)).astype(F32)
    earlier = (lax.broadcasted_iota(jnp.int32, (tm, tm), 1) < lax.broadcasted_iota(jnp.int32, (tm, tm), 0))
    before = _bdot(earlier.astype(BF16), chosen.astype(BF16)) + cnt_scr[0:1, :]
    r1 = jnp.sum(jnp.where(lane == i1, before, 0.0), axis=-1, keepdims=True)
    r2 = jnp.sum(jnp.where(lane == i2, before, 0.0), axis=-1, keepdims=True)
    cols = (i1, i2, w1, w2, r1, r2)
    info = jnp.zeros((tm, ROUTER_LANES), F32)
    for j, col in enumerate(cols):
        info = jnp.where(lane_i == j, col, info)
    info_ref[...] = info
    cnt_scr[...] = cnt_scr[...] + jnp.sum(chosen, axis=0, keepdims=True)
    cnt_ref[...] = cnt_scr[...]


def _route(xa, xb, g_moe, wr_hi, wr_lo, tm=512):
    n_a, n_b = xa.shape[0] // (tm * TILE_ROWS), xb.shape[0] // (tm * TILE_ROWS)
    t = (n_a + n_b) * tm

    def const(shape):
        return pl.BlockSpec(shape, lambda i: (0,) * len(shape))

    body = functools.partial(_route_body, tm=tm, n_a=n_a)
    return pl.pallas_call(
        body,
        grid=(n_a + n_b,),
        in_specs=_two_part_specs(tm * TILE_ROWS, n_a, n_b) + [
            const((1, D_MODEL)), const((D_MODEL, ROUTER_LANES)), const((D_MODEL, ROUTER_LANES))],
        out_specs=[pl.BlockSpec((tm, ROUTER_LANES), lambda i: (i, 0)), const((SUBLANES, ROUTER_LANES)),
                   pl.BlockSpec((tm * TILE_ROWS, LANES), lambda i: (i, 0))],
        out_shape=[jax.ShapeDtypeStruct((t, ROUTER_LANES), F32),
                   jax.ShapeDtypeStruct((SUBLANES, ROUTER_LANES), F32),
                   jax.ShapeDtypeStruct((t * TILE_ROWS, LANES), BF16)],
        scratch_shapes=[pltpu.VMEM((SUBLANES, ROUTER_LANES), F32), pltpu.VMEM((tm * TILE_ROWS, LANES), F32)],
        compiler_params=_params("arbitrary"),
        name="route",
    )(xa, xb, g_moe, wr_hi, wr_lo)


def _route_meta(info, cnt, tme):
    t = info.shape[0]
    n_tiles = 2 * t // tme
    counts = cnt[0, N_EXPERT_GROUPS:N_EXPERT_GROUPS + N_EXPERTS].astype(jnp.int32)
    ends = jnp.cumsum(counts)
    starts = ends - counts
    e = info[:, 0:2].astype(jnp.int32) - N_EXPERT_GROUPS
    onehot = e[:, :, None] == jnp.arange(N_EXPERTS, dtype=jnp.int32)
    slot = (jnp.sum(jnp.where(onehot, starts, 0), axis=-1) + info[:, 4:6].astype(jnp.int32)).reshape(-1)
    pts = jnp.concatenate([jnp.arange(n_tiles, dtype=jnp.int32) * tme, starts])
    n = pts.shape[0]
    idx = jnp.arange(n, dtype=jnp.int32)
    pos = jnp.sum((pts[None, :] < pts[:, None]) | ((pts[None, :] == pts[:, None]) & (idx[None, :] < idx[:, None])),
                  axis=1)
    lo_abs = jnp.sum(jnp.where(pos[:, None] == idx[None, :], pts[:, None], 0), axis=0)
    hi_abs = jnp.concatenate([lo_abs[1:], jnp.full((1,), 2 * t, jnp.int32)])
    tile = jnp.minimum(lo_abs // tme, n_tiles - 1)
    expert = jnp.clip(jnp.sum(starts[None, :] <= lo_abs[:, None], axis=1) - 1, 0, N_EXPERTS - 1)
    first = jnp.concatenate([jnp.ones((1,), jnp.int32), (tile[1:] != tile[:-1]).astype(jnp.int32)])
    last = jnp.concatenate([(tile[1:] != tile[:-1]).astype(jnp.int32), jnp.ones((1,), jnp.int32)])
    work = [a.astype(jnp.int32) for a in (tile, expert, lo_abs - tile * tme, hi_abs - tile * tme, first, last)]
    return slot.astype(jnp.int32), work


def _tile_copy(src, src_tok, dst, dst_tok, sem):
    return pltpu.make_async_copy(src.at[pl.ds(pl.multiple_of(src_tok * TILE_ROWS, TILE_ROWS), TILE_ROWS), :],
                                 dst.at[pl.ds(pl.multiple_of(dst_tok * TILE_ROWS, TILE_ROWS), TILE_ROWS), :], sem)


def _scatter_body(slot_ref, hp_ref, xs_hbm, sem, *, tm):
    def issue(r, c):
        _tile_copy(hp_ref, r, xs_hbm, slot_ref[2 * r], sem).start(priority=0)
        _tile_copy(hp_ref, r, xs_hbm, slot_ref[2 * r + 1], sem).start(priority=1)
        return c

    lax.fori_loop(0, tm, issue, 0, unroll=8)

    def drain(r, c):
        _tile_copy(hp_ref, r, xs_hbm, 0, sem).wait()
        _tile_copy(hp_ref, r, xs_hbm, 0, sem).wait()
        return c

    lax.fori_loop(0, tm, drain, 0, unroll=8)


def _scatter(slot, hp, tm=512):
    t = hp.shape[0] // TILE_ROWS
    body = functools.partial(_scatter_body, tm=tm)
    return pl.pallas_call(
        body,
        grid=(t // tm,),
        in_specs=[pl.BlockSpec((2 * tm,), lambda i: (i,), memory_space=pltpu.SMEM),
                  pl.BlockSpec((tm * TILE_ROWS, LANES), lambda i: (i, 0))],
        out_specs=pl.BlockSpec(memory_space=pl.ANY),
        out_shape=jax.ShapeDtypeStruct((2 * t * TILE_ROWS, LANES), BF16),
        scratch_shapes=[pltpu.SemaphoreType.DMA(())],
        compiler_params=_params("arbitrary"),
        name="scatter",
    )(slot, hp)


def _expert_body(tile_ref, exp_ref, lo_ref, hi_ref, first_ref, last_ref, xs_ref, wg_ref, wu_ref, wd_ref,
                 ys_ref, hb_scr, acc_scr, tmp_scr, *, tme):
    k = pl.program_id(0)
    first = first_ref[k] == 1
    last = last_ref[k] == 1
    nonempty = hi_ref[k] > lo_ref[k]

    @pl.when(first)
    def _():
        hb_scr[...] = _btiles_to_rows(xs_ref, tme, tmp_scr).astype(BF16)

    @pl.when(first & jnp.logical_not(nonempty))
    def _():
        acc_scr[...] = jnp.zeros_like(acc_scr)

    @pl.when(nonempty)
    def _():
        hb = hb_scr[...]
        act = jax.nn.silu(_bdot(hb, wg_ref[0])) * _bdot(hb, wu_ref[0])
        row = lax.broadcasted_iota(jnp.int32, (tme, 1), 0)
        y = jnp.where((row >= lo_ref[k]) & (row < hi_ref[k]), _bdot(act.astype(BF16), wd_ref[0]), 0.0)

        @pl.when(first & last)
        def _():
            _rows_to_btiles(ys_ref, y, tme, tmp_scr)

        @pl.when(first & jnp.logical_not(last))
        def _():
            acc_scr[...] = y

        @pl.when(jnp.logical_not(first))
        def _():
            acc_scr[...] += y

    @pl.when(last & jnp.logical_not(first & nonempty))
    def _():
        _rows_to_btiles(ys_ref, acc_scr[...], tme, tmp_scr)


def _experts(work, xs, wg, wu, wd, tme):
    n_work = work[0].shape[0]
    body = functools.partial(_expert_body, tme=tme)
    tile_blk = pl.BlockSpec((tme * TILE_ROWS, LANES), lambda k, tile, *_: (tile[k], 0))

    def wspec(shape):
        return pl.BlockSpec(shape, lambda k, tile, exp, *_: (exp[k], 0, 0))

    grid_spec = pltpu.PrefetchScalarGridSpec(
        num_scalar_prefetch=len(work),
        grid=(n_work,),
        in_specs=[tile_blk, wspec((1, D_MODEL, D_FF_EXPERT)), wspec((1, D_MODEL, D_FF_EXPERT)),
                  wspec((1, D_FF_EXPERT, D_MODEL))],
        out_specs=tile_blk,
        scratch_shapes=[pltpu.VMEM((tme, D_MODEL), BF16), pltpu.VMEM((tme, D_MODEL), F32),
                        pltpu.VMEM((tme * TILE_ROWS, LANES), F32)],
    )
    return pl.pallas_call(
        body,
        grid_spec=grid_spec,
        out_shape=jax.ShapeDtypeStruct(xs.shape, BF16),
        compiler_params=_params("arbitrary"),
        name="experts",
    )(*work, xs, wg, wu, wd)


def _combine_body(slot_ref, slot_next_ref, xa_ref, xb_ref, info_ref, ys_hbm, gf_ref, ya_ref, yb_ref, b0, b1, sem,
                  tmp_scr, *, tm, nstep, n_a):
    i = pl.program_id(0)
    cur = i % 2

    def gather(slots, buf):
        def issue(r, c):
            _tile_copy(ys_hbm, slots[2 * r], b0.at[buf], r, sem.at[buf]).start(priority=0)
            _tile_copy(ys_hbm, slots[2 * r + 1], b1.at[buf], r, sem.at[buf]).start(priority=1)
            return c

        lax.fori_loop(0, tm, issue, 0, unroll=8)

    @pl.when(i == 0)
    def _():
        gather(slot_ref, 0)

    @pl.when(i + 1 < nstep)
    def _():
        gather(slot_next_ref, 1 - cur)

    def drain(r, c):
        _tile_copy(ys_hbm, 0, b0.at[cur], r, sem.at[cur]).wait()
        _tile_copy(ys_hbm, 0, b1.at[cur], r, sem.at[cur]).wait()
        return c

    lax.fori_loop(0, tm, drain, 0, unroll=8)
    info = info_ref[...]
    moe = info[:, 2:3] * _btiles_to_rows(b0.at[cur], tm, tmp_scr)
    moe = moe + info[:, 3:4] * _btiles_to_rows(b1.at[cur], tm, tmp_scr)

    def finish(x2t_ref, y_ref):
        y_ref[...] = _rms(_tiles_to_rows(x2t_ref, tm) + moe, gf_ref[...])

    _on_part(n_a, finish, (xa_ref, xb_ref), (ya_ref, yb_ref))


def _combine(slot, xa, xb, info, ys, g_final, tm=256):
    n_a, n_b = xa.shape[0] // (tm * TILE_ROWS), xb.shape[0] // (tm * TILE_ROWS)
    nstep = n_a + n_b
    body = functools.partial(_combine_body, tm=tm, nstep=nstep, n_a=n_a)
    buf = pltpu.VMEM((2, tm * TILE_ROWS, LANES), BF16)
    return pl.pallas_call(
        body,
        grid=(nstep,),
        in_specs=[pl.BlockSpec((2 * tm,), lambda i: (i,), memory_space=pltpu.SMEM),
                  pl.BlockSpec((2 * tm,), lambda i: (jnp.minimum(i + 1, nstep - 1),), memory_space=pltpu.SMEM)]
        + _two_part_specs(tm * TILE_ROWS, n_a, n_b) + [
                  pl.BlockSpec((tm, ROUTER_LANES), lambda i: (i, 0)),
                  pl.BlockSpec(memory_space=pl.ANY),
                  pl.BlockSpec((1, D_MODEL), lambda i: (0, 0))],
        out_specs=[pl.BlockSpec((tm, D_MODEL), lambda i: (jnp.minimum(i, n_a - 1), 0)),
                   pl.BlockSpec((tm, D_MODEL), lambda i: (jnp.clip(i - n_a, 0, n_b - 1), 0))],
        out_shape=[jax.ShapeDtypeStruct((n_a * tm, D_MODEL), F32), jax.ShapeDtypeStruct((n_b * tm, D_MODEL), F32)],
        scratch_shapes=[buf, buf, pltpu.SemaphoreType.DMA((2,)), pltpu.VMEM((tm * TILE_ROWS, LANES), F32)],
        compiler_params=_params("arbitrary"),
        name="combine",
    )(slot, slot, xa, xb, info, ys, g_final)


def _moe(xa, xb, w, tme=512):
    info, cnt, hp = _route(xa, xb, w["g_moe"], w["wr_hi"], w["wr_lo"])
    slot, work = _route_meta(info, cnt, tme)
    xs = _scatter(slot, hp)
    ys = _experts(work, xs, w["moe_wg"], w["moe_wu"], w["moe_wd"], tme)
    return _combine(slot, xa, xb, info, ys, w["g_final"])


def _layer(x3d, k4, v4, s5_h0, lru_h0, lru_conv, w, chunk, nb, tq):
    bsz, seq, _ = x3d.shape
    x2d = x3d.reshape(bsz * seq, D_MODEL)
    u, xl, gl = _mixin(x2d, w["g_mix"], w["w_in"])

    nstate = S5_GROUPS * S5_STATE
    if s5_h0 is None:
        zero = jnp.zeros((bsz, 1, nstate), F32)
        ys, s5_re, s5_im = _s5(u, w["s5"][chunk], zero, zero, chunk, seq // chunk, True)
    else:
        ys, s5_re, s5_im = _s5(u, w["s5"][chunk], s5_h0[0].reshape(bsz, nstate), s5_h0[1].reshape(bsz, nstate),
                               chunk, bsz, False)
    s5_re = s5_re.reshape(bsz, S5_GROUPS, S5_STATE)
    s5_im = s5_im.reshape(bsz, S5_GROUPS, S5_STATE)

    lru_w = (w["conv_w"], w["conv_b"], w["lru_wg"], w["lru_bg"], w["lru_lam"])
    if lru_h0 is None:
        yl, hl = _lru_seq(xl, gl, *lru_w, bsz, seq)
        lru_h = hl.reshape(bsz, SUBLANES, D_LRU)[:, 0]
    else:
        hist = jnp.pad(lru_conv, ((0, 0), (SUBLANES - (CONV_W - 1), 0), (0, 0))).reshape(bsz * seq, D_LRU)
        h0rep = jnp.repeat(lru_h0, seq, axis=0)
        yl, hall = _lru_dec(xl, gl, hist, h0rep, *lru_w)
        lru_h = hall.reshape(bsz, seq, D_LRU)[:, seq - 1]
    conv_new = xl.reshape(bsz, seq, D_LRU)[:, seq - (CONV_W - 1):]

    x1, q = _mixout(x2d, ys, yl, w["w_glu"], w["b_glu"], w["w_out"], w["g_xa"], w["w_q"])
    x2t = _attn(q.reshape(bsz, seq, D_MODEL), x1.reshape(bsz, seq, D_MODEL), k4, v4, w["w_o"], nb, tq)
    return x2t, s5_re, s5_im, lru_h, conv_new


def kernel(x_prompt, x_sample, mem_prompt, cache_mem_k, cache_mem_v, state_s5_re, state_s5_im, state_lru_h, state_lru_conv, g_mix, w_in, s5_lam_re, s5_lam_im, s5_log_dt, s5_b_re, s5_b_im, s5_c_re, s5_c_im, s5_d, s5_w_glu, s5_b_glu, lru_conv_w, lru_conv_b, lru_w_a, lru_b_a, lru_w_x, lru_b_x, lru_lam, w_out, g_xa, g_mem, xa_w_q, xa_w_k, xa_w_v, xa_w_o, g_moe, moe_w_group, moe_w_expert, moe_w_gate, moe_w_up, moe_w_down, g_final):
    depth = g_mix.shape[0]
    assert depth == 1, "single-layer step"
    l = 0
    bsz, seq, _ = x_prompt.shape
    dbsz, dseq, _ = x_sample.shape
    chunk = SUBLANES
    assert seq % chunk == 0 and dseq == chunk

    s5_args = (s5_lam_re[l], s5_lam_im[l], s5_log_dt[l], s5_b_re[l], s5_b_im[l], s5_c_re[l], s5_c_im[l], s5_d[l])
    wg, bg = _lru_weights(lru_w_a[l], lru_w_x[l], lru_b_a[l], lru_b_x[l])
    wr = jnp.concatenate([moe_w_group[l], moe_w_expert[l],
                          jnp.zeros((D_MODEL, ROUTER_LANES - N_EXPERT_GROUPS - N_EXPERTS), F32)], axis=1)
    wr_hi = wr.astype(BF16)
    wr_lo = (wr - wr_hi.astype(F32)).astype(BF16)
    w = {
        "g_mix": g_mix[l][None], "w_in": w_in[l].astype(BF16),
        "s5": {chunk: _s5_weights(chunk, *s5_args)},
        "conv_w": lru_conv_w[l], "conv_b": lru_conv_b[l][None], "lru_wg": wg, "lru_bg": bg,
        "lru_lam": lru_lam[l][None],
        "w_glu": s5_w_glu[l].astype(BF16), "b_glu": s5_b_glu[l][None], "w_out": w_out[l].astype(BF16),
        "g_xa": g_xa[l][None], "w_q": xa_w_q[l].astype(BF16), "w_o": xa_w_o[l].astype(BF16),
        "g_moe": g_moe[l][None], "wr_hi": wr_hi, "wr_lo": wr_lo,
        "moe_wg": moe_w_gate[l].astype(BF16), "moe_wu": moe_w_up[l].astype(BF16),
        "moe_wd": moe_w_down[l].astype(BF16), "g_final": g_final[None],
    }

    mk, mv = _memkv(mem_prompt.reshape(bsz * N_MEM, D_MODEL), g_mem[l][None],
                    xa_w_k[l].astype(BF16), xa_w_v[l].astype(BF16))
    xp, p_re, p_im, p_h, p_conv = _layer(x_prompt, mk.reshape(bsz, N_MEM, D_MODEL), mv.reshape(bsz, N_MEM, D_MODEL),
                                         None, None, None, w, chunk, nb=1, tq=1024)
    xs, s_re, s_im, s_h, s_conv = _layer(x_sample, cache_mem_k[l], cache_mem_v[l],
                                         (state_s5_re[l], state_s5_im[l]), state_lru_h[l],
                                         state_lru_conv[l], w, chunk, nb=4, tq=dseq)
    yp, ysmp = _moe(xp, xs, w)
    yp = yp.reshape(bsz, seq, D_MODEL)
    ysmp = ysmp.reshape(dbsz, dseq, D_MODEL)

    return (yp, ysmp,
            mk.reshape(1, bsz, N_MEM, XA_HEADS, XA_HEAD_DIM), mv.reshape(1, bsz, N_MEM, XA_HEADS, XA_HEAD_DIM),
            p_re[None], p_im[None], p_h[None], p_conv[None],
            s_re[None], s_im[None], s_h[None], s_conv[None])
```

```python
import functools

import jax
import jax.numpy as jnp
import numpy as np
from jax import lax
from jax.experimental import pallas as pl
from jax.experimental.pallas import tpu as pltpu

F32 = jnp.float32
BF16 = jnp.bfloat16

D_MODEL = 1024
D_S5 = 512
D_LRU = 512
S5_GROUP = 16
S5_GROUPS = 32
S5_STATE = 64
LRU_HEADS = 8
LRU_HEAD_DIM = 64
CONV_W = 4
LRU_C = 8.0
N_MEM = 256
XA_HEADS = 4
XA_HEAD_DIM = 256
N_EXPERT_GROUPS = 4
EXPERTS_PER_GROUP = 8
N_EXPERTS = 32
D_FF_EXPERT = 256
EPS = 1e-6

SUBLANES = 8
LANES = 128
ROUTER_LANES = 128
VMEM_LIMIT = 56 * 1024 * 1024


def _params(*sem):
    return pltpu.CompilerParams(dimension_semantics=sem, vmem_limit_bytes=VMEM_LIMIT)


def _rms(x, g):
    ms = jnp.mean(x * x, axis=-1, keepdims=True)
    return x * lax.rsqrt(ms + EPS) * g


def _bdot(a, b):
    return jnp.dot(a, b, preferred_element_type=F32)


def _memkv_body(m_ref, g_ref, wk_ref, wv_ref, k_ref, v_ref):
    hb = _rms(m_ref[...], g_ref[...]).astype(BF16)
    k_ref[...] = _bdot(hb, wk_ref[...])
    v_ref[...] = _bdot(hb, wv_ref[...])


def _memkv(mem2d, g_mem, wk, wv, tm=512):
    t = mem2d.shape[0]
    row = pl.BlockSpec((tm, D_MODEL), lambda i: (i, 0))
    full = pl.BlockSpec((D_MODEL, D_MODEL), lambda i: (0, 0))
    return pl.pallas_call(
        _memkv_body,
        grid=(t // tm,),
        in_specs=[row, pl.BlockSpec((1, D_MODEL), lambda i: (0, 0)), full, full],
        out_specs=[row, row],
        out_shape=[jax.ShapeDtypeStruct((t, D_MODEL), F32)] * 2,
        compiler_params=_params("parallel"),
        name="memkv",
    )(mem2d, g_mem, wk, wv)


def _mixin_body(x_ref, g_ref, w_ref, u_ref, xl_ref, gl_ref):
    hb = _rms(x_ref[...], g_ref[...]).astype(BF16)
    proj = _bdot(hb, w_ref[...])
    u_ref[...] = proj[:, :D_S5]
    xl_ref[...] = proj[:, D_S5:D_S5 + D_LRU]
    gl_ref[...] = proj[:, D_S5 + D_LRU:]


def _mixin(x2d, g_mix, w_in, tm=1024):
    t = x2d.shape[0]
    half = pl.BlockSpec((tm, D_S5), lambda i: (i, 0))
    return pl.pallas_call(
        _mixin_body,
        grid=(t // tm,),
        in_specs=[pl.BlockSpec((tm, D_MODEL), lambda i: (i, 0)),
                  pl.BlockSpec((1, D_MODEL), lambda i: (0, 0)),
                  pl.BlockSpec((D_MODEL, D_S5 + 2 * D_LRU), lambda i: (0, 0))],
        out_specs=[half, half, half],
        out_shape=[jax.ShapeDtypeStruct((t, D_S5), F32)] * 3,
        compiler_params=_params("parallel"),
        name="mixin",
    )(x2d, g_mix, w_in)


GROUPS_PER_BLOCK = LANES // S5_GROUP
PAIRS_PER_BLOCK = GROUPS_PER_BLOCK // 2
STATE_BLOCK = GROUPS_PER_BLOCK * S5_STATE
PW_ROWS = 24


def _s5_body(u_ref, wa_ref, wh_ref, pw_ref, d_ref, h0r_ref, h0i_ref, y_ref, hfr_ref, hfi_ref, hpr_scr, hpi_scr,
             *, chunk, nrow, scan):
    nh = chunk // SUBLANES
    lc = chunk * S5_GROUP
    slot = lax.broadcasted_iota(jnp.int32, (nrow, LANES), 1) // S5_GROUP
    in_slot = [slot == s for s in range(GROUPS_PER_BLOCK)]

    def pick(src, sel):
        out = src[sel(0)]
        for s in range(1, GROUPS_PER_BLOCK):
            out = jnp.where(in_slot[s], src[sel(s)], out)
        return out

    nat, skew = [], []
    for t in range(chunk):
        a = u_ref[pl.ds(t, nrow, stride=chunk), :]
        nat.append(a)
        s = t % GROUPS_PER_BLOCK
        skew.append(pltpu.roll(a, s * S5_GROUP, axis=1) if s else a)
    def lane_roll(x, slots):
        slots %= GROUPS_PER_BLOCK
        return pltpu.roll(x, slots * S5_GROUP, axis=1) if slots else x

    z = [[lane_roll(pick(skew, lambda s, hh=hh, m=m: SUBLANES * hh + (s - m) % GROUPS_PER_BLOCK), -m)
          for hh in range(nh)] for m in range(GROUPS_PER_BLOCK)]

    ys, er, ei = [], [], []
    for p in range(PAIRS_PER_BLOCK):
        lhs = jnp.concatenate(z[2 * p] + z[2 * p + 1], axis=1).astype(BF16)
        full = _bdot(lhs, wa_ref[p])
        ys.append(full[:, :2 * lc])
        er.append(full[:, 2 * lc:2 * lc + LANES])
        ei.append(full[:, 2 * lc + LANES:])
    er = jnp.concatenate(er, axis=1)
    ei = jnp.concatenate(ei, axis=1)

    if scan:
        ntile = nrow // SUBLANES
        xr = er.reshape(ntile, SUBLANES, STATE_BLOCK)
        xi = ei.reshape(ntile, SUBLANES, STATE_BLOCK)
        row = lax.broadcasted_iota(jnp.int32, xr.shape, 1)
        for k, d in enumerate((1, 2, 4)):
            pr, pi = pw_ref[0, k:k + 1, :], pw_ref[0, 4 + k:5 + k, :]
            sr, si = pltpu.roll(xr, d, axis=1), pltpu.roll(xi, d, axis=1)
            m = row >= d
            xr, xi = jnp.where(m, xr + pr * sr - pi * si, xr), jnp.where(m, xi + pr * si + pi * sr, xi)
        sxr = jnp.where(row >= 1, pltpu.roll(xr, 1, axis=1), 0.0)
        sxi = jnp.where(row >= 1, pltpu.roll(xi, 1, axis=1), 0.0)
        p8r, p8i = pw_ref[0, 3:4, :], pw_ref[0, 7:8, :]
        qr, qi = pw_ref[0, 8:16, :], pw_ref[0, 16:24, :]
        hr, hi = h0r_ref[0], h0i_ref[0]
        for t in range(ntile):
            rows = slice(SUBLANES * t, SUBLANES * (t + 1))
            hpr_scr[rows, :] = sxr[t] + qr * hr - qi * hi
            hpi_scr[rows, :] = sxi[t] + qr * hi + qi * hr
            hr, hi = (xr[t, SUBLANES - 1:] + p8r * hr - p8i * hi, xi[t, SUBLANES - 1:] + p8r * hi + p8i * hr)
        hfr_ref[0] = hr
        hfi_ref[0] = hi
        hpr, hpi = hpr_scr[...], hpi_scr[...]
    else:
        hpr, hpi = h0r_ref[...], h0i_ref[...]
        p1r, p1i = pw_ref[0, 0:1, :], pw_ref[0, 4:5, :]
        hfr_ref[...] = er + p1r * hpr - p1i * hpi
        hfi_ref[...] = ei + p1r * hpi + p1i * hpr

    yg = []
    for p in range(PAIRS_PER_BLOCK):
        lanes = slice(LANES * p, LANES * (p + 1))
        hp = jnp.concatenate([hpr[:, lanes], hpi[:, lanes]], axis=1).astype(BF16)
        out = ys[p] + _bdot(hp, wh_ref[p])
        for half in range(2):
            q = 2 * p + half
            yg.append([lane_roll(out[:, half * lc + hh * LANES:half * lc + (hh + 1) * LANES], q)
                       for hh in range(nh)])
    d = d_ref[...]
    for t in range(chunk):
        hh, s = divmod(t, GROUPS_PER_BLOCK)
        c = lane_roll(pick([g[hh] for g in yg], lambda sl, s=s: (sl - s) % GROUPS_PER_BLOCK), -s)
        y_ref[pl.ds(t, nrow, stride=chunk), :] = c + d * nat[t]


def _s5(u2d, weights, h0r, h0i, chunk, nrow, scan):
    wa, wh, pw, dvec = weights
    t = u2d.shape[0]
    lc = chunk * S5_GROUP
    rows = nrow * chunk
    nblk = D_S5 // LANES
    body = functools.partial(_s5_body, chunk=chunk, nrow=nrow, scan=scan)
    ublk = pl.BlockSpec((rows, LANES), lambda i, j: (i, j))
    if scan:
        hblk = pl.BlockSpec((1, 1, STATE_BLOCK), lambda i, j: (i, 0, j))
        hshape = jax.ShapeDtypeStruct((t // rows, 1, S5_GROUPS * S5_STATE), F32)
    else:
        hblk = pl.BlockSpec((nrow, STATE_BLOCK), lambda i, j: (i, j))
        hshape = jax.ShapeDtypeStruct((t // chunk, S5_GROUPS * S5_STATE), F32)
    return pl.pallas_call(
        body,
        grid=(t // rows, nblk),
        in_specs=[ublk,
                  pl.BlockSpec((PAIRS_PER_BLOCK, 2 * lc, 2 * lc + 2 * LANES), lambda i, j: (j, 0, 0)),
                  pl.BlockSpec((PAIRS_PER_BLOCK, 2 * LANES, 2 * lc), lambda i, j: (j, 0, 0)),
                  pl.BlockSpec((1, PW_ROWS, STATE_BLOCK), lambda i, j: (j, 0, 0)),
                  pl.BlockSpec((1, LANES), lambda i, j: (0, j)),
                  hblk, hblk],
        out_specs=[ublk, hblk, hblk],
        out_shape=[jax.ShapeDtypeStruct((t, D_S5), F32), hshape, hshape],
        scratch_shapes=[pltpu.VMEM((nrow, STATE_BLOCK), F32), pltpu.VMEM((nrow, STATE_BLOCK), F32)],
        compiler_params=_params("parallel", "parallel"),
        name="s5",
    )(u2d, wa, wh, pw, dvec, h0r, h0i)


def _s5_weights(chunk, lam_re, lam_im, log_dt, b_re, b_im, c_re, c_im, d_skip):
    hi = lax.Precision.HIGHEST
    dt = jnp.exp(log_dt)[:, None]
    mag = jnp.exp(lam_re * dt)
    ab_re = mag * jnp.cos(lam_im * dt)
    ab_im = mag * jnp.sin(lam_im * dt)
    den = lam_re * lam_re + lam_im * lam_im
    nr = ab_re - 1.0
    f_re = (nr * lam_re + ab_im * lam_im) / den
    f_im = (ab_im * lam_re - nr * lam_im) / den
    bb_re = f_re[..., None] * b_re - f_im[..., None] * b_im
    bb_im = f_re[..., None] * b_im + f_im[..., None] * b_re
    def powers(ks):
        k = jnp.asarray(np.asarray(ks, np.float32))[:, None, None]
        m = jnp.exp(k * (lam_re * dt))
        return m * jnp.cos(k * (lam_im * dt)), m * jnp.sin(k * (lam_im * dt))

    pw_re, pw_im = powers(range(chunk + 1))
    pb_re = pw_re[:chunk, ..., None] * bb_re - pw_im[:chunk, ..., None] * bb_im
    pb_im = pw_re[:chunk, ..., None] * bb_im + pw_im[:chunk, ..., None] * bb_re
    kk = (jnp.einsum("gon,kgni->kgoi", c_re, pb_re, precision=hi)
          - jnp.einsum("gon,kgni->kgoi", c_im, pb_im, precision=hi))
    npair = S5_GROUPS // 2
    lc = chunk * S5_GROUP
    row0 = kk.transpose(1, 3, 0, 2).reshape(S5_GROUPS, S5_GROUP, lc)
    row0 = jnp.pad(row0, ((0, 0), (0, 0), (lc, 0)))
    conv = jnp.stack([row0[:, :, lc - S5_GROUP * ti:2 * lc - S5_GROUP * ti] for ti in range(chunk)], axis=1)
    conv = conv.reshape(npair, 2, lc, lc)
    ends = [x[::-1].transpose(1, 0, 3, 2).reshape(npair, 2, lc, S5_STATE) for x in (pb_re, pb_im)]
    zc = jnp.zeros((npair, lc, lc), F32)
    zs = jnp.zeros((npair, lc, S5_STATE), F32)
    wa = jnp.concatenate([
        jnp.concatenate([conv[:, 0], zc, ends[0][:, 0], zs, ends[1][:, 0], zs], axis=-1),
        jnp.concatenate([zc, conv[:, 1], zs, ends[0][:, 1], zs, ends[1][:, 1]], axis=-1)], axis=1)
    gr = c_re[None] * pw_re[1:, :, None, :] - c_im[None] * pw_im[1:, :, None, :]
    gi = c_re[None] * pw_im[1:, :, None, :] + c_im[None] * pw_re[1:, :, None, :]
    zr = jnp.zeros((npair, S5_STATE, lc), F32)
    blocks = []
    for x in (gr, -gi):
        hy = x.transpose(1, 3, 0, 2).reshape(npair, 2, S5_STATE, lc)
        blocks += [jnp.concatenate([hy[:, 0], zr], axis=-1), jnp.concatenate([zr, hy[:, 1]], axis=-1)]
    wh = jnp.concatenate(blocks, axis=1)
    tab = [1, 2, 4, 8] + list(range(SUBLANES))
    qr, qi = powers([chunk * k for k in tab])
    nblk = D_S5 // LANES
    pw = jnp.concatenate([qr[:4], qi[:4], qr[4:], qi[4:]]).reshape(PW_ROWS, nblk, STATE_BLOCK).transpose(1, 0, 2)
    return wa.astype(BF16), wh.astype(BF16), pw, d_skip.reshape(1, D_S5)


def _lru_ab(xc, wg_ref, bg_ref, lam_ref):
    xb = xc.astype(BF16)
    half = D_LRU // 2
    g0 = _bdot(xb[:, :half], wg_ref[0]) + bg_ref[0]
    g1 = _bdot(xb[:, half:], wg_ref[1]) + bg_ref[1]
    r = jax.nn.sigmoid(jnp.concatenate([g0[:, :half], g1[:, :half]], axis=1))
    ig = jax.nn.sigmoid(jnp.concatenate([g0[:, half:], g1[:, half:]], axis=1))
    zl = -lam_ref[...]
    softplus = jnp.maximum(zl, 0.0) + jnp.log1p(jnp.exp(-jnp.abs(zl)))
    log_a = -LRU_C * r * softplus
    a = jnp.exp(log_a)
    b = jnp.sqrt(-jnp.tanh(log_a) * (a * a + 1.0)) * (ig * xc)
    return a, b


def _tile_scan(a, b):
    shape = a.shape
    a = a.reshape(shape[0] // SUBLANES, SUBLANES, shape[1])
    b = b.reshape(a.shape)
    row = lax.broadcasted_iota(jnp.int32, a.shape, 1)
    for d in (1, 2, 4):
        a_prev = pltpu.roll(a, d, axis=1)
        b_prev = pltpu.roll(b, d, axis=1)
        m = row >= d
        b = jnp.where(m, b + a * b_prev, b)
        a = jnp.where(m, a * a_prev, a)
    return a.reshape(shape), b.reshape(shape)


def _mixlru_body(x_ref, gmix_ref, win_ref, cw_ref, cb_ref, wg_ref, bg_ref, lam_ref, u_ref, y_ref, hl_ref, xt_ref,
                 xp_scr, a_scr, b_scr, h_scr, hc_scr, *, ts):
    ti = pl.program_id(1)

    @pl.when(ti == 0)
    def _():
        xp_scr[0:SUBLANES, :] = jnp.zeros((SUBLANES, D_LRU), F32)
        hc_scr[...] = jnp.zeros((SUBLANES, D_LRU), F32)

    proj = _bdot(_rms(x_ref[...], gmix_ref[...]).astype(BF16), win_ref[...])
    u_ref[...] = proj[:, :D_S5]
    xl = proj[:, D_S5:D_S5 + D_LRU]
    gl = proj[:, D_S5 + D_LRU:]
    xt_ref[...] = xl[ts - SUBLANES:, :]
    xp_scr[SUBLANES:SUBLANES + ts, :] = xl
    xc = cb_ref[...] + xl * cw_ref[CONV_W - 1:CONV_W, :]
    for j in range(1, CONV_W):
        xc = xc + xp_scr[SUBLANES - j:SUBLANES - j + ts, :] * cw_ref[CONV_W - 1 - j:CONV_W - j, :]
    xp_scr[0:SUBLANES, :] = xl[ts - SUBLANES:, :]
    a, b = _lru_ab(xc, wg_ref, bg_ref, lam_ref)
    a, b = _tile_scan(a, b)
    a_scr[...] = a
    b_scr[...] = b

    def step(i, hin):
        rows = pl.ds(pl.multiple_of(i * SUBLANES, SUBLANES), SUBLANES)
        h = b_scr[rows, :] + a_scr[rows, :] * hin
        h_scr[rows, :] = h
        return h[SUBLANES - 1:SUBLANES, :]

    hlast = lax.fori_loop(0, ts // SUBLANES, step, hc_scr[0:1, :], unroll=4)
    hc_scr[...] = jnp.broadcast_to(hlast, (SUBLANES, D_LRU))
    hl_ref[...] = hc_scr[...]
    y_ref[...] = h_scr[...] * jax.nn.gelu(gl)


def _lru_dec_body(xl_ref, gl_ref, hist_ref, h0_ref, cw_ref, cb_ref, wg_ref, bg_ref, lam_ref, y_ref, hl_ref, *, tm):
    xl = xl_ref[...]
    hist = hist_ref[...]
    row = lax.broadcasted_iota(jnp.int32, xl.shape, 0) % SUBLANES
    xc = cb_ref[...] + xl * cw_ref[CONV_W - 1:CONV_W, :]
    for j in range(1, CONV_W):
        prev = jnp.where(row >= j, pltpu.roll(xl, j, axis=0), pltpu.roll(hist, tm - SUBLANES + j, axis=0))
        xc = xc + prev * cw_ref[CONV_W - 1 - j:CONV_W - j, :]
    a, b = _lru_ab(xc, wg_ref, bg_ref, lam_ref)
    a, b = _tile_scan(a, b)
    h = b + a * h0_ref[...]
    hl_ref[...] = h
    y_ref[...] = h * jax.nn.gelu(gl_ref[...])


def _lru_weights(w_a, w_x, b_a, b_x):
    eye = jnp.eye(LRU_HEADS, dtype=F32)
    bd_a = jnp.einsum("hij,hg->higj", w_a, eye).reshape(D_LRU, D_LRU)
    bd_x = jnp.einsum("hij,hg->higj", w_x, eye).reshape(D_LRU, D_LRU)
    half = D_LRU // 2
    wg = jnp.stack([jnp.concatenate([bd_a[:half, :half], bd_x[:half, :half]], axis=1),
                    jnp.concatenate([bd_a[half:, half:], bd_x[half:, half:]], axis=1)]).astype(BF16)
    ba = b_a.reshape(1, D_LRU)
    bx = b_x.reshape(1, D_LRU)
    bg = jnp.stack([jnp.concatenate([ba[:, :half], bx[:, :half]], axis=1),
                    jnp.concatenate([ba[:, half:], bx[:, half:]], axis=1)])
    return wg, bg


def _lru_common_specs(const):
    return [const((CONV_W, D_LRU)), const((1, D_LRU)), const((2, D_LRU // 2, D_LRU)),
            const((2, 1, D_LRU)), const((1, D_LRU))]


def _mixlru(x2d, g_mix, w_in, cw, cb, wg, bg, lam, bsz, seq, ts=512):
    nt = seq // ts
    blk = pl.BlockSpec((ts, D_LRU), lambda b, t: (b * nt + t, 0))
    per_seq = pl.BlockSpec((SUBLANES, D_LRU), lambda b, t: (b, 0))

    def const(shape):
        return pl.BlockSpec(shape, lambda b, t: (0,) * len(shape))

    body = functools.partial(_mixlru_body, ts=ts)
    tile = jax.ShapeDtypeStruct((bsz * SUBLANES, D_LRU), F32)
    return pl.pallas_call(
        body,
        grid=(bsz, nt),
        in_specs=[pl.BlockSpec((ts, D_MODEL), lambda b, t: (b * nt + t, 0)), const((1, D_MODEL)),
                  const((D_MODEL, D_S5 + 2 * D_LRU))] + _lru_common_specs(const),
        out_specs=[blk, blk, per_seq, per_seq],
        out_shape=[jax.ShapeDtypeStruct((bsz * seq, D_S5), F32), jax.ShapeDtypeStruct((bsz * seq, D_LRU), F32),
                   tile, tile],
        scratch_shapes=[pltpu.VMEM((ts + SUBLANES, D_LRU), F32), pltpu.VMEM((ts, D_LRU), F32),
                        pltpu.VMEM((ts, D_LRU), F32), pltpu.VMEM((ts, D_LRU), F32),
                        pltpu.VMEM((SUBLANES, D_LRU), F32)],
        compiler_params=_params("arbitrary", "arbitrary"),
        name="mixlru",
    )(x2d, g_mix, w_in, cw, cb, wg, bg, lam)


def _lru_dec(xl, gl, hist, h0rep, cw, cb, wg, bg, lam, tm=256):
    t = xl.shape[0]
    blk = pl.BlockSpec((tm, D_LRU), lambda i: (i, 0))

    def const(shape):
        return pl.BlockSpec(shape, lambda i: (0,) * len(shape))

    body = functools.partial(_lru_dec_body, tm=tm)
    return pl.pallas_call(
        body,
        grid=(t // tm,),
        in_specs=[blk, blk, blk, blk] + _lru_common_specs(const),
        out_specs=[blk, blk],
        out_shape=[jax.ShapeDtypeStruct((t, D_LRU), F32)] * 2,
        compiler_params=_params("parallel"),
        name="lru_dec",
    )(xl, gl, hist, h0rep, cw, cb, wg, bg, lam)


def _mixout_body(x_ref, ys_ref, yl_ref, wglu_ref, bglu_ref, wo_ref, gxa_ref, wq_ref, x1_ref, q_ref):
    ys = jax.nn.gelu(ys_ref[...])
    gate = jax.nn.sigmoid(_bdot(ys.astype(BF16), wglu_ref[...]) + bglu_ref[...])
    s5 = (ys * gate).astype(BF16)
    x1 = (x_ref[...] + _bdot(s5, wo_ref[0:D_S5, :]) + _bdot(yl_ref[...].astype(BF16), wo_ref[D_S5:, :]))
    x1_ref[...] = x1
    q_ref[...] = _bdot(_rms(x1, gxa_ref[...]).astype(BF16), wq_ref[...]).astype(BF16)


def _mixout(x2d, ys, yl, w_glu, b_glu, w_out, g_xa, w_q, tm=1024):
    t = x2d.shape[0]
    row = pl.BlockSpec((tm, D_MODEL), lambda i: (i, 0))
    half = pl.BlockSpec((tm, D_S5), lambda i: (i, 0))

    def const(shape):
        return pl.BlockSpec(shape, lambda i: (0,) * len(shape))

    return pl.pallas_call(
        _mixout_body,
        grid=(t // tm,),
        in_specs=[row, half, half, const((D_S5, D_S5)), const((1, D_S5)), const((D_MODEL, D_MODEL)),
                  const((1, D_MODEL)), const((D_MODEL, D_MODEL))],
        out_specs=[row, row],
        out_shape=[jax.ShapeDtypeStruct((t, D_MODEL), F32), jax.ShapeDtypeStruct((t, D_MODEL), BF16)],
        compiler_params=_params("parallel"),
        name="mixout",
    )(x2d, ys, yl, w_glu, b_glu, w_out, g_xa, w_q)


def _softmax(sc):
    p = jnp.exp(sc - jnp.max(sc, axis=-1, keepdims=True))
    return p / jnp.sum(p, axis=-1, keepdims=True)


def _attn_body(q_ref, x1_ref, k_ref, v_ref, wo_ref, x2t_ref, *, nb, tq):
    scale = XA_HEAD_DIM ** -0.5
    for i in range(nb):
        q = q_ref[i]
        heads = [q[:, h * XA_HEAD_DIM:(h + 1) * XA_HEAD_DIM] for h in range(XA_HEADS)]
        if len(k_ref.shape) == 4:
            kf = k_ref[i].reshape(N_MEM * XA_HEADS, XA_HEAD_DIM).astype(BF16)
            vf = v_ref[i].reshape(N_MEM * XA_HEADS, XA_HEAD_DIM).astype(BF16)
            sc = lax.dot_general(jnp.concatenate(heads, axis=0), kf, (((1,), (1,)), ((), ())),
                                 preferred_element_type=F32) * scale
            own = (lax.broadcasted_iota(jnp.int32, sc.shape, 1) % XA_HEADS
                   == lax.broadcasted_iota(jnp.int32, sc.shape, 0) // tq)
            oh = _bdot(_softmax(jnp.where(own, sc, -jnp.inf)).astype(BF16), vf)
            outs = [oh[h * tq:(h + 1) * tq] for h in range(XA_HEADS)]
        else:
            kb = k_ref[i].astype(BF16)
            vb = v_ref[i].astype(BF16)
            outs = []
            for h in range(XA_HEADS):
                cols = slice(h * XA_HEAD_DIM, (h + 1) * XA_HEAD_DIM)
                sc = lax.dot_general(heads[h], kb[:, cols], (((1,), (1,)), ((), ())),
                                     preferred_element_type=F32) * scale
                outs.append(_bdot(_softmax(sc).astype(BF16), vb[:, cols]))
        o = jnp.concatenate(outs, axis=1).astype(BF16)
        _rows_to_tiles(x2t_ref, x1_ref[i] + _bdot(o, wo_ref[...]), tq, base=i * tq * TILE_ROWS)


def _attn(q3, x13, k4, v4, w_o, nb, tq):
    bsz, seq, _ = q3.shape
    nt = seq // tq
    blk = pl.BlockSpec((nb, tq, D_MODEL), lambda b, t: (b, t, 0))
    kv = pl.BlockSpec((nb,) + k4.shape[1:], lambda b, t: (b,) + (0,) * (k4.ndim - 1))
    body = functools.partial(_attn_body, nb=nb, tq=tq)
    return pl.pallas_call(
        body,
        grid=(bsz // nb, nt),
        in_specs=[blk, blk, kv, kv, pl.BlockSpec((D_MODEL, D_MODEL), lambda b, t: (0, 0))],
        out_specs=pl.BlockSpec((nb * tq * TILE_ROWS, LANES), lambda b, t: (b * nt + t, 0)),
        out_shape=jax.ShapeDtypeStruct((bsz * seq * TILE_ROWS, LANES), F32),
        compiler_params=_params("parallel", "parallel"),
        name="attn",
    )(q3, x13, k4, v4, w_o)


def _router(hm, wr_hi_ref, wr_lo_ref):
    a_hi = hm.astype(BF16)
    a_lo = (hm - a_hi.astype(F32)).astype(BF16)
    logits = _bdot(a_hi, wr_hi_ref[...]) + (_bdot(a_hi, wr_lo_ref[...]) + _bdot(a_lo, wr_hi_ref[...]))
    lane_i = lax.broadcasted_iota(jnp.int32, logits.shape, 1)
    lane = lane_i.astype(F32)
    neg = -jnp.inf
    big = float(ROUTER_LANES)
    is_g = lane_i < N_EXPERT_GROUPS
    glog = jnp.where(is_g, logits, neg)
    gmax = jnp.max(glog, axis=-1, keepdims=True)
    gsel = jnp.min(jnp.where(glog == gmax, lane, big), axis=-1, keepdims=True)
    pg_sel = 1.0 / jnp.sum(jnp.where(is_g, jnp.exp(logits - gmax), 0.0), axis=-1, keepdims=True)
    eidx = lane_i - N_EXPERT_GROUPS
    in_group = (eidx >= 0) & (eidx < N_EXPERTS) & ((eidx >> 3).astype(F32) == gsel)
    el = jnp.where(in_group, logits, neg)
    v1 = jnp.max(el, axis=-1, keepdims=True)
    i1 = jnp.min(jnp.where(el == v1, lane, big), axis=-1, keepdims=True)
    el2 = jnp.where(lane == i1, neg, el)
    v2 = jnp.max(el2, axis=-1, keepdims=True)
    i2 = jnp.min(jnp.where(el2 == v2, lane, big), axis=-1, keepdims=True)
    e2 = jnp.exp(v2 - v1)
    w1 = pg_sel / (1.0 + e2)
    w2 = pg_sel * e2 / (1.0 + e2)
    return i1, i2, w1, w2


TILE_ROWS = D_MODEL // LANES


def _tiles_to_rows(ref, n, base=0):
    return jnp.concatenate([ref[pl.ds(base + s, n, stride=TILE_ROWS), :] for s in range(TILE_ROWS)], axis=1)


def _rows_to_tiles(ref, val, n, base=0):
    for s in range(TILE_ROWS):
        ref[pl.ds(base + s, n, stride=TILE_ROWS), :] = val[:, s * LANES:(s + 1) * LANES]


def _two_part_specs(block_rows, n_a, n_b):
    return [pl.BlockSpec((block_rows, LANES), lambda i: (jnp.minimum(i, n_a - 1), 0)),
            pl.BlockSpec((block_rows, LANES), lambda i: (jnp.clip(i - n_a, 0, n_b - 1), 0))]


def _on_part(n_a, fn, *ref_pairs):
    i = pl.program_id(0)
    pl.when(i < n_a)(lambda: fn(*[p[0] for p in ref_pairs]))
    pl.when(i >= n_a)(lambda: fn(*[p[1] for p in ref_pairs]))


def _route_body(xa_ref, xb_ref, gm_ref, wr_hi_ref, wr_lo_ref, info_ref, cnt_ref, cnt_scr, *, tm, n_a):
    @pl.when(pl.program_id(0) == 0)
    def _():
        cnt_scr[...] = jnp.zeros_like(cnt_scr)

    _on_part(n_a, functools.partial(_route_step, gm_ref, wr_hi_ref, wr_lo_ref, info_ref, cnt_ref, cnt_scr, tm),
             (xa_ref, xb_ref))


def _route_step(gm_ref, wr_hi_ref, wr_lo_ref, info_ref, cnt_ref, cnt_scr, tm, x2t_ref):
    hm = _rms(_tiles_to_rows(x2t_ref, tm), gm_ref[...])
    i1, i2, w1, w2 = _router(hm, wr_hi_ref, wr_lo_ref)
    lane_i = lax.broadcasted_iota(jnp.int32, (tm, ROUTER_LANES), 1)
    lane = lane_i.astype(F32)
    chosen = ((lane == i1) | (lane == i2)).astype(F32)
    earlier = (lax.broadcasted_iota(jnp.int32, (tm, tm), 1) < lax.broadcasted_iota(jnp.int32, (tm, tm), 0))
    before = _bdot(earlier.astype(BF16), chosen.astype(BF16)) + cnt_scr[0:1, :]
    r1 = jnp.sum(jnp.where(lane == i1, before, 0.0), axis=-1, keepdims=True)
    r2 = jnp.sum(jnp.where(lane == i2, before, 0.0), axis=-1, keepdims=True)
    cols = (i1, i2, w1, w2, r1, r2)
    info = jnp.zeros((tm, ROUTER_LANES), F32)
    for j, col in enumerate(cols):
        info = jnp.where(lane_i == j, col, info)
    info_ref[...] = info
    cnt_scr[...] = cnt_scr[...] + jnp.sum(chosen, axis=0, keepdims=True)
    cnt_ref[...] = cnt_scr[...]


def _route(xa, xb, g_moe, wr_hi, wr_lo, tm=512):
    n_a, n_b = xa.shape[0] // (tm * TILE_ROWS), xb.shape[0] // (tm * TILE_ROWS)
    t = (n_a + n_b) * tm

    def const(shape):
        return pl.BlockSpec(shape, lambda i: (0,) * len(shape))

    body = functools.partial(_route_body, tm=tm, n_a=n_a)
    return pl.pallas_call(
        body,
        grid=(n_a + n_b,),
        in_specs=_two_part_specs(tm * TILE_ROWS, n_a, n_b) + [
            const((1, D_MODEL)), const((D_MODEL, ROUTER_LANES)), const((D_MODEL, ROUTER_LANES))],
        out_specs=[pl.BlockSpec((tm, ROUTER_LANES), lambda i: (i, 0)), const((SUBLANES, ROUTER_LANES))],
        out_shape=[jax.ShapeDtypeStruct((t, ROUTER_LANES), F32),
                   jax.ShapeDtypeStruct((SUBLANES, ROUTER_LANES), F32)],
        scratch_shapes=[pltpu.VMEM((SUBLANES, ROUTER_LANES), F32)],
        compiler_params=_params("arbitrary"),
        name="route",
    )(xa, xb, g_moe, wr_hi, wr_lo)


def _route_meta(info, cnt, tme):
    t = info.shape[0]
    n_tiles = 2 * t // tme
    counts = cnt[0, N_EXPERT_GROUPS:N_EXPERT_GROUPS + N_EXPERTS].astype(jnp.int32)
    ends = jnp.cumsum(counts)
    starts = ends - counts
    e = info[:, 0:2].astype(jnp.int32) - N_EXPERT_GROUPS
    onehot = e[:, :, None] == jnp.arange(N_EXPERTS, dtype=jnp.int32)
    slot = (jnp.sum(jnp.where(onehot, starts, 0), axis=-1) + info[:, 4:6].astype(jnp.int32)).reshape(-1)
    pts = jnp.concatenate([jnp.arange(n_tiles, dtype=jnp.int32) * tme, starts])
    n = pts.shape[0]
    idx = jnp.arange(n, dtype=jnp.int32)
    pos = jnp.sum((pts[None, :] < pts[:, None]) | ((pts[None, :] == pts[:, None]) & (idx[None, :] < idx[:, None])),
                  axis=1)
    lo_abs = jnp.sum(jnp.where(pos[:, None] == idx[None, :], pts[:, None], 0), axis=0)
    hi_abs = jnp.concatenate([lo_abs[1:], jnp.full((1,), 2 * t, jnp.int32)])
    tile = jnp.minimum(lo_abs // tme, n_tiles - 1)
    expert = jnp.clip(jnp.sum(starts[None, :] <= lo_abs[:, None], axis=1) - 1, 0, N_EXPERTS - 1)
    first = jnp.concatenate([jnp.ones((1,), jnp.int32), (tile[1:] != tile[:-1]).astype(jnp.int32)])
    last = jnp.concatenate([(tile[1:] != tile[:-1]).astype(jnp.int32), jnp.ones((1,), jnp.int32)])
    work = [a.astype(jnp.int32) for a in (tile, expert, lo_abs - tile * tme, hi_abs - tile * tme, first, last)]
    return slot.astype(jnp.int32), work


def _tile_copy(src, src_tok, dst, dst_tok, sem):
    return pltpu.make_async_copy(src.at[pl.ds(pl.multiple_of(src_tok * TILE_ROWS, TILE_ROWS), TILE_ROWS), :],
                                 dst.at[pl.ds(pl.multiple_of(dst_tok * TILE_ROWS, TILE_ROWS), TILE_ROWS), :], sem)


def _scatter_body(slot_ref, xa_ref, xb_ref, xs_hbm, sem, *, tm, n_a):
    _on_part(n_a, functools.partial(_scatter_step, slot_ref, xs_hbm, sem, tm), (xa_ref, xb_ref))


def _scatter_step(slot_ref, xs_hbm, sem, tm, x2t_ref):
    def issue(r, c):
        _tile_copy(x2t_ref, r, xs_hbm, slot_ref[2 * r], sem).start(priority=0)
        _tile_copy(x2t_ref, r, xs_hbm, slot_ref[2 * r + 1], sem).start(priority=1)
        return c

    lax.fori_loop(0, tm, issue, 0, unroll=8)

    def drain(r, c):
        _tile_copy(x2t_ref, r, xs_hbm, 0, sem).wait()
        _tile_copy(x2t_ref, r, xs_hbm, 0, sem).wait()
        return c

    lax.fori_loop(0, tm, drain, 0, unroll=8)


def _scatter(slot, xa, xb, tm=512):
    n_a, n_b = xa.shape[0] // (tm * TILE_ROWS), xb.shape[0] // (tm * TILE_ROWS)
    t = (n_a + n_b) * tm
    body = functools.partial(_scatter_body, tm=tm, n_a=n_a)
    return pl.pallas_call(
        body,
        grid=(n_a + n_b,),
        in_specs=[pl.BlockSpec((2 * tm,), lambda i: (i,), memory_space=pltpu.SMEM)]
        + _two_part_specs(tm * TILE_ROWS, n_a, n_b),
        out_specs=pl.BlockSpec(memory_space=pl.ANY),
        out_shape=jax.ShapeDtypeStruct((2 * t * TILE_ROWS, LANES), F32),
        scratch_shapes=[pltpu.SemaphoreType.DMA(())],
        compiler_params=_params("arbitrary"),
        name="scatter",
    )(slot, xa, xb)


def _expert_body(tile_ref, exp_ref, lo_ref, hi_ref, first_ref, last_ref, xs_ref, gm_ref, wg_ref, wu_ref, wd_ref,
                 ys_ref, hb_scr, acc_scr, *, tme):
    k = pl.program_id(0)
    first = first_ref[k] == 1
    last = last_ref[k] == 1
    nonempty = hi_ref[k] > lo_ref[k]

    @pl.when(first)
    def _():
        hb_scr[...] = _rms(_tiles_to_rows(xs_ref, tme), gm_ref[...]).astype(BF16)

    @pl.when(first & jnp.logical_not(nonempty))
    def _():
        acc_scr[...] = jnp.zeros_like(acc_scr)

    @pl.when(nonempty)
    def _():
        hb = hb_scr[...]
        act = jax.nn.silu(_bdot(hb, wg_ref[0])) * _bdot(hb, wu_ref[0])
        row = lax.broadcasted_iota(jnp.int32, (tme, 1), 0)
        y = jnp.where((row >= lo_ref[k]) & (row < hi_ref[k]), _bdot(act.astype(BF16), wd_ref[0]), 0.0)

        @pl.when(first & last)
        def _():
            _rows_to_tiles(ys_ref, y, tme)

        @pl.when(first & jnp.logical_not(last))
        def _():
            acc_scr[...] = y

        @pl.when(jnp.logical_not(first))
        def _():
            acc_scr[...] += y

    @pl.when(last & jnp.logical_not(first & nonempty))
    def _():
        _rows_to_tiles(ys_ref, acc_scr[...], tme)


def _experts(work, xs, g_moe, wg, wu, wd, tme):
    n_work = work[0].shape[0]
    body = functools.partial(_expert_body, tme=tme)
    tile_blk = pl.BlockSpec((tme * TILE_ROWS, LANES), lambda k, tile, *_: (tile[k], 0))

    def wspec(shape):
        return pl.BlockSpec(shape, lambda k, tile, exp, *_: (exp[k], 0, 0))

    grid_spec = pltpu.PrefetchScalarGridSpec(
        num_scalar_prefetch=len(work),
        grid=(n_work,),
        in_specs=[tile_blk, pl.BlockSpec((1, D_MODEL), lambda k, *_: (0, 0)),
                  wspec((1, D_MODEL, D_FF_EXPERT)), wspec((1, D_MODEL, D_FF_EXPERT)),
                  wspec((1, D_FF_EXPERT, D_MODEL))],
        out_specs=tile_blk,
        scratch_shapes=[pltpu.VMEM((tme, D_MODEL), BF16), pltpu.VMEM((tme, D_MODEL), F32)],
    )
    return pl.pallas_call(
        body,
        grid_spec=grid_spec,
        out_shape=jax.ShapeDtypeStruct(xs.shape, F32),
        compiler_params=_params("arbitrary"),
        name="experts",
    )(*work, xs, g_moe, wg, wu, wd)


def _combine_body(slot_ref, slot_next_ref, xa_ref, xb_ref, info_ref, ys_hbm, gf_ref, ya_ref, yb_ref, b0, b1, sem,
                  *, tm, nstep, n_a):
    i = pl.program_id(0)
    cur = i % 2

    def gather(slots, buf):
        def issue(r, c):
            _tile_copy(ys_hbm, slots[2 * r], b0.at[buf], r, sem.at[buf]).start(priority=0)
            _tile_copy(ys_hbm, slots[2 * r + 1], b1.at[buf], r, sem.at[buf]).start(priority=1)
            return c

        lax.fori_loop(0, tm, issue, 0, unroll=8)

    @pl.when(i == 0)
    def _():
        gather(slot_ref, 0)

    @pl.when(i + 1 < nstep)
    def _():
        gather(slot_next_ref, 1 - cur)

    def drain(r, c):
        _tile_copy(ys_hbm, 0, b0.at[cur], r, sem.at[cur]).wait()
        _tile_copy(ys_hbm, 0, b1.at[cur], r, sem.at[cur]).wait()
        return c

    lax.fori_loop(0, tm, drain, 0, unroll=8)
    info = info_ref[...]
    moe = info[:, 2:3] * _tiles_to_rows(b0.at[cur], tm) + info[:, 3:4] * _tiles_to_rows(b1.at[cur], tm)

    def finish(x2t_ref, y_ref):
        y_ref[...] = _rms(_tiles_to_rows(x2t_ref, tm) + moe, gf_ref[...])

    _on_part(n_a, finish, (xa_ref, xb_ref), (ya_ref, yb_ref))


def _combine(slot, xa, xb, info, ys, g_final, tm=512):
    n_a, n_b = xa.shape[0] // (tm * TILE_ROWS), xb.shape[0] // (tm * TILE_ROWS)
    nstep = n_a + n_b
    body = functools.partial(_combine_body, tm=tm, nstep=nstep, n_a=n_a)
    buf = pltpu.VMEM((2, tm * TILE_ROWS, LANES), F32)
    return pl.pallas_call(
        body,
        grid=(nstep,),
        in_specs=[pl.BlockSpec((2 * tm,), lambda i: (i,), memory_space=pltpu.SMEM),
                  pl.BlockSpec((2 * tm,), lambda i: (jnp.minimum(i + 1, nstep - 1),), memory_space=pltpu.SMEM)]
        + _two_part_specs(tm * TILE_ROWS, n_a, n_b) + [
                  pl.BlockSpec((tm, ROUTER_LANES), lambda i: (i, 0)),
                  pl.BlockSpec(memory_space=pl.ANY),
                  pl.BlockSpec((1, D_MODEL), lambda i: (0, 0))],
        out_specs=[pl.BlockSpec((tm, D_MODEL), lambda i: (jnp.minimum(i, n_a - 1), 0)),
                   pl.BlockSpec((tm, D_MODEL), lambda i: (jnp.clip(i - n_a, 0, n_b - 1), 0))],
        out_shape=[jax.ShapeDtypeStruct((n_a * tm, D_MODEL), F32), jax.ShapeDtypeStruct((n_b * tm, D_MODEL), F32)],
        scratch_shapes=[buf, buf, pltpu.SemaphoreType.DMA((2,))],
        compiler_params=_params("arbitrary"),
        name="combine",
    )(slot, slot, xa, xb, info, ys, g_final)


def _moe(xa, xb, w, tme=1024):
    info, cnt = _route(xa, xb, w["g_moe"], w["wr_hi"], w["wr_lo"])
    slot, work = _route_meta(info, cnt, tme)
    xs = _scatter(slot, xa, xb)
    ys = _experts(work, xs, w["g_moe"], w["moe_wg"], w["moe_wu"], w["moe_wd"], tme)
    return _combine(slot, xa, xb, info, ys, w["g_final"])


def _layer(x3d, k4, v4, s5_h0, lru_h0, lru_conv, w, chunk, nb, tq):
    bsz, seq, _ = x3d.shape
    x2d = x3d.reshape(bsz * seq, D_MODEL)
    lru_w = (w["conv_w"], w["conv_b"], w["lru_wg"], w["lru_bg"], w["lru_lam"])
    tail = SUBLANES - (CONV_W - 1)
    if lru_h0 is None:
        u, yl, hl, xt = _mixlru(x2d, w["g_mix"], w["w_in"], *lru_w, bsz, seq)
        lru_h = hl.reshape(bsz, SUBLANES, D_LRU)[:, 0]
        conv_new = xt.reshape(bsz, SUBLANES, D_LRU)[:, tail:]
    else:
        u, xl, gl = _mixin(x2d, w["g_mix"], w["w_in"])
        hist = jnp.pad(lru_conv, ((0, 0), (tail, 0), (0, 0))).reshape(bsz * seq, D_LRU)
        yl, hall = _lru_dec(xl, gl, hist, jnp.repeat(lru_h0, seq, axis=0), *lru_w)
        lru_h = hall.reshape(bsz, seq, D_LRU)[:, seq - 1]
        conv_new = xl.reshape(bsz, seq, D_LRU)[:, seq - (CONV_W - 1):]

    nstate = S5_GROUPS * S5_STATE
    if s5_h0 is None:
        zero = jnp.zeros((bsz, 1, nstate), F32)
        ys, s5_re, s5_im = _s5(u, w["s5"][chunk], zero, zero, chunk, seq // chunk, True)
    else:
        ys, s5_re, s5_im = _s5(u, w["s5"][chunk], s5_h0[0].reshape(bsz, nstate), s5_h0[1].reshape(bsz, nstate),
                               chunk, bsz, False)
    s5_re = s5_re.reshape(bsz, S5_GROUPS, S5_STATE)
    s5_im = s5_im.reshape(bsz, S5_GROUPS, S5_STATE)

    x1, q = _mixout(x2d, ys, yl, w["w_glu"], w["b_glu"], w["w_out"], w["g_xa"], w["w_q"])
    x2t = _attn(q.reshape(bsz, seq, D_MODEL), x1.reshape(bsz, seq, D_MODEL), k4, v4, w["w_o"], nb, tq)
    return x2t, s5_re, s5_im, lru_h, conv_new


def kernel(x_prompt, x_sample, mem_prompt, cache_mem_k, cache_mem_v, state_s5_re, state_s5_im, state_lru_h, state_lru_conv, g_mix, w_in, s5_lam_re, s5_lam_im, s5_log_dt, s5_b_re, s5_b_im, s5_c_re, s5_c_im, s5_d, s5_w_glu, s5_b_glu, lru_conv_w, lru_conv_b, lru_w_a, lru_b_a, lru_w_x, lru_b_x, lru_lam, w_out, g_xa, g_mem, xa_w_q, xa_w_k, xa_w_v, xa_w_o, g_moe, moe_w_group, moe_w_expert, moe_w_gate, moe_w_up, moe_w_down, g_final):
    depth = g_mix.shape[0]
    assert depth == 1, "single-layer step"
    l = 0
    bsz, seq, _ = x_prompt.shape
    dbsz, dseq, _ = x_sample.shape
    chunk = SUBLANES
    assert seq % chunk == 0 and dseq == chunk

    s5_args = (s5_lam_re[l], s5_lam_im[l], s5_log_dt[l], s5_b_re[l], s5_b_im[l], s5_c_re[l], s5_c_im[l], s5_d[l])
    wg, bg = _lru_weights(lru_w_a[l], lru_w_x[l], lru_b_a[l], lru_b_x[l])
    wr = jnp.concatenate([moe_w_group[l], moe_w_expert[l],
                          jnp.zeros((D_MODEL, ROUTER_LANES - N_EXPERT_GROUPS - N_EXPERTS), F32)], axis=1)
    wr_hi = wr.astype(BF16)
    wr_lo = (wr - wr_hi.astype(F32)).astype(BF16)
    w = {
        "g_mix": g_mix[l][None], "w_in": w_in[l].astype(BF16),
        "s5": {chunk: _s5_weights(chunk, *s5_args)},
        "conv_w": lru_conv_w[l], "conv_b": lru_conv_b[l][None], "lru_wg": wg, "lru_bg": bg,
        "lru_lam": lru_lam[l][None],
        "w_glu": s5_w_glu[l].astype(BF16), "b_glu": s5_b_glu[l][None], "w_out": w_out[l].astype(BF16),
        "g_xa": g_xa[l][None], "w_q": xa_w_q[l].astype(BF16), "w_o": xa_w_o[l].astype(BF16),
        "g_moe": g_moe[l][None], "wr_hi": wr_hi, "wr_lo": wr_lo,
        "moe_wg": moe_w_gate[l].astype(BF16), "moe_wu": moe_w_up[l].astype(BF16),
        "moe_wd": moe_w_down[l].astype(BF16), "g_final": g_final[None],
    }

    mk, mv = _memkv(mem_prompt.reshape(bsz * N_MEM, D_MODEL), g_mem[l][None],
                    xa_w_k[l].astype(BF16), xa_w_v[l].astype(BF16))
    xp, p_re, p_im, p_h, p_conv = _layer(x_prompt, mk.reshape(bsz, N_MEM, D_MODEL), mv.reshape(bsz, N_MEM, D_MODEL),
                                         None, None, None, w, chunk, nb=1, tq=1024)
    xs, s_re, s_im, s_h, s_conv = _layer(x_sample, cache_mem_k[l], cache_mem_v[l],
                                         (state_s5_re[l], state_s5_im[l]), state_lru_h[l],
                                         state_lru_conv[l], w, chunk, nb=8, tq=dseq)
    yp, ysmp = _moe(xp, xs, w)
    yp = yp.reshape(bsz, seq, D_MODEL)
    ysmp = ysmp.reshape(dbsz, dseq, D_MODEL)

    return (yp, ysmp,
            mk.reshape(1, bsz, N_MEM, XA_HEADS, XA_HEAD_DIM), mv.reshape(1, bsz, N_MEM, XA_HEADS, XA_HEAD_DIM),
            p_re[None], p_im[None], p_h[None], p_conv[None],
            s_re[None], s_im[None], s_h[None], s_conv[None])
```

```python
import functools

import jax
import jax.numpy as jnp
import numpy as np
from jax import lax
from jax.experimental import pallas as pl
from jax.experimental.pallas import tpu as pltpu

F32 = jnp.float32
BF16 = jnp.bfloat16

D_MODEL = 1024
D_S5 = 512
D_LRU = 512
S5_GROUP = 16
S5_GROUPS = 32
S5_STATE = 64
LRU_HEADS = 8
LRU_HEAD_DIM = 64
CONV_W = 4
LRU_C = 8.0
N_MEM = 256
XA_HEADS = 4
XA_HEAD_DIM = 256
N_EXPERT_GROUPS = 4
EXPERTS_PER_GROUP = 8
N_EXPERTS = 32
D_FF_EXPERT = 256
EPS = 1e-6

SUBLANES = 8
LANES = 128
ROUTER_LANES = 128
VMEM_LIMIT = 56 * 1024 * 1024


def _params(*sem):
    return pltpu.CompilerParams(dimension_semantics=sem, vmem_limit_bytes=VMEM_LIMIT)


def _rms(x, g):
    ms = jnp.mean(x * x, axis=-1, keepdims=True)
    return x * lax.rsqrt(ms + EPS) * g


def _bdot(a, b):
    return jnp.dot(a, b, preferred_element_type=F32)


def _memkv_body(m_ref, g_ref, wk_ref, wv_ref, k_ref, v_ref):
    hb = _rms(m_ref[...], g_ref[...]).astype(BF16)
    k_ref[...] = _bdot(hb, wk_ref[...])
    v_ref[...] = _bdot(hb, wv_ref[...])


def _memkv(mem2d, g_mem, wk, wv, tm=512):
    t = mem2d.shape[0]
    row = pl.BlockSpec((tm, D_MODEL), lambda i: (i, 0))
    full = pl.BlockSpec((D_MODEL, D_MODEL), lambda i: (0, 0))
    return pl.pallas_call(
        _memkv_body,
        grid=(t // tm,),
        in_specs=[row, pl.BlockSpec((1, D_MODEL), lambda i: (0, 0)), full, full],
        out_specs=[row, row],
        out_shape=[jax.ShapeDtypeStruct((t, D_MODEL), F32)] * 2,
        compiler_params=_params("parallel"),
        name="memkv",
    )(mem2d, g_mem, wk, wv)


def _mixin_body(x_ref, g_ref, w_ref, u_ref, xl_ref, gl_ref):
    hb = _rms(x_ref[...], g_ref[...]).astype(BF16)
    proj = _bdot(hb, w_ref[...])
    u_ref[...] = proj[:, :D_S5]
    xl_ref[...] = proj[:, D_S5:D_S5 + D_LRU]
    gl_ref[...] = proj[:, D_S5 + D_LRU:]


def _mixin(x2d, g_mix, w_in, tm=1024):
    t = x2d.shape[0]
    half = pl.BlockSpec((tm, D_S5), lambda i: (i, 0))
    return pl.pallas_call(
        _mixin_body,
        grid=(t // tm,),
        in_specs=[pl.BlockSpec((tm, D_MODEL), lambda i: (i, 0)),
                  pl.BlockSpec((1, D_MODEL), lambda i: (0, 0)),
                  pl.BlockSpec((D_MODEL, D_S5 + 2 * D_LRU), lambda i: (0, 0))],
        out_specs=[half, half, half],
        out_shape=[jax.ShapeDtypeStruct((t, D_S5), F32)] * 3,
        compiler_params=_params("parallel"),
        name="mixin",
    )(x2d, g_mix, w_in)


GROUPS_PER_BLOCK = LANES // S5_GROUP
PAIRS_PER_BLOCK = GROUPS_PER_BLOCK // 2
STATE_BLOCK = GROUPS_PER_BLOCK * S5_STATE
PW_ROWS = 24


def _s5_body(u_ref, wa_ref, wh_ref, pw_ref, d_ref, h0r_ref, h0i_ref, y_ref, hfr_ref, hfi_ref, hpr_scr, hpi_scr,
             *, chunk, nrow, scan):
    nh = chunk // SUBLANES
    lc = chunk * S5_GROUP
    slot = lax.broadcasted_iota(jnp.int32, (nrow, LANES), 1) // S5_GROUP
    in_slot = [slot == s for s in range(GROUPS_PER_BLOCK)]

    def pick(src, sel):
        out = src[sel(0)]
        for s in range(1, GROUPS_PER_BLOCK):
            out = jnp.where(in_slot[s], src[sel(s)], out)
        return out

    nat, skew = [], []
    for t in range(chunk):
        a = u_ref[pl.ds(t, nrow, stride=chunk), :]
        nat.append(a)
        s = t % GROUPS_PER_BLOCK
        skew.append(pltpu.roll(a, s * S5_GROUP, axis=1) if s else a)
    def lane_roll(x, slots):
        slots %= GROUPS_PER_BLOCK
        return pltpu.roll(x, slots * S5_GROUP, axis=1) if slots else x

    z = [[lane_roll(pick(skew, lambda s, hh=hh, m=m: SUBLANES * hh + (s - m) % GROUPS_PER_BLOCK), -m)
          for hh in range(nh)] for m in range(GROUPS_PER_BLOCK)]

    ys, er, ei = [], [], []
    for p in range(PAIRS_PER_BLOCK):
        lhs = jnp.concatenate(z[2 * p] + z[2 * p + 1], axis=1).astype(BF16)
        full = _bdot(lhs, wa_ref[p])
        ys.append(full[:, :2 * lc])
        er.append(full[:, 2 * lc:2 * lc + LANES])
        ei.append(full[:, 2 * lc + LANES:])
    er = jnp.concatenate(er, axis=1)
    ei = jnp.concatenate(ei, axis=1)

    if scan:
        ntile = nrow // SUBLANES
        xr = er.reshape(ntile, SUBLANES, STATE_BLOCK)
        xi = ei.reshape(ntile, SUBLANES, STATE_BLOCK)
        row = lax.broadcasted_iota(jnp.int32, xr.shape, 1)
        for k, d in enumerate((1, 2, 4)):
            pr, pi = pw_ref[0, k:k + 1, :], pw_ref[0, 4 + k:5 + k, :]
            sr, si = pltpu.roll(xr, d, axis=1), pltpu.roll(xi, d, axis=1)
            m = row >= d
            xr, xi = jnp.where(m, xr + pr * sr - pi * si, xr), jnp.where(m, xi + pr * si + pi * sr, xi)
        sxr = jnp.where(row >= 1, pltpu.roll(xr, 1, axis=1), 0.0)
        sxi = jnp.where(row >= 1, pltpu.roll(xi, 1, axis=1), 0.0)
        p8r, p8i = pw_ref[0, 3:4, :], pw_ref[0, 7:8, :]
        qr, qi = pw_ref[0, 8:16, :], pw_ref[0, 16:24, :]
        hr, hi = h0r_ref[0], h0i_ref[0]
        for t in range(ntile):
            rows = slice(SUBLANES * t, SUBLANES * (t + 1))
            hpr_scr[rows, :] = sxr[t] + qr * hr - qi * hi
            hpi_scr[rows, :] = sxi[t] + qr * hi + qi * hr
            hr, hi = (xr[t, SUBLANES - 1:] + p8r * hr - p8i * hi, xi[t, SUBLANES - 1:] + p8r * hi + p8i * hr)
        hfr_ref[0] = hr
        hfi_ref[0] = hi
        hpr, hpi = hpr_scr[...], hpi_scr[...]
    else:
        hpr, hpi = h0r_ref[...], h0i_ref[...]
        p1r, p1i = pw_ref[0, 0:1, :], pw_ref[0, 4:5, :]
        hfr_ref[...] = er + p1r * hpr - p1i * hpi
        hfi_ref[...] = ei + p1r * hpi + p1i * hpr

    yg = []
    for p in range(PAIRS_PER_BLOCK):
        lanes = slice(LANES * p, LANES * (p + 1))
        hp = jnp.concatenate([hpr[:, lanes], hpi[:, lanes]], axis=1).astype(BF16)
        out = ys[p] + _bdot(hp, wh_ref[p])
        for half in range(2):
            q = 2 * p + half
            yg.append([lane_roll(out[:, half * lc + hh * LANES:half * lc + (hh + 1) * LANES], q)
                       for hh in range(nh)])
    d = d_ref[...]
    for t in range(chunk):
        hh, s = divmod(t, GROUPS_PER_BLOCK)
        c = lane_roll(pick([g[hh] for g in yg], lambda sl, s=s: (sl - s) % GROUPS_PER_BLOCK), -s)
        y_ref[pl.ds(t, nrow, stride=chunk), :] = c + d * nat[t]


def _s5(u2d, weights, h0r, h0i, chunk, nrow, scan):
    wa, wh, pw, dvec = weights
    t = u2d.shape[0]
    lc = chunk * S5_GROUP
    rows = nrow * chunk
    nblk = D_S5 // LANES
    body = functools.partial(_s5_body, chunk=chunk, nrow=nrow, scan=scan)
    ublk = pl.BlockSpec((rows, LANES), lambda i, j: (i, j))
    if scan:
        hblk = pl.BlockSpec((1, 1, STATE_BLOCK), lambda i, j: (i, 0, j))
        hshape = jax.ShapeDtypeStruct((t // rows, 1, S5_GROUPS * S5_STATE), F32)
    else:
        hblk = pl.BlockSpec((nrow, STATE_BLOCK), lambda i, j: (i, j))
        hshape = jax.ShapeDtypeStruct((t // chunk, S5_GROUPS * S5_STATE), F32)
    return pl.pallas_call(
        body,
        grid=(t // rows, nblk),
        in_specs=[ublk,
                  pl.BlockSpec((PAIRS_PER_BLOCK, 2 * lc, 2 * lc + 2 * LANES), lambda i, j: (j, 0, 0)),
                  pl.BlockSpec((PAIRS_PER_BLOCK, 2 * LANES, 2 * lc), lambda i, j: (j, 0, 0)),
                  pl.BlockSpec((1, PW_ROWS, STATE_BLOCK), lambda i, j: (j, 0, 0)),
                  pl.BlockSpec((1, LANES), lambda i, j: (0, j)),
                  hblk, hblk],
        out_specs=[ublk, hblk, hblk],
        out_shape=[jax.ShapeDtypeStruct((t, D_S5), F32), hshape, hshape],
        scratch_shapes=[pltpu.VMEM((nrow, STATE_BLOCK), F32), pltpu.VMEM((nrow, STATE_BLOCK), F32)],
        compiler_params=_params("parallel", "parallel"),
        name="s5",
    )(u2d, wa, wh, pw, dvec, h0r, h0i)


def _s5_weights(chunk, lam_re, lam_im, log_dt, b_re, b_im, c_re, c_im, d_skip):
    hi = lax.Precision.HIGHEST
    dt = jnp.exp(log_dt)[:, None]
    mag = jnp.exp(lam_re * dt)
    ab_re = mag * jnp.cos(lam_im * dt)
    ab_im = mag * jnp.sin(lam_im * dt)
    den = lam_re * lam_re + lam_im * lam_im
    nr = ab_re - 1.0
    f_re = (nr * lam_re + ab_im * lam_im) / den
    f_im = (ab_im * lam_re - nr * lam_im) / den
    bb_re = f_re[..., None] * b_re - f_im[..., None] * b_im
    bb_im = f_re[..., None] * b_im + f_im[..., None] * b_re
    def powers(ks):
        k = jnp.asarray(np.asarray(ks, np.float32))[:, None, None]
        m = jnp.exp(k * (lam_re * dt))
        return m * jnp.cos(k * (lam_im * dt)), m * jnp.sin(k * (lam_im * dt))

    pw_re, pw_im = powers(range(chunk + 1))
    pb_re = pw_re[:chunk, ..., None] * bb_re - pw_im[:chunk, ..., None] * bb_im
    pb_im = pw_re[:chunk, ..., None] * bb_im + pw_im[:chunk, ..., None] * bb_re
    kk = (jnp.einsum("gon,kgni->kgoi", c_re, pb_re, precision=hi)
          - jnp.einsum("gon,kgni->kgoi", c_im, pb_im, precision=hi))
    npair = S5_GROUPS // 2
    lc = chunk * S5_GROUP
    row0 = kk.transpose(1, 3, 0, 2).reshape(S5_GROUPS, S5_GROUP, lc)
    row0 = jnp.pad(row0, ((0, 0), (0, 0), (lc, 0)))
    conv = jnp.stack([row0[:, :, lc - S5_GROUP * ti:2 * lc - S5_GROUP * ti] for ti in range(chunk)], axis=1)
    conv = conv.reshape(npair, 2, lc, lc)
    ends = [x[::-1].transpose(1, 0, 3, 2).reshape(npair, 2, lc, S5_STATE) for x in (pb_re, pb_im)]
    zc = jnp.zeros((npair, lc, lc), F32)
    zs = jnp.zeros((npair, lc, S5_STATE), F32)
    wa = jnp.concatenate([
        jnp.concatenate([conv[:, 0], zc, ends[0][:, 0], zs, ends[1][:, 0], zs], axis=-1),
        jnp.concatenate([zc, conv[:, 1], zs, ends[0][:, 1], zs, ends[1][:, 1]], axis=-1)], axis=1)
    gr = c_re[None] * pw_re[1:, :, None, :] - c_im[None] * pw_im[1:, :, None, :]
    gi = c_re[None] * pw_im[1:, :, None, :] + c_im[None] * pw_re[1:, :, None, :]
    zr = jnp.zeros((npair, S5_STATE, lc), F32)
    blocks = []
    for x in (gr, -gi):
        hy = x.transpose(1, 3, 0, 2).reshape(npair, 2, S5_STATE, lc)
        blocks += [jnp.concatenate([hy[:, 0], zr], axis=-1), jnp.concatenate([zr, hy[:, 1]], axis=-1)]
    wh = jnp.concatenate(blocks, axis=1)
    tab = [1, 2, 4, 8] + list(range(SUBLANES))
    qr, qi = powers([chunk * k for k in tab])
    nblk = D_S5 // LANES
    pw = jnp.concatenate([qr[:4], qi[:4], qr[4:], qi[4:]]).reshape(PW_ROWS, nblk, STATE_BLOCK).transpose(1, 0, 2)
    return wa.astype(BF16), wh.astype(BF16), pw, d_skip.reshape(1, D_S5)


def _lru_ab(xc, wg_ref, bg_ref, lam_ref):
    xb = xc.astype(BF16)
    half = D_LRU // 2
    g0 = _bdot(xb[:, :half], wg_ref[0]) + bg_ref[0]
    g1 = _bdot(xb[:, half:], wg_ref[1]) + bg_ref[1]
    r = jax.nn.sigmoid(jnp.concatenate([g0[:, :half], g1[:, :half]], axis=1))
    ig = jax.nn.sigmoid(jnp.concatenate([g0[:, half:], g1[:, half:]], axis=1))
    zl = -lam_ref[...]
    softplus = jnp.maximum(zl, 0.0) + jnp.log1p(jnp.exp(-jnp.abs(zl)))
    log_a = -LRU_C * r * softplus
    a = jnp.exp(log_a)
    b = jnp.sqrt(-jnp.tanh(log_a) * (a * a + 1.0)) * (ig * xc)
    return a, b


def _tile_scan(a, b):
    shape = a.shape
    a = a.reshape(shape[0] // SUBLANES, SUBLANES, shape[1])
    b = b.reshape(a.shape)
    row = lax.broadcasted_iota(jnp.int32, a.shape, 1)
    for d in (1, 2, 4):
        a_prev = pltpu.roll(a, d, axis=1)
        b_prev = pltpu.roll(b, d, axis=1)
        m = row >= d
        b = jnp.where(m, b + a * b_prev, b)
        a = jnp.where(m, a * a_prev, a)
    return a.reshape(shape), b.reshape(shape)


def _mixlru_body(x_ref, gmix_ref, win_ref, cw_ref, cb_ref, wg_ref, bg_ref, lam_ref, u_ref, y_ref, hl_ref, xt_ref,
                 xp_scr, a_scr, b_scr, h_scr, hc_scr, *, ts):
    ti = pl.program_id(1)

    @pl.when(ti == 0)
    def _():
        xp_scr[0:SUBLANES, :] = jnp.zeros((SUBLANES, D_LRU), F32)
        hc_scr[...] = jnp.zeros((SUBLANES, D_LRU), F32)

    proj = _bdot(_rms(x_ref[...], gmix_ref[...]).astype(BF16), win_ref[...])
    u_ref[...] = proj[:, :D_S5]
    xl = proj[:, D_S5:D_S5 + D_LRU]
    gl = proj[:, D_S5 + D_LRU:]
    xt_ref[...] = xl[ts - SUBLANES:, :]
    xp_scr[SUBLANES:SUBLANES + ts, :] = xl
    xc = cb_ref[...] + xl * cw_ref[CONV_W - 1:CONV_W, :]
    for j in range(1, CONV_W):
        xc = xc + xp_scr[SUBLANES - j:SUBLANES - j + ts, :] * cw_ref[CONV_W - 1 - j:CONV_W - j, :]
    xp_scr[0:SUBLANES, :] = xl[ts - SUBLANES:, :]
    a, b = _lru_ab(xc, wg_ref, bg_ref, lam_ref)
    a, b = _tile_scan(a, b)
    a_scr[...] = a
    b_scr[...] = b

    def step(i, hin):
        rows = pl.ds(pl.multiple_of(i * SUBLANES, SUBLANES), SUBLANES)
        h = b_scr[rows, :] + a_scr[rows, :] * hin
        h_scr[rows, :] = h
        return h[SUBLANES - 1:SUBLANES, :]

    hlast = lax.fori_loop(0, ts // SUBLANES, step, hc_scr[0:1, :], unroll=4)
    hc_scr[...] = jnp.broadcast_to(hlast, (SUBLANES, D_LRU))
    hl_ref[...] = hc_scr[...]
    y_ref[...] = h_scr[...] * jax.nn.gelu(gl)


def _lru_dec_body(xl_ref, gl_ref, hist_ref, h0_ref, cw_ref, cb_ref, wg_ref, bg_ref, lam_ref, y_ref, hl_ref, *, tm):
    xl = xl_ref[...]
    hist = hist_ref[...]
    row = lax.broadcasted_iota(jnp.int32, xl.shape, 0) % SUBLANES
    xc = cb_ref[...] + xl * cw_ref[CONV_W - 1:CONV_W, :]
    for j in range(1, CONV_W):
        prev = jnp.where(row >= j, pltpu.roll(xl, j, axis=0), pltpu.roll(hist, tm - SUBLANES + j, axis=0))
        xc = xc + prev * cw_ref[CONV_W - 1 - j:CONV_W - j, :]
    a, b = _lru_ab(xc, wg_ref, bg_ref, lam_ref)
    a, b = _tile_scan(a, b)
    h = b + a * h0_ref[...]
    hl_ref[...] = h
    y_ref[...] = h * jax.nn.gelu(gl_ref[...])


def _lru_weights(w_a, w_x, b_a, b_x):
    eye = jnp.eye(LRU_HEADS, dtype=F32)
    bd_a = jnp.einsum("hij,hg->higj", w_a, eye).reshape(D_LRU, D_LRU)
    bd_x = jnp.einsum("hij,hg->higj", w_x, eye).reshape(D_LRU, D_LRU)
    half = D_LRU // 2
    wg = jnp.stack([jnp.concatenate([bd_a[:half, :half], bd_x[:half, :half]], axis=1),
                    jnp.concatenate([bd_a[half:, half:], bd_x[half:, half:]], axis=1)]).astype(BF16)
    ba = b_a.reshape(1, D_LRU)
    bx = b_x.reshape(1, D_LRU)
    bg = jnp.stack([jnp.concatenate([ba[:, :half], bx[:, :half]], axis=1),
                    jnp.concatenate([ba[:, half:], bx[:, half:]], axis=1)])
    return wg, bg


def _lru_common_specs(const):
    return [const((CONV_W, D_LRU)), const((1, D_LRU)), const((2, D_LRU // 2, D_LRU)),
            const((2, 1, D_LRU)), const((1, D_LRU))]


def _mixlru(x2d, g_mix, w_in, cw, cb, wg, bg, lam, bsz, seq, ts=512):
    nt = seq // ts
    blk = pl.BlockSpec((ts, D_LRU), lambda b, t: (b * nt + t, 0))
    per_seq = pl.BlockSpec((SUBLANES, D_LRU), lambda b, t: (b, 0))

    def const(shape):
        return pl.BlockSpec(shape, lambda b, t: (0,) * len(shape))

    body = functools.partial(_mixlru_body, ts=ts)
    tile = jax.ShapeDtypeStruct((bsz * SUBLANES, D_LRU), F32)
    return pl.pallas_call(
        body,
        grid=(bsz, nt),
        in_specs=[pl.BlockSpec((ts, D_MODEL), lambda b, t: (b * nt + t, 0)), const((1, D_MODEL)),
                  const((D_MODEL, D_S5 + 2 * D_LRU))] + _lru_common_specs(const),
        out_specs=[blk, blk, per_seq, per_seq],
        out_shape=[jax.ShapeDtypeStruct((bsz * seq, D_S5), F32), jax.ShapeDtypeStruct((bsz * seq, D_LRU), F32),
                   tile, tile],
        scratch_shapes=[pltpu.VMEM((ts + SUBLANES, D_LRU), F32), pltpu.VMEM((ts, D_LRU), F32),
                        pltpu.VMEM((ts, D_LRU), F32), pltpu.VMEM((ts, D_LRU), F32),
                        pltpu.VMEM((SUBLANES, D_LRU), F32)],
        compiler_params=_params("arbitrary", "arbitrary"),
        name="mixlru",
    )(x2d, g_mix, w_in, cw, cb, wg, bg, lam)


def _lru_dec(xl, gl, hist, h0rep, cw, cb, wg, bg, lam, tm=256):
    t = xl.shape[0]
    blk = pl.BlockSpec((tm, D_LRU), lambda i: (i, 0))

    def const(shape):
        return pl.BlockSpec(shape, lambda i: (0,) * len(shape))

    body = functools.partial(_lru_dec_body, tm=tm)
    return pl.pallas_call(
        body,
        grid=(t // tm,),
        in_specs=[blk, blk, blk, blk] + _lru_common_specs(const),
        out_specs=[blk, blk],
        out_shape=[jax.ShapeDtypeStruct((t, D_LRU), F32)] * 2,
        compiler_params=_params("parallel"),
        name="lru_dec",
    )(xl, gl, hist, h0rep, cw, cb, wg, bg, lam)


def _mixout_body(x_ref, ys_ref, yl_ref, wglu_ref, bglu_ref, wo_ref, gxa_ref, wq_ref, x1_ref, q_ref):
    ys = jax.nn.gelu(ys_ref[...])
    gate = jax.nn.sigmoid(_bdot(ys.astype(BF16), wglu_ref[...]) + bglu_ref[...])
    s5 = (ys * gate).astype(BF16)
    x1 = (x_ref[...] + _bdot(s5, wo_ref[0:D_S5, :]) + _bdot(yl_ref[...].astype(BF16), wo_ref[D_S5:, :]))
    x1_ref[...] = x1
    q_ref[...] = _bdot(_rms(x1, gxa_ref[...]).astype(BF16), wq_ref[...]).astype(BF16)


def _mixout(x2d, ys, yl, w_glu, b_glu, w_out, g_xa, w_q, tm=1024):
    t = x2d.shape[0]
    row = pl.BlockSpec((tm, D_MODEL), lambda i: (i, 0))
    half = pl.BlockSpec((tm, D_S5), lambda i: (i, 0))

    def const(shape):
        return pl.BlockSpec(shape, lambda i: (0,) * len(shape))

    return pl.pallas_call(
        _mixout_body,
        grid=(t // tm,),
        in_specs=[row, half, half, const((D_S5, D_S5)), const((1, D_S5)), const((D_MODEL, D_MODEL)),
                  const((1, D_MODEL)), const((D_MODEL, D_MODEL))],
        out_specs=[row, row],
        out_shape=[jax.ShapeDtypeStruct((t, D_MODEL), F32), jax.ShapeDtypeStruct((t, D_MODEL), BF16)],
        compiler_params=_params("parallel"),
        name="mixout",
    )(x2d, ys, yl, w_glu, b_glu, w_out, g_xa, w_q)


def _softmax(sc):
    p = jnp.exp(sc - jnp.max(sc, axis=-1, keepdims=True))
    return p / jnp.sum(p, axis=-1, keepdims=True)


def _attn_body(q_ref, x1_ref, k_ref, v_ref, wo_ref, x2t_ref, *, nb, tq):
    scale = XA_HEAD_DIM ** -0.5
    for i in range(nb):
        q = q_ref[i]
        heads = [q[:, h * XA_HEAD_DIM:(h + 1) * XA_HEAD_DIM] for h in range(XA_HEADS)]
        if len(k_ref.shape) == 4:
            kf = k_ref[i].reshape(N_MEM * XA_HEADS, XA_HEAD_DIM).astype(BF16)
            vf = v_ref[i].reshape(N_MEM * XA_HEADS, XA_HEAD_DIM).astype(BF16)
            sc = lax.dot_general(jnp.concatenate(heads, axis=0), kf, (((1,), (1,)), ((), ())),
                                 preferred_element_type=F32) * scale
            own = (lax.broadcasted_iota(jnp.int32, sc.shape, 1) % XA_HEADS
                   == lax.broadcasted_iota(jnp.int32, sc.shape, 0) // tq)
            oh = _bdot(_softmax(jnp.where(own, sc, -jnp.inf)).astype(BF16), vf)
            outs = [oh[h * tq:(h + 1) * tq] for h in range(XA_HEADS)]
        else:
            kb = k_ref[i].astype(BF16)
            vb = v_ref[i].astype(BF16)
            outs = []
            for h in range(XA_HEADS):
                cols = slice(h * XA_HEAD_DIM, (h + 1) * XA_HEAD_DIM)
                sc = lax.dot_general(heads[h], kb[:, cols], (((1,), (1,)), ((), ())),
                                     preferred_element_type=F32) * scale
                outs.append(_bdot(_softmax(sc).astype(BF16), vb[:, cols]))
        o = jnp.concatenate(outs, axis=1).astype(BF16)
        _rows_to_tiles(x2t_ref, x1_ref[i] + _bdot(o, wo_ref[...]), tq, base=i * tq * TILE_ROWS)


def _attn(q3, x13, k4, v4, w_o, nb, tq):
    bsz, seq, _ = q3.shape
    nt = seq // tq
    blk = pl.BlockSpec((nb, tq, D_MODEL), lambda b, t: (b, t, 0))
    kv = pl.BlockSpec((nb,) + k4.shape[1:], lambda b, t: (b,) + (0,) * (k4.ndim - 1))
    body = functools.partial(_attn_body, nb=nb, tq=tq)
    return pl.pallas_call(
        body,
        grid=(bsz // nb, nt),
        in_specs=[blk, blk, kv, kv, pl.BlockSpec((D_MODEL, D_MODEL), lambda b, t: (0, 0))],
        out_specs=pl.BlockSpec((nb * tq * TILE_ROWS, LANES), lambda b, t: (b * nt + t, 0)),
        out_shape=jax.ShapeDtypeStruct((bsz * seq * TILE_ROWS, LANES), F32),
        compiler_params=_params("parallel", "parallel"),
        name="attn",
    )(q3, x13, k4, v4, w_o)


def _router(hm, wr_hi_ref, wr_lo_ref):
    a_hi = hm.astype(BF16)
    a_lo = (hm - a_hi.astype(F32)).astype(BF16)
    logits = _bdot(a_hi, wr_hi_ref[...]) + (_bdot(a_hi, wr_lo_ref[...]) + _bdot(a_lo, wr_hi_ref[...]))
    lane_i = lax.broadcasted_iota(jnp.int32, logits.shape, 1)
    lane = lane_i.astype(F32)
    neg = -jnp.inf
    big = float(ROUTER_LANES)
    is_g = lane_i < N_EXPERT_GROUPS
    glog = jnp.where(is_g, logits, neg)
    gmax = jnp.max(glog, axis=-1, keepdims=True)
    gsel = jnp.min(jnp.where(glog == gmax, lane, big), axis=-1, keepdims=True)
    pg_sel = 1.0 / jnp.sum(jnp.where(is_g, jnp.exp(logits - gmax), 0.0), axis=-1, keepdims=True)
    eidx = lane_i - N_EXPERT_GROUPS
    in_group = (eidx >= 0) & (eidx < N_EXPERTS) & ((eidx >> 3).astype(F32) == gsel)
    el = jnp.where(in_group, logits, neg)
    v1 = jnp.max(el, axis=-1, keepdims=True)
    i1 = jnp.min(jnp.where(el == v1, lane, big), axis=-1, keepdims=True)
    el2 = jnp.where(lane == i1, neg, el)
    v2 = jnp.max(el2, axis=-1, keepdims=True)
    i2 = jnp.min(jnp.where(el2 == v2, lane, big), axis=-1, keepdims=True)
    e2 = jnp.exp(v2 - v1)
    w1 = pg_sel / (1.0 + e2)
    w2 = pg_sel * e2 / (1.0 + e2)
    return i1, i2, w1, w2


TILE_ROWS = D_MODEL // LANES


def _tiles_to_rows(ref, n, base=0):
    return jnp.concatenate([ref[pl.ds(base + s, n, stride=TILE_ROWS), :] for s in range(TILE_ROWS)], axis=1)


def _rows_to_tiles(ref, val, n, base=0):
    for s in range(TILE_ROWS):
        ref[pl.ds(base + s, n, stride=TILE_ROWS), :] = val[:, s * LANES:(s + 1) * LANES]


def _two_part_specs(block_rows, n_a, n_b):
    return [pl.BlockSpec((block_rows, LANES), lambda i: (jnp.minimum(i, n_a - 1), 0)),
            pl.BlockSpec((block_rows, LANES), lambda i: (jnp.clip(i - n_a, 0, n_b - 1), 0))]


def _on_part(n_a, fn, *ref_pairs):
    i = pl.program_id(0)
    pl.when(i < n_a)(lambda: fn(*[p[0] for p in ref_pairs]))
    pl.when(i >= n_a)(lambda: fn(*[p[1] for p in ref_pairs]))


def _route_body(xa_ref, xb_ref, gm_ref, wr_hi_ref, wr_lo_ref, info_ref, cnt_ref, cnt_scr, *, tm, n_a):
    @pl.when(pl.program_id(0) == 0)
    def _():
        cnt_scr[...] = jnp.zeros_like(cnt_scr)

    _on_part(n_a, functools.partial(_route_step, gm_ref, wr_hi_ref, wr_lo_ref, info_ref, cnt_ref, cnt_scr, tm),
             (xa_ref, xb_ref))


def _route_step(gm_ref, wr_hi_ref, wr_lo_ref, info_ref, cnt_ref, cnt_scr, tm, x2t_ref):
    hm = _rms(_tiles_to_rows(x2t_ref, tm), gm_ref[...])
    i1, i2, w1, w2 = _router(hm, wr_hi_ref, wr_lo_ref)
    lane_i = lax.broadcasted_iota(jnp.int32, (tm, ROUTER_LANES), 1)
    lane = lane_i.astype(F32)
    chosen = ((lane == i1) | (lane == i2)).astype(F32)
    earlier = (lax.broadcasted_iota(jnp.int32, (tm, tm), 1) < lax.broadcasted_iota(jnp.int32, (tm, tm), 0))
    before = _bdot(earlier.astype(BF16), chosen.astype(BF16)) + cnt_scr[0:1, :]
    r1 = jnp.sum(jnp.where(lane == i1, before, 0.0), axis=-1, keepdims=True)
    r2 = jnp.sum(jnp.where(lane == i2, before, 0.0), axis=-1, keepdims=True)
    cols = (i1, i2, w1, w2, r1, r2)
    info = jnp.zeros((tm, ROUTER_LANES), F32)
    for j, col in enumerate(cols):
        info = jnp.where(lane_i == j, col, info)
    info_ref[...] = info
    cnt_scr[...] = cnt_scr[...] + jnp.sum(chosen, axis=0, keepdims=True)
    cnt_ref[...] = cnt_scr[...]


def _route(xa, xb, g_moe, wr_hi, wr_lo, tm=512):
    n_a, n_b = xa.shape[0] // (tm * TILE_ROWS), xb.shape[0] // (tm * TILE_ROWS)
    t = (n_a + n_b) * tm

    def const(shape):
        return pl.BlockSpec(shape, lambda i: (0,) * len(shape))

    body = functools.partial(_route_body, tm=tm, n_a=n_a)
    return pl.pallas_call(
        body,
        grid=(n_a + n_b,),
        in_specs=_two_part_specs(tm * TILE_ROWS, n_a, n_b) + [
            const((1, D_MODEL)), const((D_MODEL, ROUTER_LANES)), const((D_MODEL, ROUTER_LANES))],
        out_specs=[pl.BlockSpec((tm, ROUTER_LANES), lambda i: (i, 0)), const((SUBLANES, ROUTER_LANES))],
        out_shape=[jax.ShapeDtypeStruct((t, ROUTER_LANES), F32),
                   jax.ShapeDtypeStruct((SUBLANES, ROUTER_LANES), F32)],
        scratch_shapes=[pltpu.VMEM((SUBLANES, ROUTER_LANES), F32)],
        compiler_params=_params("arbitrary"),
        name="route",
    )(xa, xb, g_moe, wr_hi, wr_lo)


def _route_meta(info, cnt, tme):
    t = info.shape[0]
    n_tiles = 2 * t // tme
    counts = cnt[0, N_EXPERT_GROUPS:N_EXPERT_GROUPS + N_EXPERTS].astype(jnp.int32)
    ends = jnp.cumsum(counts)
    starts = ends - counts
    e = info[:, 0:2].astype(jnp.int32) - N_EXPERT_GROUPS
    onehot = e[:, :, None] == jnp.arange(N_EXPERTS, dtype=jnp.int32)
    slot = (jnp.sum(jnp.where(onehot, starts, 0), axis=-1) + info[:, 4:6].astype(jnp.int32)).reshape(-1)
    pts = jnp.concatenate([jnp.arange(n_tiles, dtype=jnp.int32) * tme, starts])
    n = pts.shape[0]
    idx = jnp.arange(n, dtype=jnp.int32)
    pos = jnp.sum((pts[None, :] < pts[:, None]) | ((pts[None, :] == pts[:, None]) & (idx[None, :] < idx[:, None])),
                  axis=1)
    lo_abs = jnp.sum(jnp.where(pos[:, None] == idx[None, :], pts[:, None], 0), axis=0)
    hi_abs = jnp.concatenate([lo_abs[1:], jnp.full((1,), 2 * t, jnp.int32)])
    tile = jnp.minimum(lo_abs // tme, n_tiles - 1)
    expert = jnp.clip(jnp.sum(starts[None, :] <= lo_abs[:, None], axis=1) - 1, 0, N_EXPERTS - 1)
    first = jnp.concatenate([jnp.ones((1,), jnp.int32), (tile[1:] != tile[:-1]).astype(jnp.int32)])
    last = jnp.concatenate([(tile[1:] != tile[:-1]).astype(jnp.int32), jnp.ones((1,), jnp.int32)])
    work = [a.astype(jnp.int32) for a in (tile, expert, lo_abs - tile * tme, hi_abs - tile * tme, first, last)]
    return slot.astype(jnp.int32), work


def _tile_copy(src, src_tok, dst, dst_tok, sem):
    return pltpu.make_async_copy(src.at[pl.ds(pl.multiple_of(src_tok * TILE_ROWS, TILE_ROWS), TILE_ROWS), :],
                                 dst.at[pl.ds(pl.multiple_of(dst_tok * TILE_ROWS, TILE_ROWS), TILE_ROWS), :], sem)


def _scatter_body(slot_ref, xa_ref, xb_ref, xs_hbm, sem, *, tm, n_a):
    _on_part(n_a, functools.partial(_scatter_step, slot_ref, xs_hbm, sem, tm), (xa_ref, xb_ref))


def _scatter_step(slot_ref, xs_hbm, sem, tm, x2t_ref):
    def issue(r, c):
        _tile_copy(x2t_ref, r, xs_hbm, slot_ref[2 * r], sem).start(priority=0)
        _tile_copy(x2t_ref, r, xs_hbm, slot_ref[2 * r + 1], sem).start(priority=1)
        return c

    lax.fori_loop(0, tm, issue, 0, unroll=8)

    def drain(r, c):
        _tile_copy(x2t_ref, r, xs_hbm, 0, sem).wait()
        _tile_copy(x2t_ref, r, xs_hbm, 0, sem).wait()
        return c

    lax.fori_loop(0, tm, drain, 0, unroll=8)


def _scatter(slot, xa, xb, tm=512):
    n_a, n_b = xa.shape[0] // (tm * TILE_ROWS), xb.shape[0] // (tm * TILE_ROWS)
    t = (n_a + n_b) * tm
    body = functools.partial(_scatter_body, tm=tm, n_a=n_a)
    return pl.pallas_call(
        body,
        grid=(n_a + n_b,),
        in_specs=[pl.BlockSpec((2 * tm,), lambda i: (i,), memory_space=pltpu.SMEM)]
        + _two_part_specs(tm * TILE_ROWS, n_a, n_b),
        out_specs=pl.BlockSpec(memory_space=pl.ANY),
        out_shape=jax.ShapeDtypeStruct((2 * t * TILE_ROWS, LANES), F32),
        scratch_shapes=[pltpu.SemaphoreType.DMA(())],
        compiler_params=_params("arbitrary"),
        name="scatter",
    )(slot, xa, xb)


def _expert_body(tile_ref, exp_ref, lo_ref, hi_ref, first_ref, last_ref, xs_ref, gm_ref, wg_ref, wu_ref, wd_ref,
                 ys_ref, hb_scr, acc_scr, *, tme):
    k = pl.program_id(0)
    first = first_ref[k] == 1
    last = last_ref[k] == 1
    nonempty = hi_ref[k] > lo_ref[k]

    @pl.when(first)
    def _():
        hb_scr[...] = _rms(_tiles_to_rows(xs_ref, tme), gm_ref[...]).astype(BF16)

    @pl.when(first & jnp.logical_not(nonempty))
    def _():
        acc_scr[...] = jnp.zeros_like(acc_scr)

    @pl.when(nonempty)
    def _():
        hb = hb_scr[...]
        act = jax.nn.silu(_bdot(hb, wg_ref[0])) * _bdot(hb, wu_ref[0])
        row = lax.broadcasted_iota(jnp.int32, (tme, 1), 0)
        y = jnp.where((row >= lo_ref[k]) & (row < hi_ref[k]), _bdot(act.astype(BF16), wd_ref[0]), 0.0)

        @pl.when(first & last)
        def _():
            _rows_to_tiles(ys_ref, y, tme)

        @pl.when(first & jnp.logical_not(last))
        def _():
            acc_scr[...] = y

        @pl.when(jnp.logical_not(first))
        def _():
            acc_scr[...] += y

    @pl.when(last & jnp.logical_not(first & nonempty))
    def _():
        _rows_to_tiles(ys_ref, acc_scr[...], tme)


def _experts(work, xs, g_moe, wg, wu, wd, tme):
    n_work = work[0].shape[0]
    body = functools.partial(_expert_body, tme=tme)
    tile_blk = pl.BlockSpec((tme * TILE_ROWS, LANES), lambda k, tile, *_: (tile[k], 0))

    def wspec(shape):
        return pl.BlockSpec(shape, lambda k, tile, exp, *_: (exp[k], 0, 0))

    grid_spec = pltpu.PrefetchScalarGridSpec(
        num_scalar_prefetch=len(work),
        grid=(n_work,),
        in_specs=[tile_blk, pl.BlockSpec((1, D_MODEL), lambda k, *_: (0, 0)),
                  wspec((1, D_MODEL, D_FF_EXPERT)), wspec((1, D_MODEL, D_FF_EXPERT)),
                  wspec((1, D_FF_EXPERT, D_MODEL))],
        out_specs=tile_blk,
        scratch_shapes=[pltpu.VMEM((tme, D_MODEL), BF16), pltpu.VMEM((tme, D_MODEL), F32)],
    )
    return pl.pallas_call(
        body,
        grid_spec=grid_spec,
        out_shape=jax.ShapeDtypeStruct(xs.shape, F32),
        compiler_params=_params("arbitrary"),
        name="experts",
    )(*work, xs, g_moe, wg, wu, wd)


def _combine_body(slot_ref, slot_next_ref, xa_ref, xb_ref, info_ref, ys_hbm, gf_ref, ya_ref, yb_ref, b0, b1, sem,
                  *, tm, nstep, n_a):
    i = pl.program_id(0)
    cur = i % 2

    def gather(slots, buf):
        def issue(r, c):
            _tile_copy(ys_hbm, slots[2 * r], b0.at[buf], r, sem.at[buf]).start(priority=0)
            _tile_copy(ys_hbm, slots[2 * r + 1], b1.at[buf], r, sem.at[buf]).start(priority=1)
            return c

        lax.fori_loop(0, tm, issue, 0, unroll=8)

    @pl.when(i == 0)
    def _():
        gather(slot_ref, 0)

    @pl.when(i + 1 < nstep)
    def _():
        gather(slot_next_ref, 1 - cur)

    def drain(r, c):
        _tile_copy(ys_hbm, 0, b0.at[cur], r, sem.at[cur]).wait()
        _tile_copy(ys_hbm, 0, b1.at[cur], r, sem.at[cur]).wait()
        return c

    lax.fori_loop(0, tm, drain, 0, unroll=8)
    info = info_ref[...]
    moe = info[:, 2:3] * _tiles_to_rows(b0.at[cur], tm) + info[:, 3:4] * _tiles_to_rows(b1.at[cur], tm)

    def finish(x2t_ref, y_ref):
        y_ref[...] = _rms(_tiles_to_rows(x2t_ref, tm) + moe, gf_ref[...])

    _on_part(n_a, finish, (xa_ref, xb_ref), (ya_ref, yb_ref))


def _combine(slot, xa, xb, info, ys, g_final, tm=256):
    n_a, n_b = xa.shape[0] // (tm * TILE_ROWS), xb.shape[0] // (tm * TILE_ROWS)
    nstep = n_a + n_b
    body = functools.partial(_combine_body, tm=tm, nstep=nstep, n_a=n_a)
    buf = pltpu.VMEM((2, tm * TILE_ROWS, LANES), F32)
    return pl.pallas_call(
        body,
        grid=(nstep,),
        in_specs=[pl.BlockSpec((2 * tm,), lambda i: (i,), memory_space=pltpu.SMEM),
                  pl.BlockSpec((2 * tm,), lambda i: (jnp.minimum(i + 1, nstep - 1),), memory_space=pltpu.SMEM)]
        + _two_part_specs(tm * TILE_ROWS, n_a, n_b) + [
                  pl.BlockSpec((tm, ROUTER_LANES), lambda i: (i, 0)),
                  pl.BlockSpec(memory_space=pl.ANY),
                  pl.BlockSpec((1, D_MODEL), lambda i: (0, 0))],
        out_specs=[pl.BlockSpec((tm, D_MODEL), lambda i: (jnp.minimum(i, n_a - 1), 0)),
                   pl.BlockSpec((tm, D_MODEL), lambda i: (jnp.clip(i - n_a, 0, n_b - 1), 0))],
        out_shape=[jax.ShapeDtypeStruct((n_a * tm, D_MODEL), F32), jax.ShapeDtypeStruct((n_b * tm, D_MODEL), F32)],
        scratch_shapes=[buf, buf, pltpu.SemaphoreType.DMA((2,))],
        compiler_params=_params("arbitrary"),
        name="combine",
    )(slot, slot, xa, xb, info, ys, g_final)


def _moe(xa, xb, w, tme=512):
    info, cnt = _route(xa, xb, w["g_moe"], w["wr_hi"], w["wr_lo"])
    slot, work = _route_meta(info, cnt, tme)
    xs = _scatter(slot, xa, xb)
    ys = _experts(work, xs, w["g_moe"], w["moe_wg"], w["moe_wu"], w["moe_wd"], tme)
    return _combine(slot, xa, xb, info, ys, w["g_final"])


def _layer(x3d, k4, v4, s5_h0, lru_h0, lru_conv, w, chunk, nb, tq):
    bsz, seq, _ = x3d.shape
    x2d = x3d.reshape(bsz * seq, D_MODEL)
    lru_w = (w["conv_w"], w["conv_b"], w["lru_wg"], w["lru_bg"], w["lru_lam"])
    tail = SUBLANES - (CONV_W - 1)
    if lru_h0 is None:
        u, yl, hl, xt = _mixlru(x2d, w["g_mix"], w["w_in"], *lru_w, bsz, seq)
        lru_h = hl.reshape(bsz, SUBLANES, D_LRU)[:, 0]
        conv_new = xt.reshape(bsz, SUBLANES, D_LRU)[:, tail:]
    else:
        u, xl, gl = _mixin(x2d, w["g_mix"], w["w_in"])
        hist = jnp.pad(lru_conv, ((0, 0), (tail, 0), (0, 0))).reshape(bsz * seq, D_LRU)
        yl, hall = _lru_dec(xl, gl, hist, jnp.repeat(lru_h0, seq, axis=0), *lru_w)
        lru_h = hall.reshape(bsz, seq, D_LRU)[:, seq - 1]
        conv_new = xl.reshape(bsz, seq, D_LRU)[:, seq - (CONV_W - 1):]

    nstate = S5_GROUPS * S5_STATE
    if s5_h0 is None:
        zero = jnp.zeros((bsz, 1, nstate), F32)
        ys, s5_re, s5_im = _s5(u, w["s5"][chunk], zero, zero, chunk, seq // chunk, True)
    else:
        ys, s5_re, s5_im = _s5(u, w["s5"][chunk], s5_h0[0].reshape(bsz, nstate), s5_h0[1].reshape(bsz, nstate),
                               chunk, bsz, False)
    s5_re = s5_re.reshape(bsz, S5_GROUPS, S5_STATE)
    s5_im = s5_im.reshape(bsz, S5_GROUPS, S5_STATE)

    x1, q = _mixout(x2d, ys, yl, w["w_glu"], w["b_glu"], w["w_out"], w["g_xa"], w["w_q"])
    x2t = _attn(q.reshape(bsz, seq, D_MODEL), x1.reshape(bsz, seq, D_MODEL), k4, v4, w["w_o"], nb, tq)
    return x2t, s5_re, s5_im, lru_h, conv_new


def kernel(x_prompt, x_sample, mem_prompt, cache_mem_k, cache_mem_v, state_s5_re, state_s5_im, state_lru_h, state_lru_conv, g_mix, w_in, s5_lam_re, s5_lam_im, s5_log_dt, s5_b_re, s5_b_im, s5_c_re, s5_c_im, s5_d, s5_w_glu, s5_b_glu, lru_conv_w, lru_conv_b, lru_w_a, lru_b_a, lru_w_x, lru_b_x, lru_lam, w_out, g_xa, g_mem, xa_w_q, xa_w_k, xa_w_v, xa_w_o, g_moe, moe_w_group, moe_w_expert, moe_w_gate, moe_w_up, moe_w_down, g_final):
    depth = g_mix.shape[0]
    assert depth == 1, "single-layer step"
    l = 0
    bsz, seq, _ = x_prompt.shape
    dbsz, dseq, _ = x_sample.shape
    chunk = SUBLANES
    assert seq % chunk == 0 and dseq == chunk

    s5_args = (s5_lam_re[l], s5_lam_im[l], s5_log_dt[l], s5_b_re[l], s5_b_im[l], s5_c_re[l], s5_c_im[l], s5_d[l])
    wg, bg = _lru_weights(lru_w_a[l], lru_w_x[l], lru_b_a[l], lru_b_x[l])
    wr = jnp.concatenate([moe_w_group[l], moe_w_expert[l],
                          jnp.zeros((D_MODEL, ROUTER_LANES - N_EXPERT_GROUPS - N_EXPERTS), F32)], axis=1)
    wr_hi = wr.astype(BF16)
    wr_lo = (wr - wr_hi.astype(F32)).astype(BF16)
    w = {
        "g_mix": g_mix[l][None], "w_in": w_in[l].astype(BF16),
        "s5": {chunk: _s5_weights(chunk, *s5_args)},
        "conv_w": lru_conv_w[l], "conv_b": lru_conv_b[l][None], "lru_wg": wg, "lru_bg": bg,
        "lru_lam": lru_lam[l][None],
        "w_glu": s5_w_glu[l].astype(BF16), "b_glu": s5_b_glu[l][None], "w_out": w_out[l].astype(BF16),
        "g_xa": g_xa[l][None], "w_q": xa_w_q[l].astype(BF16), "w_o": xa_w_o[l].astype(BF16),
        "g_moe": g_moe[l][None], "wr_hi": wr_hi, "wr_lo": wr_lo,
        "moe_wg": moe_w_gate[l].astype(BF16), "moe_wu": moe_w_up[l].astype(BF16),
        "moe_wd": moe_w_down[l].astype(BF16), "g_final": g_final[None],
    }

    mk, mv = _memkv(mem_prompt.reshape(bsz * N_MEM, D_MODEL), g_mem[l][None],
                    xa_w_k[l].astype(BF16), xa_w_v[l].astype(BF16))
    xp, p_re, p_im, p_h, p_conv = _layer(x_prompt, mk.reshape(bsz, N_MEM, D_MODEL), mv.reshape(bsz, N_MEM, D_MODEL),
                                         None, None, None, w, chunk, nb=1, tq=1024)
    xs, s_re, s_im, s_h, s_conv = _layer(x_sample, cache_mem_k[l], cache_mem_v[l],
                                         (state_s5_re[l], state_s5_im[l]), state_lru_h[l],
                                         state_lru_conv[l], w, chunk, nb=8, tq=dseq)
    yp, ysmp = _moe(xp, xs, w)
    yp = yp.reshape(bsz, seq, D_MODEL)
    ysmp = ysmp.reshape(dbsz, dseq, D_MODEL)

    return (yp, ysmp,
            mk.reshape(1, bsz, N_MEM, XA_HEADS, XA_HEAD_DIM), mv.reshape(1, bsz, N_MEM, XA_HEADS, XA_HEAD_DIM),
            p_re[None], p_im[None], p_h[None], p_conv[None],
            s_re[None], s_im[None], s_h[None], s_conv[None])
```

```python
import functools

import jax
import jax.numpy as jnp
import numpy as np
from jax import lax
from jax.experimental import pallas as pl
from jax.experimental.pallas import tpu as pltpu

F32 = jnp.float32
BF16 = jnp.bfloat16

D_MODEL = 1024
D_S5 = 512
D_LRU = 512
S5_GROUP = 16
S5_GROUPS = 32
S5_STATE = 64
LRU_HEADS = 8
LRU_HEAD_DIM = 64
CONV_W = 4
LRU_C = 8.0
N_MEM = 256
XA_HEADS = 4
XA_HEAD_DIM = 256
N_EXPERT_GROUPS = 4
EXPERTS_PER_GROUP = 8
N_EXPERTS = 32
D_FF_EXPERT = 256
EPS = 1e-6

SUBLANES = 8
LANES = 128
ROUTER_LANES = 128
VMEM_LIMIT = 56 * 1024 * 1024


def _params(*sem):
    return pltpu.CompilerParams(dimension_semantics=sem, vmem_limit_bytes=VMEM_LIMIT)


def _rms(x, g):
    ms = jnp.mean(x * x, axis=-1, keepdims=True)
    return x * lax.rsqrt(ms + EPS) * g


def _bdot(a, b):
    return jnp.dot(a, b, preferred_element_type=F32)


def _memkv_body(m_ref, g_ref, wk_ref, wv_ref, k_ref, v_ref):
    hb = _rms(m_ref[...], g_ref[...]).astype(BF16)
    k_ref[...] = _bdot(hb, wk_ref[...])
    v_ref[...] = _bdot(hb, wv_ref[...])


def _memkv(mem2d, g_mem, wk, wv, tm=512):
    t = mem2d.shape[0]
    row = pl.BlockSpec((tm, D_MODEL), lambda i: (i, 0))
    full = pl.BlockSpec((D_MODEL, D_MODEL), lambda i: (0, 0))
    return pl.pallas_call(
        _memkv_body,
        grid=(t // tm,),
        in_specs=[row, pl.BlockSpec((1, D_MODEL), lambda i: (0, 0)), full, full],
        out_specs=[row, row],
        out_shape=[jax.ShapeDtypeStruct((t, D_MODEL), F32)] * 2,
        compiler_params=_params("parallel"),
        name="memkv",
    )(mem2d, g_mem, wk, wv)


def _mixin_body(x_ref, g_ref, w_ref, u_ref, xl_ref, gl_ref):
    hb = _rms(x_ref[...], g_ref[...]).astype(BF16)
    proj = _bdot(hb, w_ref[...])
    u_ref[...] = proj[:, :D_S5]
    xl_ref[...] = proj[:, D_S5:D_S5 + D_LRU]
    gl_ref[...] = proj[:, D_S5 + D_LRU:]


def _mixin(x2d, g_mix, w_in, tm=1024):
    t = x2d.shape[0]
    half = pl.BlockSpec((tm, D_S5), lambda i: (i, 0))
    return pl.pallas_call(
        _mixin_body,
        grid=(t // tm,),
        in_specs=[pl.BlockSpec((tm, D_MODEL), lambda i: (i, 0)),
                  pl.BlockSpec((1, D_MODEL), lambda i: (0, 0)),
                  pl.BlockSpec((D_MODEL, D_S5 + 2 * D_LRU), lambda i: (0, 0))],
        out_specs=[half, half, half],
        out_shape=[jax.ShapeDtypeStruct((t, D_S5), F32)] * 3,
        compiler_params=_params("parallel"),
        name="mixin",
    )(x2d, g_mix, w_in)


GROUPS_PER_BLOCK = LANES // S5_GROUP
PAIRS_PER_BLOCK = GROUPS_PER_BLOCK // 2
STATE_BLOCK = GROUPS_PER_BLOCK * S5_STATE
PW_ROWS = 24


def _s5_body(u_ref, wa_ref, wh_ref, pw_ref, d_ref, h0r_ref, h0i_ref, y_ref, hfr_ref, hfi_ref, hpr_scr, hpi_scr,
             *, chunk, nrow, scan):
    nh = chunk // SUBLANES
    lc = chunk * S5_GROUP
    slot = lax.broadcasted_iota(jnp.int32, (nrow, LANES), 1) // S5_GROUP
    in_slot = [slot == s for s in range(GROUPS_PER_BLOCK)]

    def pick(src, sel):
        out = src[sel(0)]
        for s in range(1, GROUPS_PER_BLOCK):
            out = jnp.where(in_slot[s], src[sel(s)], out)
        return out

    nat, skew = [], []
    for t in range(chunk):
        a = u_ref[pl.ds(t, nrow, stride=chunk), :]
        nat.append(a)
        s = t % GROUPS_PER_BLOCK
        skew.append(pltpu.roll(a, s * S5_GROUP, axis=1) if s else a)
    def lane_roll(x, slots):
        slots %= GROUPS_PER_BLOCK
        return pltpu.roll(x, slots * S5_GROUP, axis=1) if slots else x

    z = [[lane_roll(pick(skew, lambda s, hh=hh, m=m: SUBLANES * hh + (s - m) % GROUPS_PER_BLOCK), -m)
          for hh in range(nh)] for m in range(GROUPS_PER_BLOCK)]

    ys, er, ei = [], [], []
    for p in range(PAIRS_PER_BLOCK):
        lhs = jnp.concatenate(z[2 * p] + z[2 * p + 1], axis=1).astype(BF16)
        full = _bdot(lhs, wa_ref[p])
        ys.append(full[:, :2 * lc])
        er.append(full[:, 2 * lc:2 * lc + LANES])
        ei.append(full[:, 2 * lc + LANES:])
    er = jnp.concatenate(er, axis=1)
    ei = jnp.concatenate(ei, axis=1)

    if scan:
        ntile = nrow // SUBLANES
        xr = er.reshape(ntile, SUBLANES, STATE_BLOCK)
        xi = ei.reshape(ntile, SUBLANES, STATE_BLOCK)
        row = lax.broadcasted_iota(jnp.int32, xr.shape, 1)
        for k, d in enumerate((1, 2, 4)):
            pr, pi = pw_ref[0, k:k + 1, :], pw_ref[0, 4 + k:5 + k, :]
            sr, si = pltpu.roll(xr, d, axis=1), pltpu.roll(xi, d, axis=1)
            m = row >= d
            xr, xi = jnp.where(m, xr + pr * sr - pi * si, xr), jnp.where(m, xi + pr * si + pi * sr, xi)
        sxr = jnp.where(row >= 1, pltpu.roll(xr, 1, axis=1), 0.0)
        sxi = jnp.where(row >= 1, pltpu.roll(xi, 1, axis=1), 0.0)
        p8r, p8i = pw_ref[0, 3:4, :], pw_ref[0, 7:8, :]
        qr, qi = pw_ref[0, 8:16, :], pw_ref[0, 16:24, :]
        hr, hi = h0r_ref[0], h0i_ref[0]
        for t in range(ntile):
            rows = slice(SUBLANES * t, SUBLANES * (t + 1))
            hpr_scr[rows, :] = sxr[t] + qr * hr - qi * hi
            hpi_scr[rows, :] = sxi[t] + qr * hi + qi * hr
            hr, hi = (xr[t, SUBLANES - 1:] + p8r * hr - p8i * hi, xi[t, SUBLANES - 1:] + p8r * hi + p8i * hr)
        hfr_ref[0] = hr
        hfi_ref[0] = hi
        hpr, hpi = hpr_scr[...], hpi_scr[...]
    else:
        hpr, hpi = h0r_ref[...], h0i_ref[...]
        p1r, p1i = pw_ref[0, 0:1, :], pw_ref[0, 4:5, :]
        hfr_ref[...] = er + p1r * hpr - p1i * hpi
        hfi_ref[...] = ei + p1r * hpi + p1i * hpr

    yg = []
    for p in range(PAIRS_PER_BLOCK):
        lanes = slice(LANES * p, LANES * (p + 1))
        hp = jnp.concatenate([hpr[:, lanes], hpi[:, lanes]], axis=1).astype(BF16)
        out = ys[p] + _bdot(hp, wh_ref[p])
        for half in range(2):
            q = 2 * p + half
            yg.append([lane_roll(out[:, half * lc + hh * LANES:half * lc + (hh + 1) * LANES], q)
                       for hh in range(nh)])
    d = d_ref[...]
    for t in range(chunk):
        hh, s = divmod(t, GROUPS_PER_BLOCK)
        c = lane_roll(pick([g[hh] for g in yg], lambda sl, s=s: (sl - s) % GROUPS_PER_BLOCK), -s)
        y_ref[pl.ds(t, nrow, stride=chunk), :] = c + d * nat[t]


def _s5(u2d, weights, h0r, h0i, chunk, nrow, scan):
    wa, wh, pw, dvec = weights
    t = u2d.shape[0]
    lc = chunk * S5_GROUP
    rows = nrow * chunk
    nblk = D_S5 // LANES
    body = functools.partial(_s5_body, chunk=chunk, nrow=nrow, scan=scan)
    ublk = pl.BlockSpec((rows, LANES), lambda i, j: (i, j))
    if scan:
        hblk = pl.BlockSpec((1, 1, STATE_BLOCK), lambda i, j: (i, 0, j))
        hshape = jax.ShapeDtypeStruct((t // rows, 1, S5_GROUPS * S5_STATE), F32)
    else:
        hblk = pl.BlockSpec((nrow, STATE_BLOCK), lambda i, j: (i, j))
        hshape = jax.ShapeDtypeStruct((t // chunk, S5_GROUPS * S5_STATE), F32)
    return pl.pallas_call(
        body,
        grid=(t // rows, nblk),
        in_specs=[ublk,
                  pl.BlockSpec((PAIRS_PER_BLOCK, 2 * lc, 2 * lc + 2 * LANES), lambda i, j: (j, 0, 0)),
                  pl.BlockSpec((PAIRS_PER_BLOCK, 2 * LANES, 2 * lc), lambda i, j: (j, 0, 0)),
                  pl.BlockSpec((1, PW_ROWS, STATE_BLOCK), lambda i, j: (j, 0, 0)),
                  pl.BlockSpec((1, LANES), lambda i, j: (0, j)),
                  hblk, hblk],
        out_specs=[ublk, hblk, hblk],
        out_shape=[jax.ShapeDtypeStruct((t, D_S5), F32), hshape, hshape],
        scratch_shapes=[pltpu.VMEM((nrow, STATE_BLOCK), F32), pltpu.VMEM((nrow, STATE_BLOCK), F32)],
        compiler_params=_params("parallel", "parallel"),
        name="s5",
    )(u2d, wa, wh, pw, dvec, h0r, h0i)


def _s5_weights(chunk, lam_re, lam_im, log_dt, b_re, b_im, c_re, c_im, d_skip):
    hi = lax.Precision.HIGHEST
    dt = jnp.exp(log_dt)[:, None]
    mag = jnp.exp(lam_re * dt)
    ab_re = mag * jnp.cos(lam_im * dt)
    ab_im = mag * jnp.sin(lam_im * dt)
    den = lam_re * lam_re + lam_im * lam_im
    nr = ab_re - 1.0
    f_re = (nr * lam_re + ab_im * lam_im) / den
    f_im = (ab_im * lam_re - nr * lam_im) / den
    bb_re = f_re[..., None] * b_re - f_im[..., None] * b_im
    bb_im = f_re[..., None] * b_im + f_im[..., None] * b_re
    def powers(ks):
        k = jnp.asarray(np.asarray(ks, np.float32))[:, None, None]
        m = jnp.exp(k * (lam_re * dt))
        return m * jnp.cos(k * (lam_im * dt)), m * jnp.sin(k * (lam_im * dt))

    pw_re, pw_im = powers(range(chunk + 1))
    pb_re = pw_re[:chunk, ..., None] * bb_re - pw_im[:chunk, ..., None] * bb_im
    pb_im = pw_re[:chunk, ..., None] * bb_im + pw_im[:chunk, ..., None] * bb_re
    kk = (jnp.einsum("gon,kgni->kgoi", c_re, pb_re, precision=hi)
          - jnp.einsum("gon,kgni->kgoi", c_im, pb_im, precision=hi))
    npair = S5_GROUPS // 2
    lc = chunk * S5_GROUP
    row0 = kk.transpose(1, 3, 0, 2).reshape(S5_GROUPS, S5_GROUP, lc)
    row0 = jnp.pad(row0, ((0, 0), (0, 0), (lc, 0)))
    conv = jnp.stack([row0[:, :, lc - S5_GROUP * ti:2 * lc - S5_GROUP * ti] for ti in range(chunk)], axis=1)
    conv = conv.reshape(npair, 2, lc, lc)
    ends = [x[::-1].transpose(1, 0, 3, 2).reshape(npair, 2, lc, S5_STATE) for x in (pb_re, pb_im)]
    zc = jnp.zeros((npair, lc, lc), F32)
    zs = jnp.zeros((npair, lc, S5_STATE), F32)
    wa = jnp.concatenate([
        jnp.concatenate([conv[:, 0], zc, ends[0][:, 0], zs, ends[1][:, 0], zs], axis=-1),
        jnp.concatenate([zc, conv[:, 1], zs, ends[0][:, 1], zs, ends[1][:, 1]], axis=-1)], axis=1)
    gr = c_re[None] * pw_re[1:, :, None, :] - c_im[None] * pw_im[1:, :, None, :]
    gi = c_re[None] * pw_im[1:, :, None, :] + c_im[None] * pw_re[1:, :, None, :]
    zr = jnp.zeros((npair, S5_STATE, lc), F32)
    blocks = []
    for x in (gr, -gi):
        hy = x.transpose(1, 3, 0, 2).reshape(npair, 2, S5_STATE, lc)
        blocks += [jnp.concatenate([hy[:, 0], zr], axis=-1), jnp.concatenate([zr, hy[:, 1]], axis=-1)]
    wh = jnp.concatenate(blocks, axis=1)
    tab = [1, 2, 4, 8] + list(range(SUBLANES))
    qr, qi = powers([chunk * k for k in tab])
    nblk = D_S5 // LANES
    pw = jnp.concatenate([qr[:4], qi[:4], qr[4:], qi[4:]]).reshape(PW_ROWS, nblk, STATE_BLOCK).transpose(1, 0, 2)
    return wa.astype(BF16), wh.astype(BF16), pw, d_skip.reshape(1, D_S5)


def _lru_ab(xc, wg_ref, bg_ref, lam_ref):
    xb = xc.astype(BF16)
    half = D_LRU // 2
    g0 = _bdot(xb[:, :half], wg_ref[0]) + bg_ref[0]
    g1 = _bdot(xb[:, half:], wg_ref[1]) + bg_ref[1]
    r = jax.nn.sigmoid(jnp.concatenate([g0[:, :half], g1[:, :half]], axis=1))
    ig = jax.nn.sigmoid(jnp.concatenate([g0[:, half:], g1[:, half:]], axis=1))
    zl = -lam_ref[...]
    softplus = jnp.maximum(zl, 0.0) + jnp.log1p(jnp.exp(-jnp.abs(zl)))
    log_a = -LRU_C * r * softplus
    a = jnp.exp(log_a)
    b = jnp.sqrt(-jnp.tanh(log_a) * (a * a + 1.0)) * (ig * xc)
    return a, b


def _tile_scan(a, b):
    shape = a.shape
    a = a.reshape(shape[0] // SUBLANES, SUBLANES, shape[1])
    b = b.reshape(a.shape)
    row = lax.broadcasted_iota(jnp.int32, a.shape, 1)
    for d in (1, 2, 4):
        a_prev = pltpu.roll(a, d, axis=1)
        b_prev = pltpu.roll(b, d, axis=1)
        m = row >= d
        b = jnp.where(m, b + a * b_prev, b)
        a = jnp.where(m, a * a_prev, a)
    return a.reshape(shape), b.reshape(shape)


def _mixlru_body(x_ref, gmix_ref, win_ref, cw_ref, cb_ref, wg_ref, bg_ref, lam_ref, u_ref, y_ref, hl_ref, xt_ref,
                 xp_scr, a_scr, b_scr, h_scr, hc_scr, *, ts):
    ti = pl.program_id(1)

    @pl.when(ti == 0)
    def _():
        xp_scr[0:SUBLANES, :] = jnp.zeros((SUBLANES, D_LRU), F32)
        hc_scr[...] = jnp.zeros((SUBLANES, D_LRU), F32)

    proj = _bdot(_rms(x_ref[...], gmix_ref[...]).astype(BF16), win_ref[...])
    u_ref[...] = proj[:, :D_S5]
    xl = proj[:, D_S5:D_S5 + D_LRU]
    gl = proj[:, D_S5 + D_LRU:]
    xt_ref[...] = xl[ts - SUBLANES:, :]
    xp_scr[SUBLANES:SUBLANES + ts, :] = xl
    xc = cb_ref[...] + xl * cw_ref[CONV_W - 1:CONV_W, :]
    for j in range(1, CONV_W):
        xc = xc + xp_scr[SUBLANES - j:SUBLANES - j + ts, :] * cw_ref[CONV_W - 1 - j:CONV_W - j, :]
    xp_scr[0:SUBLANES, :] = xl[ts - SUBLANES:, :]
    a, b = _lru_ab(xc, wg_ref, bg_ref, lam_ref)
    a, b = _tile_scan(a, b)
    a_scr[...] = a
    b_scr[...] = b

    def step(i, hin):
        rows = pl.ds(pl.multiple_of(i * SUBLANES, SUBLANES), SUBLANES)
        h = b_scr[rows, :] + a_scr[rows, :] * hin
        h_scr[rows, :] = h
        return h[SUBLANES - 1:SUBLANES, :]

    hlast = lax.fori_loop(0, ts // SUBLANES, step, hc_scr[0:1, :], unroll=4)
    hc_scr[...] = jnp.broadcast_to(hlast, (SUBLANES, D_LRU))
    hl_ref[...] = hc_scr[...]
    y_ref[...] = h_scr[...] * jax.nn.gelu(gl)


def _lru_dec_body(xl_ref, gl_ref, hist_ref, h0_ref, cw_ref, cb_ref, wg_ref, bg_ref, lam_ref, y_ref, hl_ref, *, tm):
    xl = xl_ref[...]
    hist = hist_ref[...]
    row = lax.broadcasted_iota(jnp.int32, xl.shape, 0) % SUBLANES
    xc = cb_ref[...] + xl * cw_ref[CONV_W - 1:CONV_W, :]
    for j in range(1, CONV_W):
        prev = jnp.where(row >= j, pltpu.roll(xl, j, axis=0), pltpu.roll(hist, tm - SUBLANES + j, axis=0))
        xc = xc + prev * cw_ref[CONV_W - 1 - j:CONV_W - j, :]
    a, b = _lru_ab(xc, wg_ref, bg_ref, lam_ref)
    a, b = _tile_scan(a, b)
    h = b + a * h0_ref[...]
    hl_ref[...] = h
    y_ref[...] = h * jax.nn.gelu(gl_ref[...])


def _lru_weights(w_a, w_x, b_a, b_x):
    eye = jnp.eye(LRU_HEADS, dtype=F32)
    bd_a = jnp.einsum("hij,hg->higj", w_a, eye).reshape(D_LRU, D_LRU)
    bd_x = jnp.einsum("hij,hg->higj", w_x, eye).reshape(D_LRU, D_LRU)
    half = D_LRU // 2
    wg = jnp.stack([jnp.concatenate([bd_a[:half, :half], bd_x[:half, :half]], axis=1),
                    jnp.concatenate([bd_a[half:, half:], bd_x[half:, half:]], axis=1)]).astype(BF16)
    ba = b_a.reshape(1, D_LRU)
    bx = b_x.reshape(1, D_LRU)
    bg = jnp.stack([jnp.concatenate([ba[:, :half], bx[:, :half]], axis=1),
                    jnp.concatenate([ba[:, half:], bx[:, half:]], axis=1)])
    return wg, bg


def _lru_common_specs(const):
    return [const((CONV_W, D_LRU)), const((1, D_LRU)), const((2, D_LRU // 2, D_LRU)),
            const((2, 1, D_LRU)), const((1, D_LRU))]


def _mixlru(x2d, g_mix, w_in, cw, cb, wg, bg, lam, bsz, seq, ts=512):
    nt = seq // ts
    blk = pl.BlockSpec((ts, D_LRU), lambda b, t: (b * nt + t, 0))
    per_seq = pl.BlockSpec((SUBLANES, D_LRU), lambda b, t: (b, 0))

    def const(shape):
        return pl.BlockSpec(shape, lambda b, t: (0,) * len(shape))

    body = functools.partial(_mixlru_body, ts=ts)
    tile = jax.ShapeDtypeStruct((bsz * SUBLANES, D_LRU), F32)
    return pl.pallas_call(
        body,
        grid=(bsz, nt),
        in_specs=[pl.BlockSpec((ts, D_MODEL), lambda b, t: (b * nt + t, 0)), const((1, D_MODEL)),
                  const((D_MODEL, D_S5 + 2 * D_LRU))] + _lru_common_specs(const),
        out_specs=[blk, blk, per_seq, per_seq],
        out_shape=[jax.ShapeDtypeStruct((bsz * seq, D_S5), F32), jax.ShapeDtypeStruct((bsz * seq, D_LRU), F32),
                   tile, tile],
        scratch_shapes=[pltpu.VMEM((ts + SUBLANES, D_LRU), F32), pltpu.VMEM((ts, D_LRU), F32),
                        pltpu.VMEM((ts, D_LRU), F32), pltpu.VMEM((ts, D_LRU), F32),
                        pltpu.VMEM((SUBLANES, D_LRU), F32)],
        compiler_params=_params("arbitrary", "arbitrary"),
        name="mixlru",
    )(x2d, g_mix, w_in, cw, cb, wg, bg, lam)


def _lru_dec(xl, gl, hist, h0rep, cw, cb, wg, bg, lam, tm=256):
    t = xl.shape[0]
    blk = pl.BlockSpec((tm, D_LRU), lambda i: (i, 0))

    def const(shape):
        return pl.BlockSpec(shape, lambda i: (0,) * len(shape))

    body = functools.partial(_lru_dec_body, tm=tm)
    return pl.pallas_call(
        body,
        grid=(t // tm,),
        in_specs=[blk, blk, blk, blk] + _lru_common_specs(const),
        out_specs=[blk, blk],
        out_shape=[jax.ShapeDtypeStruct((t, D_LRU), F32)] * 2,
        compiler_params=_params("parallel"),
        name="lru_dec",
    )(xl, gl, hist, h0rep, cw, cb, wg, bg, lam)


def _softmax(sc):
    p = jnp.exp(sc - jnp.max(sc, axis=-1, keepdims=True))
    return p / jnp.sum(p, axis=-1, keepdims=True)


def _attend(q, k_ref, v_ref, i, tq):
    scale = XA_HEAD_DIM ** -0.5
    heads = [q[:, h * XA_HEAD_DIM:(h + 1) * XA_HEAD_DIM] for h in range(XA_HEADS)]
    if len(k_ref.shape) == 4:
        kf = k_ref[i].reshape(N_MEM * XA_HEADS, XA_HEAD_DIM).astype(BF16)
        vf = v_ref[i].reshape(N_MEM * XA_HEADS, XA_HEAD_DIM).astype(BF16)
        sc = lax.dot_general(jnp.concatenate(heads, axis=0), kf, (((1,), (1,)), ((), ())),
                             preferred_element_type=F32) * scale
        own = (lax.broadcasted_iota(jnp.int32, sc.shape, 1) % XA_HEADS
               == lax.broadcasted_iota(jnp.int32, sc.shape, 0) // tq)
        oh = _bdot(_softmax(jnp.where(own, sc, -jnp.inf)).astype(BF16), vf)
        outs = [oh[h * tq:(h + 1) * tq] for h in range(XA_HEADS)]
    else:
        kb = k_ref[i].astype(BF16)
        vb = v_ref[i].astype(BF16)
        outs = []
        for h in range(XA_HEADS):
            cols = slice(h * XA_HEAD_DIM, (h + 1) * XA_HEAD_DIM)
            sc = lax.dot_general(heads[h], kb[:, cols], (((1,), (1,)), ((), ())),
                                 preferred_element_type=F32) * scale
            outs.append(_bdot(_softmax(sc).astype(BF16), vb[:, cols]))
    return jnp.concatenate(outs, axis=1)


def _mixattn_body(x_ref, ys_ref, yl_ref, k_ref, v_ref, wglu_ref, bglu_ref, wout_ref, gxa_ref, wq_ref, wo_ref,
                  x2t_ref, *, nb, tq):
    rows = nb * tq
    ys = jax.nn.gelu(ys_ref[...].reshape(rows, D_S5))
    gate = jax.nn.sigmoid(_bdot(ys.astype(BF16), wglu_ref[...]) + bglu_ref[...])
    s5 = (ys * gate).astype(BF16)
    yl = yl_ref[...].reshape(rows, D_LRU).astype(BF16)
    x1 = x_ref[...].reshape(rows, D_MODEL) + _bdot(s5, wout_ref[0:D_S5, :]) + _bdot(yl, wout_ref[D_S5:, :])
    q = _bdot(_rms(x1, gxa_ref[...]).astype(BF16), wq_ref[...]).astype(BF16)
    o = jnp.concatenate([_attend(q[i * tq:(i + 1) * tq], k_ref, v_ref, i, tq) for i in range(nb)], axis=0)
    _rows_to_tiles(x2t_ref, x1 + _bdot(o.astype(BF16), wo_ref[...]), rows)


def _mixattn(x3, ys3, yl3, k4, v4, w, nb, tq):
    bsz, seq, _ = x3.shape
    nt = seq // tq
    blk = pl.BlockSpec((nb, tq, D_MODEL), lambda b, t: (b, t, 0))
    half = pl.BlockSpec((nb, tq, D_S5), lambda b, t: (b, t, 0))
    kv = pl.BlockSpec((nb,) + k4.shape[1:], lambda b, t: (b,) + (0,) * (k4.ndim - 1))

    def const(shape):
        return pl.BlockSpec(shape, lambda b, t: (0,) * len(shape))

    body = functools.partial(_mixattn_body, nb=nb, tq=tq)
    return pl.pallas_call(
        body,
        grid=(bsz // nb, nt),
        in_specs=[blk, half, half, kv, kv, const((D_S5, D_S5)), const((1, D_S5)), const((D_MODEL, D_MODEL)),
                  const((1, D_MODEL)), const((D_MODEL, D_MODEL)), const((D_MODEL, D_MODEL))],
        out_specs=pl.BlockSpec((nb * tq * TILE_ROWS, LANES), lambda b, t: (b * nt + t, 0)),
        out_shape=jax.ShapeDtypeStruct((bsz * seq * TILE_ROWS, LANES), F32),
        compiler_params=_params("parallel", "parallel"),
        name="mixattn",
    )(x3, ys3, yl3, k4, v4, w["w_glu"], w["b_glu"], w["w_out"], w["g_xa"], w["w_q"], w["w_o"])


def _router(hm, wr_hi_ref, wr_lo_ref):
    a_hi = hm.astype(BF16)
    a_lo = (hm - a_hi.astype(F32)).astype(BF16)
    logits = _bdot(a_hi, wr_hi_ref[...]) + (_bdot(a_hi, wr_lo_ref[...]) + _bdot(a_lo, wr_hi_ref[...]))
    lane_i = lax.broadcasted_iota(jnp.int32, logits.shape, 1)
    lane = lane_i.astype(F32)
    neg = -jnp.inf
    big = float(ROUTER_LANES)
    is_g = lane_i < N_EXPERT_GROUPS
    glog = jnp.where(is_g, logits, neg)
    gmax = jnp.max(glog, axis=-1, keepdims=True)
    gsel = jnp.min(jnp.where(glog == gmax, lane, big), axis=-1, keepdims=True)
    pg_sel = 1.0 / jnp.sum(jnp.where(is_g, jnp.exp(logits - gmax), 0.0), axis=-1, keepdims=True)
    eidx = lane_i - N_EXPERT_GROUPS
    in_group = (eidx >= 0) & (eidx < N_EXPERTS) & ((eidx >> 3).astype(F32) == gsel)
    el = jnp.where(in_group, logits, neg)
    v1 = jnp.max(el, axis=-1, keepdims=True)
    i1 = jnp.min(jnp.where(el == v1, lane, big), axis=-1, keepdims=True)
    el2 = jnp.where(lane == i1, neg, el)
    v2 = jnp.max(el2, axis=-1, keepdims=True)
    i2 = jnp.min(jnp.where(el2 == v2, lane, big), axis=-1, keepdims=True)
    e2 = jnp.exp(v2 - v1)
    w1 = pg_sel / (1.0 + e2)
    w2 = pg_sel * e2 / (1.0 + e2)
    return i1, i2, w1, w2


TILE_ROWS = D_MODEL // LANES


def _tiles_to_rows(ref, n, base=0):
    return jnp.concatenate([ref[pl.ds(base + s, n, stride=TILE_ROWS), :] for s in range(TILE_ROWS)], axis=1)


def _rows_to_tiles(ref, val, n, base=0):
    for s in range(TILE_ROWS):
        ref[pl.ds(base + s, n, stride=TILE_ROWS), :] = val[:, s * LANES:(s + 1) * LANES]


def _two_part_specs(block_rows, n_a, n_b):
    return [pl.BlockSpec((block_rows, LANES), lambda i: (jnp.minimum(i, n_a - 1), 0)),
            pl.BlockSpec((block_rows, LANES), lambda i: (jnp.clip(i - n_a, 0, n_b - 1), 0))]


def _on_part(n_a, fn, *ref_pairs):
    i = pl.program_id(0)
    pl.when(i < n_a)(lambda: fn(*[p[0] for p in ref_pairs]))
    pl.when(i >= n_a)(lambda: fn(*[p[1] for p in ref_pairs]))


def _route_body(xa_ref, xb_ref, gm_ref, wr_hi_ref, wr_lo_ref, info_ref, cnt_ref, cnt_scr, *, tm, n_a):
    @pl.when(pl.program_id(0) == 0)
    def _():
        cnt_scr[...] = jnp.zeros_like(cnt_scr)

    _on_part(n_a, functools.partial(_route_step, gm_ref, wr_hi_ref, wr_lo_ref, info_ref, cnt_ref, cnt_scr, tm),
             (xa_ref, xb_ref))


def _route_step(gm_ref, wr_hi_ref, wr_lo_ref, info_ref, cnt_ref, cnt_scr, tm, x2t_ref):
    hm = _rms(_tiles_to_rows(x2t_ref, tm), gm_ref[...])
    i1, i2, w1, w2 = _router(hm, wr_hi_ref, wr_lo_ref)
    lane_i = lax.broadcasted_iota(jnp.int32, (tm, ROUTER_LANES), 1)
    lane = lane_i.astype(F32)
    chosen = ((lane == i1) | (lane == i2)).astype(F32)
    earlier = (lax.broadcasted_iota(jnp.int32, (tm, tm), 1) < lax.broadcasted_iota(jnp.int32, (tm, tm), 0))
    before = _bdot(earlier.astype(BF16), chosen.astype(BF16)) + cnt_scr[0:1, :]
    r1 = jnp.sum(jnp.where(lane == i1, before, 0.0), axis=-1, keepdims=True)
    r2 = jnp.sum(jnp.where(lane == i2, before, 0.0), axis=-1, keepdims=True)
    cols = (i1, i2, w1, w2, r1, r2)
    info = jnp.zeros((tm, ROUTER_LANES), F32)
    for j, col in enumerate(cols):
        info = jnp.where(lane_i == j, col, info)
    info_ref[...] = info
    cnt_scr[...] = cnt_scr[...] + jnp.sum(chosen, axis=0, keepdims=True)
    cnt_ref[...] = cnt_scr[...]


def _route(xa, xb, g_moe, wr_hi, wr_lo, tm=512):
    n_a, n_b = xa.shape[0] // (tm * TILE_ROWS), xb.shape[0] // (tm * TILE_ROWS)
    t = (n_a + n_b) * tm

    def const(shape):
        return pl.BlockSpec(shape, lambda i: (0,) * len(shape))

    body = functools.partial(_route_body, tm=tm, n_a=n_a)
    return pl.pallas_call(
        body,
        grid=(n_a + n_b,),
        in_specs=_two_part_specs(tm * TILE_ROWS, n_a, n_b) + [
            const((1, D_MODEL)), const((D_MODEL, ROUTER_LANES)), const((D_MODEL, ROUTER_LANES))],
        out_specs=[pl.BlockSpec((tm, ROUTER_LANES), lambda i: (i, 0)), const((SUBLANES, ROUTER_LANES))],
        out_shape=[jax.ShapeDtypeStruct((t, ROUTER_LANES), F32),
                   jax.ShapeDtypeStruct((SUBLANES, ROUTER_LANES), F32)],
        scratch_shapes=[pltpu.VMEM((SUBLANES, ROUTER_LANES), F32)],
        compiler_params=_params("arbitrary"),
        name="route",
    )(xa, xb, g_moe, wr_hi, wr_lo)


def _route_meta(info, cnt, tme):
    t = info.shape[0]
    n_tiles = 2 * t // tme
    counts = cnt[0, N_EXPERT_GROUPS:N_EXPERT_GROUPS + N_EXPERTS].astype(jnp.int32)
    ends = jnp.cumsum(counts)
    starts = ends - counts
    e = info[:, 0:2].astype(jnp.int32) - N_EXPERT_GROUPS
    onehot = e[:, :, None] == jnp.arange(N_EXPERTS, dtype=jnp.int32)
    slot = (jnp.sum(jnp.where(onehot, starts, 0), axis=-1) + info[:, 4:6].astype(jnp.int32)).reshape(-1)
    pts = jnp.concatenate([jnp.arange(n_tiles, dtype=jnp.int32) * tme, starts])
    n = pts.shape[0]
    idx = jnp.arange(n, dtype=jnp.int32)
    pos = jnp.sum((pts[None, :] < pts[:, None]) | ((pts[None, :] == pts[:, None]) & (idx[None, :] < idx[:, None])),
                  axis=1)
    lo_abs = jnp.sum(jnp.where(pos[:, None] == idx[None, :], pts[:, None], 0), axis=0)
    hi_abs = jnp.concatenate([lo_abs[1:], jnp.full((1,), 2 * t, jnp.int32)])
    tile = jnp.minimum(lo_abs // tme, n_tiles - 1)
    expert = jnp.clip(jnp.sum(starts[None, :] <= lo_abs[:, None], axis=1) - 1, 0, N_EXPERTS - 1)
    first = jnp.concatenate([jnp.ones((1,), jnp.int32), (tile[1:] != tile[:-1]).astype(jnp.int32)])
    last = jnp.concatenate([(tile[1:] != tile[:-1]).astype(jnp.int32), jnp.ones((1,), jnp.int32)])
    work = [a.astype(jnp.int32) for a in (tile, expert, lo_abs - tile * tme, hi_abs - tile * tme, first, last)]
    return slot.astype(jnp.int32), work


def _tile_copy(src, src_tok, dst, dst_tok, sem):
    return pltpu.make_async_copy(src.at[pl.ds(pl.multiple_of(src_tok * TILE_ROWS, TILE_ROWS), TILE_ROWS), :],
                                 dst.at[pl.ds(pl.multiple_of(dst_tok * TILE_ROWS, TILE_ROWS), TILE_ROWS), :], sem)


def _scatter_body(slot_ref, xa_ref, xb_ref, xs_hbm, sem, *, tm, n_a):
    _on_part(n_a, functools.partial(_scatter_step, slot_ref, xs_hbm, sem, tm), (xa_ref, xb_ref))


def _scatter_step(slot_ref, xs_hbm, sem, tm, x2t_ref):
    def issue(r, c):
        _tile_copy(x2t_ref, r, xs_hbm, slot_ref[2 * r], sem).start(priority=0)
        _tile_copy(x2t_ref, r, xs_hbm, slot_ref[2 * r + 1], sem).start(priority=1)
        return c

    lax.fori_loop(0, tm, issue, 0, unroll=8)

    def drain(r, c):
        _tile_copy(x2t_ref, r, xs_hbm, 0, sem).wait()
        _tile_copy(x2t_ref, r, xs_hbm, 0, sem).wait()
        return c

    lax.fori_loop(0, tm, drain, 0, unroll=8)


def _scatter(slot, xa, xb, tm=512):
    n_a, n_b = xa.shape[0] // (tm * TILE_ROWS), xb.shape[0] // (tm * TILE_ROWS)
    t = (n_a + n_b) * tm
    body = functools.partial(_scatter_body, tm=tm, n_a=n_a)
    return pl.pallas_call(
        body,
        grid=(n_a + n_b,),
        in_specs=[pl.BlockSpec((2 * tm,), lambda i: (i,), memory_space=pltpu.SMEM)]
        + _two_part_specs(tm * TILE_ROWS, n_a, n_b),
        out_specs=pl.BlockSpec(memory_space=pl.ANY),
        out_shape=jax.ShapeDtypeStruct((2 * t * TILE_ROWS, LANES), F32),
        scratch_shapes=[pltpu.SemaphoreType.DMA(())],
        compiler_params=_params("arbitrary"),
        name="scatter",
    )(slot, xa, xb)


def _expert_body(tile_ref, exp_ref, lo_ref, hi_ref, first_ref, last_ref, xs_ref, gm_ref, wg_ref, wu_ref, wd_ref,
                 ys_ref, hb_scr, acc_scr, *, tme):
    k = pl.program_id(0)
    first = first_ref[k] == 1
    last = last_ref[k] == 1
    nonempty = hi_ref[k] > lo_ref[k]

    @pl.when(first)
    def _():
        hb_scr[...] = _rms(_tiles_to_rows(xs_ref, tme), gm_ref[...]).astype(BF16)

    @pl.when(first & jnp.logical_not(nonempty))
    def _():
        acc_scr[...] = jnp.zeros_like(acc_scr)

    @pl.when(nonempty)
    def _():
        hb = hb_scr[...]
        act = jax.nn.silu(_bdot(hb, wg_ref[0])) * _bdot(hb, wu_ref[0])
        row = lax.broadcasted_iota(jnp.int32, (tme, 1), 0)
        y = jnp.where((row >= lo_ref[k]) & (row < hi_ref[k]), _bdot(act.astype(BF16), wd_ref[0]), 0.0)

        @pl.when(first & last)
        def _():
            _rows_to_tiles(ys_ref, y, tme)

        @pl.when(first & jnp.logical_not(last))
        def _():
            acc_scr[...] = y

        @pl.when(jnp.logical_not(first))
        def _():
            acc_scr[...] += y

    @pl.when(last & jnp.logical_not(first & nonempty))
    def _():
        _rows_to_tiles(ys_ref, acc_scr[...], tme)


def _experts(work, xs, g_moe, wg, wu, wd, tme):
    n_work = work[0].shape[0]
    body = functools.partial(_expert_body, tme=tme)
    tile_blk = pl.BlockSpec((tme * TILE_ROWS, LANES), lambda k, tile, *_: (tile[k], 0))

    def wspec(shape):
        return pl.BlockSpec(shape, lambda k, tile, exp, *_: (exp[k], 0, 0))

    grid_spec = pltpu.PrefetchScalarGridSpec(
        num_scalar_prefetch=len(work),
        grid=(n_work,),
        in_specs=[tile_blk, pl.BlockSpec((1, D_MODEL), lambda k, *_: (0, 0)),
                  wspec((1, D_MODEL, D_FF_EXPERT)), wspec((1, D_MODEL, D_FF_EXPERT)),
                  wspec((1, D_FF_EXPERT, D_MODEL))],
        out_specs=tile_blk,
        scratch_shapes=[pltpu.VMEM((tme, D_MODEL), BF16), pltpu.VMEM((tme, D_MODEL), F32)],
    )
    return pl.pallas_call(
        body,
        grid_spec=grid_spec,
        out_shape=jax.ShapeDtypeStruct(xs.shape, F32),
        compiler_params=_params("arbitrary"),
        name="experts",
    )(*work, xs, g_moe, wg, wu, wd)


def _combine_body(slot_ref, slot_next_ref, xa_ref, xb_ref, info_ref, ys_hbm, gf_ref, ya_ref, yb_ref, b0, b1, sem,
                  *, tm, nstep, n_a):
    i = pl.program_id(0)
    cur = i % 2

    def gather(slots, buf):
        def issue(r, c):
            _tile_copy(ys_hbm, slots[2 * r], b0.at[buf], r, sem.at[buf]).start(priority=0)
            _tile_copy(ys_hbm, slots[2 * r + 1], b1.at[buf], r, sem.at[buf]).start(priority=1)
            return c

        lax.fori_loop(0, tm, issue, 0, unroll=8)

    @pl.when(i == 0)
    def _():
        gather(slot_ref, 0)

    @pl.when(i + 1 < nstep)
    def _():
        gather(slot_next_ref, 1 - cur)

    def drain(r, c):
        _tile_copy(ys_hbm, 0, b0.at[cur], r, sem.at[cur]).wait()
        _tile_copy(ys_hbm, 0, b1.at[cur], r, sem.at[cur]).wait()
        return c

    lax.fori_loop(0, tm, drain, 0, unroll=8)
    info = info_ref[...]
    moe = info[:, 2:3] * _tiles_to_rows(b0.at[cur], tm) + info[:, 3:4] * _tiles_to_rows(b1.at[cur], tm)

    def finish(x2t_ref, y_ref):
        y_ref[...] = _rms(_tiles_to_rows(x2t_ref, tm) + moe, gf_ref[...])

    _on_part(n_a, finish, (xa_ref, xb_ref), (ya_ref, yb_ref))


def _combine(slot, xa, xb, info, ys, g_final, tm=256):
    n_a, n_b = xa.shape[0] // (tm * TILE_ROWS), xb.shape[0] // (tm * TILE_ROWS)
    nstep = n_a + n_b
    body = functools.partial(_combine_body, tm=tm, nstep=nstep, n_a=n_a)
    buf = pltpu.VMEM((2, tm * TILE_ROWS, LANES), F32)
    return pl.pallas_call(
        body,
        grid=(nstep,),
        in_specs=[pl.BlockSpec((2 * tm,), lambda i: (i,), memory_space=pltpu.SMEM),
                  pl.BlockSpec((2 * tm,), lambda i: (jnp.minimum(i + 1, nstep - 1),), memory_space=pltpu.SMEM)]
        + _two_part_specs(tm * TILE_ROWS, n_a, n_b) + [
                  pl.BlockSpec((tm, ROUTER_LANES), lambda i: (i, 0)),
                  pl.BlockSpec(memory_space=pl.ANY),
                  pl.BlockSpec((1, D_MODEL), lambda i: (0, 0))],
        out_specs=[pl.BlockSpec((tm, D_MODEL), lambda i: (jnp.minimum(i, n_a - 1), 0)),
                   pl.BlockSpec((tm, D_MODEL), lambda i: (jnp.clip(i - n_a, 0, n_b - 1), 0))],
        out_shape=[jax.ShapeDtypeStruct((n_a * tm, D_MODEL), F32), jax.ShapeDtypeStruct((n_b * tm, D_MODEL), F32)],
        scratch_shapes=[buf, buf, pltpu.SemaphoreType.DMA((2,))],
        compiler_params=_params("arbitrary"),
        name="combine",
    )(slot, slot, xa, xb, info, ys, g_final)


def _moe(xa, xb, w, tme=512):
    info, cnt = _route(xa, xb, w["g_moe"], w["wr_hi"], w["wr_lo"])
    slot, work = _route_meta(info, cnt, tme)
    xs = _scatter(slot, xa, xb)
    ys = _experts(work, xs, w["g_moe"], w["moe_wg"], w["moe_wu"], w["moe_wd"], tme)
    return _combine(slot, xa, xb, info, ys, w["g_final"])


def _layer(x3d, k4, v4, s5_h0, lru_h0, lru_conv, w, chunk, nb, tq):
    bsz, seq, _ = x3d.shape
    x2d = x3d.reshape(bsz * seq, D_MODEL)
    lru_w = (w["conv_w"], w["conv_b"], w["lru_wg"], w["lru_bg"], w["lru_lam"])
    tail = SUBLANES - (CONV_W - 1)
    if lru_h0 is None:
        u, yl, hl, xt = _mixlru(x2d, w["g_mix"], w["w_in"], *lru_w, bsz, seq)
        lru_h = hl.reshape(bsz, SUBLANES, D_LRU)[:, 0]
        conv_new = xt.reshape(bsz, SUBLANES, D_LRU)[:, tail:]
    else:
        u, xl, gl = _mixin(x2d, w["g_mix"], w["w_in"])
        hist = jnp.pad(lru_conv, ((0, 0), (tail, 0), (0, 0))).reshape(bsz * seq, D_LRU)
        yl, hall = _lru_dec(xl, gl, hist, jnp.repeat(lru_h0, seq, axis=0), *lru_w)
        lru_h = hall.reshape(bsz, seq, D_LRU)[:, seq - 1]
        conv_new = xl.reshape(bsz, seq, D_LRU)[:, seq - (CONV_W - 1):]

    nstate = S5_GROUPS * S5_STATE
    if s5_h0 is None:
        zero = jnp.zeros((bsz, 1, nstate), F32)
        ys, s5_re, s5_im = _s5(u, w["s5"][chunk], zero, zero, chunk, seq // chunk, True)
    else:
        ys, s5_re, s5_im = _s5(u, w["s5"][chunk], s5_h0[0].reshape(bsz, nstate), s5_h0[1].reshape(bsz, nstate),
                               chunk, bsz, False)
    s5_re = s5_re.reshape(bsz, S5_GROUPS, S5_STATE)
    s5_im = s5_im.reshape(bsz, S5_GROUPS, S5_STATE)

    x2t = _mixattn(x3d, ys.reshape(bsz, seq, D_S5), yl.reshape(bsz, seq, D_LRU), k4, v4, w, nb, tq)
    return x2t, s5_re, s5_im, lru_h, conv_new


def kernel(x_prompt, x_sample, mem_prompt, cache_mem_k, cache_mem_v, state_s5_re, state_s5_im, state_lru_h, state_lru_conv, g_mix, w_in, s5_lam_re, s5_lam_im, s5_log_dt, s5_b_re, s5_b_im, s5_c_re, s5_c_im, s5_d, s5_w_glu, s5_b_glu, lru_conv_w, lru_conv_b, lru_w_a, lru_b_a, lru_w_x, lru_b_x, lru_lam, w_out, g_xa, g_mem, xa_w_q, xa_w_k, xa_w_v, xa_w_o, g_moe, moe_w_group, moe_w_expert, moe_w_gate, moe_w_up, moe_w_down, g_final):
    depth = g_mix.shape[0]
    assert depth == 1, "single-layer step"
    l = 0
    bsz, seq, _ = x_prompt.shape
    dbsz, dseq, _ = x_sample.shape
    chunk = SUBLANES
    assert seq % chunk == 0 and dseq == chunk

    s5_args = (s5_lam_re[l], s5_lam_im[l], s5_log_dt[l], s5_b_re[l], s5_b_im[l], s5_c_re[l], s5_c_im[l], s5_d[l])
    wg, bg = _lru_weights(lru_w_a[l], lru_w_x[l], lru_b_a[l], lru_b_x[l])
    wr = jnp.concatenate([moe_w_group[l], moe_w_expert[l],
                          jnp.zeros((D_MODEL, ROUTER_LANES - N_EXPERT_GROUPS - N_EXPERTS), F32)], axis=1)
    wr_hi = wr.astype(BF16)
    wr_lo = (wr - wr_hi.astype(F32)).astype(BF16)
    w = {
        "g_mix": g_mix[l][None], "w_in": w_in[l].astype(BF16),
        "s5": {chunk: _s5_weights(chunk, *s5_args)},
        "conv_w": lru_conv_w[l], "conv_b": lru_conv_b[l][None], "lru_wg": wg, "lru_bg": bg,
        "lru_lam": lru_lam[l][None],
        "w_glu": s5_w_glu[l].astype(BF16), "b_glu": s5_b_glu[l][None], "w_out": w_out[l].astype(BF16),
        "g_xa": g_xa[l][None], "w_q": xa_w_q[l].astype(BF16), "w_o": xa_w_o[l].astype(BF16),
        "g_moe": g_moe[l][None], "wr_hi": wr_hi, "wr_lo": wr_lo,
        "moe_wg": moe_w_gate[l].astype(BF16), "moe_wu": moe_w_up[l].astype(BF16),
        "moe_wd": moe_w_down[l].astype(BF16), "g_final": g_final[None],
    }

    mk, mv = _memkv(mem_prompt.reshape(bsz * N_MEM, D_MODEL), g_mem[l][None],
                    xa_w_k[l].astype(BF16), xa_w_v[l].astype(BF16))
    xp, p_re, p_im, p_h, p_conv = _layer(x_prompt, mk.reshape(bsz, N_MEM, D_MODEL), mv.reshape(bsz, N_MEM, D_MODEL),
                                         None, None, None, w, chunk, nb=1, tq=512)
    xs, s_re, s_im, s_h, s_conv = _layer(x_sample, cache_mem_k[l], cache_mem_v[l],
                                         (state_s5_re[l], state_s5_im[l]), state_lru_h[l],
                                         state_lru_conv[l], w, chunk, nb=8, tq=dseq)
    yp, ysmp = _moe(xp, xs, w)
    yp = yp.reshape(bsz, seq, D_MODEL)
    ysmp = ysmp.reshape(dbsz, dseq, D_MODEL)

    return (yp, ysmp,
            mk.reshape(1, bsz, N_MEM, XA_HEADS, XA_HEAD_DIM), mv.reshape(1, bsz, N_MEM, XA_HEADS, XA_HEAD_DIM),
            p_re[None], p_im[None], p_h[None], p_conv[None],
            s_re[None], s_im[None], s_h[None], s_conv[None])
```

```python
import functools

import jax
import jax.numpy as jnp
import numpy as np
from jax import lax
from jax.experimental import pallas as pl
from jax.experimental.pallas import tpu as pltpu

F32 = jnp.float32
BF16 = jnp.bfloat16

D_MODEL = 1024
D_S5 = 512
D_LRU = 512
S5_GROUP = 16
S5_GROUPS = 32
S5_STATE = 64
LRU_HEADS = 8
LRU_HEAD_DIM = 64
CONV_W = 4
LRU_C = 8.0
N_MEM = 256
XA_HEADS = 4
XA_HEAD_DIM = 256
N_EXPERT_GROUPS = 4
EXPERTS_PER_GROUP = 8
N_EXPERTS = 32
D_FF_EXPERT = 256
EPS = 1e-6

SUBLANES = 8
LANES = 128
ROUTER_LANES = 128
VMEM_LIMIT = 56 * 1024 * 1024


def _params(*sem):
    return pltpu.CompilerParams(dimension_semantics=sem, vmem_limit_bytes=VMEM_LIMIT)


def _rms(x, g):
    ms = jnp.mean(x * x, axis=-1, keepdims=True)
    return x * lax.rsqrt(ms + EPS) * g


def _bdot(a, b):
    return jnp.dot(a, b, preferred_element_type=F32)


def _memkv_body(m_ref, g_ref, wk_ref, wv_ref, k_ref, v_ref):
    hb = _rms(m_ref[...], g_ref[...]).astype(BF16)
    k_ref[...] = _bdot(hb, wk_ref[...])
    v_ref[...] = _bdot(hb, wv_ref[...])


def _memkv(mem2d, g_mem, wk, wv, tm=512):
    t = mem2d.shape[0]
    row = pl.BlockSpec((tm, D_MODEL), lambda i: (i, 0))
    full = pl.BlockSpec((D_MODEL, D_MODEL), lambda i: (0, 0))
    return pl.pallas_call(
        _memkv_body,
        grid=(t // tm,),
        in_specs=[row, pl.BlockSpec((1, D_MODEL), lambda i: (0, 0)), full, full],
        out_specs=[row, row],
        out_shape=[jax.ShapeDtypeStruct((t, D_MODEL), F32)] * 2,
        compiler_params=_params("parallel"),
        name="memkv",
    )(mem2d, g_mem, wk, wv)


def _mixin_body(x_ref, g_ref, w_ref, u_ref, xl_ref, gl_ref):
    hb = _rms(x_ref[...], g_ref[...]).astype(BF16)
    proj = _bdot(hb, w_ref[...])
    u_ref[...] = proj[:, :D_S5]
    xl_ref[...] = proj[:, D_S5:D_S5 + D_LRU]
    gl_ref[...] = proj[:, D_S5 + D_LRU:]


def _mixin(x2d, g_mix, w_in, tm=1024):
    t = x2d.shape[0]
    half = pl.BlockSpec((tm, D_S5), lambda i: (i, 0))
    return pl.pallas_call(
        _mixin_body,
        grid=(t // tm,),
        in_specs=[pl.BlockSpec((tm, D_MODEL), lambda i: (i, 0)),
                  pl.BlockSpec((1, D_MODEL), lambda i: (0, 0)),
                  pl.BlockSpec((D_MODEL, D_S5 + 2 * D_LRU), lambda i: (0, 0))],
        out_specs=[half, half, half],
        out_shape=[jax.ShapeDtypeStruct((t, D_S5), F32)] * 3,
        compiler_params=_params("parallel"),
        name="mixin",
    )(x2d, g_mix, w_in)


GROUPS_PER_BLOCK = LANES // S5_GROUP
PAIRS_PER_BLOCK = GROUPS_PER_BLOCK // 2
STATE_BLOCK = GROUPS_PER_BLOCK * S5_STATE
PW_ROWS = 24


def _s5_body(u_ref, wa_ref, wh_ref, pw_ref, d_ref, h0r_ref, h0i_ref, y_ref, hfr_ref, hfi_ref, hpr_scr, hpi_scr,
             *, chunk, nrow, scan):
    nh = chunk // SUBLANES
    lc = chunk * S5_GROUP
    slot = lax.broadcasted_iota(jnp.int32, (nrow, LANES), 1) // S5_GROUP
    in_slot = [slot == s for s in range(GROUPS_PER_BLOCK)]

    def pick(src, sel):
        out = src[sel(0)]
        for s in range(1, GROUPS_PER_BLOCK):
            out = jnp.where(in_slot[s], src[sel(s)], out)
        return out

    nat, skew = [], []
    for t in range(chunk):
        a = u_ref[pl.ds(t, nrow, stride=chunk), :]
        nat.append(a)
        s = t % GROUPS_PER_BLOCK
        skew.append(pltpu.roll(a, s * S5_GROUP, axis=1) if s else a)
    def lane_roll(x, slots):
        slots %= GROUPS_PER_BLOCK
        return pltpu.roll(x, slots * S5_GROUP, axis=1) if slots else x

    z = [[lane_roll(pick(skew, lambda s, hh=hh, m=m: SUBLANES * hh + (s - m) % GROUPS_PER_BLOCK), -m)
          for hh in range(nh)] for m in range(GROUPS_PER_BLOCK)]

    ys, er, ei = [], [], []
    for p in range(PAIRS_PER_BLOCK):
        lhs = jnp.concatenate(z[2 * p] + z[2 * p + 1], axis=1).astype(BF16)
        full = _bdot(lhs, wa_ref[p])
        ys.append(full[:, :2 * lc])
        er.append(full[:, 2 * lc:2 * lc + LANES])
        ei.append(full[:, 2 * lc + LANES:])
    er = jnp.concatenate(er, axis=1)
    ei = jnp.concatenate(ei, axis=1)

    if scan:
        ntile = nrow // SUBLANES
        xr = er.reshape(ntile, SUBLANES, STATE_BLOCK)
        xi = ei.reshape(ntile, SUBLANES, STATE_BLOCK)
        row = lax.broadcasted_iota(jnp.int32, xr.shape, 1)
        for k, d in enumerate((1, 2, 4)):
            pr, pi = pw_ref[0, k:k + 1, :], pw_ref[0, 4 + k:5 + k, :]
            sr, si = pltpu.roll(xr, d, axis=1), pltpu.roll(xi, d, axis=1)
            m = row >= d
            xr, xi = jnp.where(m, xr + pr * sr - pi * si, xr), jnp.where(m, xi + pr * si + pi * sr, xi)
        sxr = jnp.where(row >= 1, pltpu.roll(xr, 1, axis=1), 0.0)
        sxi = jnp.where(row >= 1, pltpu.roll(xi, 1, axis=1), 0.0)
        p8r, p8i = pw_ref[0, 3:4, :], pw_ref[0, 7:8, :]
        qr, qi = pw_ref[0, 8:16, :], pw_ref[0, 16:24, :]
        hr, hi = h0r_ref[0], h0i_ref[0]
        for t in range(ntile):
            rows = slice(SUBLANES * t, SUBLANES * (t + 1))
            hpr_scr[rows, :] = sxr[t] + qr * hr - qi * hi
            hpi_scr[rows, :] = sxi[t] + qr * hi + qi * hr
            hr, hi = (xr[t, SUBLANES - 1:] + p8r * hr - p8i * hi, xi[t, SUBLANES - 1:] + p8r * hi + p8i * hr)
        hfr_ref[0] = hr
        hfi_ref[0] = hi
        hpr, hpi = hpr_scr[...], hpi_scr[...]
    else:
        hpr, hpi = h0r_ref[...], h0i_ref[...]
        p1r, p1i = pw_ref[0, 0:1, :], pw_ref[0, 4:5, :]
        hfr_ref[...] = er + p1r * hpr - p1i * hpi
        hfi_ref[...] = ei + p1r * hpi + p1i * hpr

    yg = []
    for p in range(PAIRS_PER_BLOCK):
        lanes = slice(LANES * p, LANES * (p + 1))
        hp = jnp.concatenate([hpr[:, lanes], hpi[:, lanes]], axis=1).astype(BF16)
        out = ys[p] + _bdot(hp, wh_ref[p])
        for half in range(2):
            q = 2 * p + half
            yg.append([lane_roll(out[:, half * lc + hh * LANES:half * lc + (hh + 1) * LANES], q)
                       for hh in range(nh)])
    d = d_ref[...]
    for t in range(chunk):
        hh, s = divmod(t, GROUPS_PER_BLOCK)
        c = lane_roll(pick([g[hh] for g in yg], lambda sl, s=s: (sl - s) % GROUPS_PER_BLOCK), -s)
        y_ref[pl.ds(t, nrow, stride=chunk), :] = c + d * nat[t]


def _s5(u2d, weights, h0r, h0i, chunk, nrow, scan):
    wa, wh, pw, dvec = weights
    t = u2d.shape[0]
    lc = chunk * S5_GROUP
    rows = nrow * chunk
    nblk = D_S5 // LANES
    body = functools.partial(_s5_body, chunk=chunk, nrow=nrow, scan=scan)
    ublk = pl.BlockSpec((rows, LANES), lambda i, j: (i, j))
    if scan:
        hblk = pl.BlockSpec((1, 1, STATE_BLOCK), lambda i, j: (i, 0, j))
        hshape = jax.ShapeDtypeStruct((t // rows, 1, S5_GROUPS * S5_STATE), F32)
    else:
        hblk = pl.BlockSpec((nrow, STATE_BLOCK), lambda i, j: (i, j))
        hshape = jax.ShapeDtypeStruct((t // chunk, S5_GROUPS * S5_STATE), F32)
    return pl.pallas_call(
        body,
        grid=(t // rows, nblk),
        in_specs=[ublk,
                  pl.BlockSpec((PAIRS_PER_BLOCK, 2 * lc, 2 * lc + 2 * LANES), lambda i, j: (j, 0, 0)),
                  pl.BlockSpec((PAIRS_PER_BLOCK, 2 * LANES, 2 * lc), lambda i, j: (j, 0, 0)),
                  pl.BlockSpec((1, PW_ROWS, STATE_BLOCK), lambda i, j: (j, 0, 0)),
                  pl.BlockSpec((1, LANES), lambda i, j: (0, j)),
                  hblk, hblk],
        out_specs=[ublk, hblk, hblk],
        out_shape=[jax.ShapeDtypeStruct((t, D_S5), F32), hshape, hshape],
        scratch_shapes=[pltpu.VMEM((nrow, STATE_BLOCK), F32), pltpu.VMEM((nrow, STATE_BLOCK), F32)],
        compiler_params=_params("parallel", "parallel"),
        name="s5",
    )(u2d, wa, wh, pw, dvec, h0r, h0i)


def _s5_weights(chunk, lam_re, lam_im, log_dt, b_re, b_im, c_re, c_im, d_skip):
    hi = lax.Precision.HIGHEST
    dt = jnp.exp(log_dt)[:, None]
    mag = jnp.exp(lam_re * dt)
    ab_re = mag * jnp.cos(lam_im * dt)
    ab_im = mag * jnp.sin(lam_im * dt)
    den = lam_re * lam_re + lam_im * lam_im
    nr = ab_re - 1.0
    f_re = (nr * lam_re + ab_im * lam_im) / den
    f_im = (ab_im * lam_re - nr * lam_im) / den
    bb_re = f_re[..., None] * b_re - f_im[..., None] * b_im
    bb_im = f_re[..., None] * b_im + f_im[..., None] * b_re
    def powers(ks):
        k = jnp.asarray(np.asarray(ks, np.float32))[:, None, None]
        m = jnp.exp(k * (lam_re * dt))
        return m * jnp.cos(k * (lam_im * dt)), m * jnp.sin(k * (lam_im * dt))

    pw_re, pw_im = powers(range(chunk + 1))
    pb_re = pw_re[:chunk, ..., None] * bb_re - pw_im[:chunk, ..., None] * bb_im
    pb_im = pw_re[:chunk, ..., None] * bb_im + pw_im[:chunk, ..., None] * bb_re
    kk = (jnp.einsum("gon,kgni->kgoi", c_re, pb_re, precision=hi)
          - jnp.einsum("gon,kgni->kgoi", c_im, pb_im, precision=hi))
    npair = S5_GROUPS // 2
    lc = chunk * S5_GROUP
    row0 = kk.transpose(1, 3, 0, 2).reshape(S5_GROUPS, S5_GROUP, lc)
    row0 = jnp.pad(row0, ((0, 0), (0, 0), (lc, 0)))
    conv = jnp.stack([row0[:, :, lc - S5_GROUP * ti:2 * lc - S5_GROUP * ti] for ti in range(chunk)], axis=1)
    conv = conv.reshape(npair, 2, lc, lc)
    ends = [x[::-1].transpose(1, 0, 3, 2).reshape(npair, 2, lc, S5_STATE) for x in (pb_re, pb_im)]
    zc = jnp.zeros((npair, lc, lc), F32)
    zs = jnp.zeros((npair, lc, S5_STATE), F32)
    wa = jnp.concatenate([
        jnp.concatenate([conv[:, 0], zc, ends[0][:, 0], zs, ends[1][:, 0], zs], axis=-1),
        jnp.concatenate([zc, conv[:, 1], zs, ends[0][:, 1], zs, ends[1][:, 1]], axis=-1)], axis=1)
    gr = c_re[None] * pw_re[1:, :, None, :] - c_im[None] * pw_im[1:, :, None, :]
    gi = c_re[None] * pw_im[1:, :, None, :] + c_im[None] * pw_re[1:, :, None, :]
    zr = jnp.zeros((npair, S5_STATE, lc), F32)
    blocks = []
    for x in (gr, -gi):
        hy = x.transpose(1, 3, 0, 2).reshape(npair, 2, S5_STATE, lc)
        blocks += [jnp.concatenate([hy[:, 0], zr], axis=-1), jnp.concatenate([zr, hy[:, 1]], axis=-1)]
    wh = jnp.concatenate(blocks, axis=1)
    tab = [1, 2, 4, 8] + list(range(SUBLANES))
    qr, qi = powers([chunk * k for k in tab])
    nblk = D_S5 // LANES
    pw = jnp.concatenate([qr[:4], qi[:4], qr[4:], qi[4:]]).reshape(PW_ROWS, nblk, STATE_BLOCK).transpose(1, 0, 2)
    return wa.astype(BF16), wh.astype(BF16), pw, d_skip.reshape(1, D_S5)


def _lru_ab(xc, wg_ref, bg_ref, lam_ref):
    xb = xc.astype(BF16)
    half = D_LRU // 2
    g0 = _bdot(xb[:, :half], wg_ref[0]) + bg_ref[0]
    g1 = _bdot(xb[:, half:], wg_ref[1]) + bg_ref[1]
    r = jax.nn.sigmoid(jnp.concatenate([g0[:, :half], g1[:, :half]], axis=1))
    ig = jax.nn.sigmoid(jnp.concatenate([g0[:, half:], g1[:, half:]], axis=1))
    zl = -lam_ref[...]
    softplus = jnp.maximum(zl, 0.0) + jnp.log1p(jnp.exp(-jnp.abs(zl)))
    log_a = -LRU_C * r * softplus
    a = jnp.exp(log_a)
    b = jnp.sqrt(-jnp.tanh(log_a) * (a * a + 1.0)) * (ig * xc)
    return a, b


def _tile_scan(a, b):
    shape = a.shape
    a = a.reshape(shape[0] // SUBLANES, SUBLANES, shape[1])
    b = b.reshape(a.shape)
    row = lax.broadcasted_iota(jnp.int32, a.shape, 1)
    for d in (1, 2, 4):
        a_prev = pltpu.roll(a, d, axis=1)
        b_prev = pltpu.roll(b, d, axis=1)
        m = row >= d
        b = jnp.where(m, b + a * b_prev, b)
        a = jnp.where(m, a * a_prev, a)
    return a.reshape(shape), b.reshape(shape)


def _mixlru_body(x_ref, gmix_ref, win_ref, cw_ref, cb_ref, wg_ref, bg_ref, lam_ref, u_ref, y_ref, hl_ref, xt_ref,
                 xp_scr, a_scr, b_scr, h_scr, hc_scr, *, ts):
    ti = pl.program_id(1)

    @pl.when(ti == 0)
    def _():
        xp_scr[0:SUBLANES, :] = jnp.zeros((SUBLANES, D_LRU), F32)
        hc_scr[...] = jnp.zeros((SUBLANES, D_LRU), F32)

    proj = _bdot(_rms(x_ref[...], gmix_ref[...]).astype(BF16), win_ref[...])
    u_ref[...] = proj[:, :D_S5]
    xl = proj[:, D_S5:D_S5 + D_LRU]
    gl = proj[:, D_S5 + D_LRU:]
    xt_ref[...] = xl[ts - SUBLANES:, :]
    xp_scr[SUBLANES:SUBLANES + ts, :] = xl
    xc = cb_ref[...] + xl * cw_ref[CONV_W - 1:CONV_W, :]
    for j in range(1, CONV_W):
        xc = xc + xp_scr[SUBLANES - j:SUBLANES - j + ts, :] * cw_ref[CONV_W - 1 - j:CONV_W - j, :]
    xp_scr[0:SUBLANES, :] = xl[ts - SUBLANES:, :]
    a, b = _lru_ab(xc, wg_ref, bg_ref, lam_ref)
    a, b = _tile_scan(a, b)
    a_scr[...] = a
    b_scr[...] = b

    def step(i, hin):
        rows = pl.ds(pl.multiple_of(i * SUBLANES, SUBLANES), SUBLANES)
        h = b_scr[rows, :] + a_scr[rows, :] * hin
        h_scr[rows, :] = h
        return h[SUBLANES - 1:SUBLANES, :]

    hlast = lax.fori_loop(0, ts // SUBLANES, step, hc_scr[0:1, :], unroll=4)
    hc_scr[...] = jnp.broadcast_to(hlast, (SUBLANES, D_LRU))
    hl_ref[...] = hc_scr[...]
    y_ref[...] = h_scr[...] * jax.nn.gelu(gl)


def _lru_dec_body(xl_ref, gl_ref, hist_ref, h0_ref, cw_ref, cb_ref, wg_ref, bg_ref, lam_ref, y_ref, hl_ref, *, tm):
    xl = xl_ref[...]
    hist = hist_ref[...]
    row = lax.broadcasted_iota(jnp.int32, xl.shape, 0) % SUBLANES
    xc = cb_ref[...] + xl * cw_ref[CONV_W - 1:CONV_W, :]
    for j in range(1, CONV_W):
        prev = jnp.where(row >= j, pltpu.roll(xl, j, axis=0), pltpu.roll(hist, tm - SUBLANES + j, axis=0))
        xc = xc + prev * cw_ref[CONV_W - 1 - j:CONV_W - j, :]
    a, b = _lru_ab(xc, wg_ref, bg_ref, lam_ref)
    a, b = _tile_scan(a, b)
    h = b + a * h0_ref[...]
    hl_ref[...] = h
    y_ref[...] = h * jax.nn.gelu(gl_ref[...])


def _lru_weights(w_a, w_x, b_a, b_x):
    eye = jnp.eye(LRU_HEADS, dtype=F32)
    bd_a = jnp.einsum("hij,hg->higj", w_a, eye).reshape(D_LRU, D_LRU)
    bd_x = jnp.einsum("hij,hg->higj", w_x, eye).reshape(D_LRU, D_LRU)
    half = D_LRU // 2
    wg = jnp.stack([jnp.concatenate([bd_a[:half, :half], bd_x[:half, :half]], axis=1),
                    jnp.concatenate([bd_a[half:, half:], bd_x[half:, half:]], axis=1)]).astype(BF16)
    ba = b_a.reshape(1, D_LRU)
    bx = b_x.reshape(1, D_LRU)
    bg = jnp.stack([jnp.concatenate([ba[:, :half], bx[:, :half]], axis=1),
                    jnp.concatenate([ba[:, half:], bx[:, half:]], axis=1)])
    return wg, bg


def _lru_common_specs(const):
    return [const((CONV_W, D_LRU)), const((1, D_LRU)), const((2, D_LRU // 2, D_LRU)),
            const((2, 1, D_LRU)), const((1, D_LRU))]


def _mixlru(x2d, g_mix, w_in, cw, cb, wg, bg, lam, bsz, seq, ts=1024):
    nt = seq // ts
    blk = pl.BlockSpec((ts, D_LRU), lambda b, t: (b * nt + t, 0))
    per_seq = pl.BlockSpec((SUBLANES, D_LRU), lambda b, t: (b, 0))

    def const(shape):
        return pl.BlockSpec(shape, lambda b, t: (0,) * len(shape))

    body = functools.partial(_mixlru_body, ts=ts)
    tile = jax.ShapeDtypeStruct((bsz * SUBLANES, D_LRU), F32)
    return pl.pallas_call(
        body,
        grid=(bsz, nt),
        in_specs=[pl.BlockSpec((ts, D_MODEL), lambda b, t: (b * nt + t, 0)), const((1, D_MODEL)),
                  const((D_MODEL, D_S5 + 2 * D_LRU))] + _lru_common_specs(const),
        out_specs=[blk, blk, per_seq, per_seq],
        out_shape=[jax.ShapeDtypeStruct((bsz * seq, D_S5), F32), jax.ShapeDtypeStruct((bsz * seq, D_LRU), F32),
                   tile, tile],
        scratch_shapes=[pltpu.VMEM((ts + SUBLANES, D_LRU), F32), pltpu.VMEM((ts, D_LRU), F32),
                        pltpu.VMEM((ts, D_LRU), F32), pltpu.VMEM((ts, D_LRU), F32),
                        pltpu.VMEM((SUBLANES, D_LRU), F32)],
        compiler_params=_params("arbitrary", "arbitrary"),
        name="mixlru",
    )(x2d, g_mix, w_in, cw, cb, wg, bg, lam)


def _lru_dec(xl, gl, hist, h0rep, cw, cb, wg, bg, lam, tm=256):
    t = xl.shape[0]
    blk = pl.BlockSpec((tm, D_LRU), lambda i: (i, 0))

    def const(shape):
        return pl.BlockSpec(shape, lambda i: (0,) * len(shape))

    body = functools.partial(_lru_dec_body, tm=tm)
    return pl.pallas_call(
        body,
        grid=(t // tm,),
        in_specs=[blk, blk, blk, blk] + _lru_common_specs(const),
        out_specs=[blk, blk],
        out_shape=[jax.ShapeDtypeStruct((t, D_LRU), F32)] * 2,
        compiler_params=_params("parallel"),
        name="lru_dec",
    )(xl, gl, hist, h0rep, cw, cb, wg, bg, lam)


def _softmax(sc):
    p = jnp.exp(sc - jnp.max(sc, axis=-1, keepdims=True))
    return p / jnp.sum(p, axis=-1, keepdims=True)


def _attend(q, k_ref, v_ref, i, tq):
    scale = XA_HEAD_DIM ** -0.5
    heads = [q[:, h * XA_HEAD_DIM:(h + 1) * XA_HEAD_DIM] for h in range(XA_HEADS)]
    if len(k_ref.shape) == 4:
        kf = k_ref[i].reshape(N_MEM * XA_HEADS, XA_HEAD_DIM).astype(BF16)
        vf = v_ref[i].reshape(N_MEM * XA_HEADS, XA_HEAD_DIM).astype(BF16)
        sc = lax.dot_general(jnp.concatenate(heads, axis=0), kf, (((1,), (1,)), ((), ())),
                             preferred_element_type=F32) * scale
        own = (lax.broadcasted_iota(jnp.int32, sc.shape, 1) % XA_HEADS
               == lax.broadcasted_iota(jnp.int32, sc.shape, 0) // tq)
        oh = _bdot(_softmax(jnp.where(own, sc, -jnp.inf)).astype(BF16), vf)
        outs = [oh[h * tq:(h + 1) * tq] for h in range(XA_HEADS)]
    else:
        kb = k_ref[i].astype(BF16)
        vb = v_ref[i].astype(BF16)
        outs = []
        for h in range(XA_HEADS):
            cols = slice(h * XA_HEAD_DIM, (h + 1) * XA_HEAD_DIM)
            sc = lax.dot_general(heads[h], kb[:, cols], (((1,), (1,)), ((), ())),
                                 preferred_element_type=F32) * scale
            outs.append(_bdot(_softmax(sc).astype(BF16), vb[:, cols]))
    return jnp.concatenate(outs, axis=1)


def _mixattn_body(x_ref, ys_ref, yl_ref, k_ref, v_ref, wglu_ref, bglu_ref, wout_ref, gxa_ref, wq_ref, wo_ref,
                  x2t_ref, *, nb, tq):
    rows = nb * tq
    ys = jax.nn.gelu(ys_ref[...].reshape(rows, D_S5))
    gate = jax.nn.sigmoid(_bdot(ys.astype(BF16), wglu_ref[...]) + bglu_ref[...])
    s5 = (ys * gate).astype(BF16)
    yl = yl_ref[...].reshape(rows, D_LRU).astype(BF16)
    x1 = x_ref[...].reshape(rows, D_MODEL) + _bdot(s5, wout_ref[0:D_S5, :]) + _bdot(yl, wout_ref[D_S5:, :])
    q = _bdot(_rms(x1, gxa_ref[...]).astype(BF16), wq_ref[...]).astype(BF16)
    o = jnp.concatenate([_attend(q[i * tq:(i + 1) * tq], k_ref, v_ref, i, tq) for i in range(nb)], axis=0)
    _rows_to_tiles(x2t_ref, x1 + _bdot(o.astype(BF16), wo_ref[...]), rows)


def _mixattn(x3, ys3, yl3, k4, v4, w, nb, tq):
    bsz, seq, _ = x3.shape
    nt = seq // tq
    blk = pl.BlockSpec((nb, tq, D_MODEL), lambda b, t: (b, t, 0))
    half = pl.BlockSpec((nb, tq, D_S5), lambda b, t: (b, t, 0))
    kv = pl.BlockSpec((nb,) + k4.shape[1:], lambda b, t: (b,) + (0,) * (k4.ndim - 1))

    def const(shape):
        return pl.BlockSpec(shape, lambda b, t: (0,) * len(shape))

    body = functools.partial(_mixattn_body, nb=nb, tq=tq)
    return pl.pallas_call(
        body,
        grid=(bsz // nb, nt),
        in_specs=[blk, half, half, kv, kv, const((D_S5, D_S5)), const((1, D_S5)), const((D_MODEL, D_MODEL)),
                  const((1, D_MODEL)), const((D_MODEL, D_MODEL)), const((D_MODEL, D_MODEL))],
        out_specs=pl.BlockSpec((nb * tq * TILE_ROWS, LANES), lambda b, t: (b * nt + t, 0)),
        out_shape=jax.ShapeDtypeStruct((bsz * seq * TILE_ROWS, LANES), F32),
        compiler_params=_params("parallel", "parallel"),
        name="mixattn",
    )(x3, ys3, yl3, k4, v4, w["w_glu"], w["b_glu"], w["w_out"], w["g_xa"], w["w_q"], w["w_o"])


def _router(hm, wr_hi_ref, wr_lo_ref):
    a_hi = hm.astype(BF16)
    a_lo = (hm - a_hi.astype(F32)).astype(BF16)
    logits = _bdot(a_hi, wr_hi_ref[...]) + (_bdot(a_hi, wr_lo_ref[...]) + _bdot(a_lo, wr_hi_ref[...]))
    lane_i = lax.broadcasted_iota(jnp.int32, logits.shape, 1)
    lane = lane_i.astype(F32)
    neg = -jnp.inf
    big = float(ROUTER_LANES)
    is_g = lane_i < N_EXPERT_GROUPS
    glog = jnp.where(is_g, logits, neg)
    gmax = jnp.max(glog, axis=-1, keepdims=True)
    gsel = jnp.min(jnp.where(glog == gmax, lane, big), axis=-1, keepdims=True)
    pg_sel = 1.0 / jnp.sum(jnp.where(is_g, jnp.exp(logits - gmax), 0.0), axis=-1, keepdims=True)
    eidx = lane_i - N_EXPERT_GROUPS
    in_group = (eidx >= 0) & (eidx < N_EXPERTS) & ((eidx >> 3).astype(F32) == gsel)
    el = jnp.where(in_group, logits, neg)
    v1 = jnp.max(el, axis=-1, keepdims=True)
    i1 = jnp.min(jnp.where(el == v1, lane, big), axis=-1, keepdims=True)
    el2 = jnp.where(lane == i1, neg, el)
    v2 = jnp.max(el2, axis=-1, keepdims=True)
    i2 = jnp.min(jnp.where(el2 == v2, lane, big), axis=-1, keepdims=True)
    e2 = jnp.exp(v2 - v1)
    w1 = pg_sel / (1.0 + e2)
    w2 = pg_sel * e2 / (1.0 + e2)
    return i1, i2, w1, w2


TILE_ROWS = D_MODEL // LANES


def _tiles_to_rows(ref, n, base=0):
    return jnp.concatenate([ref[pl.ds(base + s, n, stride=TILE_ROWS), :] for s in range(TILE_ROWS)], axis=1)


def _rows_to_tiles(ref, val, n, base=0):
    for s in range(TILE_ROWS):
        ref[pl.ds(base + s, n, stride=TILE_ROWS), :] = val[:, s * LANES:(s + 1) * LANES]


def _two_part_specs(block_rows, n_a, n_b):
    return [pl.BlockSpec((block_rows, LANES), lambda i: (jnp.minimum(i, n_a - 1), 0)),
            pl.BlockSpec((block_rows, LANES), lambda i: (jnp.clip(i - n_a, 0, n_b - 1), 0))]


def _on_part(n_a, fn, *ref_pairs):
    i = pl.program_id(0)
    pl.when(i < n_a)(lambda: fn(*[p[0] for p in ref_pairs]))
    pl.when(i >= n_a)(lambda: fn(*[p[1] for p in ref_pairs]))


def _route_body(xa_ref, xb_ref, gm_ref, wr_hi_ref, wr_lo_ref, info_ref, cnt_ref, cnt_scr, *, tm, n_a):
    @pl.when(pl.program_id(0) == 0)
    def _():
        cnt_scr[...] = jnp.zeros_like(cnt_scr)

    _on_part(n_a, functools.partial(_route_step, gm_ref, wr_hi_ref, wr_lo_ref, info_ref, cnt_ref, cnt_scr, tm),
             (xa_ref, xb_ref))


def _route_step(gm_ref, wr_hi_ref, wr_lo_ref, info_ref, cnt_ref, cnt_scr, tm, x2t_ref):
    hm = _rms(_tiles_to_rows(x2t_ref, tm), gm_ref[...])
    i1, i2, w1, w2 = _router(hm, wr_hi_ref, wr_lo_ref)
    lane_i = lax.broadcasted_iota(jnp.int32, (tm, ROUTER_LANES), 1)
    lane = lane_i.astype(F32)
    chosen = ((lane == i1) | (lane == i2)).astype(F32)
    earlier = (lax.broadcasted_iota(jnp.int32, (tm, tm), 1) < lax.broadcasted_iota(jnp.int32, (tm, tm), 0))
    before = _bdot(earlier.astype(BF16), chosen.astype(BF16)) + cnt_scr[0:1, :]
    r1 = jnp.sum(jnp.where(lane == i1, before, 0.0), axis=-1, keepdims=True)
    r2 = jnp.sum(jnp.where(lane == i2, before, 0.0), axis=-1, keepdims=True)
    cols = (i1, i2, w1, w2, r1, r2)
    info = jnp.zeros((tm, ROUTER_LANES), F32)
    for j, col in enumerate(cols):
        info = jnp.where(lane_i == j, col, info)
    info_ref[...] = info
    cnt_scr[...] = cnt_scr[...] + jnp.sum(chosen, axis=0, keepdims=True)
    cnt_ref[...] = cnt_scr[...]


def _route(xa, xb, g_moe, wr_hi, wr_lo, tm=1024):
    n_a, n_b = xa.shape[0] // (tm * TILE_ROWS), xb.shape[0] // (tm * TILE_ROWS)
    t = (n_a + n_b) * tm

    def const(shape):
        return pl.BlockSpec(shape, lambda i: (0,) * len(shape))

    body = functools.partial(_route_body, tm=tm, n_a=n_a)
    return pl.pallas_call(
        body,
        grid=(n_a + n_b,),
        in_specs=_two_part_specs(tm * TILE_ROWS, n_a, n_b) + [
            const((1, D_MODEL)), const((D_MODEL, ROUTER_LANES)), const((D_MODEL, ROUTER_LANES))],
        out_specs=[pl.BlockSpec((tm, ROUTER_LANES), lambda i: (i, 0)), const((SUBLANES, ROUTER_LANES))],
        out_shape=[jax.ShapeDtypeStruct((t, ROUTER_LANES), F32),
                   jax.ShapeDtypeStruct((SUBLANES, ROUTER_LANES), F32)],
        scratch_shapes=[pltpu.VMEM((SUBLANES, ROUTER_LANES), F32)],
        compiler_params=_params("arbitrary"),
        name="route",
    )(xa, xb, g_moe, wr_hi, wr_lo)


def _route_meta(info, cnt, tme):
    t = info.shape[0]
    n_tiles = 2 * t // tme
    counts = cnt[0, N_EXPERT_GROUPS:N_EXPERT_GROUPS + N_EXPERTS].astype(jnp.int32)
    ends = jnp.cumsum(counts)
    starts = ends - counts
    e = info[:, 0:2].astype(jnp.int32) - N_EXPERT_GROUPS
    onehot = e[:, :, None] == jnp.arange(N_EXPERTS, dtype=jnp.int32)
    slot = (jnp.sum(jnp.where(onehot, starts, 0), axis=-1) + info[:, 4:6].astype(jnp.int32)).reshape(-1)
    pts = jnp.concatenate([jnp.arange(n_tiles, dtype=jnp.int32) * tme, starts])
    n = pts.shape[0]
    idx = jnp.arange(n, dtype=jnp.int32)
    pos = jnp.sum((pts[None, :] < pts[:, None]) | ((pts[None, :] == pts[:, None]) & (idx[None, :] < idx[:, None])),
                  axis=1)
    lo_abs = jnp.sum(jnp.where(pos[:, None] == idx[None, :], pts[:, None], 0), axis=0)
    hi_abs = jnp.concatenate([lo_abs[1:], jnp.full((1,), 2 * t, jnp.int32)])
    tile = jnp.minimum(lo_abs // tme, n_tiles - 1)
    expert = jnp.clip(jnp.sum(starts[None, :] <= lo_abs[:, None], axis=1) - 1, 0, N_EXPERTS - 1)
    first = jnp.concatenate([jnp.ones((1,), jnp.int32), (tile[1:] != tile[:-1]).astype(jnp.int32)])
    last = jnp.concatenate([(tile[1:] != tile[:-1]).astype(jnp.int32), jnp.ones((1,), jnp.int32)])
    work = [a.astype(jnp.int32) for a in (tile, expert, lo_abs - tile * tme, hi_abs - tile * tme, first, last)]
    return slot.astype(jnp.int32), work


def _tile_copy(src, src_tok, dst, dst_tok, sem):
    return pltpu.make_async_copy(src.at[pl.ds(pl.multiple_of(src_tok * TILE_ROWS, TILE_ROWS), TILE_ROWS), :],
                                 dst.at[pl.ds(pl.multiple_of(dst_tok * TILE_ROWS, TILE_ROWS), TILE_ROWS), :], sem)


def _scatter_body(slot_ref, xa_ref, xb_ref, xs_hbm, sem, *, tm, n_a):
    _on_part(n_a, functools.partial(_scatter_step, slot_ref, xs_hbm, sem, tm), (xa_ref, xb_ref))


def _scatter_step(slot_ref, xs_hbm, sem, tm, x2t_ref):
    def issue(r, c):
        _tile_copy(x2t_ref, r, xs_hbm, slot_ref[2 * r], sem).start(priority=0)
        _tile_copy(x2t_ref, r, xs_hbm, slot_ref[2 * r + 1], sem).start(priority=1)
        return c

    lax.fori_loop(0, tm, issue, 0, unroll=8)

    def drain(r, c):
        _tile_copy(x2t_ref, r, xs_hbm, 0, sem).wait()
        _tile_copy(x2t_ref, r, xs_hbm, 0, sem).wait()
        return c

    lax.fori_loop(0, tm, drain, 0, unroll=8)


def _scatter(slot, xa, xb, tm=1024):
    n_a, n_b = xa.shape[0] // (tm * TILE_ROWS), xb.shape[0] // (tm * TILE_ROWS)
    t = (n_a + n_b) * tm
    body = functools.partial(_scatter_body, tm=tm, n_a=n_a)
    return pl.pallas_call(
        body,
        grid=(n_a + n_b,),
        in_specs=[pl.BlockSpec((2 * tm,), lambda i: (i,), memory_space=pltpu.SMEM)]
        + _two_part_specs(tm * TILE_ROWS, n_a, n_b),
        out_specs=pl.BlockSpec(memory_space=pl.ANY),
        out_shape=jax.ShapeDtypeStruct((2 * t * TILE_ROWS, LANES), F32),
        scratch_shapes=[pltpu.SemaphoreType.DMA(())],
        compiler_params=_params("arbitrary"),
        name="scatter",
    )(slot, xa, xb)


def _expert_body(tile_ref, exp_ref, lo_ref, hi_ref, first_ref, last_ref, xs_ref, gm_ref, wg_ref, wu_ref, wd_ref,
                 ys_ref, hb_scr, acc_scr, *, tme):
    k = pl.program_id(0)
    first = first_ref[k] == 1
    last = last_ref[k] == 1
    nonempty = hi_ref[k] > lo_ref[k]

    @pl.when(first)
    def _():
        hb_scr[...] = _rms(_tiles_to_rows(xs_ref, tme), gm_ref[...]).astype(BF16)

    @pl.when(first & jnp.logical_not(nonempty))
    def _():
        acc_scr[...] = jnp.zeros_like(acc_scr)

    @pl.when(nonempty)
    def _():
        hb = hb_scr[...]
        act = jax.nn.silu(_bdot(hb, wg_ref[0])) * _bdot(hb, wu_ref[0])
        row = lax.broadcasted_iota(jnp.int32, (tme, 1), 0)
        y = jnp.where((row >= lo_ref[k]) & (row < hi_ref[k]), _bdot(act.astype(BF16), wd_ref[0]), 0.0)

        @pl.when(first & last)
        def _():
            _rows_to_tiles(ys_ref, y, tme)

        @pl.when(first & jnp.logical_not(last))
        def _():
            acc_scr[...] = y

        @pl.when(jnp.logical_not(first))
        def _():
            acc_scr[...] += y

    @pl.when(last & jnp.logical_not(first & nonempty))
    def _():
        _rows_to_tiles(ys_ref, acc_scr[...], tme)


def _experts(work, xs, g_moe, wg, wu, wd, tme):
    n_work = work[0].shape[0]
    body = functools.partial(_expert_body, tme=tme)
    tile_blk = pl.BlockSpec((tme * TILE_ROWS, LANES), lambda k, tile, *_: (tile[k], 0))

    def wspec(shape):
        return pl.BlockSpec(shape, lambda k, tile, exp, *_: (exp[k], 0, 0))

    grid_spec = pltpu.PrefetchScalarGridSpec(
        num_scalar_prefetch=len(work),
        grid=(n_work,),
        in_specs=[tile_blk, pl.BlockSpec((1, D_MODEL), lambda k, *_: (0, 0)),
                  wspec((1, D_MODEL, D_FF_EXPERT)), wspec((1, D_MODEL, D_FF_EXPERT)),
                  wspec((1, D_FF_EXPERT, D_MODEL))],
        out_specs=tile_blk,
        scratch_shapes=[pltpu.VMEM((tme, D_MODEL), BF16), pltpu.VMEM((tme, D_MODEL), F32)],
    )
    return pl.pallas_call(
        body,
        grid_spec=grid_spec,
        out_shape=jax.ShapeDtypeStruct(xs.shape, F32),
        compiler_params=_params("arbitrary"),
        name="experts",
    )(*work, xs, g_moe, wg, wu, wd)


def _combine_body(slot_ref, slot_next_ref, xa_ref, xb_ref, info_ref, ys_hbm, gf_ref, ya_ref, yb_ref, b0, b1, sem,
                  *, tm, nstep, n_a):
    i = pl.program_id(0)
    cur = i % 2

    def gather(slots, buf):
        def issue(r, c):
            _tile_copy(ys_hbm, slots[2 * r], b0.at[buf], r, sem.at[buf]).start(priority=0)
            _tile_copy(ys_hbm, slots[2 * r + 1], b1.at[buf], r, sem.at[buf]).start(priority=1)
            return c

        lax.fori_loop(0, tm, issue, 0, unroll=8)

    @pl.when(i == 0)
    def _():
        gather(slot_ref, 0)

    @pl.when(i + 1 < nstep)
    def _():
        gather(slot_next_ref, 1 - cur)

    def drain(r, c):
        _tile_copy(ys_hbm, 0, b0.at[cur], r, sem.at[cur]).wait()
        _tile_copy(ys_hbm, 0, b1.at[cur], r, sem.at[cur]).wait()
        return c

    lax.fori_loop(0, tm, drain, 0, unroll=8)
    info = info_ref[...]
    moe = info[:, 2:3] * _tiles_to_rows(b0.at[cur], tm) + info[:, 3:4] * _tiles_to_rows(b1.at[cur], tm)

    def finish(x2t_ref, y_ref):
        y_ref[...] = _rms(_tiles_to_rows(x2t_ref, tm) + moe, gf_ref[...])

    _on_part(n_a, finish, (xa_ref, xb_ref), (ya_ref, yb_ref))


def _combine(slot, xa, xb, info, ys, g_final, tm=256):
    n_a, n_b = xa.shape[0] // (tm * TILE_ROWS), xb.shape[0] // (tm * TILE_ROWS)
    nstep = n_a + n_b
    body = functools.partial(_combine_body, tm=tm, nstep=nstep, n_a=n_a)
    buf = pltpu.VMEM((2, tm * TILE_ROWS, LANES), F32)
    return pl.pallas_call(
        body,
        grid=(nstep,),
        in_specs=[pl.BlockSpec((2 * tm,), lambda i: (i,), memory_space=pltpu.SMEM),
                  pl.BlockSpec((2 * tm,), lambda i: (jnp.minimum(i + 1, nstep - 1),), memory_space=pltpu.SMEM)]
        + _two_part_specs(tm * TILE_ROWS, n_a, n_b) + [
                  pl.BlockSpec((tm, ROUTER_LANES), lambda i: (i, 0)),
                  pl.BlockSpec(memory_space=pl.ANY),
                  pl.BlockSpec((1, D_MODEL), lambda i: (0, 0))],
        out_specs=[pl.BlockSpec((tm, D_MODEL), lambda i: (jnp.minimum(i, n_a - 1), 0)),
                   pl.BlockSpec((tm, D_MODEL), lambda i: (jnp.clip(i - n_a, 0, n_b - 1), 0))],
        out_shape=[jax.ShapeDtypeStruct((n_a * tm, D_MODEL), F32), jax.ShapeDtypeStruct((n_b * tm, D_MODEL), F32)],
        scratch_shapes=[buf, buf, pltpu.SemaphoreType.DMA((2,))],
        compiler_params=_params("arbitrary"),
        name="combine",
    )(slot, slot, xa, xb, info, ys, g_final)


def _moe(xa, xb, w, tme=512):
    info, cnt = _route(xa, xb, w["g_moe"], w["wr_hi"], w["wr_lo"])
    slot, work = _route_meta(info, cnt, tme)
    xs = _scatter(slot, xa, xb)
    ys = _experts(work, xs, w["g_moe"], w["moe_wg"], w["moe_wu"], w["moe_wd"], tme)
    return _combine(slot, xa, xb, info, ys, w["g_final"])


def _layer(x3d, k4, v4, s5_h0, lru_h0, lru_conv, w, chunk, nb, tq):
    bsz, seq, _ = x3d.shape
    x2d = x3d.reshape(bsz * seq, D_MODEL)
    lru_w = (w["conv_w"], w["conv_b"], w["lru_wg"], w["lru_bg"], w["lru_lam"])
    tail = SUBLANES - (CONV_W - 1)
    if lru_h0 is None:
        u, yl, hl, xt = _mixlru(x2d, w["g_mix"], w["w_in"], *lru_w, bsz, seq)
        lru_h = hl.reshape(bsz, SUBLANES, D_LRU)[:, 0]
        conv_new = xt.reshape(bsz, SUBLANES, D_LRU)[:, tail:]
    else:
        u, xl, gl = _mixin(x2d, w["g_mix"], w["w_in"])
        hist = jnp.pad(lru_conv, ((0, 0), (tail, 0), (0, 0))).reshape(bsz * seq, D_LRU)
        yl, hall = _lru_dec(xl, gl, hist, jnp.repeat(lru_h0, seq, axis=0), *lru_w)
        lru_h = hall.reshape(bsz, seq, D_LRU)[:, seq - 1]
        conv_new = xl.reshape(bsz, seq, D_LRU)[:, seq - (CONV_W - 1):]

    nstate = S5_GROUPS * S5_STATE
    if s5_h0 is None:
        zero = jnp.zeros((bsz, 1, nstate), F32)
        ys, s5_re, s5_im = _s5(u, w["s5"][chunk], zero, zero, chunk, seq // chunk, True)
    else:
        ys, s5_re, s5_im = _s5(u, w["s5"][chunk], s5_h0[0].reshape(bsz, nstate), s5_h0[1].reshape(bsz, nstate),
                               chunk, bsz, False)
    s5_re = s5_re.reshape(bsz, S5_GROUPS, S5_STATE)
    s5_im = s5_im.reshape(bsz, S5_GROUPS, S5_STATE)

    x2t = _mixattn(x3d, ys.reshape(bsz, seq, D_S5), yl.reshape(bsz, seq, D_LRU), k4, v4, w, nb, tq)
    return x2t, s5_re, s5_im, lru_h, conv_new


def kernel(x_prompt, x_sample, mem_prompt, cache_mem_k, cache_mem_v, state_s5_re, state_s5_im, state_lru_h, state_lru_conv, g_mix, w_in, s5_lam_re, s5_lam_im, s5_log_dt, s5_b_re, s5_b_im, s5_c_re, s5_c_im, s5_d, s5_w_glu, s5_b_glu, lru_conv_w, lru_conv_b, lru_w_a, lru_b_a, lru_w_x, lru_b_x, lru_lam, w_out, g_xa, g_mem, xa_w_q, xa_w_k, xa_w_v, xa_w_o, g_moe, moe_w_group, moe_w_expert, moe_w_gate, moe_w_up, moe_w_down, g_final):
    depth = g_mix.shape[0]
    assert depth == 1, "single-layer step"
    l = 0
    bsz, seq, _ = x_prompt.shape
    dbsz, dseq, _ = x_sample.shape
    chunk = SUBLANES
    assert seq % chunk == 0 and dseq == chunk

    s5_args = (s5_lam_re[l], s5_lam_im[l], s5_log_dt[l], s5_b_re[l], s5_b_im[l], s5_c_re[l], s5_c_im[l], s5_d[l])
    wg, bg = _lru_weights(lru_w_a[l], lru_w_x[l], lru_b_a[l], lru_b_x[l])
    wr = jnp.concatenate([moe_w_group[l], moe_w_expert[l],
                          jnp.zeros((D_MODEL, ROUTER_LANES - N_EXPERT_GROUPS - N_EXPERTS), F32)], axis=1)
    wr_hi = wr.astype(BF16)
    wr_lo = (wr - wr_hi.astype(F32)).astype(BF16)
    w = {
        "g_mix": g_mix[l][None], "w_in": w_in[l].astype(BF16),
        "s5": {chunk: _s5_weights(chunk, *s5_args)},
        "conv_w": lru_conv_w[l], "conv_b": lru_conv_b[l][None], "lru_wg": wg, "lru_bg": bg,
        "lru_lam": lru_lam[l][None],
        "w_glu": s5_w_glu[l].astype(BF16), "b_glu": s5_b_glu[l][None], "w_out": w_out[l].astype(BF16),
        "g_xa": g_xa[l][None], "w_q": xa_w_q[l].astype(BF16), "w_o": xa_w_o[l].astype(BF16),
        "g_moe": g_moe[l][None], "wr_hi": wr_hi, "wr_lo": wr_lo,
        "moe_wg": moe_w_gate[l].astype(BF16), "moe_wu": moe_w_up[l].astype(BF16),
        "moe_wd": moe_w_down[l].astype(BF16), "g_final": g_final[None],
    }

    mk, mv = _memkv(mem_prompt.reshape(bsz * N_MEM, D_MODEL), g_mem[l][None],
                    xa_w_k[l].astype(BF16), xa_w_v[l].astype(BF16))
    xp, p_re, p_im, p_h, p_conv = _layer(x_prompt, mk.reshape(bsz, N_MEM, D_MODEL), mv.reshape(bsz, N_MEM, D_MODEL),
                                         None, None, None, w, chunk, nb=1, tq=512)
    xs, s_re, s_im, s_h, s_conv = _layer(x_sample, cache_mem_k[l], cache_mem_v[l],
                                         (state_s5_re[l], state_s5_im[l]), state_lru_h[l],
                                         state_lru_conv[l], w, chunk, nb=8, tq=dseq)
    yp, ysmp = _moe(xp, xs, w)
    yp = yp.reshape(bsz, seq, D_MODEL)
    ysmp = ysmp.reshape(dbsz, dseq, D_MODEL)

    return (yp, ysmp,
            mk.reshape(1, bsz, N_MEM, XA_HEADS, XA_HEAD_DIM), mv.reshape(1, bsz, N_MEM, XA_HEADS, XA_HEAD_DIM),
            p_re[None], p_im[None], p_h[None], p_conv[None],
            s_re[None], s_im[None], s_h[None], s_conv[None])
```

```python
import functools

import jax
import jax.numpy as jnp
import numpy as np
from jax import lax
from jax.experimental import pallas as pl
from jax.experimental.pallas import tpu as pltpu

F32 = jnp.float32
BF16 = jnp.bfloat16

D_MODEL = 1024
D_S5 = 512
D_LRU = 512
S5_GROUP = 16
S5_GROUPS = 32
S5_STATE = 64
LRU_HEADS = 8
LRU_HEAD_DIM = 64
CONV_W = 4
LRU_C = 8.0
N_MEM = 256
XA_HEADS = 4
XA_HEAD_DIM = 256
N_EXPERT_GROUPS = 4
EXPERTS_PER_GROUP = 8
N_EXPERTS = 32
D_FF_EXPERT = 256
EPS = 1e-6

SUBLANES = 8
LANES = 128
ROUTER_LANES = 128
VMEM_LIMIT = 56 * 1024 * 1024
DMA_UNROLL = 16


def _params(*sem):
    return pltpu.CompilerParams(dimension_semantics=sem, vmem_limit_bytes=VMEM_LIMIT)


def _rms(x, g):
    ms = jnp.mean(x * x, axis=-1, keepdims=True)
    return x * lax.rsqrt(ms + EPS) * g


def _bdot(a, b):
    return jnp.dot(a, b, preferred_element_type=F32)


def _memkv_body(m_ref, g_ref, wk_ref, wv_ref, k_ref, v_ref):
    hb = _rms(m_ref[...], g_ref[...]).astype(BF16)
    k_ref[...] = _bdot(hb, wk_ref[...])
    v_ref[...] = _bdot(hb, wv_ref[...])


def _memkv(mem2d, g_mem, wk, wv, tm=512):
    t = mem2d.shape[0]
    row = pl.BlockSpec((tm, D_MODEL), lambda i: (i, 0))
    full = pl.BlockSpec((D_MODEL, D_MODEL), lambda i: (0, 0))
    return pl.pallas_call(
        _memkv_body,
        grid=(t // tm,),
        in_specs=[row, pl.BlockSpec((1, D_MODEL), lambda i: (0, 0)), full, full],
        out_specs=[row, row],
        out_shape=[jax.ShapeDtypeStruct((t, D_MODEL), F32)] * 2,
        compiler_params=_params("parallel"),
        name="memkv",
    )(mem2d, g_mem, wk, wv)


def _mixin_body(x_ref, g_ref, w_ref, u_ref, xl_ref, gl_ref):
    hb = _rms(x_ref[...], g_ref[...]).astype(BF16)
    proj = _bdot(hb, w_ref[...])
    u_ref[...] = proj[:, :D_S5]
    xl_ref[...] = proj[:, D_S5:D_S5 + D_LRU]
    gl_ref[...] = proj[:, D_S5 + D_LRU:]


def _mixin(x2d, g_mix, w_in, tm=1024):
    t = x2d.shape[0]
    half = pl.BlockSpec((tm, D_S5), lambda i: (i, 0))
    return pl.pallas_call(
        _mixin_body,
        grid=(t // tm,),
        in_specs=[pl.BlockSpec((tm, D_MODEL), lambda i: (i, 0)),
                  pl.BlockSpec((1, D_MODEL), lambda i: (0, 0)),
                  pl.BlockSpec((D_MODEL, D_S5 + 2 * D_LRU), lambda i: (0, 0))],
        out_specs=[half, half, half],
        out_shape=[jax.ShapeDtypeStruct((t, D_S5), F32)] * 3,
        compiler_params=_params("parallel"),
        name="mixin",
    )(x2d, g_mix, w_in)


GROUPS_PER_BLOCK = LANES // S5_GROUP
PAIRS_PER_BLOCK = GROUPS_PER_BLOCK // 2
STATE_BLOCK = GROUPS_PER_BLOCK * S5_STATE
PW_ROWS = 24


def _s5_body(u_ref, wa_ref, wh_ref, pw_ref, d_ref, h0r_ref, h0i_ref, y_ref, hfr_ref, hfi_ref, hpr_scr, hpi_scr,
             *, chunk, nrow, scan):
    nh = chunk // SUBLANES
    lc = chunk * S5_GROUP
    slot = lax.broadcasted_iota(jnp.int32, (nrow, LANES), 1) // S5_GROUP
    in_slot = [slot == s for s in range(GROUPS_PER_BLOCK)]

    def pick(src, sel):
        out = src[sel(0)]
        for s in range(1, GROUPS_PER_BLOCK):
            out = jnp.where(in_slot[s], src[sel(s)], out)
        return out

    nat, skew = [], []
    for t in range(chunk):
        a = u_ref[pl.ds(t, nrow, stride=chunk), :]
        nat.append(a)
        s = t % GROUPS_PER_BLOCK
        skew.append(pltpu.roll(a, s * S5_GROUP, axis=1) if s else a)
    def lane_roll(x, slots):
        slots %= GROUPS_PER_BLOCK
        return pltpu.roll(x, slots * S5_GROUP, axis=1) if slots else x

    z = [[lane_roll(pick(skew, lambda s, hh=hh, m=m: SUBLANES * hh + (s - m) % GROUPS_PER_BLOCK), -m)
          for hh in range(nh)] for m in range(GROUPS_PER_BLOCK)]

    ys, er, ei = [], [], []
    for p in range(PAIRS_PER_BLOCK):
        lhs = jnp.concatenate(z[2 * p] + z[2 * p + 1], axis=1).astype(BF16)
        full = _bdot(lhs, wa_ref[p])
        ys.append(full[:, :2 * lc])
        er.append(full[:, 2 * lc:2 * lc + LANES])
        ei.append(full[:, 2 * lc + LANES:])
    er = jnp.concatenate(er, axis=1)
    ei = jnp.concatenate(ei, axis=1)

    if scan:
        ntile = nrow // SUBLANES
        xr = er.reshape(ntile, SUBLANES, STATE_BLOCK)
        xi = ei.reshape(ntile, SUBLANES, STATE_BLOCK)
        row = lax.broadcasted_iota(jnp.int32, xr.shape, 1)
        for k, d in enumerate((1, 2, 4)):
            pr, pi = pw_ref[0, k:k + 1, :], pw_ref[0, 4 + k:5 + k, :]
            sr, si = pltpu.roll(xr, d, axis=1), pltpu.roll(xi, d, axis=1)
            m = row >= d
            xr, xi = jnp.where(m, xr + pr * sr - pi * si, xr), jnp.where(m, xi + pr * si + pi * sr, xi)
        sxr = jnp.where(row >= 1, pltpu.roll(xr, 1, axis=1), 0.0)
        sxi = jnp.where(row >= 1, pltpu.roll(xi, 1, axis=1), 0.0)
        p8r, p8i = pw_ref[0, 3:4, :], pw_ref[0, 7:8, :]
        qr, qi = pw_ref[0, 8:16, :], pw_ref[0, 16:24, :]
        hr, hi = h0r_ref[0], h0i_ref[0]
        for t in range(ntile):
            rows = slice(SUBLANES * t, SUBLANES * (t + 1))
            hpr_scr[rows, :] = sxr[t] + qr * hr - qi * hi
            hpi_scr[rows, :] = sxi[t] + qr * hi + qi * hr
            hr, hi = (xr[t, SUBLANES - 1:] + p8r * hr - p8i * hi, xi[t, SUBLANES - 1:] + p8r * hi + p8i * hr)
        hfr_ref[0] = hr
        hfi_ref[0] = hi
        hpr, hpi = hpr_scr[...], hpi_scr[...]
    else:
        hpr, hpi = h0r_ref[...], h0i_ref[...]
        p1r, p1i = pw_ref[0, 0:1, :], pw_ref[0, 4:5, :]
        hfr_ref[...] = er + p1r * hpr - p1i * hpi
        hfi_ref[...] = ei + p1r * hpi + p1i * hpr

    yg = []
    for p in range(PAIRS_PER_BLOCK):
        lanes = slice(LANES * p, LANES * (p + 1))
        hp = jnp.concatenate([hpr[:, lanes], hpi[:, lanes]], axis=1).astype(BF16)
        out = ys[p] + _bdot(hp, wh_ref[p])
        for half in range(2):
            q = 2 * p + half
            yg.append([lane_roll(out[:, half * lc + hh * LANES:half * lc + (hh + 1) * LANES], q)
                       for hh in range(nh)])
    d = d_ref[...]
    for t in range(chunk):
        hh, s = divmod(t, GROUPS_PER_BLOCK)
        c = lane_roll(pick([g[hh] for g in yg], lambda sl, s=s: (sl - s) % GROUPS_PER_BLOCK), -s)
        y_ref[pl.ds(t, nrow, stride=chunk), :] = c + d * nat[t]


def _s5(u2d, weights, h0r, h0i, chunk, nrow, scan):
    wa, wh, pw, dvec = weights
    t = u2d.shape[0]
    lc = chunk * S5_GROUP
    rows = nrow * chunk
    nblk = D_S5 // LANES
    body = functools.partial(_s5_body, chunk=chunk, nrow=nrow, scan=scan)
    ublk = pl.BlockSpec((rows, LANES), lambda i, j: (i, j))
    if scan:
        hblk = pl.BlockSpec((1, 1, STATE_BLOCK), lambda i, j: (i, 0, j))
        hshape = jax.ShapeDtypeStruct((t // rows, 1, S5_GROUPS * S5_STATE), F32)
    else:
        hblk = pl.BlockSpec((nrow, STATE_BLOCK), lambda i, j: (i, j))
        hshape = jax.ShapeDtypeStruct((t // chunk, S5_GROUPS * S5_STATE), F32)
    return pl.pallas_call(
        body,
        grid=(t // rows, nblk),
        in_specs=[ublk,
                  pl.BlockSpec((PAIRS_PER_BLOCK, 2 * lc, 2 * lc + 2 * LANES), lambda i, j: (j, 0, 0)),
                  pl.BlockSpec((PAIRS_PER_BLOCK, 2 * LANES, 2 * lc), lambda i, j: (j, 0, 0)),
                  pl.BlockSpec((1, PW_ROWS, STATE_BLOCK), lambda i, j: (j, 0, 0)),
                  pl.BlockSpec((1, LANES), lambda i, j: (0, j)),
                  hblk, hblk],
        out_specs=[ublk, hblk, hblk],
        out_shape=[jax.ShapeDtypeStruct((t, D_S5), F32), hshape, hshape],
        scratch_shapes=[pltpu.VMEM((nrow, STATE_BLOCK), F32), pltpu.VMEM((nrow, STATE_BLOCK), F32)],
        compiler_params=_params("parallel", "parallel"),
        name="s5",
    )(u2d, wa, wh, pw, dvec, h0r, h0i)


def _s5_weights(chunk, lam_re, lam_im, log_dt, b_re, b_im, c_re, c_im, d_skip):
    hi = lax.Precision.HIGHEST
    dt = jnp.exp(log_dt)[:, None]
    mag = jnp.exp(lam_re * dt)
    ab_re = mag * jnp.cos(lam_im * dt)
    ab_im = mag * jnp.sin(lam_im * dt)
    den = lam_re * lam_re + lam_im * lam_im
    nr = ab_re - 1.0
    f_re = (nr * lam_re + ab_im * lam_im) / den
    f_im = (ab_im * lam_re - nr * lam_im) / den
    bb_re = f_re[..., None] * b_re - f_im[..., None] * b_im
    bb_im = f_re[..., None] * b_im + f_im[..., None] * b_re
    def powers(ks):
        k = jnp.asarray(np.asarray(ks, np.float32))[:, None, None]
        m = jnp.exp(k * (lam_re * dt))
        return m * jnp.cos(k * (lam_im * dt)), m * jnp.sin(k * (lam_im * dt))

    pw_re, pw_im = powers(range(chunk + 1))
    pb_re = pw_re[:chunk, ..., None] * bb_re - pw_im[:chunk, ..., None] * bb_im
    pb_im = pw_re[:chunk, ..., None] * bb_im + pw_im[:chunk, ..., None] * bb_re
    kk = (jnp.einsum("gon,kgni->kgoi", c_re, pb_re, precision=hi)
          - jnp.einsum("gon,kgni->kgoi", c_im, pb_im, precision=hi))
    npair = S5_GROUPS // 2
    lc = chunk * S5_GROUP
    row0 = kk.transpose(1, 3, 0, 2).reshape(S5_GROUPS, S5_GROUP, lc)
    row0 = jnp.pad(row0, ((0, 0), (0, 0), (lc, 0)))
    conv = jnp.stack([row0[:, :, lc - S5_GROUP * ti:2 * lc - S5_GROUP * ti] for ti in range(chunk)], axis=1)
    conv = conv.reshape(npair, 2, lc, lc)
    ends = [x[::-1].transpose(1, 0, 3, 2).reshape(npair, 2, lc, S5_STATE) for x in (pb_re, pb_im)]
    zc = jnp.zeros((npair, lc, lc), F32)
    zs = jnp.zeros((npair, lc, S5_STATE), F32)
    wa = jnp.concatenate([
        jnp.concatenate([conv[:, 0], zc, ends[0][:, 0], zs, ends[1][:, 0], zs], axis=-1),
        jnp.concatenate([zc, conv[:, 1], zs, ends[0][:, 1], zs, ends[1][:, 1]], axis=-1)], axis=1)
    gr = c_re[None] * pw_re[1:, :, None, :] - c_im[None] * pw_im[1:, :, None, :]
    gi = c_re[None] * pw_im[1:, :, None, :] + c_im[None] * pw_re[1:, :, None, :]
    zr = jnp.zeros((npair, S5_STATE, lc), F32)
    blocks = []
    for x in (gr, -gi):
        hy = x.transpose(1, 3, 0, 2).reshape(npair, 2, S5_STATE, lc)
        blocks += [jnp.concatenate([hy[:, 0], zr], axis=-1), jnp.concatenate([zr, hy[:, 1]], axis=-1)]
    wh = jnp.concatenate(blocks, axis=1)
    tab = [1, 2, 4, 8] + list(range(SUBLANES))
    qr, qi = powers([chunk * k for k in tab])
    nblk = D_S5 // LANES
    pw = jnp.concatenate([qr[:4], qi[:4], qr[4:], qi[4:]]).reshape(PW_ROWS, nblk, STATE_BLOCK).transpose(1, 0, 2)
    return wa.astype(BF16), wh.astype(BF16), pw, d_skip.reshape(1, D_S5)


def _lru_ab(xc, wg_ref, bg_ref, lam_ref):
    xb = xc.astype(BF16)
    half = D_LRU // 2
    g0 = _bdot(xb[:, :half], wg_ref[0]) + bg_ref[0]
    g1 = _bdot(xb[:, half:], wg_ref[1]) + bg_ref[1]
    r = jax.nn.sigmoid(jnp.concatenate([g0[:, :half], g1[:, :half]], axis=1))
    ig = jax.nn.sigmoid(jnp.concatenate([g0[:, half:], g1[:, half:]], axis=1))
    zl = -lam_ref[...]
    softplus = jnp.maximum(zl, 0.0) + jnp.log1p(jnp.exp(-jnp.abs(zl)))
    log_a = -LRU_C * r * softplus
    a = jnp.exp(log_a)
    b = jnp.sqrt(-jnp.tanh(log_a) * (a * a + 1.0)) * (ig * xc)
    return a, b


def _tile_scan(a, b):
    shape = a.shape
    a = a.reshape(shape[0] // SUBLANES, SUBLANES, shape[1])
    b = b.reshape(a.shape)
    row = lax.broadcasted_iota(jnp.int32, a.shape, 1)
    for d in (1, 2, 4):
        a_prev = pltpu.roll(a, d, axis=1)
        b_prev = pltpu.roll(b, d, axis=1)
        m = row >= d
        b = jnp.where(m, b + a * b_prev, b)
        a = jnp.where(m, a * a_prev, a)
    return a.reshape(shape), b.reshape(shape)


def _mixlru_body(x_ref, gmix_ref, win_ref, cw_ref, cb_ref, wg_ref, bg_ref, lam_ref, u_ref, y_ref, hl_ref, xt_ref,
                 xp_scr, a_scr, b_scr, h_scr, hc_scr, *, ts):
    ti = pl.program_id(1)

    @pl.when(ti == 0)
    def _():
        xp_scr[0:SUBLANES, :] = jnp.zeros((SUBLANES, D_LRU), F32)
        hc_scr[...] = jnp.zeros((SUBLANES, D_LRU), F32)

    proj = _bdot(_rms(x_ref[...], gmix_ref[...]).astype(BF16), win_ref[...])
    u_ref[...] = proj[:, :D_S5]
    xl = proj[:, D_S5:D_S5 + D_LRU]
    gl = proj[:, D_S5 + D_LRU:]
    xt_ref[...] = xl[ts - SUBLANES:, :]
    xp_scr[SUBLANES:SUBLANES + ts, :] = xl
    xc = cb_ref[...] + xl * cw_ref[CONV_W - 1:CONV_W, :]
    for j in range(1, CONV_W):
        xc = xc + xp_scr[SUBLANES - j:SUBLANES - j + ts, :] * cw_ref[CONV_W - 1 - j:CONV_W - j, :]
    xp_scr[0:SUBLANES, :] = xl[ts - SUBLANES:, :]
    a, b = _lru_ab(xc, wg_ref, bg_ref, lam_ref)
    a, b = _tile_scan(a, b)
    a_scr[...] = a
    b_scr[...] = b

    def step(i, hin):
        rows = pl.ds(pl.multiple_of(i * SUBLANES, SUBLANES), SUBLANES)
        h = b_scr[rows, :] + a_scr[rows, :] * hin
        h_scr[rows, :] = h
        return h[SUBLANES - 1:SUBLANES, :]

    hlast = lax.fori_loop(0, ts // SUBLANES, step, hc_scr[0:1, :], unroll=8)
    hc_scr[...] = jnp.broadcast_to(hlast, (SUBLANES, D_LRU))
    hl_ref[...] = hc_scr[...]
    y_ref[...] = h_scr[...] * jax.nn.gelu(gl)


def _lru_dec_body(xl_ref, gl_ref, hist_ref, h0_ref, cw_ref, cb_ref, wg_ref, bg_ref, lam_ref, y_ref, hl_ref, *, tm):
    xl = xl_ref[...]
    hist = hist_ref[...]
    row = lax.broadcasted_iota(jnp.int32, xl.shape, 0) % SUBLANES
    xc = cb_ref[...] + xl * cw_ref[CONV_W - 1:CONV_W, :]
    for j in range(1, CONV_W):
        prev = jnp.where(row >= j, pltpu.roll(xl, j, axis=0), pltpu.roll(hist, tm - SUBLANES + j, axis=0))
        xc = xc + prev * cw_ref[CONV_W - 1 - j:CONV_W - j, :]
    a, b = _lru_ab(xc, wg_ref, bg_ref, lam_ref)
    a, b = _tile_scan(a, b)
    h = b + a * h0_ref[...]
    hl_ref[...] = h
    y_ref[...] = h * jax.nn.gelu(gl_ref[...])


def _lru_weights(w_a, w_x, b_a, b_x):
    eye = jnp.eye(LRU_HEADS, dtype=F32)
    bd_a = jnp.einsum("hij,hg->higj", w_a, eye).reshape(D_LRU, D_LRU)
    bd_x = jnp.einsum("hij,hg->higj", w_x, eye).reshape(D_LRU, D_LRU)
    half = D_LRU // 2
    wg = jnp.stack([jnp.concatenate([bd_a[:half, :half], bd_x[:half, :half]], axis=1),
                    jnp.concatenate([bd_a[half:, half:], bd_x[half:, half:]], axis=1)]).astype(BF16)
    ba = b_a.reshape(1, D_LRU)
    bx = b_x.reshape(1, D_LRU)
    bg = jnp.stack([jnp.concatenate([ba[:, :half], bx[:, :half]], axis=1),
                    jnp.concatenate([ba[:, half:], bx[:, half:]], axis=1)])
    return wg, bg


def _lru_common_specs(const):
    return [const((CONV_W, D_LRU)), const((1, D_LRU)), const((2, D_LRU // 2, D_LRU)),
            const((2, 1, D_LRU)), const((1, D_LRU))]


def _mixlru(x2d, g_mix, w_in, cw, cb, wg, bg, lam, bsz, seq, ts=1024):
    nt = seq // ts
    blk = pl.BlockSpec((ts, D_LRU), lambda b, t: (b * nt + t, 0))
    per_seq = pl.BlockSpec((SUBLANES, D_LRU), lambda b, t: (b, 0))

    def const(shape):
        return pl.BlockSpec(shape, lambda b, t: (0,) * len(shape))

    body = functools.partial(_mixlru_body, ts=ts)
    tile = jax.ShapeDtypeStruct((bsz * SUBLANES, D_LRU), F32)
    return pl.pallas_call(
        body,
        grid=(bsz, nt),
        in_specs=[pl.BlockSpec((ts, D_MODEL), lambda b, t: (b * nt + t, 0)), const((1, D_MODEL)),
                  const((D_MODEL, D_S5 + 2 * D_LRU))] + _lru_common_specs(const),
        out_specs=[blk, blk, per_seq, per_seq],
        out_shape=[jax.ShapeDtypeStruct((bsz * seq, D_S5), F32), jax.ShapeDtypeStruct((bsz * seq, D_LRU), F32),
                   tile, tile],
        scratch_shapes=[pltpu.VMEM((ts + SUBLANES, D_LRU), F32), pltpu.VMEM((ts, D_LRU), F32),
                        pltpu.VMEM((ts, D_LRU), F32), pltpu.VMEM((ts, D_LRU), F32),
                        pltpu.VMEM((SUBLANES, D_LRU), F32)],
        compiler_params=_params("arbitrary", "arbitrary"),
        name="mixlru",
    )(x2d, g_mix, w_in, cw, cb, wg, bg, lam)


def _lru_dec(xl, gl, hist, h0rep, cw, cb, wg, bg, lam, tm=256):
    t = xl.shape[0]
    blk = pl.BlockSpec((tm, D_LRU), lambda i: (i, 0))

    def const(shape):
        return pl.BlockSpec(shape, lambda i: (0,) * len(shape))

    body = functools.partial(_lru_dec_body, tm=tm)
    return pl.pallas_call(
        body,
        grid=(t // tm,),
        in_specs=[blk, blk, blk, blk] + _lru_common_specs(const),
        out_specs=[blk, blk],
        out_shape=[jax.ShapeDtypeStruct((t, D_LRU), F32)] * 2,
        compiler_params=_params("parallel"),
        name="lru_dec",
    )(xl, gl, hist, h0rep, cw, cb, wg, bg, lam)


def _softmax(sc):
    p = jnp.exp(sc - jnp.max(sc, axis=-1, keepdims=True))
    return p / jnp.sum(p, axis=-1, keepdims=True)


def _attend(q, k_ref, v_ref, i, tq):
    scale = XA_HEAD_DIM ** -0.5
    heads = [q[:, h * XA_HEAD_DIM:(h + 1) * XA_HEAD_DIM] for h in range(XA_HEADS)]
    if len(k_ref.shape) == 4:
        kf = k_ref[i].reshape(N_MEM * XA_HEADS, XA_HEAD_DIM).astype(BF16)
        vf = v_ref[i].reshape(N_MEM * XA_HEADS, XA_HEAD_DIM).astype(BF16)
        sc = lax.dot_general(jnp.concatenate(heads, axis=0), kf, (((1,), (1,)), ((), ())),
                             preferred_element_type=F32) * scale
        own = (lax.broadcasted_iota(jnp.int32, sc.shape, 1) % XA_HEADS
               == lax.broadcasted_iota(jnp.int32, sc.shape, 0) // tq)
        oh = _bdot(_softmax(jnp.where(own, sc, -jnp.inf)).astype(BF16), vf)
        outs = [oh[h * tq:(h + 1) * tq] for h in range(XA_HEADS)]
    else:
        kb = k_ref[i].astype(BF16)
        vb = v_ref[i].astype(BF16)
        outs = []
        for h in range(XA_HEADS):
            cols = slice(h * XA_HEAD_DIM, (h + 1) * XA_HEAD_DIM)
            sc = lax.dot_general(heads[h], kb[:, cols], (((1,), (1,)), ((), ())),
                                 preferred_element_type=F32) * scale
            outs.append(_bdot(_softmax(sc).astype(BF16), vb[:, cols]))
    return jnp.concatenate(outs, axis=1)


def _mixattn_body(x_ref, ys_ref, yl_ref, k_ref, v_ref, wglu_ref, bglu_ref, wout_ref, gxa_ref, wq_ref, wo_ref,
                  x2t_ref, *, nb, tq):
    rows = nb * tq
    ys = jax.nn.gelu(ys_ref[...].reshape(rows, D_S5))
    gate = jax.nn.sigmoid(_bdot(ys.astype(BF16), wglu_ref[...]) + bglu_ref[...])
    s5 = (ys * gate).astype(BF16)
    yl = yl_ref[...].reshape(rows, D_LRU).astype(BF16)
    x1 = x_ref[...].reshape(rows, D_MODEL) + _bdot(s5, wout_ref[0:D_S5, :]) + _bdot(yl, wout_ref[D_S5:, :])
    q = _bdot(_rms(x1, gxa_ref[...]).astype(BF16), wq_ref[...]).astype(BF16)
    o = jnp.concatenate([_attend(q[i * tq:(i + 1) * tq], k_ref, v_ref, i, tq) for i in range(nb)], axis=0)
    _rows_to_tiles(x2t_ref, x1 + _bdot(o.astype(BF16), wo_ref[...]), rows)


def _mixattn(x3, ys3, yl3, k4, v4, w, nb, tq):
    bsz, seq, _ = x3.shape
    nt = seq // tq
    blk = pl.BlockSpec((nb, tq, D_MODEL), lambda b, t: (b, t, 0))
    half = pl.BlockSpec((nb, tq, D_S5), lambda b, t: (b, t, 0))
    kv = pl.BlockSpec((nb,) + k4.shape[1:], lambda b, t: (b,) + (0,) * (k4.ndim - 1))

    def const(shape):
        return pl.BlockSpec(shape, lambda b, t: (0,) * len(shape))

    body = functools.partial(_mixattn_body, nb=nb, tq=tq)
    return pl.pallas_call(
        body,
        grid=(bsz // nb, nt),
        in_specs=[blk, half, half, kv, kv, const((D_S5, D_S5)), const((1, D_S5)), const((D_MODEL, D_MODEL)),
                  const((1, D_MODEL)), const((D_MODEL, D_MODEL)), const((D_MODEL, D_MODEL))],
        out_specs=pl.BlockSpec((nb * tq * TILE_ROWS, LANES), lambda b, t: (b * nt + t, 0)),
        out_shape=jax.ShapeDtypeStruct((bsz * seq * TILE_ROWS, LANES), F32),
        compiler_params=_params("parallel", "parallel"),
        name="mixattn",
    )(x3, ys3, yl3, k4, v4, w["w_glu"], w["b_glu"], w["w_out"], w["g_xa"], w["w_q"], w["w_o"])


def _router(hm, wr_hi_ref, wr_lo_ref):
    a_hi = hm.astype(BF16)
    a_lo = (hm - a_hi.astype(F32)).astype(BF16)
    logits = _bdot(a_hi, wr_hi_ref[...]) + (_bdot(a_hi, wr_lo_ref[...]) + _bdot(a_lo, wr_hi_ref[...]))
    lane_i = lax.broadcasted_iota(jnp.int32, logits.shape, 1)
    lane = lane_i.astype(F32)
    neg = -jnp.inf
    big = float(ROUTER_LANES)
    is_g = lane_i < N_EXPERT_GROUPS
    glog = jnp.where(is_g, logits, neg)
    gmax = jnp.max(glog, axis=-1, keepdims=True)
    gsel = jnp.min(jnp.where(glog == gmax, lane, big), axis=-1, keepdims=True)
    pg_sel = 1.0 / jnp.sum(jnp.where(is_g, jnp.exp(logits - gmax), 0.0), axis=-1, keepdims=True)
    eidx = lane_i - N_EXPERT_GROUPS
    in_group = (eidx >= 0) & (eidx < N_EXPERTS) & ((eidx >> 3).astype(F32) == gsel)
    el = jnp.where(in_group, logits, neg)
    v1 = jnp.max(el, axis=-1, keepdims=True)
    i1 = jnp.min(jnp.where(el == v1, lane, big), axis=-1, keepdims=True)
    el2 = jnp.where(lane == i1, neg, el)
    v2 = jnp.max(el2, axis=-1, keepdims=True)
    i2 = jnp.min(jnp.where(el2 == v2, lane, big), axis=-1, keepdims=True)
    e2 = jnp.exp(v2 - v1)
    w1 = pg_sel / (1.0 + e2)
    w2 = pg_sel * e2 / (1.0 + e2)
    return i1, i2, w1, w2


TILE_ROWS = D_MODEL // LANES


def _tiles_to_rows(ref, n, base=0):
    return jnp.concatenate([ref[pl.ds(base + s, n, stride=TILE_ROWS), :] for s in range(TILE_ROWS)], axis=1)


def _rows_to_tiles(ref, val, n, base=0):
    for s in range(TILE_ROWS):
        ref[pl.ds(base + s, n, stride=TILE_ROWS), :] = val[:, s * LANES:(s + 1) * LANES]


def _two_part_specs(block_rows, n_a, n_b):
    return [pl.BlockSpec((block_rows, LANES), lambda i: (jnp.minimum(i, n_a - 1), 0)),
            pl.BlockSpec((block_rows, LANES), lambda i: (jnp.clip(i - n_a, 0, n_b - 1), 0))]


def _on_part(n_a, fn, *ref_pairs):
    i = pl.program_id(0)
    pl.when(i < n_a)(lambda: fn(*[p[0] for p in ref_pairs]))
    pl.when(i >= n_a)(lambda: fn(*[p[1] for p in ref_pairs]))


def _route_body(xa_ref, xb_ref, gm_ref, wr_hi_ref, wr_lo_ref, info_ref, cnt_ref, cnt_scr, *, tm, n_a):
    @pl.when(pl.program_id(0) == 0)
    def _():
        cnt_scr[...] = jnp.zeros_like(cnt_scr)

    _on_part(n_a, functools.partial(_route_step, gm_ref, wr_hi_ref, wr_lo_ref, info_ref, cnt_ref, cnt_scr, tm),
             (xa_ref, xb_ref))


def _route_step(gm_ref, wr_hi_ref, wr_lo_ref, info_ref, cnt_ref, cnt_scr, tm, x2t_ref):
    hm = _rms(_tiles_to_rows(x2t_ref, tm), gm_ref[...])
    i1, i2, w1, w2 = _router(hm, wr_hi_ref, wr_lo_ref)
    lane_i = lax.broadcasted_iota(jnp.int32, (tm, ROUTER_LANES), 1)
    lane = lane_i.astype(F32)
    chosen = ((lane == i1) | (lane == i2)).astype(F32)
    earlier = (lax.broadcasted_iota(jnp.int32, (tm, tm), 1) < lax.broadcasted_iota(jnp.int32, (tm, tm), 0))
    before = _bdot(earlier.astype(BF16), chosen.astype(BF16)) + cnt_scr[0:1, :]
    r1 = jnp.sum(jnp.where(lane == i1, before, 0.0), axis=-1, keepdims=True)
    r2 = jnp.sum(jnp.where(lane == i2, before, 0.0), axis=-1, keepdims=True)
    cols = (i1, i2, w1, w2, r1, r2)
    info = jnp.zeros((tm, ROUTER_LANES), F32)
    for j, col in enumerate(cols):
        info = jnp.where(lane_i == j, col, info)
    info_ref[...] = info
    cnt_scr[...] = cnt_scr[...] + jnp.sum(chosen, axis=0, keepdims=True)
    cnt_ref[...] = cnt_scr[...]


def _route(xa, xb, g_moe, wr_hi, wr_lo, tm=1024):
    n_a, n_b = xa.shape[0] // (tm * TILE_ROWS), xb.shape[0] // (tm * TILE_ROWS)
    t = (n_a + n_b) * tm

    def const(shape):
        return pl.BlockSpec(shape, lambda i: (0,) * len(shape))

    body = functools.partial(_route_body, tm=tm, n_a=n_a)
    return pl.pallas_call(
        body,
        grid=(n_a + n_b,),
        in_specs=_two_part_specs(tm * TILE_ROWS, n_a, n_b) + [
            const((1, D_MODEL)), const((D_MODEL, ROUTER_LANES)), const((D_MODEL, ROUTER_LANES))],
        out_specs=[pl.BlockSpec((tm, ROUTER_LANES), lambda i: (i, 0)), const((SUBLANES, ROUTER_LANES))],
        out_shape=[jax.ShapeDtypeStruct((t, ROUTER_LANES), F32),
                   jax.ShapeDtypeStruct((SUBLANES, ROUTER_LANES), F32)],
        scratch_shapes=[pltpu.VMEM((SUBLANES, ROUTER_LANES), F32)],
        compiler_params=_params("arbitrary"),
        name="route",
    )(xa, xb, g_moe, wr_hi, wr_lo)


def _route_meta(info, cnt, tme):
    t = info.shape[0]
    n_tiles = 2 * t // tme
    counts = cnt[0, N_EXPERT_GROUPS:N_EXPERT_GROUPS + N_EXPERTS].astype(jnp.int32)
    ends = jnp.cumsum(counts)
    starts = ends - counts
    e = info[:, 0:2].astype(jnp.int32) - N_EXPERT_GROUPS
    onehot = e[:, :, None] == jnp.arange(N_EXPERTS, dtype=jnp.int32)
    slot = (jnp.sum(jnp.where(onehot, starts, 0), axis=-1) + info[:, 4:6].astype(jnp.int32)).reshape(-1)
    pts = jnp.concatenate([jnp.arange(n_tiles, dtype=jnp.int32) * tme, starts])
    n = pts.shape[0]
    idx = jnp.arange(n, dtype=jnp.int32)
    pos = jnp.sum((pts[None, :] < pts[:, None]) | ((pts[None, :] == pts[:, None]) & (idx[None, :] < idx[:, None])),
                  axis=1)
    lo_abs = jnp.sum(jnp.where(pos[:, None] == idx[None, :], pts[:, None], 0), axis=0)
    hi_abs = jnp.concatenate([lo_abs[1:], jnp.full((1,), 2 * t, jnp.int32)])
    tile = jnp.minimum(lo_abs // tme, n_tiles - 1)
    expert = jnp.clip(jnp.sum(starts[None, :] <= lo_abs[:, None], axis=1) - 1, 0, N_EXPERTS - 1)
    first = jnp.concatenate([jnp.ones((1,), jnp.int32), (tile[1:] != tile[:-1]).astype(jnp.int32)])
    last = jnp.concatenate([(tile[1:] != tile[:-1]).astype(jnp.int32), jnp.ones((1,), jnp.int32)])
    work = [a.astype(jnp.int32) for a in (tile, expert, lo_abs - tile * tme, hi_abs - tile * tme, first, last)]
    return slot.astype(jnp.int32), work


def _tile_copy(src, src_tok, dst, dst_tok, sem):
    return pltpu.make_async_copy(src.at[pl.ds(pl.multiple_of(src_tok * TILE_ROWS, TILE_ROWS), TILE_ROWS), :],
                                 dst.at[pl.ds(pl.multiple_of(dst_tok * TILE_ROWS, TILE_ROWS), TILE_ROWS), :], sem)


def _scatter_body(slot_ref, xa_ref, xb_ref, xs_hbm, sem, *, tm, n_a):
    _on_part(n_a, functools.partial(_scatter_step, slot_ref, xs_hbm, sem, tm), (xa_ref, xb_ref))


def _scatter_step(slot_ref, xs_hbm, sem, tm, x2t_ref):
    def issue(r, c):
        _tile_copy(x2t_ref, r, xs_hbm, slot_ref[2 * r], sem).start(priority=0)
        _tile_copy(x2t_ref, r, xs_hbm, slot_ref[2 * r + 1], sem).start(priority=1)
        return c

    lax.fori_loop(0, tm, issue, 0, unroll=DMA_UNROLL)

    def drain(r, c):
        _tile_copy(x2t_ref, r, xs_hbm, 0, sem).wait()
        _tile_copy(x2t_ref, r, xs_hbm, 0, sem).wait()
        return c

    lax.fori_loop(0, tm, drain, 0, unroll=DMA_UNROLL)


def _scatter(slot, xa, xb, tm=1024):
    n_a, n_b = xa.shape[0] // (tm * TILE_ROWS), xb.shape[0] // (tm * TILE_ROWS)
    t = (n_a + n_b) * tm
    body = functools.partial(_scatter_body, tm=tm, n_a=n_a)
    return pl.pallas_call(
        body,
        grid=(n_a + n_b,),
        in_specs=[pl.BlockSpec((2 * tm,), lambda i: (i,), memory_space=pltpu.SMEM)]
        + _two_part_specs(tm * TILE_ROWS, n_a, n_b),
        out_specs=pl.BlockSpec(memory_space=pl.ANY),
        out_shape=jax.ShapeDtypeStruct((2 * t * TILE_ROWS, LANES), F32),
        scratch_shapes=[pltpu.SemaphoreType.DMA(())],
        compiler_params=_params("arbitrary"),
        name="scatter",
    )(slot, xa, xb)


def _expert_body(tile_ref, exp_ref, lo_ref, hi_ref, first_ref, last_ref, xs_ref, gm_ref, wg_ref, wu_ref, wd_ref,
                 ys_ref, hb_scr, acc_scr, *, tme):
    k = pl.program_id(0)
    first = first_ref[k] == 1
    last = last_ref[k] == 1
    nonempty = hi_ref[k] > lo_ref[k]

    @pl.when(first)
    def _():
        hb_scr[...] = _rms(_tiles_to_rows(xs_ref, tme), gm_ref[...]).astype(BF16)

    @pl.when(first & jnp.logical_not(nonempty))
    def _():
        acc_scr[...] = jnp.zeros_like(acc_scr)

    @pl.when(nonempty)
    def _():
        hb = hb_scr[...]
        act = jax.nn.silu(_bdot(hb, wg_ref[0])) * _bdot(hb, wu_ref[0])
        row = lax.broadcasted_iota(jnp.int32, (tme, 1), 0)
        y = jnp.where((row >= lo_ref[k]) & (row < hi_ref[k]), _bdot(act.astype(BF16), wd_ref[0]), 0.0)

        @pl.when(first & last)
        def _():
            _rows_to_tiles(ys_ref, y, tme)

        @pl.when(first & jnp.logical_not(last))
        def _():
            acc_scr[...] = y

        @pl.when(jnp.logical_not(first))
        def _():
            acc_scr[...] += y

    @pl.when(last & jnp.logical_not(first & nonempty))
    def _():
        _rows_to_tiles(ys_ref, acc_scr[...], tme)


def _experts(work, xs, g_moe, wg, wu, wd, tme):
    n_work = work[0].shape[0]
    body = functools.partial(_expert_body, tme=tme)
    tile_blk = pl.BlockSpec((tme * TILE_ROWS, LANES), lambda k, tile, *_: (tile[k], 0))

    def wspec(shape):
        return pl.BlockSpec(shape, lambda k, tile, exp, *_: (exp[k], 0, 0))

    grid_spec = pltpu.PrefetchScalarGridSpec(
        num_scalar_prefetch=len(work),
        grid=(n_work,),
        in_specs=[tile_blk, pl.BlockSpec((1, D_MODEL), lambda k, *_: (0, 0)),
                  wspec((1, D_MODEL, D_FF_EXPERT)), wspec((1, D_MODEL, D_FF_EXPERT)),
                  wspec((1, D_FF_EXPERT, D_MODEL))],
        out_specs=tile_blk,
        scratch_shapes=[pltpu.VMEM((tme, D_MODEL), BF16), pltpu.VMEM((tme, D_MODEL), F32)],
    )
    return pl.pallas_call(
        body,
        grid_spec=grid_spec,
        out_shape=jax.ShapeDtypeStruct(xs.shape, F32),
        compiler_params=_params("arbitrary"),
        name="experts",
    )(*work, xs, g_moe, wg, wu, wd)


def _combine_body(slot_ref, slot_next_ref, xa_ref, xb_ref, info_ref, ys_hbm, gf_ref, ya_ref, yb_ref, b0, b1, sem,
                  *, tm, nstep, n_a):
    i = pl.program_id(0)
    cur = i % 2

    def gather(slots, buf):
        def issue(r, c):
            _tile_copy(ys_hbm, slots[2 * r], b0.at[buf], r, sem.at[buf]).start(priority=0)
            _tile_copy(ys_hbm, slots[2 * r + 1], b1.at[buf], r, sem.at[buf]).start(priority=1)
            return c

        lax.fori_loop(0, tm, issue, 0, unroll=DMA_UNROLL)

    @pl.when(i == 0)
    def _():
        gather(slot_ref, 0)

    @pl.when(i + 1 < nstep)
    def _():
        gather(slot_next_ref, 1 - cur)

    def drain(r, c):
        _tile_copy(ys_hbm, 0, b0.at[cur], r, sem.at[cur]).wait()
        _tile_copy(ys_hbm, 0, b1.at[cur], r, sem.at[cur]).wait()
        return c

    lax.fori_loop(0, tm, drain, 0, unroll=DMA_UNROLL)
    info = info_ref[...]
    moe = info[:, 2:3] * _tiles_to_rows(b0.at[cur], tm) + info[:, 3:4] * _tiles_to_rows(b1.at[cur], tm)

    def finish(x2t_ref, y_ref):
        y_ref[...] = _rms(_tiles_to_rows(x2t_ref, tm) + moe, gf_ref[...])

    _on_part(n_a, finish, (xa_ref, xb_ref), (ya_ref, yb_ref))


def _combine(slot, xa, xb, info, ys, g_final, tm=256):
    n_a, n_b = xa.shape[0] // (tm * TILE_ROWS), xb.shape[0] // (tm * TILE_ROWS)
    nstep = n_a + n_b
    body = functools.partial(_combine_body, tm=tm, nstep=nstep, n_a=n_a)
    buf = pltpu.VMEM((2, tm * TILE_ROWS, LANES), F32)
    return pl.pallas_call(
        body,
        grid=(nstep,),
        in_specs=[pl.BlockSpec((2 * tm,), lambda i: (i,), memory_space=pltpu.SMEM),
                  pl.BlockSpec((2 * tm,), lambda i: (jnp.minimum(i + 1, nstep - 1),), memory_space=pltpu.SMEM)]
        + _two_part_specs(tm * TILE_ROWS, n_a, n_b) + [
                  pl.BlockSpec((tm, ROUTER_LANES), lambda i: (i, 0)),
                  pl.BlockSpec(memory_space=pl.ANY),
                  pl.BlockSpec((1, D_MODEL), lambda i: (0, 0))],
        out_specs=[pl.BlockSpec((tm, D_MODEL), lambda i: (jnp.minimum(i, n_a - 1), 0)),
                   pl.BlockSpec((tm, D_MODEL), lambda i: (jnp.clip(i - n_a, 0, n_b - 1), 0))],
        out_shape=[jax.ShapeDtypeStruct((n_a * tm, D_MODEL), F32), jax.ShapeDtypeStruct((n_b * tm, D_MODEL), F32)],
        scratch_shapes=[buf, buf, pltpu.SemaphoreType.DMA((2,))],
        compiler_params=_params("arbitrary"),
        name="combine",
    )(slot, slot, xa, xb, info, ys, g_final)


def _moe(xa, xb, w, tme=512):
    info, cnt = _route(xa, xb, w["g_moe"], w["wr_hi"], w["wr_lo"])
    slot, work = _route_meta(info, cnt, tme)
    xs = _scatter(slot, xa, xb)
    ys = _experts(work, xs, w["g_moe"], w["moe_wg"], w["moe_wu"], w["moe_wd"], tme)
    return _combine(slot, xa, xb, info, ys, w["g_final"])


def _layer(x3d, k4, v4, s5_h0, lru_h0, lru_conv, w, chunk, nb, tq):
    bsz, seq, _ = x3d.shape
    x2d = x3d.reshape(bsz * seq, D_MODEL)
    lru_w = (w["conv_w"], w["conv_b"], w["lru_wg"], w["lru_bg"], w["lru_lam"])
    tail = SUBLANES - (CONV_W - 1)
    if lru_h0 is None:
        u, yl, hl, xt = _mixlru(x2d, w["g_mix"], w["w_in"], *lru_w, bsz, seq)
        lru_h = hl.reshape(bsz, SUBLANES, D_LRU)[:, 0]
        conv_new = xt.reshape(bsz, SUBLANES, D_LRU)[:, tail:]
    else:
        u, xl, gl = _mixin(x2d, w["g_mix"], w["w_in"])
        hist = jnp.pad(lru_conv, ((0, 0), (tail, 0), (0, 0))).reshape(bsz * seq, D_LRU)
        yl, hall = _lru_dec(xl, gl, hist, jnp.repeat(lru_h0, seq, axis=0), *lru_w)
        lru_h = hall.reshape(bsz, seq, D_LRU)[:, seq - 1]
        conv_new = xl.reshape(bsz, seq, D_LRU)[:, seq - (CONV_W - 1):]

    nstate = S5_GROUPS * S5_STATE
    if s5_h0 is None:
        zero = jnp.zeros((bsz, 1, nstate), F32)
        ys, s5_re, s5_im = _s5(u, w["s5"][chunk], zero, zero, chunk, seq // chunk, True)
    else:
        ys, s5_re, s5_im = _s5(u, w["s5"][chunk], s5_h0[0].reshape(bsz, nstate), s5_h0[1].reshape(bsz, nstate),
                               chunk, bsz, False)
    s5_re = s5_re.reshape(bsz, S5_GROUPS, S5_STATE)
    s5_im = s5_im.reshape(bsz, S5_GROUPS, S5_STATE)

    x2t = _mixattn(x3d, ys.reshape(bsz, seq, D_S5), yl.reshape(bsz, seq, D_LRU), k4, v4, w, nb, tq)
    return x2t, s5_re, s5_im, lru_h, conv_new


def kernel(x_prompt, x_sample, mem_prompt, cache_mem_k, cache_mem_v, state_s5_re, state_s5_im, state_lru_h, state_lru_conv, g_mix, w_in, s5_lam_re, s5_lam_im, s5_log_dt, s5_b_re, s5_b_im, s5_c_re, s5_c_im, s5_d, s5_w_glu, s5_b_glu, lru_conv_w, lru_conv_b, lru_w_a, lru_b_a, lru_w_x, lru_b_x, lru_lam, w_out, g_xa, g_mem, xa_w_q, xa_w_k, xa_w_v, xa_w_o, g_moe, moe_w_group, moe_w_expert, moe_w_gate, moe_w_up, moe_w_down, g_final):
    depth = g_mix.shape[0]
    assert depth == 1, "single-layer step"
    l = 0
    bsz, seq, _ = x_prompt.shape
    dbsz, dseq, _ = x_sample.shape
    chunk = SUBLANES
    assert seq % chunk == 0 and dseq == chunk

    s5_args = (s5_lam_re[l], s5_lam_im[l], s5_log_dt[l], s5_b_re[l], s5_b_im[l], s5_c_re[l], s5_c_im[l], s5_d[l])
    wg, bg = _lru_weights(lru_w_a[l], lru_w_x[l], lru_b_a[l], lru_b_x[l])
    wr = jnp.concatenate([moe_w_group[l], moe_w_expert[l],
                          jnp.zeros((D_MODEL, ROUTER_LANES - N_EXPERT_GROUPS - N_EXPERTS), F32)], axis=1)
    wr_hi = wr.astype(BF16)
    wr_lo = (wr - wr_hi.astype(F32)).astype(BF16)
    w = {
        "g_mix": g_mix[l][None], "w_in": w_in[l].astype(BF16),
        "s5": {chunk: _s5_weights(chunk, *s5_args)},
        "conv_w": lru_conv_w[l], "conv_b": lru_conv_b[l][None], "lru_wg": wg, "lru_bg": bg,
        "lru_lam": lru_lam[l][None],
        "w_glu": s5_w_glu[l].astype(BF16), "b_glu": s5_b_glu[l][None], "w_out": w_out[l].astype(BF16),
        "g_xa": g_xa[l][None], "w_q": xa_w_q[l].astype(BF16), "w_o": xa_w_o[l].astype(BF16),
        "g_moe": g_moe[l][None], "wr_hi": wr_hi, "wr_lo": wr_lo,
        "moe_wg": moe_w_gate[l].astype(BF16), "moe_wu": moe_w_up[l].astype(BF16),
        "moe_wd": moe_w_down[l].astype(BF16), "g_final": g_final[None],
    }

    mk, mv = _memkv(mem_prompt.reshape(bsz * N_MEM, D_MODEL), g_mem[l][None],
                    xa_w_k[l].astype(BF16), xa_w_v[l].astype(BF16))
    xp, p_re, p_im, p_h, p_conv = _layer(x_prompt, mk.reshape(bsz, N_MEM, D_MODEL), mv.reshape(bsz, N_MEM, D_MODEL),
                                         None, None, None, w, chunk, nb=1, tq=512)
    xs, s_re, s_im, s_h, s_conv = _layer(x_sample, cache_mem_k[l], cache_mem_v[l],
                                         (state_s5_re[l], state_s5_im[l]), state_lru_h[l],
                                         state_lru_conv[l], w, chunk, nb=8, tq=dseq)
    yp, ysmp = _moe(xp, xs, w)
    yp = yp.reshape(bsz, seq, D_MODEL)
    ysmp = ysmp.reshape(dbsz, dseq, D_MODEL)

    return (yp, ysmp,
            mk.reshape(1, bsz, N_MEM, XA_HEADS, XA_HEAD_DIM), mv.reshape(1, bsz, N_MEM, XA_HEADS, XA_HEAD_DIM),
            p_re[None], p_im[None], p_h[None], p_conv[None],
            s_re[None], s_im[None], s_h[None], s_conv[None])
```

```python
import functools

import jax
import jax.numpy as jnp
import numpy as np
from jax import lax
from jax.experimental import pallas as pl
from jax.experimental.pallas import tpu as pltpu

F32 = jnp.float32
BF16 = jnp.bfloat16

D_MODEL = 1024
D_S5 = 512
D_LRU = 512
S5_GROUP = 16
S5_GROUPS = 32
S5_STATE = 64
LRU_HEADS = 8
LRU_HEAD_DIM = 64
CONV_W = 4
LRU_C = 8.0
N_MEM = 256
XA_HEADS = 4
XA_HEAD_DIM = 256
N_EXPERT_GROUPS = 4
EXPERTS_PER_GROUP = 8
N_EXPERTS = 32
D_FF_EXPERT = 256
EPS = 1e-6

SUBLANES = 8
LANES = 128
ROUTER_LANES = 128
VMEM_LIMIT = 56 * 1024 * 1024
DMA_UNROLL = 16


def _params(*sem):
    return pltpu.CompilerParams(dimension_semantics=sem, vmem_limit_bytes=VMEM_LIMIT)


def _rms(x, g):
    ms = jnp.mean(x * x, axis=-1, keepdims=True)
    return x * lax.rsqrt(ms + EPS) * g


def _bdot(a, b):
    return jnp.dot(a, b, preferred_element_type=F32)


def _memkv_body(m_ref, g_ref, wk_ref, wv_ref, k_ref, v_ref):
    hb = _rms(m_ref[...], g_ref[...]).astype(BF16)
    k_ref[...] = _bdot(hb, wk_ref[...])
    v_ref[...] = _bdot(hb, wv_ref[...])


def _memkv(mem2d, g_mem, wk, wv, tm=512):
    t = mem2d.shape[0]
    row = pl.BlockSpec((tm, D_MODEL), lambda i: (i, 0))
    full = pl.BlockSpec((D_MODEL, D_MODEL), lambda i: (0, 0))
    return pl.pallas_call(
        _memkv_body,
        grid=(t // tm,),
        in_specs=[row, pl.BlockSpec((1, D_MODEL), lambda i: (0, 0)), full, full],
        out_specs=[row, row],
        out_shape=[jax.ShapeDtypeStruct((t, D_MODEL), F32)] * 2,
        compiler_params=_params("parallel"),
        name="memkv",
    )(mem2d, g_mem, wk, wv)


def _mixin_body(x_ref, g_ref, w_ref, u_ref, xl_ref, gl_ref):
    hb = _rms(x_ref[...], g_ref[...]).astype(BF16)
    proj = _bdot(hb, w_ref[...])
    u_ref[...] = proj[:, :D_S5]
    xl_ref[...] = proj[:, D_S5:D_S5 + D_LRU]
    gl_ref[...] = proj[:, D_S5 + D_LRU:]


def _mixin(x2d, g_mix, w_in, tm=1024):
    t = x2d.shape[0]
    half = pl.BlockSpec((tm, D_S5), lambda i: (i, 0))
    return pl.pallas_call(
        _mixin_body,
        grid=(t // tm,),
        in_specs=[pl.BlockSpec((tm, D_MODEL), lambda i: (i, 0)),
                  pl.BlockSpec((1, D_MODEL), lambda i: (0, 0)),
                  pl.BlockSpec((D_MODEL, D_S5 + 2 * D_LRU), lambda i: (0, 0))],
        out_specs=[half, half, half],
        out_shape=[jax.ShapeDtypeStruct((t, D_S5), F32)] * 3,
        compiler_params=_params("parallel"),
        name="mixin",
    )(x2d, g_mix, w_in)


GROUPS_PER_BLOCK = LANES // S5_GROUP
PAIRS_PER_BLOCK = GROUPS_PER_BLOCK // 2
STATE_BLOCK = GROUPS_PER_BLOCK * S5_STATE
PW_ROWS = 24


def _s5_body(u_ref, wa_ref, wh_ref, pw_ref, d_ref, h0r_ref, h0i_ref, y_ref, hfr_ref, hfi_ref, hpr_scr, hpi_scr,
             *, chunk, nrow, scan):
    nh = chunk // SUBLANES
    lc = chunk * S5_GROUP
    slot = lax.broadcasted_iota(jnp.int32, (nrow, LANES), 1) // S5_GROUP
    in_slot = [slot == s for s in range(GROUPS_PER_BLOCK)]

    def pick(src, sel):
        out = src[sel(0)]
        for s in range(1, GROUPS_PER_BLOCK):
            out = jnp.where(in_slot[s], src[sel(s)], out)
        return out

    nat, skew = [], []
    for t in range(chunk):
        a = u_ref[pl.ds(t, nrow, stride=chunk), :]
        nat.append(a)
        s = t % GROUPS_PER_BLOCK
        skew.append(pltpu.roll(a, s * S5_GROUP, axis=1) if s else a)
    def lane_roll(x, slots):
        slots %= GROUPS_PER_BLOCK
        return pltpu.roll(x, slots * S5_GROUP, axis=1) if slots else x

    z = [[lane_roll(pick(skew, lambda s, hh=hh, m=m: SUBLANES * hh + (s - m) % GROUPS_PER_BLOCK), -m)
          for hh in range(nh)] for m in range(GROUPS_PER_BLOCK)]

    ys, er, ei = [], [], []
    for p in range(PAIRS_PER_BLOCK):
        lhs = jnp.concatenate(z[2 * p] + z[2 * p + 1], axis=1).astype(BF16)
        full = _bdot(lhs, wa_ref[p])
        ys.append(full[:, :2 * lc])
        er.append(full[:, 2 * lc:2 * lc + LANES])
        ei.append(full[:, 2 * lc + LANES:])
    er = jnp.concatenate(er, axis=1)
    ei = jnp.concatenate(ei, axis=1)

    if scan:
        ntile = nrow // SUBLANES
        xr = er.reshape(ntile, SUBLANES, STATE_BLOCK)
        xi = ei.reshape(ntile, SUBLANES, STATE_BLOCK)
        row = lax.broadcasted_iota(jnp.int32, xr.shape, 1)
        for k, d in enumerate((1, 2, 4)):
            pr, pi = pw_ref[0, k:k + 1, :], pw_ref[0, 4 + k:5 + k, :]
            sr, si = pltpu.roll(xr, d, axis=1), pltpu.roll(xi, d, axis=1)
            m = row >= d
            xr, xi = jnp.where(m, xr + pr * sr - pi * si, xr), jnp.where(m, xi + pr * si + pi * sr, xi)
        sxr = jnp.where(row >= 1, pltpu.roll(xr, 1, axis=1), 0.0)
        sxi = jnp.where(row >= 1, pltpu.roll(xi, 1, axis=1), 0.0)
        p8r, p8i = pw_ref[0, 3:4, :], pw_ref[0, 7:8, :]
        qr, qi = pw_ref[0, 8:16, :], pw_ref[0, 16:24, :]
        hr, hi = h0r_ref[0], h0i_ref[0]
        for t in range(ntile):
            rows = slice(SUBLANES * t, SUBLANES * (t + 1))
            hpr_scr[rows, :] = sxr[t] + qr * hr - qi * hi
            hpi_scr[rows, :] = sxi[t] + qr * hi + qi * hr
            hr, hi = (xr[t, SUBLANES - 1:] + p8r * hr - p8i * hi, xi[t, SUBLANES - 1:] + p8r * hi + p8i * hr)
        hfr_ref[0] = hr
        hfi_ref[0] = hi
        hpr, hpi = hpr_scr[...], hpi_scr[...]
    else:
        hpr, hpi = h0r_ref[...], h0i_ref[...]
        p1r, p1i = pw_ref[0, 0:1, :], pw_ref[0, 4:5, :]
        hfr_ref[...] = er + p1r * hpr - p1i * hpi
        hfi_ref[...] = ei + p1r * hpi + p1i * hpr

    yg = []
    for p in range(PAIRS_PER_BLOCK):
        lanes = slice(LANES * p, LANES * (p + 1))
        hp = jnp.concatenate([hpr[:, lanes], hpi[:, lanes]], axis=1).astype(BF16)
        out = ys[p] + _bdot(hp, wh_ref[p])
        for half in range(2):
            q = 2 * p + half
            yg.append([lane_roll(out[:, half * lc + hh * LANES:half * lc + (hh + 1) * LANES], q)
                       for hh in range(nh)])
    d = d_ref[...]
    for t in range(chunk):
        hh, s = divmod(t, GROUPS_PER_BLOCK)
        c = lane_roll(pick([g[hh] for g in yg], lambda sl, s=s: (sl - s) % GROUPS_PER_BLOCK), -s)
        y_ref[pl.ds(t, nrow, stride=chunk), :] = c + d * nat[t]


def _s5(u2d, weights, h0r, h0i, chunk, nrow, scan):
    wa, wh, pw, dvec = weights
    t = u2d.shape[0]
    lc = chunk * S5_GROUP
    rows = nrow * chunk
    nblk = D_S5 // LANES
    body = functools.partial(_s5_body, chunk=chunk, nrow=nrow, scan=scan)
    ublk = pl.BlockSpec((rows, LANES), lambda i, j: (i, j))
    if scan:
        hblk = pl.BlockSpec((1, 1, STATE_BLOCK), lambda i, j: (i, 0, j))
        hshape = jax.ShapeDtypeStruct((t // rows, 1, S5_GROUPS * S5_STATE), F32)
    else:
        hblk = pl.BlockSpec((nrow, STATE_BLOCK), lambda i, j: (i, j))
        hshape = jax.ShapeDtypeStruct((t // chunk, S5_GROUPS * S5_STATE), F32)
    return pl.pallas_call(
        body,
        grid=(t // rows, nblk),
        in_specs=[ublk,
                  pl.BlockSpec((PAIRS_PER_BLOCK, 2 * lc, 2 * lc + 2 * LANES), lambda i, j: (j, 0, 0)),
                  pl.BlockSpec((PAIRS_PER_BLOCK, 2 * LANES, 2 * lc), lambda i, j: (j, 0, 0)),
                  pl.BlockSpec((1, PW_ROWS, STATE_BLOCK), lambda i, j: (j, 0, 0)),
                  pl.BlockSpec((1, LANES), lambda i, j: (0, j)),
                  hblk, hblk],
        out_specs=[ublk, hblk, hblk],
        out_shape=[jax.ShapeDtypeStruct((t, D_S5), F32), hshape, hshape],
        scratch_shapes=[pltpu.VMEM((nrow, STATE_BLOCK), F32), pltpu.VMEM((nrow, STATE_BLOCK), F32)],
        compiler_params=_params("parallel", "parallel"),
        name="s5",
    )(u2d, wa, wh, pw, dvec, h0r, h0i)


def _s5_weights(chunk, lam_re, lam_im, log_dt, b_re, b_im, c_re, c_im, d_skip):
    hi = lax.Precision.HIGHEST
    dt = jnp.exp(log_dt)[:, None]
    mag = jnp.exp(lam_re * dt)
    ab_re = mag * jnp.cos(lam_im * dt)
    ab_im = mag * jnp.sin(lam_im * dt)
    den = lam_re * lam_re + lam_im * lam_im
    nr = ab_re - 1.0
    f_re = (nr * lam_re + ab_im * lam_im) / den
    f_im = (ab_im * lam_re - nr * lam_im) / den
    bb_re = f_re[..., None] * b_re - f_im[..., None] * b_im
    bb_im = f_re[..., None] * b_im + f_im[..., None] * b_re
    def powers(ks):
        k = jnp.asarray(np.asarray(ks, np.float32))[:, None, None]
        m = jnp.exp(k * (lam_re * dt))
        return m * jnp.cos(k * (lam_im * dt)), m * jnp.sin(k * (lam_im * dt))

    pw_re, pw_im = powers(range(chunk + 1))
    pb_re = pw_re[:chunk, ..., None] * bb_re - pw_im[:chunk, ..., None] * bb_im
    pb_im = pw_re[:chunk, ..., None] * bb_im + pw_im[:chunk, ..., None] * bb_re
    kk = (jnp.einsum("gon,kgni->kgoi", c_re, pb_re, precision=hi)
          - jnp.einsum("gon,kgni->kgoi", c_im, pb_im, precision=hi))
    npair = S5_GROUPS // 2
    lc = chunk * S5_GROUP
    row0 = kk.transpose(1, 3, 0, 2).reshape(S5_GROUPS, S5_GROUP, lc)
    row0 = jnp.pad(row0, ((0, 0), (0, 0), (lc, 0)))
    conv = jnp.stack([row0[:, :, lc - S5_GROUP * ti:2 * lc - S5_GROUP * ti] for ti in range(chunk)], axis=1)
    conv = conv.reshape(npair, 2, lc, lc)
    ends = [x[::-1].transpose(1, 0, 3, 2).reshape(npair, 2, lc, S5_STATE) for x in (pb_re, pb_im)]
    zc = jnp.zeros((npair, lc, lc), F32)
    zs = jnp.zeros((npair, lc, S5_STATE), F32)
    wa = jnp.concatenate([
        jnp.concatenate([conv[:, 0], zc, ends[0][:, 0], zs, ends[1][:, 0], zs], axis=-1),
        jnp.concatenate([zc, conv[:, 1], zs, ends[0][:, 1], zs, ends[1][:, 1]], axis=-1)], axis=1)
    gr = c_re[None] * pw_re[1:, :, None, :] - c_im[None] * pw_im[1:, :, None, :]
    gi = c_re[None] * pw_im[1:, :, None, :] + c_im[None] * pw_re[1:, :, None, :]
    zr = jnp.zeros((npair, S5_STATE, lc), F32)
    blocks = []
    for x in (gr, -gi):
        hy = x.transpose(1, 3, 0, 2).reshape(npair, 2, S5_STATE, lc)
        blocks += [jnp.concatenate([hy[:, 0], zr], axis=-1), jnp.concatenate([zr, hy[:, 1]], axis=-1)]
    wh = jnp.concatenate(blocks, axis=1)
    tab = [1, 2, 4, 8] + list(range(SUBLANES))
    qr, qi = powers([chunk * k for k in tab])
    nblk = D_S5 // LANES
    pw = jnp.concatenate([qr[:4], qi[:4], qr[4:], qi[4:]]).reshape(PW_ROWS, nblk, STATE_BLOCK).transpose(1, 0, 2)
    return wa.astype(BF16), wh.astype(BF16), pw, d_skip.reshape(1, D_S5)


def _lru_ab(xc, wg_ref, bg_ref, lam_ref):
    xb = xc.astype(BF16)
    half = D_LRU // 2
    g0 = _bdot(xb[:, :half], wg_ref[0]) + bg_ref[0]
    g1 = _bdot(xb[:, half:], wg_ref[1]) + bg_ref[1]
    r = jax.nn.sigmoid(jnp.concatenate([g0[:, :half], g1[:, :half]], axis=1))
    ig = jax.nn.sigmoid(jnp.concatenate([g0[:, half:], g1[:, half:]], axis=1))
    zl = -lam_ref[...]
    softplus = jnp.maximum(zl, 0.0) + jnp.log1p(jnp.exp(-jnp.abs(zl)))
    log_a = -LRU_C * r * softplus
    a = jnp.exp(log_a)
    b = jnp.sqrt(-jnp.tanh(log_a) * (a * a + 1.0)) * (ig * xc)
    return a, b


def _tile_scan(a, b):
    shape = a.shape
    a = a.reshape(shape[0] // SUBLANES, SUBLANES, shape[1])
    b = b.reshape(a.shape)
    row = lax.broadcasted_iota(jnp.int32, a.shape, 1)
    for d in (1, 2, 4):
        a_prev = pltpu.roll(a, d, axis=1)
        b_prev = pltpu.roll(b, d, axis=1)
        m = row >= d
        b = jnp.where(m, b + a * b_prev, b)
        a = jnp.where(m, a * a_prev, a)
    return a.reshape(shape), b.reshape(shape)


def _mixlru_body(x_ref, gmix_ref, win_ref, cw_ref, cb_ref, wg_ref, bg_ref, lam_ref, u_ref, y_ref, hl_ref, xt_ref,
                 xp_scr, a_scr, b_scr, h_scr, hc_scr, *, ts):
    ti = pl.program_id(1)

    @pl.when(ti == 0)
    def _():
        xp_scr[0:SUBLANES, :] = jnp.zeros((SUBLANES, D_LRU), F32)
        hc_scr[...] = jnp.zeros((SUBLANES, D_LRU), F32)

    proj = _bdot(_rms(x_ref[...], gmix_ref[...]).astype(BF16), win_ref[...])
    u_ref[...] = proj[:, :D_S5]
    xl = proj[:, D_S5:D_S5 + D_LRU]
    gl = proj[:, D_S5 + D_LRU:]
    xt_ref[...] = xl[ts - SUBLANES:, :]
    xp_scr[SUBLANES:SUBLANES + ts, :] = xl
    xc = cb_ref[...] + xl * cw_ref[CONV_W - 1:CONV_W, :]
    for j in range(1, CONV_W):
        xc = xc + xp_scr[SUBLANES - j:SUBLANES - j + ts, :] * cw_ref[CONV_W - 1 - j:CONV_W - j, :]
    xp_scr[0:SUBLANES, :] = xl[ts - SUBLANES:, :]
    a, b = _lru_ab(xc, wg_ref, bg_ref, lam_ref)
    a, b = _tile_scan(a, b)
    a_scr[...] = a
    b_scr[...] = b

    def step(i, hin):
        rows = pl.ds(pl.multiple_of(i * SUBLANES, SUBLANES), SUBLANES)
        h = b_scr[rows, :] + a_scr[rows, :] * hin
        h_scr[rows, :] = h
        return h[SUBLANES - 1:SUBLANES, :]

    hlast = lax.fori_loop(0, ts // SUBLANES, step, hc_scr[0:1, :], unroll=8)
    hc_scr[...] = jnp.broadcast_to(hlast, (SUBLANES, D_LRU))
    hl_ref[...] = hc_scr[...]
    y_ref[...] = h_scr[...] * jax.nn.gelu(gl)


def _lru_dec_body(xl_ref, gl_ref, hist_ref, h0_ref, cw_ref, cb_ref, wg_ref, bg_ref, lam_ref, y_ref, hl_ref, *, tm):
    xl = xl_ref[...]
    hist = hist_ref[...]
    row = lax.broadcasted_iota(jnp.int32, xl.shape, 0) % SUBLANES
    xc = cb_ref[...] + xl * cw_ref[CONV_W - 1:CONV_W, :]
    for j in range(1, CONV_W):
        prev = jnp.where(row >= j, pltpu.roll(xl, j, axis=0), pltpu.roll(hist, tm - SUBLANES + j, axis=0))
        xc = xc + prev * cw_ref[CONV_W - 1 - j:CONV_W - j, :]
    a, b = _lru_ab(xc, wg_ref, bg_ref, lam_ref)
    a, b = _tile_scan(a, b)
    h = b + a * h0_ref[...]
    hl_ref[...] = h
    y_ref[...] = h * jax.nn.gelu(gl_ref[...])


def _lru_weights(w_a, w_x, b_a, b_x):
    eye = jnp.eye(LRU_HEADS, dtype=F32)
    bd_a = jnp.einsum("hij,hg->higj", w_a, eye).reshape(D_LRU, D_LRU)
    bd_x = jnp.einsum("hij,hg->higj", w_x, eye).reshape(D_LRU, D_LRU)
    half = D_LRU // 2
    wg = jnp.stack([jnp.concatenate([bd_a[:half, :half], bd_x[:half, :half]], axis=1),
                    jnp.concatenate([bd_a[half:, half:], bd_x[half:, half:]], axis=1)]).astype(BF16)
    ba = b_a.reshape(1, D_LRU)
    bx = b_x.reshape(1, D_LRU)
    bg = jnp.stack([jnp.concatenate([ba[:, :half], bx[:, :half]], axis=1),
                    jnp.concatenate([ba[:, half:], bx[:, half:]], axis=1)])
    return wg, bg


def _lru_common_specs(const):
    return [const((CONV_W, D_LRU)), const((1, D_LRU)), const((2, D_LRU // 2, D_LRU)),
            const((2, 1, D_LRU)), const((1, D_LRU))]


def _mixlru(x2d, g_mix, w_in, cw, cb, wg, bg, lam, bsz, seq, ts=1024):
    nt = seq // ts
    blk = pl.BlockSpec((ts, D_LRU), lambda b, t: (b * nt + t, 0))
    per_seq = pl.BlockSpec((SUBLANES, D_LRU), lambda b, t: (b, 0))

    def const(shape):
        return pl.BlockSpec(shape, lambda b, t: (0,) * len(shape))

    body = functools.partial(_mixlru_body, ts=ts)
    tile = jax.ShapeDtypeStruct((bsz * SUBLANES, D_LRU), F32)
    return pl.pallas_call(
        body,
        grid=(bsz, nt),
        in_specs=[pl.BlockSpec((ts, D_MODEL), lambda b, t: (b * nt + t, 0)), const((1, D_MODEL)),
                  const((D_MODEL, D_S5 + 2 * D_LRU))] + _lru_common_specs(const),
        out_specs=[blk, blk, per_seq, per_seq],
        out_shape=[jax.ShapeDtypeStruct((bsz * seq, D_S5), F32), jax.ShapeDtypeStruct((bsz * seq, D_LRU), F32),
                   tile, tile],
        scratch_shapes=[pltpu.VMEM((ts + SUBLANES, D_LRU), F32), pltpu.VMEM((ts, D_LRU), F32),
                        pltpu.VMEM((ts, D_LRU), F32), pltpu.VMEM((ts, D_LRU), F32),
                        pltpu.VMEM((SUBLANES, D_LRU), F32)],
        compiler_params=_params("arbitrary", "arbitrary"),
        name="mixlru",
    )(x2d, g_mix, w_in, cw, cb, wg, bg, lam)


def _lru_dec(xl, gl, hist, h0rep, cw, cb, wg, bg, lam, tm=256):
    t = xl.shape[0]
    blk = pl.BlockSpec((tm, D_LRU), lambda i: (i, 0))

    def const(shape):
        return pl.BlockSpec(shape, lambda i: (0,) * len(shape))

    body = functools.partial(_lru_dec_body, tm=tm)
    return pl.pallas_call(
        body,
        grid=(t // tm,),
        in_specs=[blk, blk, blk, blk] + _lru_common_specs(const),
        out_specs=[blk, blk],
        out_shape=[jax.ShapeDtypeStruct((t, D_LRU), F32)] * 2,
        compiler_params=_params("parallel"),
        name="lru_dec",
    )(xl, gl, hist, h0rep, cw, cb, wg, bg, lam)


def _softmax(sc):
    p = jnp.exp(sc - jnp.max(sc, axis=-1, keepdims=True))
    return p / jnp.sum(p, axis=-1, keepdims=True)


def _attend(q, k_ref, v_ref, i, tq):
    scale = XA_HEAD_DIM ** -0.5
    heads = [q[:, h * XA_HEAD_DIM:(h + 1) * XA_HEAD_DIM] for h in range(XA_HEADS)]
    if len(k_ref.shape) == 4:
        kf = k_ref[i].reshape(N_MEM * XA_HEADS, XA_HEAD_DIM).astype(BF16)
        vf = v_ref[i].reshape(N_MEM * XA_HEADS, XA_HEAD_DIM).astype(BF16)
        sc = lax.dot_general(jnp.concatenate(heads, axis=0), kf, (((1,), (1,)), ((), ())),
                             preferred_element_type=F32) * scale
        own = (lax.broadcasted_iota(jnp.int32, sc.shape, 1) % XA_HEADS
               == lax.broadcasted_iota(jnp.int32, sc.shape, 0) // tq)
        oh = _bdot(_softmax(jnp.where(own, sc, -jnp.inf)).astype(BF16), vf)
        outs = [oh[h * tq:(h + 1) * tq] for h in range(XA_HEADS)]
    else:
        kb = k_ref[i].astype(BF16)
        vb = v_ref[i].astype(BF16)
        outs = []
        for h in range(XA_HEADS):
            cols = slice(h * XA_HEAD_DIM, (h + 1) * XA_HEAD_DIM)
            sc = lax.dot_general(heads[h], kb[:, cols], (((1,), (1,)), ((), ())),
                                 preferred_element_type=F32) * scale
            outs.append(_bdot(_softmax(sc).astype(BF16), vb[:, cols]))
    return jnp.concatenate(outs, axis=1)


def _mixattn_body(x_ref, ys_ref, yl_ref, k_ref, v_ref, wglu_ref, bglu_ref, wout_ref, gxa_ref, wq_ref, wo_ref,
                  x2t_ref, *, nb, tq):
    rows = nb * tq
    ys = jax.nn.gelu(ys_ref[...].reshape(rows, D_S5))
    gate = jax.nn.sigmoid(_bdot(ys.astype(BF16), wglu_ref[...]) + bglu_ref[...])
    s5 = (ys * gate).astype(BF16)
    yl = yl_ref[...].reshape(rows, D_LRU).astype(BF16)
    x1 = x_ref[...].reshape(rows, D_MODEL) + _bdot(s5, wout_ref[0:D_S5, :]) + _bdot(yl, wout_ref[D_S5:, :])
    q = _bdot(_rms(x1, gxa_ref[...]).astype(BF16), wq_ref[...]).astype(BF16)
    o = jnp.concatenate([_attend(q[i * tq:(i + 1) * tq], k_ref, v_ref, i, tq) for i in range(nb)], axis=0)
    _rows_to_tiles(x2t_ref, x1 + _bdot(o.astype(BF16), wo_ref[...]), rows)


def _mixattn(x3, ys3, yl3, k4, v4, w, nb, tq):
    bsz, seq, _ = x3.shape
    nt = seq // tq
    blk = pl.BlockSpec((nb, tq, D_MODEL), lambda b, t: (b, t, 0))
    half = pl.BlockSpec((nb, tq, D_S5), lambda b, t: (b, t, 0))
    kv = pl.BlockSpec((nb,) + k4.shape[1:], lambda b, t: (b,) + (0,) * (k4.ndim - 1))

    def const(shape):
        return pl.BlockSpec(shape, lambda b, t: (0,) * len(shape), pipeline_mode=pl.Buffered(1))

    body = functools.partial(_mixattn_body, nb=nb, tq=tq)
    return pl.pallas_call(
        body,
        grid=(bsz // nb, nt),
        in_specs=[blk, half, half, kv, kv, const((D_S5, D_S5)), const((1, D_S5)), const((D_MODEL, D_MODEL)),
                  const((1, D_MODEL)), const((D_MODEL, D_MODEL)), const((D_MODEL, D_MODEL))],
        out_specs=pl.BlockSpec((nb * tq * TILE_ROWS, LANES), lambda b, t: (b * nt + t, 0)),
        out_shape=jax.ShapeDtypeStruct((bsz * seq * TILE_ROWS, LANES), F32),
        compiler_params=_params("parallel", "parallel"),
        name="mixattn",
    )(x3, ys3, yl3, k4, v4, w["w_glu"], w["b_glu"], w["w_out"], w["g_xa"], w["w_q"], w["w_o"])


def _router(hm, wr_hi_ref, wr_lo_ref):
    a_hi = hm.astype(BF16)
    a_lo = (hm - a_hi.astype(F32)).astype(BF16)
    logits = _bdot(a_hi, wr_hi_ref[...]) + (_bdot(a_hi, wr_lo_ref[...]) + _bdot(a_lo, wr_hi_ref[...]))
    lane_i = lax.broadcasted_iota(jnp.int32, logits.shape, 1)
    lane = lane_i.astype(F32)
    neg = -jnp.inf
    big = float(ROUTER_LANES)
    is_g = lane_i < N_EXPERT_GROUPS
    glog = jnp.where(is_g, logits, neg)
    gmax = jnp.max(glog, axis=-1, keepdims=True)
    gsel = jnp.min(jnp.where(glog == gmax, lane, big), axis=-1, keepdims=True)
    pg_sel = 1.0 / jnp.sum(jnp.where(is_g, jnp.exp(logits - gmax), 0.0), axis=-1, keepdims=True)
    eidx = lane_i - N_EXPERT_GROUPS
    in_group = (eidx >= 0) & (eidx < N_EXPERTS) & ((eidx >> 3).astype(F32) == gsel)
    el = jnp.where(in_group, logits, neg)
    v1 = jnp.max(el, axis=-1, keepdims=True)
    i1 = jnp.min(jnp.where(el == v1, lane, big), axis=-1, keepdims=True)
    el2 = jnp.where(lane == i1, neg, el)
    v2 = jnp.max(el2, axis=-1, keepdims=True)
    i2 = jnp.min(jnp.where(el2 == v2, lane, big), axis=-1, keepdims=True)
    e2 = jnp.exp(v2 - v1)
    w1 = pg_sel / (1.0 + e2)
    w2 = pg_sel * e2 / (1.0 + e2)
    return i1, i2, w1, w2


TILE_ROWS = D_MODEL // LANES


def _tiles_to_rows(ref, n, base=0):
    return jnp.concatenate([ref[pl.ds(base + s, n, stride=TILE_ROWS), :] for s in range(TILE_ROWS)], axis=1)


def _rows_to_tiles(ref, val, n, base=0):
    for s in range(TILE_ROWS):
        ref[pl.ds(base + s, n, stride=TILE_ROWS), :] = val[:, s * LANES:(s + 1) * LANES]


def _two_part_specs(block_rows, n_a, n_b):
    return [pl.BlockSpec((block_rows, LANES), lambda i: (jnp.minimum(i, n_a - 1), 0)),
            pl.BlockSpec((block_rows, LANES), lambda i: (jnp.clip(i - n_a, 0, n_b - 1), 0))]


def _on_part(n_a, fn, *ref_pairs):
    i = pl.program_id(0)
    pl.when(i < n_a)(lambda: fn(*[p[0] for p in ref_pairs]))
    pl.when(i >= n_a)(lambda: fn(*[p[1] for p in ref_pairs]))


def _route_body(xa_ref, xb_ref, gm_ref, wr_hi_ref, wr_lo_ref, info_ref, cnt_ref, cnt_scr, *, tm, n_a):
    @pl.when(pl.program_id(0) == 0)
    def _():
        cnt_scr[...] = jnp.zeros_like(cnt_scr)

    _on_part(n_a, functools.partial(_route_step, gm_ref, wr_hi_ref, wr_lo_ref, info_ref, cnt_ref, cnt_scr, tm),
             (xa_ref, xb_ref))


def _route_step(gm_ref, wr_hi_ref, wr_lo_ref, info_ref, cnt_ref, cnt_scr, tm, x2t_ref):
    hm = _rms(_tiles_to_rows(x2t_ref, tm), gm_ref[...])
    i1, i2, w1, w2 = _router(hm, wr_hi_ref, wr_lo_ref)
    lane_i = lax.broadcasted_iota(jnp.int32, (tm, ROUTER_LANES), 1)
    lane = lane_i.astype(F32)
    chosen = ((lane == i1) | (lane == i2)).astype(F32)
    earlier = (lax.broadcasted_iota(jnp.int32, (tm, tm), 1) < lax.broadcasted_iota(jnp.int32, (tm, tm), 0))
    before = _bdot(earlier.astype(BF16), chosen.astype(BF16)) + cnt_scr[0:1, :]
    r1 = jnp.sum(jnp.where(lane == i1, before, 0.0), axis=-1, keepdims=True)
    r2 = jnp.sum(jnp.where(lane == i2, before, 0.0), axis=-1, keepdims=True)
    cols = (i1, i2, w1, w2, r1, r2)
    info = jnp.zeros((tm, ROUTER_LANES), F32)
    for j, col in enumerate(cols):
        info = jnp.where(lane_i == j, col, info)
    info_ref[...] = info
    cnt_scr[...] = cnt_scr[...] + jnp.sum(chosen, axis=0, keepdims=True)
    cnt_ref[...] = cnt_scr[...]


def _route(xa, xb, g_moe, wr_hi, wr_lo, tm=1024):
    n_a, n_b = xa.shape[0] // (tm * TILE_ROWS), xb.shape[0] // (tm * TILE_ROWS)
    t = (n_a + n_b) * tm

    def const(shape):
        return pl.BlockSpec(shape, lambda i: (0,) * len(shape))

    body = functools.partial(_route_body, tm=tm, n_a=n_a)
    return pl.pallas_call(
        body,
        grid=(n_a + n_b,),
        in_specs=_two_part_specs(tm * TILE_ROWS, n_a, n_b) + [
            const((1, D_MODEL)), const((D_MODEL, ROUTER_LANES)), const((D_MODEL, ROUTER_LANES))],
        out_specs=[pl.BlockSpec((tm, ROUTER_LANES), lambda i: (i, 0)), const((SUBLANES, ROUTER_LANES))],
        out_shape=[jax.ShapeDtypeStruct((t, ROUTER_LANES), F32),
                   jax.ShapeDtypeStruct((SUBLANES, ROUTER_LANES), F32)],
        scratch_shapes=[pltpu.VMEM((SUBLANES, ROUTER_LANES), F32)],
        compiler_params=_params("arbitrary"),
        name="route",
    )(xa, xb, g_moe, wr_hi, wr_lo)


def _route_meta(info, cnt, tme):
    t = info.shape[0]
    n_tiles = 2 * t // tme
    counts = cnt[0, N_EXPERT_GROUPS:N_EXPERT_GROUPS + N_EXPERTS].astype(jnp.int32)
    ends = jnp.cumsum(counts)
    starts = ends - counts
    e = info[:, 0:2].astype(jnp.int32) - N_EXPERT_GROUPS
    onehot = e[:, :, None] == jnp.arange(N_EXPERTS, dtype=jnp.int32)
    slot = (jnp.sum(jnp.where(onehot, starts, 0), axis=-1) + info[:, 4:6].astype(jnp.int32)).reshape(-1)
    pts = jnp.concatenate([jnp.arange(n_tiles, dtype=jnp.int32) * tme, starts])
    n = pts.shape[0]
    idx = jnp.arange(n, dtype=jnp.int32)
    pos = jnp.sum((pts[None, :] < pts[:, None]) | ((pts[None, :] == pts[:, None]) & (idx[None, :] < idx[:, None])),
                  axis=1)
    lo_abs = jnp.sum(jnp.where(pos[:, None] == idx[None, :], pts[:, None], 0), axis=0)
    hi_abs = jnp.concatenate([lo_abs[1:], jnp.full((1,), 2 * t, jnp.int32)])
    tile = jnp.minimum(lo_abs // tme, n_tiles - 1)
    expert = jnp.clip(jnp.sum(starts[None, :] <= lo_abs[:, None], axis=1) - 1, 0, N_EXPERTS - 1)
    first = jnp.concatenate([jnp.ones((1,), jnp.int32), (tile[1:] != tile[:-1]).astype(jnp.int32)])
    last = jnp.concatenate([(tile[1:] != tile[:-1]).astype(jnp.int32), jnp.ones((1,), jnp.int32)])
    work = [a.astype(jnp.int32) for a in (tile, expert, lo_abs - tile * tme, hi_abs - tile * tme, first, last)]
    return slot.astype(jnp.int32), work


def _tile_copy(src, src_tok, dst, dst_tok, sem):
    return pltpu.make_async_copy(src.at[pl.ds(pl.multiple_of(src_tok * TILE_ROWS, TILE_ROWS), TILE_ROWS), :],
                                 dst.at[pl.ds(pl.multiple_of(dst_tok * TILE_ROWS, TILE_ROWS), TILE_ROWS), :], sem)


def _scatter_body(slot_ref, xa_ref, xb_ref, xs_hbm, sem, *, tm, n_a):
    _on_part(n_a, functools.partial(_scatter_step, slot_ref, xs_hbm, sem, tm), (xa_ref, xb_ref))


def _scatter_step(slot_ref, xs_hbm, sem, tm, x2t_ref):
    def issue(r, c):
        _tile_copy(x2t_ref, r, xs_hbm, slot_ref[2 * r], sem).start(priority=0)
        _tile_copy(x2t_ref, r, xs_hbm, slot_ref[2 * r + 1], sem).start(priority=1)
        return c

    lax.fori_loop(0, tm, issue, 0, unroll=DMA_UNROLL)

    def drain(r, c):
        _tile_copy(x2t_ref, r, xs_hbm, 0, sem).wait()
        _tile_copy(x2t_ref, r, xs_hbm, 0, sem).wait()
        return c

    lax.fori_loop(0, tm, drain, 0, unroll=DMA_UNROLL)


def _scatter(slot, xa, xb, tm=1024):
    n_a, n_b = xa.shape[0] // (tm * TILE_ROWS), xb.shape[0] // (tm * TILE_ROWS)
    t = (n_a + n_b) * tm
    body = functools.partial(_scatter_body, tm=tm, n_a=n_a)
    return pl.pallas_call(
        body,
        grid=(n_a + n_b,),
        in_specs=[pl.BlockSpec((2 * tm,), lambda i: (i,), memory_space=pltpu.SMEM)]
        + _two_part_specs(tm * TILE_ROWS, n_a, n_b),
        out_specs=pl.BlockSpec(memory_space=pl.ANY),
        out_shape=jax.ShapeDtypeStruct((2 * t * TILE_ROWS, LANES), F32),
        scratch_shapes=[pltpu.SemaphoreType.DMA(())],
        compiler_params=_params("arbitrary"),
        name="scatter",
    )(slot, xa, xb)


def _expert_body(tile_ref, exp_ref, lo_ref, hi_ref, first_ref, last_ref, xs_ref, gm_ref, wg_ref, wu_ref, wd_ref,
                 ys_ref, hb_scr, acc_scr, *, tme):
    k = pl.program_id(0)
    first = first_ref[k] == 1
    last = last_ref[k] == 1
    nonempty = hi_ref[k] > lo_ref[k]

    @pl.when(first)
    def _():
        hb_scr[...] = _rms(_tiles_to_rows(xs_ref, tme), gm_ref[...]).astype(BF16)

    @pl.when(first & jnp.logical_not(nonempty))
    def _():
        acc_scr[...] = jnp.zeros_like(acc_scr)

    @pl.when(nonempty)
    def _():
        hb = hb_scr[...]
        act = jax.nn.silu(_bdot(hb, wg_ref[0])) * _bdot(hb, wu_ref[0])
        row = lax.broadcasted_iota(jnp.int32, (tme, 1), 0)
        y = jnp.where((row >= lo_ref[k]) & (row < hi_ref[k]), _bdot(act.astype(BF16), wd_ref[0]), 0.0)

        @pl.when(first & last)
        def _():
            _rows_to_tiles(ys_ref, y, tme)

        @pl.when(first & jnp.logical_not(last))
        def _():
            acc_scr[...] = y

        @pl.when(jnp.logical_not(first))
        def _():
            acc_scr[...] += y

    @pl.when(last & jnp.logical_not(first & nonempty))
    def _():
        _rows_to_tiles(ys_ref, acc_scr[...], tme)


def _experts(work, xs, g_moe, wg, wu, wd, tme):
    n_work = work[0].shape[0]
    body = functools.partial(_expert_body, tme=tme)
    tile_blk = pl.BlockSpec((tme * TILE_ROWS, LANES), lambda k, tile, *_: (tile[k], 0))

    def wspec(shape):
        return pl.BlockSpec(shape, lambda k, tile, exp, *_: (exp[k], 0, 0))

    grid_spec = pltpu.PrefetchScalarGridSpec(
        num_scalar_prefetch=len(work),
        grid=(n_work,),
        in_specs=[tile_blk, pl.BlockSpec((1, D_MODEL), lambda k, *_: (0, 0)),
                  wspec((1, D_MODEL, D_FF_EXPERT)), wspec((1, D_MODEL, D_FF_EXPERT)),
                  wspec((1, D_FF_EXPERT, D_MODEL))],
        out_specs=tile_blk,
        scratch_shapes=[pltpu.VMEM((tme, D_MODEL), BF16), pltpu.VMEM((tme, D_MODEL), F32)],
    )
    return pl.pallas_call(
        body,
        grid_spec=grid_spec,
        out_shape=jax.ShapeDtypeStruct(xs.shape, F32),
        compiler_params=_params("arbitrary"),
        name="experts",
    )(*work, xs, g_moe, wg, wu, wd)


def _combine_body(slot_ref, slot_next_ref, xa_ref, xb_ref, info_ref, ys_hbm, gf_ref, ya_ref, yb_ref, b0, b1, sem,
                  *, tm, nstep, n_a):
    i = pl.program_id(0)
    cur = i % 2

    def gather(slots, buf):
        def issue(r, c):
            _tile_copy(ys_hbm, slots[2 * r], b0.at[buf], r, sem.at[buf]).start(priority=0)
            _tile_copy(ys_hbm, slots[2 * r + 1], b1.at[buf], r, sem.at[buf]).start(priority=1)
            return c

        lax.fori_loop(0, tm, issue, 0, unroll=DMA_UNROLL)

    @pl.when(i == 0)
    def _():
        gather(slot_ref, 0)

    @pl.when(i + 1 < nstep)
    def _():
        gather(slot_next_ref, 1 - cur)

    def drain(r, c):
        _tile_copy(ys_hbm, 0, b0.at[cur], r, sem.at[cur]).wait()
        _tile_copy(ys_hbm, 0, b1.at[cur], r, sem.at[cur]).wait()
        return c

    lax.fori_loop(0, tm, drain, 0, unroll=DMA_UNROLL)
    info = info_ref[...]
    moe = info[:, 2:3] * _tiles_to_rows(b0.at[cur], tm) + info[:, 3:4] * _tiles_to_rows(b1.at[cur], tm)

    def finish(x2t_ref, y_ref):
        y_ref[...] = _rms(_tiles_to_rows(x2t_ref, tm) + moe, gf_ref[...])

    _on_part(n_a, finish, (xa_ref, xb_ref), (ya_ref, yb_ref))


def _combine(slot, xa, xb, info, ys, g_final, tm=256):
    n_a, n_b = xa.shape[0] // (tm * TILE_ROWS), xb.shape[0] // (tm * TILE_ROWS)
    nstep = n_a + n_b
    body = functools.partial(_combine_body, tm=tm, nstep=nstep, n_a=n_a)
    buf = pltpu.VMEM((2, tm * TILE_ROWS, LANES), F32)
    return pl.pallas_call(
        body,
        grid=(nstep,),
        in_specs=[pl.BlockSpec((2 * tm,), lambda i: (i,), memory_space=pltpu.SMEM),
                  pl.BlockSpec((2 * tm,), lambda i: (jnp.minimum(i + 1, nstep - 1),), memory_space=pltpu.SMEM)]
        + _two_part_specs(tm * TILE_ROWS, n_a, n_b) + [
                  pl.BlockSpec((tm, ROUTER_LANES), lambda i: (i, 0)),
                  pl.BlockSpec(memory_space=pl.ANY),
                  pl.BlockSpec((1, D_MODEL), lambda i: (0, 0))],
        out_specs=[pl.BlockSpec((tm, D_MODEL), lambda i: (jnp.minimum(i, n_a - 1), 0)),
                   pl.BlockSpec((tm, D_MODEL), lambda i: (jnp.clip(i - n_a, 0, n_b - 1), 0))],
        out_shape=[jax.ShapeDtypeStruct((n_a * tm, D_MODEL), F32), jax.ShapeDtypeStruct((n_b * tm, D_MODEL), F32)],
        scratch_shapes=[buf, buf, pltpu.SemaphoreType.DMA((2,))],
        compiler_params=_params("arbitrary"),
        name="combine",
    )(slot, slot, xa, xb, info, ys, g_final)


def _moe(xa, xb, w, tme=512):
    info, cnt = _route(xa, xb, w["g_moe"], w["wr_hi"], w["wr_lo"])
    slot, work = _route_meta(info, cnt, tme)
    xs = _scatter(slot, xa, xb)
    ys = _experts(work, xs, w["g_moe"], w["moe_wg"], w["moe_wu"], w["moe_wd"], tme)
    return _combine(slot, xa, xb, info, ys, w["g_final"])


def _layer(x3d, k4, v4, s5_h0, lru_h0, lru_conv, w, chunk, nb, tq):
    bsz, seq, _ = x3d.shape
    x2d = x3d.reshape(bsz * seq, D_MODEL)
    lru_w = (w["conv_w"], w["conv_b"], w["lru_wg"], w["lru_bg"], w["lru_lam"])
    tail = SUBLANES - (CONV_W - 1)
    if lru_h0 is None:
        u, yl, hl, xt = _mixlru(x2d, w["g_mix"], w["w_in"], *lru_w, bsz, seq)
        lru_h = hl.reshape(bsz, SUBLANES, D_LRU)[:, 0]
        conv_new = xt.reshape(bsz, SUBLANES, D_LRU)[:, tail:]
    else:
        u, xl, gl = _mixin(x2d, w["g_mix"], w["w_in"])
        hist = jnp.pad(lru_conv, ((0, 0), (tail, 0), (0, 0))).reshape(bsz * seq, D_LRU)
        yl, hall = _lru_dec(xl, gl, hist, jnp.repeat(lru_h0, seq, axis=0), *lru_w)
        lru_h = hall.reshape(bsz, seq, D_LRU)[:, seq - 1]
        conv_new = xl.reshape(bsz, seq, D_LRU)[:, seq - (CONV_W - 1):]

    nstate = S5_GROUPS * S5_STATE
    if s5_h0 is None:
        zero = jnp.zeros((bsz, 1, nstate), F32)
        ys, s5_re, s5_im = _s5(u, w["s5"][chunk], zero, zero, chunk, seq // chunk, True)
    else:
        ys, s5_re, s5_im = _s5(u, w["s5"][chunk], s5_h0[0].reshape(bsz, nstate), s5_h0[1].reshape(bsz, nstate),
                               chunk, bsz, False)
    s5_re = s5_re.reshape(bsz, S5_GROUPS, S5_STATE)
    s5_im = s5_im.reshape(bsz, S5_GROUPS, S5_STATE)

    x2t = _mixattn(x3d, ys.reshape(bsz, seq, D_S5), yl.reshape(bsz, seq, D_LRU), k4, v4, w, nb, tq)
    return x2t, s5_re, s5_im, lru_h, conv_new


def kernel(x_prompt, x_sample, mem_prompt, cache_mem_k, cache_mem_v, state_s5_re, state_s5_im, state_lru_h, state_lru_conv, g_mix, w_in, s5_lam_re, s5_lam_im, s5_log_dt, s5_b_re, s5_b_im, s5_c_re, s5_c_im, s5_d, s5_w_glu, s5_b_glu, lru_conv_w, lru_conv_b, lru_w_a, lru_b_a, lru_w_x, lru_b_x, lru_lam, w_out, g_xa, g_mem, xa_w_q, xa_w_k, xa_w_v, xa_w_o, g_moe, moe_w_group, moe_w_expert, moe_w_gate, moe_w_up, moe_w_down, g_final):
    depth = g_mix.shape[0]
    assert depth == 1, "single-layer step"
    l = 0
    bsz, seq, _ = x_prompt.shape
    dbsz, dseq, _ = x_sample.shape
    chunk = SUBLANES
    assert seq % chunk == 0 and dseq == chunk

    s5_args = (s5_lam_re[l], s5_lam_im[l], s5_log_dt[l], s5_b_re[l], s5_b_im[l], s5_c_re[l], s5_c_im[l], s5_d[l])
    wg, bg = _lru_weights(lru_w_a[l], lru_w_x[l], lru_b_a[l], lru_b_x[l])
    wr = jnp.concatenate([moe_w_group[l], moe_w_expert[l],
                          jnp.zeros((D_MODEL, ROUTER_LANES - N_EXPERT_GROUPS - N_EXPERTS), F32)], axis=1)
    wr_hi = wr.astype(BF16)
    wr_lo = (wr - wr_hi.astype(F32)).astype(BF16)
    w = {
        "g_mix": g_mix[l][None], "w_in": w_in[l].astype(BF16),
        "s5": {chunk: _s5_weights(chunk, *s5_args)},
        "conv_w": lru_conv_w[l], "conv_b": lru_conv_b[l][None], "lru_wg": wg, "lru_bg": bg,
        "lru_lam": lru_lam[l][None],
        "w_glu": s5_w_glu[l].astype(BF16), "b_glu": s5_b_glu[l][None], "w_out": w_out[l].astype(BF16),
        "g_xa": g_xa[l][None], "w_q": xa_w_q[l].astype(BF16), "w_o": xa_w_o[l].astype(BF16),
        "g_moe": g_moe[l][None], "wr_hi": wr_hi, "wr_lo": wr_lo,
        "moe_wg": moe_w_gate[l].astype(BF16), "moe_wu": moe_w_up[l].astype(BF16),
        "moe_wd": moe_w_down[l].astype(BF16), "g_final": g_final[None],
    }

    mk, mv = _memkv(mem_prompt.reshape(bsz * N_MEM, D_MODEL), g_mem[l][None],
                    xa_w_k[l].astype(BF16), xa_w_v[l].astype(BF16))
    xp, p_re, p_im, p_h, p_conv = _layer(x_prompt, mk.reshape(bsz, N_MEM, D_MODEL), mv.reshape(bsz, N_MEM, D_MODEL),
                                         None, None, None, w, chunk, nb=1, tq=1024)
    xs, s_re, s_im, s_h, s_conv = _layer(x_sample, cache_mem_k[l], cache_mem_v[l],
                                         (state_s5_re[l], state_s5_im[l]), state_lru_h[l],
                                         state_lru_conv[l], w, chunk, nb=8, tq=dseq)
    yp, ysmp = _moe(xp, xs, w)
    yp = yp.reshape(bsz, seq, D_MODEL)
    ysmp = ysmp.reshape(dbsz, dseq, D_MODEL)

    return (yp, ysmp,
            mk.reshape(1, bsz, N_MEM, XA_HEADS, XA_HEAD_DIM), mv.reshape(1, bsz, N_MEM, XA_HEADS, XA_HEAD_DIM),
            p_re[None], p_im[None], p_h[None], p_conv[None],
            s_re[None], s_im[None], s_h[None], s_conv[None])
```
